```python
import math
import jax, jax.numpy as jnp
from jax import lax
import numpy as np

D_MODEL = 1024
BATCH = 2
SEQ = 8192
DEPTH = 1
DEC_BATCH = 32
DEC_SEQ = 4
PAST_LEN = 16384
PAGE_SIZE = 128

ATT_WIDTH = D_MODEL // 2
CONV_WIDTH = D_MODEL - ATT_WIDTH
DIFF_HEAD_DIM = 64
ATT_V_DIM = 2 * DIFF_HEAD_DIM
N_ATT_HEADS = ATT_WIDTH // ATT_V_DIM
IN_COLS = 3 * ATT_WIDTH + 2 * CONV_WIDTH
CONV_K = 31
N_BUCKETS = 32
MAX_DISTANCE = 128
N_MEM = 256
N_X_HEADS = 4
X_HEAD_DIM = D_MODEL // N_X_HEADS
D_FF = ((8 * D_MODEL + 3 * 256 - 1) // (3 * 256)) * 256
Q_BLOCK = 128
RMS_EPS = 1e-6
LN_EPS = 1e-5
ATT_SCALE = DIFF_HEAD_DIM ** -0.5
X_SCALE = X_HEAD_DIM ** -0.5

kernel_name = "hymba_diffattn_conformer_step"


def rms_norm(x, g):
    xf = x.astype(jnp.float32)
    y = xf * lax.rsqrt(jnp.mean(xf * xf, axis=-1, keepdims=True) + RMS_EPS)
    return (y * g.astype(jnp.float32)).astype(x.dtype)


def layer_norm(x, g, b):
    xf = x.astype(jnp.float32)
    mu = jnp.mean(xf, axis=-1, keepdims=True)
    xc = xf - mu
    y = xc * lax.rsqrt(jnp.mean(xc * xc, axis=-1, keepdims=True) + LN_EPS)
    return (y * g.astype(jnp.float32) + b.astype(jnp.float32)).astype(x.dtype)


def rel_bucket(q_pos, k_pos):
    n = jnp.maximum(q_pos - k_pos, 0)
    max_exact = N_BUCKETS // 2
    nf = jnp.maximum(n, 1).astype(jnp.float32)
    large = max_exact + (jnp.log(nf / max_exact) / math.log(MAX_DISTANCE / max_exact)
                         * (N_BUCKETS - max_exact)).astype(jnp.int32)
    large = jnp.minimum(large, N_BUCKETS - 1)
    return jnp.where(n < max_exact, n, large)


def rel_bias(q_pos, k_pos, table):
    b = table.astype(jnp.float32)[rel_bucket(q_pos[:, None], k_pos[None, :])]
    b = jnp.moveaxis(b, -1, 0)
    return jnp.where(k_pos[None, None, :] > q_pos[None, :, None], -jnp.inf, b)


def diff_weights(s, bias, lam):
    logits = s.astype(jnp.float32) * ATT_SCALE + bias[None, None]
    p = jax.nn.softmax(logits, axis=-1)
    return p[:, 0] - lam * p[:, 1]


def diff_attention_prompt(q, k, v, table, lam):
    B, T, H, _ = q.shape
    q2 = q.reshape(B, T, H, 2, DIFF_HEAD_DIM)
    k2 = k.reshape(B, T, H, 2, DIFF_HEAD_DIM)
    k_pos = jnp.arange(T, dtype=jnp.int32)

    def one_block(start):
        qb = lax.dynamic_slice_in_dim(q2, start, Q_BLOCK, axis=1)
        q_pos = start + jnp.arange(Q_BLOCK, dtype=jnp.int32)
        s = jnp.einsum('bqhjd,bkhjd->bjhqk', qb, k2)
        w = diff_weights(s, rel_bias(q_pos, k_pos, table), lam)
        return jnp.einsum('bhqk,bkhe->bqhe', w.astype(v.dtype), v)

    starts = jnp.arange(T // Q_BLOCK, dtype=jnp.int32) * Q_BLOCK
    o = lax.map(one_block, starts)
    return jnp.transpose(o, (1, 0, 2, 3, 4)).reshape(B, T, H, ATT_V_DIM)


def diff_attention_sample(q, k_new, v_new, k_past, v_past, table, lam):
    B, T, H, _ = q.shape
    P = k_past.shape[1]
    q2 = q.reshape(B, T, H, 2, DIFF_HEAD_DIM)
    s_past = jnp.einsum('bqhjd,bkhjd->bjhqk', q2, k_past.reshape(B, P, H, 2, DIFF_HEAD_DIM))
    s_new = jnp.einsum('bqhjd,bkhjd->bjhqk', q2, k_new.reshape(B, T, H, 2, DIFF_HEAD_DIM))
    s = jnp.concatenate([s_past, s_new], axis=-1)
    q_pos = P + jnp.arange(T, dtype=jnp.int32)
    k_pos = jnp.arange(P + T, dtype=jnp.int32)
    w = diff_weights(s, rel_bias(q_pos, k_pos, table), lam).astype(v_new.dtype)
    return (jnp.einsum('bhqk,bkhe->bqhe', w[..., :P], v_past)
            + jnp.einsum('bhqk,bkhe->bqhe', w[..., P:], v_new))


def conv_branch(glu, prev, dw_w, dw_b, ln_g, ln_b):
    xp = jnp.concatenate([prev, glu], axis=1)
    y = lax.conv_general_dilated(xp, dw_w[:, None, :], window_strides=(1,), padding='VALID',
                                 dimension_numbers=('NWC', 'WIO', 'NWC'),
                                 feature_group_count=CONV_WIDTH)
    y = layer_norm(y + dw_b, ln_g, ln_b)
    return jax.nn.silu(y), xp[:, -(CONV_K - 1):]


def cross_attention(xn, mem_k, mem_v, w_xq, w_xo):
    B, T, _ = xn.shape
    q = (xn @ w_xq).reshape(B, T, N_X_HEADS, X_HEAD_DIM)
    s = jnp.einsum('bqhd,bkhd->bhqk', q, mem_k).astype(jnp.float32) * X_SCALE
    p = jax.nn.softmax(s, axis=-1).astype(mem_v.dtype)
    o = jnp.einsum('bhqk,bkhd->bqhd', p, mem_v).reshape(B, T, D_MODEL)
    return o @ w_xo


def mem_kv(mem, w_xk, w_xv):
    B, M, _ = mem.shape
    return ((mem @ w_xk).reshape(B, M, N_X_HEADS, X_HEAD_DIM),
            (mem @ w_xv).reshape(B, M, N_X_HEADS, X_HEAD_DIM))


def trunk_layer(h, attn_fn, conv_prev, mem_k, mem_v, lam_init, lw):
    (n_mix_pre, n_mix_post, w_in, subln_g, dw_w, dw_b, ln_g, ln_b, beta_att, beta_conv,
     w_out, n_x_pre, n_x_post, w_xq, w_xo, n_ffn_pre, n_ffn_post, w_gate, w_up, w_down) = lw
    B, T, _ = h.shape
    xn = rms_norm(h, n_mix_pre)
    u = xn @ w_in
    q, k, v, ca, cg = jnp.split(u, [ATT_WIDTH, 2 * ATT_WIDTH, 3 * ATT_WIDTH,
                                    3 * ATT_WIDTH + CONV_WIDTH], axis=-1)
    q = q.reshape(B, T, N_ATT_HEADS, ATT_V_DIM)
    k = k.reshape(B, T, N_ATT_HEADS, ATT_V_DIM)
    v = v.reshape(B, T, N_ATT_HEADS, ATT_V_DIM)
    glu = ca * jax.nn.sigmoid(cg)
    att = rms_norm(attn_fn(q, k, v), subln_g) * (1.0 - lam_init)
    conv, conv_state = conv_branch(glu, conv_prev, dw_w, dw_b, ln_g, ln_b)
    merged = jnp.concatenate([att.reshape(B, T, ATT_WIDTH) * beta_att, conv * beta_conv], axis=-1)
    h = h + rms_norm(merged @ w_out, n_mix_post)
    h = h + rms_norm(cross_attention(rms_norm(h, n_x_pre), mem_k, mem_v, w_xq, w_xo), n_x_post)
    xf = rms_norm(h, n_ffn_pre)
    f = (jax.nn.silu(xf @ w_gate) * (xf @ w_up)) @ w_down
    h = h + rms_norm(f, n_ffn_post)
    return h, k, v, conv_state


def setup_inputs(seed: int = 0) -> dict:
    key = jax.random.key(seed)
    ks = iter(jax.random.split(key, 48))
    f32 = jnp.float32

    def nrm(shape, scale):
        return jax.random.normal(next(ks), shape, f32) * scale

    def gain(shape):
        return 1.0 + 0.05 * jax.random.normal(next(ks), shape, f32)

    n_pages = PAST_LEN // PAGE_SIZE
    n_used = DEC_BATCH * n_pages
    n_phys = (5 * n_used + 3) // 4
    L = DEPTH
    d_in = D_MODEL ** -0.5
    inp = {}
    inp["x_prompt"] = nrm((BATCH, SEQ, D_MODEL), 1.0)
    inp["x_sample"] = nrm((DEC_BATCH, DEC_SEQ, D_MODEL), 1.0)
    inp["mem_prompt"] = nrm((BATCH, N_MEM, D_MODEL), 1.0)
    inp["cache_k"] = nrm((L, n_phys, PAGE_SIZE, N_ATT_HEADS, ATT_V_DIM), 1.0)
    inp["cache_v"] = nrm((L, n_phys, PAGE_SIZE, N_ATT_HEADS, ATT_V_DIM), 1.0)
    inp["state_conv"] = nrm((L, DEC_BATCH, CONV_K - 1, CONV_WIDTH), 0.5)
    inp["cache_mem_k"] = nrm((L, DEC_BATCH, N_MEM, N_X_HEADS, X_HEAD_DIM), 1.0)
    inp["cache_mem_v"] = nrm((L, DEC_BATCH, N_MEM, N_X_HEADS, X_HEAD_DIM), 1.0)
    inp["page_table"] = jax.random.permutation(next(ks), n_phys)[:n_used].reshape(
        DEC_BATCH, n_pages).astype(jnp.int32)
    inp["rel_bias_table"] = nrm((N_BUCKETS, N_ATT_HEADS), 0.5)
    inp["norm_mix_pre"] = gain((L, D_MODEL))
    inp["norm_mix_post"] = gain((L, D_MODEL))
    inp["w_in"] = nrm((L, D_MODEL, IN_COLS), d_in)
    inp["lambda_q1"] = nrm((L, DIFF_HEAD_DIM), 0.1)
    inp["lambda_k1"] = nrm((L, DIFF_HEAD_DIM), 0.1)
    inp["lambda_q2"] = nrm((L, DIFF_HEAD_DIM), 0.1)
    inp["lambda_k2"] = nrm((L, DIFF_HEAD_DIM), 0.1)
    inp["subln_g"] = gain((L, ATT_V_DIM))
    inp["dw_w"] = nrm((L, CONV_K, CONV_WIDTH), CONV_K ** -0.5)
    inp["dw_b"] = nrm((L, CONV_WIDTH), 0.02)
    inp["conv_ln_g"] = gain((L, CONV_WIDTH))
    inp["conv_ln_b"] = nrm((L, CONV_WIDTH), 0.02)
    inp["beta_att"] = gain((L, ATT_WIDTH))
    inp["beta_conv"] = gain((L, CONV_WIDTH))
    inp["w_out"] = nrm((L, D_MODEL, D_MODEL), d_in)
    inp["norm_x_pre"] = gain((L, D_MODEL))
    inp["norm_x_post"] = gain((L, D_MODEL))
    inp["w_xq"] = nrm((L, D_MODEL, D_MODEL), d_in)
    inp["w_xk"] = nrm((L, D_MODEL, D_MODEL), d_in)
    inp["w_xv"] = nrm((L, D_MODEL, D_MODEL), d_in)
    inp["w_xo"] = nrm((L, D_MODEL, D_MODEL), d_in)
    inp["norm_ffn_pre"] = gain((L, D_MODEL))
    inp["norm_ffn_post"] = gain((L, D_MODEL))
    inp["w_gate"] = nrm((L, D_MODEL, D_FF), d_in)
    inp["w_up"] = nrm((L, D_MODEL, D_FF), d_in)
    inp["w_down"] = nrm((L, D_FF, D_MODEL), D_FF ** -0.5)
    return inp


def reference(x_prompt, x_sample, mem_prompt, cache_k, cache_v, state_conv, cache_mem_k,
              cache_mem_v, page_table, rel_bias_table, norm_mix_pre, norm_mix_post, w_in,
              lambda_q1, lambda_k1, lambda_q2, lambda_k2, subln_g, dw_w, dw_b, conv_ln_g,
              conv_ln_b, beta_att, beta_conv, w_out, norm_x_pre, norm_x_post, w_xq, w_xk,
              w_xv, w_xo, norm_ffn_pre, norm_ffn_post, w_gate, w_up, w_down):
    dec_b = x_sample.shape[0]
    past = page_table.shape[1] * PAGE_SIZE
    hp, hs = x_prompt, x_sample
    kp_l, vp_l, cp_l, mkp_l, mvp_l, ks_l, vs_l, cs_l = [], [], [], [], [], [], [], []
    for l in range(DEPTH):
        lam_init = 0.8 - 0.6 * math.exp(-0.3 * l)
        f32 = jnp.float32
        lam = (jnp.exp(jnp.sum(lambda_q1[l].astype(f32) * lambda_k1[l].astype(f32)))
               - jnp.exp(jnp.sum(lambda_q2[l].astype(f32) * lambda_k2[l].astype(f32)))
               + lam_init)
        lw = (norm_mix_pre[l], norm_mix_post[l], w_in[l], subln_g[l], dw_w[l], dw_b[l],
              conv_ln_g[l], conv_ln_b[l], beta_att[l], beta_conv[l], w_out[l], norm_x_pre[l],
              norm_x_post[l], w_xq[l], w_xo[l], norm_ffn_pre[l], norm_ffn_post[l],
              w_gate[l], w_up[l], w_down[l])
        mk_p, mv_p = mem_kv(mem_prompt, w_xk[l], w_xv[l])
        conv0 = jnp.zeros((hp.shape[0], CONV_K - 1, CONV_WIDTH), hp.dtype)
        attn_p = lambda q, k, v: diff_attention_prompt(q, k, v, rel_bias_table, lam)
        hp, kp, vp, cp = trunk_layer(hp, attn_p, conv0, mk_p, mv_p, lam_init, lw)
        k_past = cache_k[l, page_table].reshape(dec_b, past, N_ATT_HEADS, ATT_V_DIM)
        v_past = cache_v[l, page_table].reshape(dec_b, past, N_ATT_HEADS, ATT_V_DIM)
        attn_s = lambda q, k, v: diff_attention_sample(q, k, v, k_past, v_past,
                                                       rel_bias_table, lam)
        hs, kn, vn, cn = trunk_layer(hs, attn_s, state_conv[l], cache_mem_k[l],
                                     cache_mem_v[l], lam_init, lw)
        kp_l.append(kp); vp_l.append(vp); cp_l.append(cp)
        mkp_l.append(mk_p); mvp_l.append(mv_p)
        ks_l.append(kn); vs_l.append(vn); cs_l.append(cn)
    return (hp, hs, jnp.stack(kp_l), jnp.stack(vp_l), jnp.stack(cp_l), jnp.stack(mkp_l),
            jnp.stack(mvp_l), jnp.stack(ks_l), jnp.stack(vs_l), jnp.stack(cs_l))
```

```python
import functools
import math

import numpy as np
import jax
import jax.numpy as jnp
from jax import lax
from jax.experimental import pallas as pl
from jax.experimental.pallas import tpu as pltpu

F32 = jnp.float32
BF16 = jnp.bfloat16

DIFF_HEAD_DIM = 64
ATT_V_DIM = 2 * DIFF_HEAD_DIM
N_ATT_HEADS = 4
ATT_WIDTH = N_ATT_HEADS * ATT_V_DIM
CONV_WIDTH = 512
CONV_K = 31
N_BUCKETS = 32
MAX_DISTANCE = 128
N_X_HEADS = 4
PAGE_SIZE = 128
RMS_EPS = 1e-6
LN_EPS = 1e-5
ATT_SCALE = DIFF_HEAD_DIM ** -0.5
LAM_INIT = 0.8 - 0.6 * math.exp(-0.3 * 0)
MASKED = -1e30

V7X_VMEM_BYTES = 64 * 1024 * 1024
LANES = 128
SUBLANES = 8

ROW_TILE = 512
ATT_BLOCK = 256
CONV_TILE = 256
CONV_CHUNK = 64
CONV_HALO = 32
PAGES_PER_STEP = 8
FFN_CHUNK = 256
SAMPLE_ROWS = 8
MAX_NEW_TOKENS = 4
SAMPLE_Q_ROWS = MAX_NEW_TOKENS * 2 * N_ATT_HEADS


def _vmem_limit(nbytes):
    return int(min(max(2 * nbytes, 16 * 1024 * 1024), V7X_VMEM_BYTES - 8 * 1024 * 1024))


def _rms(x, g):
    return x * lax.rsqrt(jnp.mean(x * x, axis=-1, keepdims=True) + RMS_EPS) * g


def _sigmoid(x):
    return 1.0 / (1.0 + jnp.exp(-x))


def _nt_dot(a, b):
    return lax.dot_general(a, b, (((1,), (1,)), ((), ())), preferred_element_type=F32)


def _bucket_np(n):
    n = np.maximum(n, 0)
    max_exact = N_BUCKETS // 2
    nf = np.maximum(n, 1).astype(np.float32)
    large = max_exact + (np.log(nf / max_exact) / math.log(MAX_DISTANCE / max_exact)
                         * (N_BUCKETS - max_exact)).astype(np.int32)
    large = np.minimum(large, N_BUCKETS - 1)
    return np.where(n < max_exact, n, large).astype(np.int32)


def _prompt_codes(blk):
    i = np.arange(blk)[:, None]
    j = np.arange(blk)[None, :]
    prev = _bucket_np(i - j + blk)
    diag = np.where(j > i, -1, _bucket_np(i - j))
    return np.stack([prev, diag]).astype(np.int32)


def _sample_codes(n_new):
    t = (np.arange(SAMPLE_Q_ROWS) // 8)[:, None]
    c = np.arange(PAGE_SIZE)[None, :]
    last = _bucket_np(t + PAGE_SIZE - c)
    new = np.where((c > t) | (c >= n_new), -1, _bucket_np(t - c))
    return np.stack([last, new]).astype(np.int32)


def _bias_kernel(tab_ref, lam_in_ref, pcode_ref, scode_ref, pbias_ref, sbias_ref, lam_ref):
    far = N_BUCKETS - 1

    def lookup(code, h):
        out = jnp.zeros(code.shape, F32)
        for b in range(far):
            out = jnp.where(code == b, tab_ref[b, h] - tab_ref[far, h], out)
        return jnp.where(code < 0, MASKED, out)

    for h in range(N_ATT_HEADS):
        for i in range(2):
            pbias_ref[h, i] = lookup(pcode_ref[i], h)
    row_head = (lax.broadcasted_iota(jnp.int32, scode_ref.shape[1:], 0) >> 1) & (N_ATT_HEADS - 1)
    for i in range(2):
        code = scode_ref[i]
        out = jnp.zeros(code.shape, F32)
        for h in range(N_ATT_HEADS):
            out = jnp.where(row_head == h, lookup(code, h), out)
        sbias_ref[i] = out
    lv = lam_in_ref[...]
    d1 = jnp.sum(lv[0:1] * lv[1:2], axis=1, keepdims=True)
    d2 = jnp.sum(lv[2:3] * lv[3:4], axis=1, keepdims=True)
    lam = jnp.exp(d1) - jnp.exp(d2) + LAM_INIT
    lam_ref[...] = jnp.broadcast_to(lam, lam_ref.shape)


def _bias_tiles(table, lam_vecs, blk, n_new):
    pcode = jnp.asarray(_prompt_codes(blk))
    scode = jnp.asarray(_sample_codes(n_new))
    vm = pl.BlockSpec(memory_space=pltpu.VMEM)
    return pl.pallas_call(
        _bias_kernel,
        out_shape=(jax.ShapeDtypeStruct((N_ATT_HEADS, 2, blk, blk), F32),
                   jax.ShapeDtypeStruct(scode.shape, F32),
                   jax.ShapeDtypeStruct((SUBLANES, LANES), F32)),
        in_specs=[pl.BlockSpec(memory_space=pltpu.SMEM), vm, vm, vm],
        out_specs=(vm, vm, vm),
        name="bias_tiles",
    )(table, lam_vecs, pcode, scode)


def _in_proj_kernel(x_ref, g_ref, w_ref, q_ref, kb_ref, vb_ref, kf_ref, vf_ref, glu_ref):
    xn = _rms(x_ref[...], g_ref[...]).astype(BF16)

    def cols(c):
        return jnp.dot(xn, w_ref[:, c * ATT_WIDTH:(c + 1) * ATT_WIDTH], preferred_element_type=F32)

    q = (cols(0) * ATT_SCALE).astype(BF16)
    k = cols(1)
    v = cols(2)
    kf_ref[...] = k
    vf_ref[...] = v
    kb = k.astype(BF16)
    vb = v.astype(BF16)
    for h in range(N_ATT_HEADS):
        sl = slice(h * ATT_V_DIM, (h + 1) * ATT_V_DIM)
        q_ref[h] = q[:, sl]
        kb_ref[h] = kb[:, sl]
        vb_ref[h] = vb[:, sl]
    glu_ref[...] = cols(3) * _sigmoid(cols(4))


def _in_proj(x, gain, w_in):
    b, t, d = x.shape
    tm = min(ROW_TILE, t)
    n_cols = w_in.shape[1]
    row = lambda w: pl.BlockSpec((None, tm, w), lambda i, j: (i, j, 0))
    heads = pl.BlockSpec((None, N_ATT_HEADS, tm, ATT_V_DIM), lambda i, j: (i, 0, j, 0))
    hm = jax.ShapeDtypeStruct((b, N_ATT_HEADS, t, ATT_V_DIM), BF16)
    flat = jax.ShapeDtypeStruct((b, t, ATT_WIDTH), F32)
    est = 2 * d * n_cols * 2 + 2 * tm * (d * 4 + 3 * ATT_WIDTH * 2 + 3 * ATT_WIDTH * 4) + 6 * tm * ATT_WIDTH * 4
    return pl.pallas_call(
        _in_proj_kernel,
        grid=(b, t // tm),
        in_specs=[row(d),
                  pl.BlockSpec((1, d), lambda i, j: (0, 0)),
                  pl.BlockSpec((d, n_cols), lambda i, j: (0, 0))],
        out_specs=(heads, heads, heads, row(ATT_WIDTH), row(ATT_WIDTH), row(CONV_WIDTH)),
        out_shape=(hm, hm, hm, flat, flat, jax.ShapeDtypeStruct((b, t, CONV_WIDTH), F32)),
        compiler_params=pltpu.CompilerParams(dimension_semantics=("parallel", "parallel"),
                                             vmem_limit_bytes=_vmem_limit(est)),
        name="in_proj",
    )(x, gain, w_in)


def _conv_post(y, b_ref, g_ref, bt_ref, beta_ref):
    y = y + b_ref[...]
    mu = jnp.mean(y, axis=-1, keepdims=True)
    yc = y - mu
    yn = yc * lax.rsqrt(jnp.mean(yc * yc, axis=-1, keepdims=True) + LN_EPS) * g_ref[...] + bt_ref[...]
    return yn * _sigmoid(yn) * beta_ref[...]


def _conv_prompt_kernel(glu_ref, prev_ref, w_ref, b_ref, g_ref, bt_ref, beta_ref, o_ref, buf):
    tt = glu_ref.shape[0]
    first = pl.program_id(1) == 0

    @pl.when(first)
    def _():
        buf[0:CONV_HALO] = prev_ref[...]

    @pl.when(jnp.logical_not(first))
    def _():
        buf[0:CONV_HALO] = buf[tt:tt + CONV_HALO]

    buf[CONV_HALO:CONV_HALO + tt] = glu_ref[...]

    shift = CONV_HALO - (CONV_K - 1)
    rc = min(CONV_CHUNK, tt)
    for c0 in range(0, tt, rc):
        acc = jnp.zeros((rc, CONV_WIDTH), F32)
        for r in range(SUBLANES):
            offs = [o for o in range(shift, shift + CONV_K) if o % SUBLANES == r]
            if not offs:
                continue
            a_max = max(offs) // SUBLANES
            xr = buf[pl.ds(c0 + r, rc + SUBLANES * a_max), :]
            for o in offs:
                a = o // SUBLANES
                j = o - shift
                acc = acc + w_ref[j:j + 1, :] * xr[SUBLANES * a:SUBLANES * a + rc]
        o_ref[c0:c0 + rc] = _conv_post(acc, b_ref, g_ref, bt_ref, beta_ref).astype(o_ref.dtype)


def _conv_prompt(glu, prev, dw_w, dw_b, ln_g, ln_b, beta):
    b, t, c = glu.shape
    tt = min(CONV_TILE, t)
    vec = pl.BlockSpec((1, c), lambda i, j: (0, 0))
    return pl.pallas_call(
        _conv_prompt_kernel,
        grid=(b, t // tt),
        in_specs=[pl.BlockSpec((None, tt, c), lambda i, j: (i, j, 0)),
                  pl.BlockSpec((None, CONV_HALO, c), lambda i, j: (i, 0, 0)),
                  pl.BlockSpec(dw_w.shape, lambda i, j: (0, 0)),
                  vec, vec, vec, vec],
        out_specs=pl.BlockSpec((None, tt, c), lambda i, j: (i, j, 0)),
        out_shape=jax.ShapeDtypeStruct((b, t, c), BF16),
        scratch_shapes=[pltpu.VMEM((tt + CONV_HALO, c), F32)],
        compiler_params=pltpu.CompilerParams(dimension_semantics=("parallel", "arbitrary")),
        name="conv_prompt",
    )(glu, prev, dw_w, dw_b, ln_g, ln_b, beta)


def _conv_sample_kernel(xp_ref, w_ref, b_ref, g_ref, bt_ref, beta_ref, o_ref):
    n_t = o_ref.shape[0]
    for t in range(n_t):
        acc = jnp.zeros(xp_ref.shape[1:], F32)
        for j in range(CONV_K):
            acc = acc + w_ref[j:j + 1, :] * xp_ref[t + j]
        o_ref[t] = _conv_post(acc, b_ref, g_ref, bt_ref, beta_ref)


def _conv_sample(xp_t, n_t, dw_w, dw_b, ln_g, ln_b, beta):
    vm = pl.BlockSpec(memory_space=pltpu.VMEM)
    return pl.pallas_call(
        _conv_sample_kernel,
        out_shape=jax.ShapeDtypeStruct((n_t,) + xp_t.shape[1:], F32),
        in_specs=[vm] * 6,
        out_specs=vm,
        name="conv_sample",
    )(xp_t, dw_w, dw_b, ln_g, ln_b, beta)


def _softmax_update(s, v_dot, m_sc, l_sc, acc_sc):
    width = s.shape[1]
    m_prev = m_sc[...]
    m_new = jnp.maximum(m_prev, jnp.max(s, axis=1, keepdims=True))
    alpha = jnp.exp(m_prev - m_new)
    p = jnp.exp(s - jnp.concatenate([m_new] * (width // LANES), axis=1))
    l_sc[...] = alpha * l_sc[...] + jnp.sum(p, axis=1, keepdims=True)
    reps = acc_sc.shape[1] // LANES
    acc_sc[...] = jnp.concatenate([alpha] * reps, axis=1) * acc_sc[...] + v_dot(p.astype(BF16))
    m_sc[...] = m_new


def _subln(att, g, beta):
    return _rms(att, g) * (1.0 - LAM_INIT) * beta


def _attn_prompt_kernel(q_ref, k_ref, v_ref, bias_ref, lam_ref, g_ref, beta_ref, o_ref, m_sc, l_sc, acc_sc):
    blk = q_ref.shape[0]
    qi = pl.program_id(2)
    q = q_ref[...]
    lane = lax.broadcasted_iota(jnp.int32, q.shape, 1)
    zero = jnp.zeros_like(q)
    qs = jnp.concatenate([jnp.where(lane < DIFF_HEAD_DIM, q, zero),
                          jnp.where(lane >= DIFF_HEAD_DIM, q, zero)], axis=0)
    m_sc[...] = jnp.full(m_sc.shape, MASKED, F32)
    l_sc[...] = jnp.zeros(l_sc.shape, F32)
    acc_sc[...] = jnp.zeros(acc_sc.shape, F32)

    def block(j, bias):
        rows = pl.ds(pl.multiple_of(j * blk, blk), blk)
        s = _nt_dot(qs, k_ref[rows, :])
        if bias is not None:
            s = s + jnp.concatenate([bias, bias], axis=0)
        v = v_ref[rows, :]
        _softmax_update(s, lambda p: jnp.dot(p, v, preferred_element_type=F32), m_sc, l_sc, acc_sc)

    def far_block(j, carry):
        block(j, None)
        return carry

    lax.fori_loop(0, qi - 1, far_block, 0)

    @pl.when(qi >= 1)
    def _():
        block(qi - 1, bias_ref[0])

    block(qi, bias_ref[1])

    o = acc_sc[...] / l_sc[...]
    att = o[:blk] - lam_ref[0:1, :] * o[blk:]
    o_ref[...] = _subln(att, g_ref[...], beta_ref[...]).astype(o_ref.dtype)


def _attn_prompt(q, k, v, bias, lam, subln_g, beta_att):
    b, h, t, e = q.shape
    blk = bias.shape[-1]
    seq = pl.BlockSpec((None, None, t, e), lambda i, j, n: (i, j, 0, 0))
    est = 2 * (2 * t * e * 2) + 2 * 2 * blk * blk * 4 + 6 * 2 * blk * blk * 4
    return pl.pallas_call(
        _attn_prompt_kernel,
        grid=(b, h, t // blk),
        in_specs=[pl.BlockSpec((None, None, blk, e), lambda i, j, n: (i, j, n, 0)),
                  seq, seq,
                  pl.BlockSpec((None, 2, blk, blk), lambda i, j, n: (j, 0, 0, 0)),
                  pl.BlockSpec(lam.shape, lambda i, j, n: (0, 0)),
                  pl.BlockSpec((1, e), lambda i, j, n: (0, 0)),
                  pl.BlockSpec((1, e), lambda i, j, n: (0, j))],
        out_specs=pl.BlockSpec((None, blk, e), lambda i, j, n: (i, n, j)),
        out_shape=jax.ShapeDtypeStruct((b, t, h * e), BF16),
        scratch_shapes=[pltpu.VMEM((2 * blk, LANES), F32),
                        pltpu.VMEM((2 * blk, LANES), F32),
                        pltpu.VMEM((2 * blk, e), F32)],
        compiler_params=pltpu.CompilerParams(dimension_semantics=("parallel", "parallel", "arbitrary"),
                                             vmem_limit_bytes=_vmem_limit(est)),
        name="attn_prompt",
    )(q, k, v, bias, lam, subln_g, beta_att)


def _attn_sample_kernel(pt_ref, q_ref, *refs):
    del pt_ref
    npg = PAGES_PER_STEP
    k_refs, v_refs = refs[:npg], refs[npg:2 * npg]
    (kn_ref, vn_ref, bias_ref, lam_ref, g_ref, beta_ref, o_ref, m_sc, l_sc, acc_sc) = refs[2 * npg:]
    c = pl.program_id(1)
    last = c == pl.num_programs(1) - 1

    @pl.when(c == 0)
    def _():
        m_sc[...] = jnp.full(m_sc.shape, MASKED, F32)
        l_sc[...] = jnp.zeros(l_sc.shape, F32)
        acc_sc[...] = jnp.zeros(acc_sc.shape, F32)

    q = q_ref[...]
    lane = lax.broadcasted_iota(jnp.int32, q.shape, 1)
    row = lax.broadcasted_iota(jnp.int32, q.shape, 0)
    pair = row & 7
    qbd = jnp.where((lane >> 6) == pair, q, jnp.zeros_like(q))

    def pages_dot(v_list):
        def v_dot(p):
            out = None
            for i, vr in enumerate(v_list):
                part = jnp.dot(p[:, i * PAGE_SIZE:(i + 1) * PAGE_SIZE], vr[...].astype(BF16),
                               preferred_element_type=F32)
                out = part if out is None else out + part
            return out
        return v_dot

    s_pages = [_nt_dot(qbd, kr[...].astype(BF16)) for kr in k_refs]
    s_pages[-1] = s_pages[-1] + jnp.where(last, bias_ref[0], jnp.zeros_like(bias_ref[0]))
    _softmax_update(jnp.concatenate(s_pages, axis=1), pages_dot(v_refs), m_sc, l_sc, acc_sc)

    @pl.when(last)
    def _():
        s_new = _nt_dot(qbd, kn_ref[...].astype(BF16)) + bias_ref[1]
        _softmax_update(s_new, pages_dot([vn_ref]), m_sc, l_sc, acc_sc)
        inv = 1.0 / l_sc[...]
        lam = lam_ref[0:1, 0:1]
        sign = jnp.where((pair & 1) == 0, 1.0, -lam)
        own = (lane >> 7) == (pair >> 1)
        z = acc_sc[...] * jnp.concatenate([inv] * N_ATT_HEADS, axis=1) * jnp.where(own, sign, 0.0)
        out_row = lax.broadcasted_iota(jnp.int32, o_ref.shape, 0)
        att = jnp.zeros(o_ref.shape, F32)
        for t in range(MAX_NEW_TOKENS):
            tok = jnp.sum(z[t * 8:(t + 1) * 8], axis=0, keepdims=True)
            att = jnp.where(out_row == t, jnp.broadcast_to(tok, att.shape), att)
        for h in range(N_ATT_HEADS):
            sl = slice(h * ATT_V_DIM, (h + 1) * ATT_V_DIM)
            o_ref[:, sl] = _subln(att[:, sl], g_ref[...], beta_ref[:, sl])


def _attn_sample(page_table, q_rep, cache_k, cache_v, k_new, v_new, bias, lam, subln_g, beta_att):
    b, n_pages = page_table.shape
    rows, width = q_rep.shape[1:]
    npg = PAGES_PER_STEP
    assert n_pages % npg == 0

    def page_spec(i):
        return pl.BlockSpec((None, PAGE_SIZE, width), lambda s, c, pt: (pt[s, c * npg + i], 0, 0))

    per_seq = lambda r: pl.BlockSpec((None, r, width), lambda s, c, pt: (s, 0, 0))
    whole = lambda a: pl.BlockSpec(a.shape, lambda s, c, pt: (0,) * a.ndim)
    grid_spec = pltpu.PrefetchScalarGridSpec(
        num_scalar_prefetch=1,
        grid=(b, n_pages // npg),
        in_specs=([per_seq(rows)] + [page_spec(i) for i in range(npg)] * 2
                  + [per_seq(PAGE_SIZE), per_seq(PAGE_SIZE), whole(bias), whole(lam), whole(subln_g),
                     whole(beta_att)]),
        out_specs=per_seq(SAMPLE_ROWS),
        scratch_shapes=[pltpu.VMEM((rows, LANES), F32), pltpu.VMEM((rows, LANES), F32),
                        pltpu.VMEM((rows, width), F32)],
    )
    est = 2 * 2 * npg * PAGE_SIZE * width * 4 + 4 * PAGE_SIZE * width * 4
    return pl.pallas_call(
        _attn_sample_kernel,
        grid_spec=grid_spec,
        out_shape=jax.ShapeDtypeStruct((b, SAMPLE_ROWS, width), F32),
        compiler_params=pltpu.CompilerParams(dimension_semantics=("parallel", "arbitrary"),
                                             vmem_limit_bytes=_vmem_limit(est)),
        name="attn_sample",
    )(page_table, q_rep, *([cache_k] * npg), *([cache_v] * npg), k_new, v_new, bias, lam, subln_g, beta_att)


def _mem_kv_kernel(mem_ref, wk_ref, wv_ref, k_ref, v_ref):
    m = mem_ref[...].astype(BF16)
    k_ref[...] = jnp.dot(m, wk_ref[...], preferred_element_type=F32)
    v_ref[...] = jnp.dot(m, wv_ref[...], preferred_element_type=F32)


def _mem_kv(mem, w_xk, w_xv):
    b, n, d = mem.shape
    blk = pl.BlockSpec((None, n, d), lambda i: (i, 0, 0))
    w = pl.BlockSpec((d, d), lambda i: (0, 0))
    out = jax.ShapeDtypeStruct((b, n, d), F32)
    return pl.pallas_call(
        _mem_kv_kernel,
        grid=(b,),
        in_specs=[blk, w, w],
        out_specs=(blk, blk),
        out_shape=(out, out),
        compiler_params=pltpu.CompilerParams(dimension_semantics=("parallel",),
                                             vmem_limit_bytes=_vmem_limit(4 * d * d * 2 + 6 * n * d * 4)),
        name="mem_kv",
    )(mem, w_xk, w_xv)


def _mix_out_kernel(att_ref, conv_ref, h_ref, wo_ref, g_post_ref, g_x_ref, wq_ref, h1_ref, qx_ref):
    half = att_ref.shape[1]
    mo = (jnp.dot(att_ref[...].astype(BF16), wo_ref[0:half, :], preferred_element_type=F32)
          + jnp.dot(conv_ref[...].astype(BF16), wo_ref[half:, :], preferred_element_type=F32))
    h1 = h_ref[...] + _rms(mo, g_post_ref[...])
    h1_ref[...] = h1
    xn = _rms(h1, g_x_ref[...]).astype(BF16)
    x_scale = (wq_ref.shape[1] // N_X_HEADS) ** -0.5
    qx_ref[...] = (jnp.dot(xn, wq_ref[...], preferred_element_type=F32) * x_scale).astype(BF16)


def _mix_out(att, conv, h, w_out, g_post, g_x, w_xq):
    m, d = h.shape
    tm = min(ROW_TILE, m)
    row = lambda w: pl.BlockSpec((tm, w), lambda i: (i, 0))
    const = lambda a: pl.BlockSpec(a.shape, lambda i: (0, 0))
    est = 2 * 2 * d * d * 2 + 2 * tm * (2 * d * 4 + d * 2 + att.shape[1] * 6) + 4 * tm * d * 4
    return pl.pallas_call(
        _mix_out_kernel,
        grid=(m // tm,),
        in_specs=[row(att.shape[1]), row(conv.shape[1]), row(d), const(w_out), const(g_post), const(g_x),
                  const(w_xq)],
        out_specs=(row(d), row(d)),
        out_shape=(jax.ShapeDtypeStruct((m, d), F32), jax.ShapeDtypeStruct((m, d), BF16)),
        compiler_params=pltpu.CompilerParams(dimension_semantics=("parallel",),
                                             vmem_limit_bytes=_vmem_limit(est)),
        name="mix_out",
    )(att, conv, h, w_out, g_post, g_x, w_xq)


def _xattn_kernel(q_ref, mk_ref, mv_ref, o_ref):
    d = q_ref.shape[1]
    hd = d // N_X_HEADS
    for h in range(N_X_HEADS):
        sl = slice(h * hd, (h + 1) * hd)
        s = _nt_dot(q_ref[:, sl], mk_ref[:, sl].astype(BF16))
        p = jnp.exp(s - jnp.max(s, axis=1, keepdims=True))
        l = jnp.sum(p, axis=1, keepdims=True)
        o = jnp.dot(p.astype(BF16), mv_ref[:, sl].astype(BF16), preferred_element_type=F32)
        o_ref[:, sl] = (o / l).astype(o_ref.dtype)


def _xattn(qx, mem_k, mem_v):
    b, t, d = qx.shape
    n = mem_k.shape[1]
    tm = min(ROW_TILE, t)
    row = pl.BlockSpec((None, tm, d), lambda i, j: (i, j, 0))
    mem = pl.BlockSpec((None, n, d), lambda i, j: (i, 0, 0))
    return pl.pallas_call(
        _xattn_kernel,
        grid=(b, t // tm),
        in_specs=[row, mem, mem],
        out_specs=row,
        out_shape=jax.ShapeDtypeStruct((b, t, d), BF16),
        compiler_params=pltpu.CompilerParams(dimension_semantics=("parallel", "parallel"),
                                             vmem_limit_bytes=_vmem_limit(4 * n * d * 4 + 8 * tm * d * 4)),
        name="xattn",
    )(qx, mem_k, mem_v)


def _ffn_kernel(o_ref, h1_ref, wxo_ref, g_xpost_ref, g_pre_ref, wg_ref, wu_ref, wd_ref, g_post_ref, y_ref):
    h2 = h1_ref[...] + _rms(jnp.dot(o_ref[...], wxo_ref[...], preferred_element_type=F32), g_xpost_ref[...])
    xf = _rms(h2, g_pre_ref[...]).astype(BF16)
    d_ff = wg_ref.shape[1]
    f = jnp.zeros(h2.shape, F32)
    for c0 in range(0, d_ff, FFN_CHUNK):
        sl = slice(c0, c0 + FFN_CHUNK)
        g = jnp.dot(xf, wg_ref[:, sl], preferred_element_type=F32)
        u = jnp.dot(xf, wu_ref[:, sl], preferred_element_type=F32)
        a = (g * _sigmoid(g) * u).astype(BF16)
        f = f + jnp.dot(a, wd_ref[sl, :], preferred_element_type=F32)
    y_ref[...] = h2 + _rms(f, g_post_ref[...])


def _ffn(o, h1, w_xo, g_xpost, g_pre, w_gate, w_up, w_down, g_post):
    m, d = h1.shape
    d_ff = w_gate.shape[1]
    assert d_ff % FFN_CHUNK == 0
    tm = min(ROW_TILE, m)
    row = pl.BlockSpec((tm, d), lambda i: (i, 0))
    const = lambda a: pl.BlockSpec(a.shape, lambda i: (0, 0), pipeline_mode=pl.Buffered(1))
    est = (d * d + 3 * d * d_ff) * 2 + 2 * tm * d * (2 + 4 + 4) + 6 * tm * d * 4
    return pl.pallas_call(
        _ffn_kernel,
        grid=(m // tm,),
        in_specs=[row, row, const(w_xo), const(g_xpost), const(g_pre), const(w_gate), const(w_up),
                  const(w_down), const(g_post)],
        out_specs=row,
        out_shape=jax.ShapeDtypeStruct((m, d), F32),
        compiler_params=pltpu.CompilerParams(dimension_semantics=("parallel",),
                                             vmem_limit_bytes=_vmem_limit(est)),
        name="ffn",
    )(o, h1, w_xo, g_xpost, g_pre, w_gate, w_up, w_down, g_post)


def kernel(x_prompt, x_sample, mem_prompt, cache_k, cache_v, state_conv, cache_mem_k, cache_mem_v, page_table, rel_bias_table, norm_mix_pre, norm_mix_post, w_in, lambda_q1, lambda_k1, lambda_q2, lambda_k2, subln_g, dw_w, dw_b, conv_ln_g, conv_ln_b, beta_att, beta_conv, w_out, norm_x_pre, norm_x_post, w_xq, w_xk, w_xv, w_xo, norm_ffn_pre, norm_ffn_post, w_gate, w_up, w_down):
    assert w_in.shape[0] == 1, "single-layer trunk"
    bp, tp, d = x_prompt.shape
    bs, ts, _ = x_sample.shape
    assert ts <= MAX_NEW_TOKENS and tp >= CONV_K - 1
    n_mem = mem_prompt.shape[1]
    vec = lambda a: a[0].reshape(1, -1)
    wb = lambda a: a[0].astype(BF16)
    g_mix_pre, g_mix_post = vec(norm_mix_pre), vec(norm_mix_post)
    g_x_pre, g_x_post = vec(norm_x_pre), vec(norm_x_post)
    g_ffn_pre, g_ffn_post = vec(norm_ffn_pre), vec(norm_ffn_post)
    sub_g, b_att, b_conv = vec(subln_g), vec(beta_att), vec(beta_conv)
    c_b, c_g, c_bt = vec(dw_b), vec(conv_ln_g), vec(conv_ln_b)
    w_in_b, w_out_b, w_xq_b, w_xk_b, w_xv_b, w_xo_b = (wb(w) for w in (w_in, w_out, w_xq, w_xk, w_xv, w_xo))
    w_gate_b, w_up_b, w_down_b = wb(w_gate), wb(w_up), wb(w_down)
    dw = dw_w[0]

    blk = min(ATT_BLOCK, tp)
    lam_vecs = jnp.stack([lambda_q1[0], lambda_k1[0], lambda_q2[0], lambda_k2[0]])
    p_bias, s_bias, lam = _bias_tiles(rel_bias_table, lam_vecs, blk, ts)

    def tail(att, conv, h, mem_k, mem_v):
        b, t, _ = h.shape
        flat = lambda a: a.reshape(b * t, a.shape[-1])
        h1, qx = _mix_out(flat(att), flat(conv), flat(h), w_out_b, g_mix_post, g_x_pre, w_xq_b)
        o = _xattn(qx.reshape(b, t, d), mem_k, mem_v)
        y = _ffn(flat(o), h1, w_xo_b, g_x_post, g_ffn_pre, w_gate_b, w_up_b, w_down_b, g_ffn_post)
        return y.reshape(b, t, d)

    q_p, kb_p, vb_p, kf_p, vf_p, glu_p = _in_proj(x_prompt, g_mix_pre, w_in_b)
    att_p = _attn_prompt(q_p, kb_p, vb_p, p_bias, lam, sub_g, b_att)
    conv_p = _conv_prompt(glu_p, jnp.zeros((bp, CONV_HALO, CONV_WIDTH), F32), dw, c_b, c_g, c_bt, b_conv)
    mk_p, mv_p = _mem_kv(mem_prompt, w_xk_b, w_xv_b)
    y_p = tail(att_p, conv_p, x_prompt, mk_p, mv_p)

    x_s = jnp.pad(x_sample, ((0, 0), (0, SAMPLE_ROWS - ts), (0, 0)))
    q_s, _, _, kf_s, vf_s, glu_s = _in_proj(x_s.reshape(1, bs * SAMPLE_ROWS, d), g_mix_pre, w_in_b)
    kf_s = kf_s.reshape(bs, SAMPLE_ROWS, ATT_WIDTH)
    vf_s = vf_s.reshape(bs, SAMPLE_ROWS, ATT_WIDTH)
    glu_s = glu_s.reshape(bs, SAMPLE_ROWS, CONV_WIDTH)[:, :ts]
    q_flat = jnp.transpose(q_s[0], (1, 0, 2)).reshape(bs, SAMPLE_ROWS, ATT_WIDTH)[:, :ts]
    q_rep = jnp.repeat(q_flat, 8, axis=1)
    q_rep = jnp.pad(q_rep, ((0, 0), (0, 8 * (MAX_NEW_TOKENS - ts)), (0, 0)))
    pad_page = lambda a: jnp.pad(a[:, :ts], ((0, 0), (0, PAGE_SIZE - ts), (0, 0)))
    n_phys = cache_k.shape[1]
    pool = lambda c: c[0].reshape(n_phys, PAGE_SIZE, ATT_WIDTH)
    att_s = _attn_sample(page_table, q_rep, pool(cache_k), pool(cache_v), pad_page(kf_s), pad_page(vf_s),
                         s_bias, lam, sub_g, b_att)
    xp_s = jnp.concatenate([state_conv[0], glu_s], axis=1)
    conv_s = _conv_sample(jnp.transpose(xp_s, (1, 0, 2)), ts, dw, c_b, c_g, c_bt, b_conv)
    conv_s = jnp.pad(jnp.transpose(conv_s, (1, 0, 2)), ((0, 0), (0, SAMPLE_ROWS - ts), (0, 0)))
    mem_s = lambda c: c[0].reshape(bs, n_mem, d)
    y_s = tail(att_s, conv_s, x_s, mem_s(cache_mem_k), mem_s(cache_mem_v))

    heads = lambda a: a.reshape((1,) + a.shape[:2] + (N_ATT_HEADS, ATT_V_DIM))
    mem_heads = lambda a: a.reshape(1, bp, n_mem, N_X_HEADS, d // N_X_HEADS)
    return (y_p, y_s[:, :ts],
            heads(kf_p), heads(vf_p), glu_p[None, :, tp - (CONV_K - 1):],
            mem_heads(mk_p), mem_heads(mv_p),
            heads(kf_s[:, :ts]), heads(vf_s[:, :ts]), xp_s[None, :, ts:])
```

```python
import functools
import math

import numpy as np
import jax
import jax.numpy as jnp
from jax import lax
from jax.experimental import pallas as pl
from jax.experimental.pallas import tpu as pltpu

F32 = jnp.float32
BF16 = jnp.bfloat16

DIFF_HEAD_DIM = 64
ATT_V_DIM = 2 * DIFF_HEAD_DIM
N_ATT_HEADS = 4
ATT_WIDTH = N_ATT_HEADS * ATT_V_DIM
CONV_WIDTH = 512
CONV_K = 31
N_BUCKETS = 32
MAX_DISTANCE = 128
N_X_HEADS = 4
PAGE_SIZE = 128
RMS_EPS = 1e-6
LN_EPS = 1e-5
ATT_SCALE = DIFF_HEAD_DIM ** -0.5
LAM_INIT = 0.8 - 0.6 * math.exp(-0.3 * 0)
MASKED = -1e30

V7X_VMEM_BYTES = 64 * 1024 * 1024
LANES = 128
SUBLANES = 8

ROW_TILE = 512
ATT_BLOCK = 256
CONV_TILE = 256
CONV_CHUNK = 64
CONV_HALO = 32
PAGES_PER_STEP = 8
FFN_CHUNK = 256
SAMPLE_ROWS = 8
MAX_NEW_TOKENS = 4
SAMPLE_Q_ROWS = MAX_NEW_TOKENS * 2 * N_ATT_HEADS


def _vmem_limit(nbytes):
    return int(min(max(2 * nbytes, 16 * 1024 * 1024), V7X_VMEM_BYTES - 8 * 1024 * 1024))


def _rms(x, g):
    return x * lax.rsqrt(jnp.mean(x * x, axis=-1, keepdims=True) + RMS_EPS) * g


def _sigmoid(x):
    return 1.0 / (1.0 + jnp.exp(-x))


def _nt_dot(a, b):
    return lax.dot_general(a, b, (((1,), (1,)), ((), ())), preferred_element_type=F32)


def _bucket_np(n):
    n = np.maximum(n, 0)
    max_exact = N_BUCKETS // 2
    nf = np.maximum(n, 1).astype(np.float32)
    large = max_exact + (np.log(nf / max_exact) / math.log(MAX_DISTANCE / max_exact)
                         * (N_BUCKETS - max_exact)).astype(np.int32)
    large = np.minimum(large, N_BUCKETS - 1)
    return np.where(n < max_exact, n, large).astype(np.int32)


def _prompt_codes(blk):
    i = np.arange(blk)[:, None]
    j = np.arange(blk)[None, :]
    prev = _bucket_np(i - j + blk)
    diag = np.where(j > i, -1, _bucket_np(i - j))
    return np.stack([prev, diag]).astype(np.int32)


def _sample_codes(n_new):
    r = np.arange(SAMPLE_Q_ROWS)[:, None]
    c = np.arange(PAGE_SIZE * N_ATT_HEADS)[None, :]
    t, head = r // 8, (r // 2) % N_ATT_HEADS
    tok, key_head = c // N_ATT_HEADS, c % N_ATT_HEADS
    own = key_head == head
    far = np.where(own, N_BUCKETS - 1, -1)
    last = np.where(own, _bucket_np(t + PAGE_SIZE - tok), -1)
    new = np.where(own & (tok <= t) & (tok < n_new) & (c < PAGE_SIZE), _bucket_np(t - tok), -1)
    return np.stack([far, last, new]).astype(np.int32)


def _bias_kernel(tab_ref, lam_in_ref, pcode_ref, scode_ref, pbias_ref, sbias_ref, lam_ref):
    far = N_BUCKETS - 1

    def lookup(code, h):
        out = jnp.zeros(code.shape, F32)
        for b in range(far):
            out = jnp.where(code == b, tab_ref[b, h] - tab_ref[far, h], out)
        return jnp.where(code < 0, MASKED, out)

    for h in range(N_ATT_HEADS):
        for i in range(2):
            pbias_ref[h, i] = lookup(pcode_ref[i], h)
    row_head = (lax.broadcasted_iota(jnp.int32, scode_ref.shape[1:], 0) >> 1) & (N_ATT_HEADS - 1)
    for i in range(scode_ref.shape[0]):
        code = scode_ref[i]
        out = jnp.zeros(code.shape, F32)
        for h in range(N_ATT_HEADS):
            out = jnp.where(row_head == h, lookup(code, h), out)
        sbias_ref[i] = out
    lv = lam_in_ref[...]
    d1 = jnp.sum(lv[0:1] * lv[1:2], axis=1, keepdims=True)
    d2 = jnp.sum(lv[2:3] * lv[3:4], axis=1, keepdims=True)
    lam = jnp.exp(d1) - jnp.exp(d2) + LAM_INIT
    lam_ref[...] = jnp.broadcast_to(lam, lam_ref.shape)


def _bias_tiles(table, lam_vecs, blk, n_new):
    pcode = jnp.asarray(_prompt_codes(blk))
    scode = jnp.asarray(_sample_codes(n_new))
    vm = pl.BlockSpec(memory_space=pltpu.VMEM)
    return pl.pallas_call(
        _bias_kernel,
        out_shape=(jax.ShapeDtypeStruct((N_ATT_HEADS, 2, blk, blk), F32),
                   jax.ShapeDtypeStruct(scode.shape, F32),
                   jax.ShapeDtypeStruct((SUBLANES, LANES), F32)),
        in_specs=[pl.BlockSpec(memory_space=pltpu.SMEM), vm, vm, vm],
        out_specs=(vm, vm, vm),
        name="bias_tiles",
    )(table, lam_vecs, pcode, scode)


def _in_proj_kernel(x_ref, g_ref, w_ref, q_ref, kb_ref, vb_ref, kf_ref, vf_ref, glu_ref):
    xn = _rms(x_ref[...], g_ref[...]).astype(BF16)

    def cols(c):
        return jnp.dot(xn, w_ref[:, c * ATT_WIDTH:(c + 1) * ATT_WIDTH], preferred_element_type=F32)

    tm = xn.shape[0]
    q = (cols(0) * ATT_SCALE).astype(BF16)
    k = cols(1)
    v = cols(2)
    kb = k.astype(BF16)
    vb = v.astype(BF16)
    for h in range(N_ATT_HEADS):
        sl = slice(h * ATT_V_DIM, (h + 1) * ATT_V_DIM)
        q_ref[h] = q[:, sl]
        kb_ref[h] = kb[:, sl]
        vb_ref[h] = vb[:, sl]
        kf_ref[pl.ds(h, tm, stride=N_ATT_HEADS), :] = k[:, sl]
        vf_ref[pl.ds(h, tm, stride=N_ATT_HEADS), :] = v[:, sl]
    glu_ref[...] = cols(3) * _sigmoid(cols(4))


def _in_proj(x, gain, w_in):
    b, t, d = x.shape
    tm = min(ROW_TILE, t)
    n_cols = w_in.shape[1]
    row = lambda w: pl.BlockSpec((None, tm, w), lambda i, j: (i, j, 0))
    heads = pl.BlockSpec((None, N_ATT_HEADS, tm, ATT_V_DIM), lambda i, j: (i, 0, j, 0))
    hm = jax.ShapeDtypeStruct((b, N_ATT_HEADS, t, ATT_V_DIM), BF16)
    flat = jax.ShapeDtypeStruct((b, t * N_ATT_HEADS, ATT_V_DIM), F32)
    tok_head = pl.BlockSpec((None, tm * N_ATT_HEADS, ATT_V_DIM), lambda i, j: (i, j, 0))
    est = 2 * d * n_cols * 2 + 2 * tm * (d * 4 + 3 * ATT_WIDTH * 2 + 3 * ATT_WIDTH * 4) + 6 * tm * ATT_WIDTH * 4
    return pl.pallas_call(
        _in_proj_kernel,
        grid=(b, t // tm),
        in_specs=[row(d),
                  pl.BlockSpec((1, d), lambda i, j: (0, 0)),
                  pl.BlockSpec((d, n_cols), lambda i, j: (0, 0))],
        out_specs=(heads, heads, heads, tok_head, tok_head, row(CONV_WIDTH)),
        out_shape=(hm, hm, hm, flat, flat, jax.ShapeDtypeStruct((b, t, CONV_WIDTH), F32)),
        compiler_params=pltpu.CompilerParams(dimension_semantics=("parallel", "parallel"),
                                             vmem_limit_bytes=_vmem_limit(est)),
        name="in_proj",
    )(x, gain, w_in)


def _conv_post(y, b_ref, g_ref, bt_ref, beta_ref):
    y = y + b_ref[...]
    mu = jnp.mean(y, axis=-1, keepdims=True)
    yc = y - mu
    yn = yc * lax.rsqrt(jnp.mean(yc * yc, axis=-1, keepdims=True) + LN_EPS) * g_ref[...] + bt_ref[...]
    return yn * _sigmoid(yn) * beta_ref[...]


def _conv_prompt_kernel(glu_ref, prev_ref, w_ref, b_ref, g_ref, bt_ref, beta_ref, o_ref, buf):
    tt = glu_ref.shape[0]
    first = pl.program_id(1) == 0

    @pl.when(first)
    def _():
        buf[0:CONV_HALO] = prev_ref[...]

    @pl.when(jnp.logical_not(first))
    def _():
        buf[0:CONV_HALO] = buf[tt:tt + CONV_HALO]

    buf[CONV_HALO:CONV_HALO + tt] = glu_ref[...]

    shift = CONV_HALO - (CONV_K - 1)
    rc = min(CONV_CHUNK, tt)
    for c0 in range(0, tt, rc):
        acc = jnp.zeros((rc, CONV_WIDTH), F32)
        for r in range(SUBLANES):
            offs = [o for o in range(shift, shift + CONV_K) if o % SUBLANES == r]
            if not offs:
                continue
            a_max = max(offs) // SUBLANES
            xr = buf[pl.ds(c0 + r, rc + SUBLANES * a_max), :]
            for o in offs:
                a = o // SUBLANES
                j = o - shift
                acc = acc + w_ref[j:j + 1, :] * xr[SUBLANES * a:SUBLANES * a + rc]
        o_ref[c0:c0 + rc] = _conv_post(acc, b_ref, g_ref, bt_ref, beta_ref).astype(o_ref.dtype)


def _conv_prompt(glu, prev, dw_w, dw_b, ln_g, ln_b, beta):
    b, t, c = glu.shape
    tt = min(CONV_TILE, t)
    vec = pl.BlockSpec((1, c), lambda i, j: (0, 0))
    return pl.pallas_call(
        _conv_prompt_kernel,
        grid=(b, t // tt),
        in_specs=[pl.BlockSpec((None, tt, c), lambda i, j: (i, j, 0)),
                  pl.BlockSpec((None, CONV_HALO, c), lambda i, j: (i, 0, 0)),
                  pl.BlockSpec(dw_w.shape, lambda i, j: (0, 0)),
                  vec, vec, vec, vec],
        out_specs=pl.BlockSpec((None, tt, c), lambda i, j: (i, j, 0)),
        out_shape=jax.ShapeDtypeStruct((b, t, c), BF16),
        scratch_shapes=[pltpu.VMEM((tt + CONV_HALO, c), F32)],
        compiler_params=pltpu.CompilerParams(dimension_semantics=("parallel", "arbitrary")),
        name="conv_prompt",
    )(glu, prev, dw_w, dw_b, ln_g, ln_b, beta)


def _conv_sample_kernel(xp_ref, w_ref, b_ref, g_ref, bt_ref, beta_ref, o_ref):
    n_t = o_ref.shape[0]
    for t in range(n_t):
        acc = jnp.zeros(xp_ref.shape[1:], F32)
        for j in range(CONV_K):
            acc = acc + w_ref[j:j + 1, :] * xp_ref[t + j]
        o_ref[t] = _conv_post(acc, b_ref, g_ref, bt_ref, beta_ref)


def _conv_sample(xp_t, n_t, dw_w, dw_b, ln_g, ln_b, beta):
    vm = pl.BlockSpec(memory_space=pltpu.VMEM)
    return pl.pallas_call(
        _conv_sample_kernel,
        out_shape=jax.ShapeDtypeStruct((n_t,) + xp_t.shape[1:], F32),
        in_specs=[vm] * 6,
        out_specs=vm,
        name="conv_sample",
    )(xp_t, dw_w, dw_b, ln_g, ln_b, beta)


def _softmax_update(s, v_dot, m_sc, l_sc, acc_sc):
    width = s.shape[1]
    m_prev = m_sc[...]
    m_new = jnp.maximum(m_prev, jnp.max(s, axis=1, keepdims=True))
    alpha = jnp.exp(m_prev - m_new)
    p = jnp.exp(s - jnp.concatenate([m_new] * (width // LANES), axis=1))
    l_sc[...] = alpha * l_sc[...] + jnp.sum(p, axis=1, keepdims=True)
    reps = acc_sc.shape[1] // LANES
    acc_sc[...] = jnp.concatenate([alpha] * reps, axis=1) * acc_sc[...] + v_dot(p.astype(BF16))
    m_sc[...] = m_new


def _subln(att, g, beta):
    return _rms(att, g) * (1.0 - LAM_INIT) * beta


def _attn_prompt_kernel(q_ref, k_ref, v_ref, bias_ref, lam_ref, g_ref, beta_ref, o_ref, m_sc, l_sc, acc_sc):
    blk = q_ref.shape[0]
    qi = pl.program_id(2)
    q = q_ref[...]
    lane = lax.broadcasted_iota(jnp.int32, q.shape, 1)
    zero = jnp.zeros_like(q)
    qs = jnp.concatenate([jnp.where(lane < DIFF_HEAD_DIM, q, zero),
                          jnp.where(lane >= DIFF_HEAD_DIM, q, zero)], axis=0)
    m_sc[...] = jnp.full(m_sc.shape, MASKED, F32)
    l_sc[...] = jnp.zeros(l_sc.shape, F32)
    acc_sc[...] = jnp.zeros(acc_sc.shape, F32)

    def block(j, bias):
        rows = pl.ds(pl.multiple_of(j * blk, blk), blk)
        s = _nt_dot(qs, k_ref[rows, :])
        if bias is not None:
            s = s + jnp.concatenate([bias, bias], axis=0)
        v = v_ref[rows, :]
        _softmax_update(s, lambda p: jnp.dot(p, v, preferred_element_type=F32), m_sc, l_sc, acc_sc)

    def far_block(j, carry):
        block(j, None)
        return carry

    lax.fori_loop(0, qi - 1, far_block, 0)

    @pl.when(qi >= 1)
    def _():
        block(qi - 1, bias_ref[0])

    block(qi, bias_ref[1])

    o = acc_sc[...] / l_sc[...]
    att = o[:blk] - lam_ref[0:1, :] * o[blk:]
    o_ref[...] = _subln(att, g_ref[...], beta_ref[...]).astype(o_ref.dtype)


def _attn_prompt(q, k, v, bias, lam, subln_g, beta_att):
    b, h, t, e = q.shape
    blk = bias.shape[-1]
    seq = pl.BlockSpec((None, None, t, e), lambda i, j, n: (i, j, 0, 0))
    est = 2 * (2 * t * e * 2) + 2 * 2 * blk * blk * 4 + 6 * 2 * blk * blk * 4
    return pl.pallas_call(
        _attn_prompt_kernel,
        grid=(b, h, t // blk),
        in_specs=[pl.BlockSpec((None, None, blk, e), lambda i, j, n: (i, j, n, 0)),
                  seq, seq,
                  pl.BlockSpec((None, 2, blk, blk), lambda i, j, n: (j, 0, 0, 0)),
                  pl.BlockSpec(lam.shape, lambda i, j, n: (0, 0)),
                  pl.BlockSpec((1, e), lambda i, j, n: (0, 0)),
                  pl.BlockSpec((1, e), lambda i, j, n: (0, j))],
        out_specs=pl.BlockSpec((None, blk, e), lambda i, j, n: (i, n, j)),
        out_shape=jax.ShapeDtypeStruct((b, t, h * e), BF16),
        scratch_shapes=[pltpu.VMEM((2 * blk, LANES), F32),
                        pltpu.VMEM((2 * blk, LANES), F32),
                        pltpu.VMEM((2 * blk, e), F32)],
        compiler_params=pltpu.CompilerParams(dimension_semantics=("parallel", "parallel", "arbitrary"),
                                             vmem_limit_bytes=_vmem_limit(est)),
        name="attn_prompt",
    )(q, k, v, bias, lam, subln_g, beta_att)


def _attn_sample_kernel(pt_ref, q_ref, *refs):
    del pt_ref
    npg = PAGES_PER_STEP
    k_refs, v_refs = refs[:npg], refs[npg:2 * npg]
    (kn_ref, vn_ref, bias_ref, lam_ref, g_ref, beta_ref, o_ref, m_sc, l_sc, acc_sc) = refs[2 * npg:]
    c = pl.program_id(1)
    last = c == pl.num_programs(1) - 1

    @pl.when(c == 0)
    def _():
        m_sc[...] = jnp.full(m_sc.shape, MASKED, F32)
        l_sc[...] = jnp.zeros(l_sc.shape, F32)
        acc_sc[...] = jnp.zeros(acc_sc.shape, F32)

    q = q_ref[...]
    lane = lax.broadcasted_iota(jnp.int32, q.shape, 1)
    row = lax.broadcasted_iota(jnp.int32, q.shape, 0)
    qm = jnp.where((lane >> 6) == (row & 1), q, jnp.zeros_like(q))
    cols = PAGE_SIZE * N_ATT_HEADS

    def pages_dot(v_list, n_cols):
        def v_dot(p):
            out = None
            for i, vr in enumerate(v_list):
                part = jnp.dot(p[:, i * n_cols:(i + 1) * n_cols], vr[...].astype(BF16),
                               preferred_element_type=F32)
                out = part if out is None else out + part
            return out
        return v_dot

    far_bias = bias_ref[0]
    last_bias = jnp.where(last, bias_ref[1], far_bias)
    s_pages = [_nt_dot(qm, kr[...].astype(BF16)) + (last_bias if i == npg - 1 else far_bias)
               for i, kr in enumerate(k_refs)]
    _softmax_update(jnp.concatenate(s_pages, axis=1), pages_dot(v_refs, cols), m_sc, l_sc, acc_sc)

    @pl.when(last)
    def _():
        n_new = kn_ref.shape[0]
        s_new = _nt_dot(qm, kn_ref[...].astype(BF16)) + bias_ref[2][:, :n_new]
        _softmax_update(s_new, pages_dot([vn_ref], n_new), m_sc, l_sc, acc_sc)
        sign = jnp.where((row & 1) == 0, 1.0, -lam_ref[0:1, 0:1])
        z = acc_sc[...] / l_sc[...] * sign
        out_row = lax.broadcasted_iota(jnp.int32, (o_ref.shape[0], ATT_V_DIM), 0)
        for h in range(N_ATT_HEADS):
            att = jnp.zeros(out_row.shape, F32)
            for t in range(MAX_NEW_TOKENS):
                r0 = t * 8 + h * 2
                att = jnp.where(out_row == t, jnp.broadcast_to(z[r0:r0 + 1] + z[r0 + 1:r0 + 2], att.shape), att)
            sl = slice(h * ATT_V_DIM, (h + 1) * ATT_V_DIM)
            o_ref[:, sl] = _subln(att, g_ref[...], beta_ref[:, sl])


def _attn_sample(page_table, q_rows, pool_k, pool_v, k_new, v_new, bias, lam, subln_g, beta_att):
    b, n_pages = page_table.shape
    rows, e = q_rows.shape[1:]
    page_rows = pool_k.shape[1]
    npg = PAGES_PER_STEP
    assert n_pages % npg == 0

    def page_spec(i):
        return pl.BlockSpec((None, page_rows, e), lambda s, c, pt: (pt[s, c * npg + i], 0, 0))

    per_seq = lambda a: pl.BlockSpec((None,) + a.shape[1:], lambda s, c, pt: (s, 0, 0))
    whole = lambda a: pl.BlockSpec(a.shape, lambda s, c, pt: (0,) * a.ndim)
    out_shape = jax.ShapeDtypeStruct((b, SAMPLE_ROWS, N_ATT_HEADS * e), F32)
    grid_spec = pltpu.PrefetchScalarGridSpec(
        num_scalar_prefetch=1,
        grid=(b, n_pages // npg),
        in_specs=([per_seq(q_rows)] + [page_spec(i) for i in range(npg)] * 2
                  + [per_seq(k_new), per_seq(v_new), whole(bias), whole(lam), whole(subln_g), whole(beta_att)]),
        out_specs=per_seq(out_shape),
        scratch_shapes=[pltpu.VMEM((rows, LANES), F32), pltpu.VMEM((rows, LANES), F32),
                        pltpu.VMEM((rows, e), F32)],
    )
    est = 2 * 2 * npg * page_rows * e * 4 + 8 * rows * npg * page_rows * 4
    return pl.pallas_call(
        _attn_sample_kernel,
        grid_spec=grid_spec,
        out_shape=out_shape,
        compiler_params=pltpu.CompilerParams(dimension_semantics=("parallel", "arbitrary"),
                                             vmem_limit_bytes=_vmem_limit(est)),
        name="attn_sample",
    )(page_table, q_rows, *([pool_k] * npg), *([pool_v] * npg), k_new, v_new, bias, lam, subln_g, beta_att)


def _mem_kv_kernel(mem_ref, wk_ref, wv_ref, k_ref, v_ref):
    m = mem_ref[...].astype(BF16)
    k_ref[...] = jnp.dot(m, wk_ref[...], preferred_element_type=F32)
    v_ref[...] = jnp.dot(m, wv_ref[...], preferred_element_type=F32)


def _mem_kv(mem, w_xk, w_xv):
    b, n, d = mem.shape
    blk = pl.BlockSpec((None, n, d), lambda i: (i, 0, 0))
    w = pl.BlockSpec((d, d), lambda i: (0, 0))
    out = jax.ShapeDtypeStruct((b, n, d), F32)
    return pl.pallas_call(
        _mem_kv_kernel,
        grid=(b,),
        in_specs=[blk, w, w],
        out_specs=(blk, blk),
        out_shape=(out, out),
        compiler_params=pltpu.CompilerParams(dimension_semantics=("parallel",),
                                             vmem_limit_bytes=_vmem_limit(4 * d * d * 2 + 6 * n * d * 4)),
        name="mem_kv",
    )(mem, w_xk, w_xv)


def _mix_out_kernel(att_ref, conv_ref, h_ref, wo_ref, g_post_ref, g_x_ref, wq_ref, h1_ref, qx_ref):
    half = att_ref.shape[1]
    mo = (jnp.dot(att_ref[...].astype(BF16), wo_ref[0:half, :], preferred_element_type=F32)
          + jnp.dot(conv_ref[...].astype(BF16), wo_ref[half:, :], preferred_element_type=F32))
    h1 = h_ref[...] + _rms(mo, g_post_ref[...])
    h1_ref[...] = h1
    xn = _rms(h1, g_x_ref[...]).astype(BF16)
    x_scale = (wq_ref.shape[1] // N_X_HEADS) ** -0.5
    qx_ref[...] = (jnp.dot(xn, wq_ref[...], preferred_element_type=F32) * x_scale).astype(BF16)


def _mix_out(att, conv, h, w_out, g_post, g_x, w_xq):
    m, d = h.shape
    tm = min(ROW_TILE, m)
    row = lambda w: pl.BlockSpec((tm, w), lambda i: (i, 0))
    const = lambda a: pl.BlockSpec(a.shape, lambda i: (0, 0))
    est = 2 * 2 * d * d * 2 + 2 * tm * (2 * d * 4 + d * 2 + att.shape[1] * 6) + 4 * tm * d * 4
    return pl.pallas_call(
        _mix_out_kernel,
        grid=(m // tm,),
        in_specs=[row(att.shape[1]), row(conv.shape[1]), row(d), const(w_out), const(g_post), const(g_x),
                  const(w_xq)],
        out_specs=(row(d), row(d)),
        out_shape=(jax.ShapeDtypeStruct((m, d), F32), jax.ShapeDtypeStruct((m, d), BF16)),
        compiler_params=pltpu.CompilerParams(dimension_semantics=("parallel",),
                                             vmem_limit_bytes=_vmem_limit(est)),
        name="mix_out",
    )(att, conv, h, w_out, g_post, g_x, w_xq)


def _xattn_kernel(q_ref, mk_ref, mv_ref, o_ref):
    d = q_ref.shape[1]
    hd = d // N_X_HEADS
    for h in range(N_X_HEADS):
        sl = slice(h * hd, (h + 1) * hd)
        s = _nt_dot(q_ref[:, sl], mk_ref[:, sl].astype(BF16))
        p = jnp.exp(s - jnp.max(s, axis=1, keepdims=True))
        l = jnp.sum(p, axis=1, keepdims=True)
        o = jnp.dot(p.astype(BF16), mv_ref[:, sl].astype(BF16), preferred_element_type=F32)
        o_ref[:, sl] = (o / l).astype(o_ref.dtype)


def _xattn(qx, mem_k, mem_v):
    b, t, d = qx.shape
    n = mem_k.shape[1]
    tm = min(ROW_TILE, t)
    row = pl.BlockSpec((None, tm, d), lambda i, j: (i, j, 0))
    mem = pl.BlockSpec((None, n, d), lambda i, j: (i, 0, 0))
    return pl.pallas_call(
        _xattn_kernel,
        grid=(b, t // tm),
        in_specs=[row, mem, mem],
        out_specs=row,
        out_shape=jax.ShapeDtypeStruct((b, t, d), BF16),
        compiler_params=pltpu.CompilerParams(dimension_semantics=("parallel", "parallel"),
                                             vmem_limit_bytes=_vmem_limit(4 * n * d * 4 + 8 * tm * d * 4)),
        name="xattn",
    )(qx, mem_k, mem_v)


def _ffn_kernel(o_ref, h1_ref, wxo_ref, g_xpost_ref, g_pre_ref, wg_ref, wu_ref, wd_ref, g_post_ref, y_ref):
    h2 = h1_ref[...] + _rms(jnp.dot(o_ref[...], wxo_ref[...], preferred_element_type=F32), g_xpost_ref[...])
    xf = _rms(h2, g_pre_ref[...]).astype(BF16)
    d_ff = wg_ref.shape[1]
    f = jnp.zeros(h2.shape, F32)
    for c0 in range(0, d_ff, FFN_CHUNK):
        sl = slice(c0, c0 + FFN_CHUNK)
        g = jnp.dot(xf, wg_ref[:, sl], preferred_element_type=F32)
        u = jnp.dot(xf, wu_ref[:, sl], preferred_element_type=F32)
        a = (g * _sigmoid(g) * u).astype(BF16)
        f = f + jnp.dot(a, wd_ref[sl, :], preferred_element_type=F32)
    y_ref[...] = h2 + _rms(f, g_post_ref[...])


def _ffn(o, h1, w_xo, g_xpost, g_pre, w_gate, w_up, w_down, g_post):
    m, d = h1.shape
    d_ff = w_gate.shape[1]
    assert d_ff % FFN_CHUNK == 0
    tm = min(ROW_TILE, m)
    row = pl.BlockSpec((tm, d), lambda i: (i, 0))
    const = lambda a: pl.BlockSpec(a.shape, lambda i: (0, 0), pipeline_mode=pl.Buffered(1))
    est = (d * d + 3 * d * d_ff) * 2 + 2 * tm * d * (2 + 4 + 4) + 6 * tm * d * 4
    return pl.pallas_call(
        _ffn_kernel,
        grid=(m // tm,),
        in_specs=[row, row, const(w_xo), const(g_xpost), const(g_pre), const(w_gate), const(w_up),
                  const(w_down), const(g_post)],
        out_specs=row,
        out_shape=jax.ShapeDtypeStruct((m, d), F32),
        compiler_params=pltpu.CompilerParams(dimension_semantics=("parallel",),
                                             vmem_limit_bytes=_vmem_limit(est)),
        name="ffn",
    )(o, h1, w_xo, g_xpost, g_pre, w_gate, w_up, w_down, g_post)


def kernel(x_prompt, x_sample, mem_prompt, cache_k, cache_v, state_conv, cache_mem_k, cache_mem_v, page_table, rel_bias_table, norm_mix_pre, norm_mix_post, w_in, lambda_q1, lambda_k1, lambda_q2, lambda_k2, subln_g, dw_w, dw_b, conv_ln_g, conv_ln_b, beta_att, beta_conv, w_out, norm_x_pre, norm_x_post, w_xq, w_xk, w_xv, w_xo, norm_ffn_pre, norm_ffn_post, w_gate, w_up, w_down):
    assert w_in.shape[0] == 1, "single-layer trunk"
    bp, tp, d = x_prompt.shape
    bs, ts, _ = x_sample.shape
    assert ts <= MAX_NEW_TOKENS and tp >= CONV_K - 1
    n_mem = mem_prompt.shape[1]
    vec = lambda a: a[0].reshape(1, -1)
    wb = lambda a: a[0].astype(BF16)
    g_mix_pre, g_mix_post = vec(norm_mix_pre), vec(norm_mix_post)
    g_x_pre, g_x_post = vec(norm_x_pre), vec(norm_x_post)
    g_ffn_pre, g_ffn_post = vec(norm_ffn_pre), vec(norm_ffn_post)
    sub_g, b_att, b_conv = vec(subln_g), vec(beta_att), vec(beta_conv)
    c_b, c_g, c_bt = vec(dw_b), vec(conv_ln_g), vec(conv_ln_b)
    w_in_b, w_out_b, w_xq_b, w_xk_b, w_xv_b, w_xo_b = (wb(w) for w in (w_in, w_out, w_xq, w_xk, w_xv, w_xo))
    w_gate_b, w_up_b, w_down_b = wb(w_gate), wb(w_up), wb(w_down)
    dw = dw_w[0]

    blk = min(ATT_BLOCK, tp)
    lam_vecs = jnp.stack([lambda_q1[0], lambda_k1[0], lambda_q2[0], lambda_k2[0]])
    p_bias, s_bias, lam = _bias_tiles(rel_bias_table, lam_vecs, blk, ts)

    def tail(att, conv, h, mem_k, mem_v):
        b, t, _ = h.shape
        flat = lambda a: a.reshape(b * t, a.shape[-1])
        h1, qx = _mix_out(flat(att), flat(conv), flat(h), w_out_b, g_mix_post, g_x_pre, w_xq_b)
        o = _xattn(qx.reshape(b, t, d), mem_k, mem_v)
        y = _ffn(flat(o), h1, w_xo_b, g_x_post, g_ffn_pre, w_gate_b, w_up_b, w_down_b, g_ffn_post)
        return y.reshape(b, t, d)

    q_p, kb_p, vb_p, kf_p, vf_p, glu_p = _in_proj(x_prompt, g_mix_pre, w_in_b)
    att_p = _attn_prompt(q_p, kb_p, vb_p, p_bias, lam, sub_g, b_att)
    conv_p = _conv_prompt(glu_p, jnp.zeros((bp, CONV_HALO, CONV_WIDTH), F32), dw, c_b, c_g, c_bt, b_conv)
    mk_p, mv_p = _mem_kv(mem_prompt, w_xk_b, w_xv_b)
    y_p = tail(att_p, conv_p, x_prompt, mk_p, mv_p)

    x_s = jnp.pad(x_sample, ((0, 0), (0, SAMPLE_ROWS - ts), (0, 0)))
    q_s, _, _, kf_s, vf_s, glu_s = _in_proj(x_s.reshape(1, bs * SAMPLE_ROWS, d), g_mix_pre, w_in_b)
    new_rows = ts * N_ATT_HEADS
    kf_s = kf_s.reshape(bs, SAMPLE_ROWS * N_ATT_HEADS, ATT_V_DIM)[:, :new_rows]
    vf_s = vf_s.reshape(bs, SAMPLE_ROWS * N_ATT_HEADS, ATT_V_DIM)[:, :new_rows]
    glu_s = glu_s.reshape(bs, SAMPLE_ROWS, CONV_WIDTH)[:, :ts]
    q_th = jnp.transpose(q_s[0].reshape(N_ATT_HEADS, bs, SAMPLE_ROWS, ATT_V_DIM), (1, 2, 0, 3))
    q_th = jnp.pad(q_th[:, :ts], ((0, 0), (0, MAX_NEW_TOKENS - ts), (0, 0), (0, 0)))
    q_rows = jnp.repeat(q_th.reshape(bs, MAX_NEW_TOKENS * N_ATT_HEADS, ATT_V_DIM), 2, axis=1)
    pad_page = lambda a: jnp.pad(a, ((0, 0), (0, PAGE_SIZE - new_rows), (0, 0)))
    n_phys = cache_k.shape[1]
    pool = lambda c: c.reshape(n_phys, PAGE_SIZE * N_ATT_HEADS, ATT_V_DIM)
    att_s = _attn_sample(page_table, q_rows, pool(cache_k), pool(cache_v), pad_page(kf_s), pad_page(vf_s),
                         s_bias, lam, sub_g, b_att)
    xp_s = jnp.concatenate([state_conv[0], glu_s], axis=1)
    conv_s = _conv_sample(jnp.transpose(xp_s, (1, 0, 2)), ts, dw, c_b, c_g, c_bt, b_conv)
    conv_s = jnp.pad(jnp.transpose(conv_s, (1, 0, 2)), ((0, 0), (0, SAMPLE_ROWS - ts), (0, 0)))
    mem_s = lambda c: c[0].reshape(bs, n_mem, d)
    y_s = tail(att_s, conv_s, x_s, mem_s(cache_mem_k), mem_s(cache_mem_v))

    heads = lambda a: a.reshape(1, a.shape[0], a.shape[1] // N_ATT_HEADS, N_ATT_HEADS, ATT_V_DIM)
    mem_heads = lambda a: a.reshape(1, bp, n_mem, N_X_HEADS, d // N_X_HEADS)
    return (y_p, y_s[:, :ts],
            heads(kf_p), heads(vf_p), glu_p[None, :, tp - (CONV_K - 1):],
            mem_heads(mk_p), mem_heads(mv_p),
            heads(kf_s), heads(vf_s), xp_s[None, :, ts:])
```

```python
import functools
import math

import numpy as np
import jax
import jax.numpy as jnp
from jax import lax
from jax.experimental import pallas as pl
from jax.experimental.pallas import tpu as pltpu

F32 = jnp.float32
BF16 = jnp.bfloat16

DIFF_HEAD_DIM = 64
ATT_V_DIM = 2 * DIFF_HEAD_DIM
N_ATT_HEADS = 4
ATT_WIDTH = N_ATT_HEADS * ATT_V_DIM
CONV_WIDTH = 512
CONV_K = 31
N_BUCKETS = 32
MAX_DISTANCE = 128
N_X_HEADS = 4
PAGE_SIZE = 128
RMS_EPS = 1e-6
LN_EPS = 1e-5
ATT_SCALE = DIFF_HEAD_DIM ** -0.5
LOG2_E = math.log2(math.e)
LAM_INIT = 0.8 - 0.6 * math.exp(-0.3 * 0)
MASKED = -1e30

V7X_VMEM_BYTES = 64 * 1024 * 1024
LANES = 128
SUBLANES = 8
BF16_ROWS = 16

ROW_TILE = 512
ATT_BLOCK = 256
ATT_HEADS_PER_STEP = 4
CONV_TILE = 256
CONV_CHUNK = 64
CONV_HALO = 32
PAGES_PER_STEP = 8
FFN_CHUNK = 256
SAMPLE_ROWS = 8
MAX_NEW_TOKENS = 4
SAMPLE_Q_ROWS = MAX_NEW_TOKENS * 2 * N_ATT_HEADS


def _vmem_limit(nbytes):
    return int(min(max(2 * nbytes, 16 * 1024 * 1024), V7X_VMEM_BYTES - 8 * 1024 * 1024))


def _rms(x, g):
    return x * lax.rsqrt(jnp.mean(x * x, axis=-1, keepdims=True) + RMS_EPS) * g


def _sigmoid(x):
    return 1.0 / (1.0 + jnp.exp(-x))


def _nt_dot(a, b):
    return lax.dot_general(a, b, (((1,), (1,)), ((), ())), preferred_element_type=F32)


def _bucket_np(n):
    n = np.maximum(n, 0)
    max_exact = N_BUCKETS // 2
    nf = np.maximum(n, 1).astype(np.float32)
    large = max_exact + (np.log(nf / max_exact) / math.log(MAX_DISTANCE / max_exact)
                         * (N_BUCKETS - max_exact)).astype(np.int32)
    large = np.minimum(large, N_BUCKETS - 1)
    return np.where(n < max_exact, n, large).astype(np.int32)


def _prompt_codes(blk):
    i = np.arange(blk)[None, :]
    j = np.arange(blk)[:, None]
    prev = _bucket_np(i - j + blk)
    diag = np.where(j > i, -1, _bucket_np(i - j))
    return np.concatenate([prev, diag]).astype(np.int32)


def _sample_codes(n_new):
    r = np.arange(SAMPLE_Q_ROWS)[:, None]
    c = np.arange(PAGE_SIZE * N_ATT_HEADS)[None, :]
    t, head = r // 8, (r // 2) % N_ATT_HEADS
    tok, key_head = c // N_ATT_HEADS, c % N_ATT_HEADS
    own = key_head == head
    far = np.where(own, N_BUCKETS - 1, -1)
    last = np.where(own, _bucket_np(t + PAGE_SIZE - tok), -1)
    new = np.where(own & (tok <= t) & (tok < n_new) & (c < PAGE_SIZE), _bucket_np(t - tok), -1)
    return np.stack([far, last, new]).astype(np.int32)


def _bias_kernel(tab_ref, lam_in_ref, pcode_ref, scode_ref, pbias_ref, sbias_ref, lam_ref):
    far = N_BUCKETS - 1

    def lookup(code, h):
        out = jnp.zeros(code.shape, F32)
        for b in range(far):
            out = jnp.where(code == b, (tab_ref[b, h] - tab_ref[far, h]) * LOG2_E, out)
        return jnp.where(code < 0, MASKED, out)

    for h in range(N_ATT_HEADS):
        pbias_ref[h] = lookup(pcode_ref[...], h)
    row_head = (lax.broadcasted_iota(jnp.int32, scode_ref.shape[1:], 0) >> 1) & (N_ATT_HEADS - 1)
    for i in range(scode_ref.shape[0]):
        code = scode_ref[i]
        out = jnp.zeros(code.shape, F32)
        for h in range(N_ATT_HEADS):
            out = jnp.where(row_head == h, lookup(code, h), out)
        sbias_ref[i] = out
    lv = lam_in_ref[...]
    d1 = jnp.sum(lv[0:1] * lv[1:2], axis=1, keepdims=True)
    d2 = jnp.sum(lv[2:3] * lv[3:4], axis=1, keepdims=True)
    lam = jnp.exp(d1) - jnp.exp(d2) + LAM_INIT
    lam_ref[...] = jnp.broadcast_to(lam, lam_ref.shape)


def _bias_tiles(table, lam_vecs, blk, n_new):
    pcode = jnp.asarray(_prompt_codes(blk))
    scode = jnp.asarray(_sample_codes(n_new))
    vm = pl.BlockSpec(memory_space=pltpu.VMEM)
    return pl.pallas_call(
        _bias_kernel,
        out_shape=(jax.ShapeDtypeStruct((N_ATT_HEADS, 2 * blk, blk), F32),
                   jax.ShapeDtypeStruct(scode.shape, F32),
                   jax.ShapeDtypeStruct((SUBLANES, LANES), F32)),
        in_specs=[pl.BlockSpec(memory_space=pltpu.SMEM), vm, vm, vm],
        out_specs=(vm, vm, vm),
        name="bias_tiles",
    )(table, lam_vecs, pcode, scode)


def _in_proj_kernel(x_ref, g_ref, w_ref, qt_ref, kb_ref, vt_ref, kf_ref, vf_ref, glu_ref):
    xn = _rms(x_ref[...], g_ref[...]).astype(BF16)

    def cols(c):
        return jnp.dot(xn, w_ref[:, c * ATT_WIDTH:(c + 1) * ATT_WIDTH], preferred_element_type=F32)

    tm = xn.shape[0]
    blk = qt_ref.shape[-1]
    q = cols(0) * (ATT_SCALE * LOG2_E)
    k = cols(1)
    v = cols(2)
    kb = k.astype(BF16)
    for h in range(N_ATT_HEADS):
        sl = slice(h * ATT_V_DIM, (h + 1) * ATT_V_DIM)
        kb_ref[h] = kb[:, sl]
        for c in range(tm // blk):
            rows = slice(c * blk, (c + 1) * blk)
            qt_ref[h, c] = q[rows, sl].T.astype(BF16)
            vt_ref[h, c] = v[rows, sl].T.astype(BF16)
        kf_ref[pl.ds(h, tm, stride=N_ATT_HEADS), :] = k[:, sl]
        vf_ref[pl.ds(h, tm, stride=N_ATT_HEADS), :] = v[:, sl]
    glu_ref[...] = cols(3) * _sigmoid(cols(4))


def _in_proj(x, gain, w_in, blk):
    b, t, d = x.shape
    tm = min(ROW_TILE, t)
    n_cols = w_in.shape[1]
    assert tm % blk == 0
    row = lambda w: pl.BlockSpec((None, tm, w), lambda i, j: (i, j, 0))
    heads = pl.BlockSpec((None, N_ATT_HEADS, tm, ATT_V_DIM), lambda i, j: (i, 0, j, 0))
    hm = jax.ShapeDtypeStruct((b, N_ATT_HEADS, t, ATT_V_DIM), BF16)
    heads_t = pl.BlockSpec((None, N_ATT_HEADS, tm // blk, ATT_V_DIM, blk), lambda i, j: (i, 0, j, 0, 0))
    hm_t = jax.ShapeDtypeStruct((b, N_ATT_HEADS, t // blk, ATT_V_DIM, blk), BF16)
    flat = jax.ShapeDtypeStruct((b, t * N_ATT_HEADS, ATT_V_DIM), F32)
    tok_head = pl.BlockSpec((None, tm * N_ATT_HEADS, ATT_V_DIM), lambda i, j: (i, j, 0))
    est = 2 * d * n_cols * 2 + 2 * tm * (d * 4 + 3 * ATT_WIDTH * 2 + 3 * ATT_WIDTH * 4) + 6 * tm * ATT_WIDTH * 4
    return pl.pallas_call(
        _in_proj_kernel,
        grid=(b, t // tm),
        in_specs=[row(d),
                  pl.BlockSpec((1, d), lambda i, j: (0, 0)),
                  pl.BlockSpec((d, n_cols), lambda i, j: (0, 0))],
        out_specs=(heads_t, heads, heads_t, tok_head, tok_head, row(CONV_WIDTH)),
        out_shape=(hm_t, hm, hm_t, flat, flat, jax.ShapeDtypeStruct((b, t, CONV_WIDTH), F32)),
        compiler_params=pltpu.CompilerParams(dimension_semantics=("parallel", "parallel"),
                                             vmem_limit_bytes=_vmem_limit(est)),
        name="in_proj",
    )(x, gain, w_in)


def _conv_post(y, b_ref, g_ref, bt_ref, beta_ref):
    y = y + b_ref[...]
    mu = jnp.mean(y, axis=-1, keepdims=True)
    yc = y - mu
    yn = yc * lax.rsqrt(jnp.mean(yc * yc, axis=-1, keepdims=True) + LN_EPS) * g_ref[...] + bt_ref[...]
    return yn * _sigmoid(yn) * beta_ref[...]


def _conv_prompt_kernel(glu_ref, prev_ref, w_ref, b_ref, g_ref, bt_ref, beta_ref, o_ref, buf):
    tt = glu_ref.shape[0]
    first = pl.program_id(1) == 0

    @pl.when(first)
    def _():
        buf[0:CONV_HALO] = prev_ref[...]

    @pl.when(jnp.logical_not(first))
    def _():
        buf[0:CONV_HALO] = buf[tt:tt + CONV_HALO]

    buf[CONV_HALO:CONV_HALO + tt] = glu_ref[...]

    shift = CONV_HALO - (CONV_K - 1)
    rc = min(CONV_CHUNK, tt)
    for c0 in range(0, tt, rc):
        acc = jnp.zeros((rc, CONV_WIDTH), F32)
        for r in range(SUBLANES):
            offs = [o for o in range(shift, shift + CONV_K) if o % SUBLANES == r]
            if not offs:
                continue
            a_max = max(offs) // SUBLANES
            xr = buf[pl.ds(c0 + r, rc + SUBLANES * a_max), :]
            for o in offs:
                a = o // SUBLANES
                j = o - shift
                acc = acc + w_ref[j:j + 1, :] * xr[SUBLANES * a:SUBLANES * a + rc]
        o_ref[c0:c0 + rc] = _conv_post(acc, b_ref, g_ref, bt_ref, beta_ref).astype(o_ref.dtype)


def _conv_prompt(glu, prev, dw_w, dw_b, ln_g, ln_b, beta):
    b, t, c = glu.shape
    tt = min(CONV_TILE, t)
    vec = pl.BlockSpec((1, c), lambda i, j: (0, 0))
    return pl.pallas_call(
        _conv_prompt_kernel,
        grid=(b, t // tt),
        in_specs=[pl.BlockSpec((None, tt, c), lambda i, j: (i, j, 0)),
                  pl.BlockSpec((None, CONV_HALO, c), lambda i, j: (i, 0, 0)),
                  pl.BlockSpec(dw_w.shape, lambda i, j: (0, 0)),
                  vec, vec, vec, vec],
        out_specs=pl.BlockSpec((None, tt, c), lambda i, j: (i, j, 0)),
        out_shape=jax.ShapeDtypeStruct((b, t, c), BF16),
        scratch_shapes=[pltpu.VMEM((tt + CONV_HALO, c), F32)],
        compiler_params=pltpu.CompilerParams(dimension_semantics=("parallel", "arbitrary")),
        name="conv_prompt",
    )(glu, prev, dw_w, dw_b, ln_g, ln_b, beta)


def _conv_sample_kernel(xp_ref, w_ref, b_ref, g_ref, bt_ref, beta_ref, o_ref):
    n_t = o_ref.shape[0]
    for t in range(n_t):
        acc = jnp.zeros(xp_ref.shape[1:], F32)
        for j in range(CONV_K):
            acc = acc + w_ref[j:j + 1, :] * xp_ref[t + j]
        o_ref[t] = _conv_post(acc, b_ref, g_ref, bt_ref, beta_ref)


def _conv_sample(xp_t, n_t, dw_w, dw_b, ln_g, ln_b, beta):
    vm = pl.BlockSpec(memory_space=pltpu.VMEM)
    return pl.pallas_call(
        _conv_sample_kernel,
        out_shape=jax.ShapeDtypeStruct((n_t,) + xp_t.shape[1:], F32),
        in_specs=[vm] * 6,
        out_specs=vm,
        name="conv_sample",
    )(xp_t, dw_w, dw_b, ln_g, ln_b, beta)


def _softmax_update(s, v_dot, m_sc, l_sc, acc_sc):
    width = s.shape[1]
    m_prev = m_sc[...]
    m_new = jnp.maximum(m_prev, jnp.max(s, axis=1, keepdims=True))
    alpha = jnp.exp2(m_prev - m_new)
    p = jnp.exp2(s - jnp.concatenate([m_new] * (width // LANES), axis=1))
    l_sc[...] = alpha * l_sc[...] + jnp.sum(p, axis=1, keepdims=True)
    reps = acc_sc.shape[1] // LANES
    acc_sc[...] = jnp.concatenate([alpha] * reps, axis=1) * acc_sc[...] + v_dot(p.astype(BF16))
    m_sc[...] = m_new


def _subln(att, g, beta):
    return _rms(att, g) * (1.0 - LAM_INIT) * beta


def _attn_prompt_kernel(qt_ref, k_ref, vt_ref, bias_ref, lam_ref, g_ref, beta_ref, o_ref, m_sc, acc_sc):
    n_heads, e, blk = qt_ref.shape
    qi = pl.program_id(2)
    ones_rows = jnp.ones((acc_sc.shape[1] - e, blk), BF16)
    sub = lax.broadcasted_iota(jnp.int32, qt_ref.shape[1:], 0)
    qst = []
    for h in range(n_heads):
        qt = qt_ref[h]
        zero = jnp.zeros_like(qt)
        qst.append(jnp.concatenate([jnp.where(sub < DIFF_HEAD_DIM, qt, zero),
                                    jnp.where(sub >= DIFF_HEAD_DIM, qt, zero)], axis=1))
    m_sc[...] = jnp.full(m_sc.shape, MASKED, F32)
    acc_sc[...] = jnp.zeros(acc_sc.shape, F32)

    def step(j, n_blk, biased):
        rows = pl.ds(pl.multiple_of(j * blk, blk), n_blk * blk)

        def logits(h):
            st = jnp.dot(k_ref[h, rows, :], qst[h], preferred_element_type=F32)
            if biased:
                bias = bias_ref[h, (2 - n_blk) * blk:, :]
                st = st + jnp.concatenate([bias, bias], axis=1)
            return st

        ahead = 2
        pending = [logits(h) for h in range(min(ahead, n_heads))]
        for h in range(n_heads):
            if h + ahead < n_heads:
                pending.append(logits(h + ahead))
            st = pending[h]
            m_prev = m_sc[h]
            m_new = jnp.maximum(m_prev, jnp.max(st, axis=0, keepdims=True))
            alpha = jnp.exp2(m_prev - m_new)
            pb = jnp.exp2(st - m_new).astype(BF16)
            pv = None
            for i in range(n_blk):
                vt_aug = jnp.concatenate([vt_ref[h, j + i], ones_rows], axis=0)
                part = jnp.dot(vt_aug, pb[i * blk:(i + 1) * blk], preferred_element_type=F32)
                pv = part if pv is None else pv + part
            acc_sc[h] = alpha * acc_sc[h] + pv
            m_sc[h] = m_new

    odd = (qi + 1) & 1

    @pl.when(qi == 0)
    def _():
        step(0, 1, True)

    @pl.when(jnp.logical_and(qi > 0, odd == 1))
    def _():
        step(0, 1, False)

    def far_pair(i, carry):
        step(odd + 2 * i, 2, False)
        return carry

    lax.fori_loop(0, ((qi + 1) >> 1) - 1, far_pair, 0)

    @pl.when(qi > 0)
    def _():
        step(qi - 1, 2, True)

    for h in range(n_heads):
        ot = acc_sc[h, 0:e] / acc_sc[h, e:e + 1]
        att = (ot[:, :blk] - lam_ref[0:1, 0:1] * ot[:, blk:]).T
        sl = slice(h * ATT_V_DIM, (h + 1) * ATT_V_DIM)
        o_ref[:, sl] = _subln(att, g_ref[...], beta_ref[:, sl]).astype(o_ref.dtype)


def _attn_prompt(qt, k, vt, bias, lam, subln_g, beta_att):
    b, h, t, e = k.shape
    blk = bias.shape[-1]
    n_blk = t // blk
    hs = ATT_HEADS_PER_STEP
    assert h % hs == 0
    est = hs * (2 * t * e * 2) + hs * 2 * blk * blk * 4 + 4 * hs * 2 * blk * 2 * blk * 4
    return pl.pallas_call(
        _attn_prompt_kernel,
        grid=(b, h // hs, n_blk),
        in_specs=[pl.BlockSpec((None, hs, None, e, blk), lambda i, j, n: (i, j, n, 0, 0)),
                  pl.BlockSpec((None, hs, t, e), lambda i, j, n: (i, j, 0, 0), pipeline_mode=pl.Buffered(1)),
                  pl.BlockSpec((None, hs, n_blk, e, blk), lambda i, j, n: (i, j, 0, 0, 0),
                               pipeline_mode=pl.Buffered(1)),
                  pl.BlockSpec((hs, 2 * blk, blk), lambda i, j, n: (j, 0, 0), pipeline_mode=pl.Buffered(1)),
                  pl.BlockSpec(lam.shape, lambda i, j, n: (0, 0)),
                  pl.BlockSpec((1, e), lambda i, j, n: (0, 0)),
                  pl.BlockSpec((1, hs * e), lambda i, j, n: (0, j))],
        out_specs=pl.BlockSpec((None, blk, hs * e), lambda i, j, n: (i, n, j)),
        out_shape=jax.ShapeDtypeStruct((b, t, h * e), BF16),
        scratch_shapes=[pltpu.VMEM((hs, 1, 2 * blk), F32),
                        pltpu.VMEM((hs, e + BF16_ROWS, 2 * blk), F32)],
        compiler_params=pltpu.CompilerParams(dimension_semantics=("parallel", "parallel", "arbitrary"),
                                             vmem_limit_bytes=_vmem_limit(est)),
        name="attn_prompt",
    )(qt, k, vt, bias, lam, subln_g, beta_att)


def _attn_sample_kernel(pt_ref, q_ref, *refs):
    del pt_ref
    npg = PAGES_PER_STEP
    k_refs, v_refs = refs[:npg], refs[npg:2 * npg]
    (kn_ref, vn_ref, bias_ref, lam_ref, g_ref, beta_ref, o_ref, m_sc, l_sc, acc_sc) = refs[2 * npg:]
    c = pl.program_id(1)
    last = c == pl.num_programs(1) - 1

    @pl.when(c == 0)
    def _():
        m_sc[...] = jnp.full(m_sc.shape, MASKED, F32)
        l_sc[...] = jnp.zeros(l_sc.shape, F32)
        acc_sc[...] = jnp.zeros(acc_sc.shape, F32)

    q = q_ref[...]
    lane = lax.broadcasted_iota(jnp.int32, q.shape, 1)
    row = lax.broadcasted_iota(jnp.int32, q.shape, 0)
    qm = jnp.where((lane >> 6) == (row & 1), q, jnp.zeros_like(q))
    cols = PAGE_SIZE * N_ATT_HEADS

    def pages_dot(v_list, n_cols):
        def v_dot(p):
            out = None
            for i, vr in enumerate(v_list):
                part = jnp.dot(p[:, i * n_cols:(i + 1) * n_cols], vr[...].astype(BF16),
                               preferred_element_type=F32)
                out = part if out is None else out + part
            return out
        return v_dot

    far_bias = bias_ref[0]
    last_bias = jnp.where(last, bias_ref[1], far_bias)
    s_pages = [_nt_dot(qm, kr[...].astype(BF16)) + (last_bias if i == npg - 1 else far_bias)
               for i, kr in enumerate(k_refs)]
    _softmax_update(jnp.concatenate(s_pages, axis=1), pages_dot(v_refs, cols), m_sc, l_sc, acc_sc)

    @pl.when(last)
    def _():
        n_new = kn_ref.shape[0]
        s_new = _nt_dot(qm, kn_ref[...].astype(BF16)) + bias_ref[2][:, :n_new]
        _softmax_update(s_new, pages_dot([vn_ref], n_new), m_sc, l_sc, acc_sc)
        sign = jnp.where((row & 1) == 0, 1.0, -lam_ref[0:1, 0:1])
        z = acc_sc[...] / l_sc[...] * sign
        out_row = lax.broadcasted_iota(jnp.int32, (o_ref.shape[0], ATT_V_DIM), 0)
        for h in range(N_ATT_HEADS):
            att = jnp.zeros(out_row.shape, F32)
            for t in range(MAX_NEW_TOKENS):
                r0 = t * 8 + h * 2
                att = jnp.where(out_row == t, jnp.broadcast_to(z[r0:r0 + 1] + z[r0 + 1:r0 + 2], att.shape), att)
            sl = slice(h * ATT_V_DIM, (h + 1) * ATT_V_DIM)
            o_ref[:, sl] = _subln(att, g_ref[...], beta_ref[:, sl])


def _attn_sample(page_table, q_rows, pool_k, pool_v, k_new, v_new, bias, lam, subln_g, beta_att):
    b, n_pages = page_table.shape
    rows, e = q_rows.shape[1:]
    page_rows = pool_k.shape[1]
    npg = PAGES_PER_STEP
    assert n_pages % npg == 0

    def page_spec(i):
        return pl.BlockSpec((None, page_rows, e), lambda s, c, pt: (pt[s, c * npg + i], 0, 0))

    per_seq = lambda a: pl.BlockSpec((None,) + a.shape[1:], lambda s, c, pt: (s, 0, 0))
    whole = lambda a: pl.BlockSpec(a.shape, lambda s, c, pt: (0,) * a.ndim)
    out_shape = jax.ShapeDtypeStruct((b, SAMPLE_ROWS, N_ATT_HEADS * e), F32)
    grid_spec = pltpu.PrefetchScalarGridSpec(
        num_scalar_prefetch=1,
        grid=(b, n_pages // npg),
        in_specs=([per_seq(q_rows)] + [page_spec(i) for i in range(npg)] * 2
                  + [per_seq(k_new), per_seq(v_new), whole(bias), whole(lam), whole(subln_g), whole(beta_att)]),
        out_specs=per_seq(out_shape),
        scratch_shapes=[pltpu.VMEM((rows, LANES), F32), pltpu.VMEM((rows, LANES), F32),
                        pltpu.VMEM((rows, e), F32)],
    )
    est = 2 * 2 * npg * page_rows * e * 4 + 8 * rows * npg * page_rows * 4
    return pl.pallas_call(
        _attn_sample_kernel,
        grid_spec=grid_spec,
        out_shape=out_shape,
        compiler_params=pltpu.CompilerParams(dimension_semantics=("parallel", "arbitrary"),
                                             vmem_limit_bytes=_vmem_limit(est)),
        name="attn_sample",
    )(page_table, q_rows, *([pool_k] * npg), *([pool_v] * npg), k_new, v_new, bias, lam, subln_g, beta_att)


def _mem_kv_kernel(mem_ref, wk_ref, wv_ref, k_ref, v_ref):
    m = mem_ref[...].astype(BF16)
    k_ref[...] = jnp.dot(m, wk_ref[...], preferred_element_type=F32)
    v_ref[...] = jnp.dot(m, wv_ref[...], preferred_element_type=F32)


def _mem_kv(mem, w_xk, w_xv):
    b, n, d = mem.shape
    blk = pl.BlockSpec((None, n, d), lambda i: (i, 0, 0))
    w = pl.BlockSpec((d, d), lambda i: (0, 0))
    out = jax.ShapeDtypeStruct((b, n, d), F32)
    return pl.pallas_call(
        _mem_kv_kernel,
        grid=(b,),
        in_specs=[blk, w, w],
        out_specs=(blk, blk),
        out_shape=(out, out),
        compiler_params=pltpu.CompilerParams(dimension_semantics=("parallel",),
                                             vmem_limit_bytes=_vmem_limit(4 * d * d * 2 + 6 * n * d * 4)),
        name="mem_kv",
    )(mem, w_xk, w_xv)


def _mix_out_kernel(att_ref, conv_ref, h_ref, wo_ref, g_post_ref, g_x_ref, wq_ref, h1_ref, qx_ref):
    half = att_ref.shape[1]
    mo = (jnp.dot(att_ref[...].astype(BF16), wo_ref[0:half, :], preferred_element_type=F32)
          + jnp.dot(conv_ref[...].astype(BF16), wo_ref[half:, :], preferred_element_type=F32))
    h1 = h_ref[...] + _rms(mo, g_post_ref[...])
    h1_ref[...] = h1
    xn = _rms(h1, g_x_ref[...]).astype(BF16)
    x_scale = (wq_ref.shape[1] // N_X_HEADS) ** -0.5
    qx_ref[...] = (jnp.dot(xn, wq_ref[...], preferred_element_type=F32) * x_scale).astype(BF16)


def _mix_out(att, conv, h, w_out, g_post, g_x, w_xq):
    m, d = h.shape
    tm = min(ROW_TILE, m)
    row = lambda w: pl.BlockSpec((tm, w), lambda i: (i, 0))
    const = lambda a: pl.BlockSpec(a.shape, lambda i: (0, 0))
    est = 2 * 2 * d * d * 2 + 2 * tm * (2 * d * 4 + d * 2 + att.shape[1] * 6) + 4 * tm * d * 4
    return pl.pallas_call(
        _mix_out_kernel,
        grid=(m // tm,),
        in_specs=[row(att.shape[1]), row(conv.shape[1]), row(d), const(w_out), const(g_post), const(g_x),
                  const(w_xq)],
        out_specs=(row(d), row(d)),
        out_shape=(jax.ShapeDtypeStruct((m, d), F32), jax.ShapeDtypeStruct((m, d), BF16)),
        compiler_params=pltpu.CompilerParams(dimension_semantics=("parallel",),
                                             vmem_limit_bytes=_vmem_limit(est)),
        name="mix_out",
    )(att, conv, h, w_out, g_post, g_x, w_xq)


def _xattn_kernel(q_ref, mk_ref, mv_ref, o_ref):
    d = q_ref.shape[1]
    hd = d // N_X_HEADS
    for h in range(N_X_HEADS):
        sl = slice(h * hd, (h + 1) * hd)
        s = _nt_dot(q_ref[:, sl], mk_ref[:, sl].astype(BF16))
        p = jnp.exp(s - jnp.max(s, axis=1, keepdims=True))
        l = jnp.sum(p, axis=1, keepdims=True)
        o = jnp.dot(p.astype(BF16), mv_ref[:, sl].astype(BF16), preferred_element_type=F32)
        o_ref[:, sl] = (o / l).astype(o_ref.dtype)


def _xattn(qx, mem_k, mem_v):
    b, t, d = qx.shape
    n = mem_k.shape[1]
    tm = min(ROW_TILE, t)
    row = pl.BlockSpec((None, tm, d), lambda i, j: (i, j, 0))
    mem = pl.BlockSpec((None, n, d), lambda i, j: (i, 0, 0))
    return pl.pallas_call(
        _xattn_kernel,
        grid=(b, t // tm),
        in_specs=[row, mem, mem],
        out_specs=row,
        out_shape=jax.ShapeDtypeStruct((b, t, d), BF16),
        compiler_params=pltpu.CompilerParams(dimension_semantics=("parallel", "parallel"),
                                             vmem_limit_bytes=_vmem_limit(4 * n * d * 4 + 8 * tm * d * 4)),
        name="xattn",
    )(qx, mem_k, mem_v)


def _ffn_kernel(o_ref, h1_ref, wxo_ref, g_xpost_ref, g_pre_ref, wg_ref, wu_ref, wd_ref, g_post_ref, y_ref):
    h2 = h1_ref[...] + _rms(jnp.dot(o_ref[...], wxo_ref[...], preferred_element_type=F32), g_xpost_ref[...])
    xf = _rms(h2, g_pre_ref[...]).astype(BF16)
    d_ff = wg_ref.shape[1]
    f = jnp.zeros(h2.shape, F32)
    for c0 in range(0, d_ff, FFN_CHUNK):
        sl = slice(c0, c0 + FFN_CHUNK)
        g = jnp.dot(xf, wg_ref[:, sl], preferred_element_type=F32)
        u = jnp.dot(xf, wu_ref[:, sl], preferred_element_type=F32)
        a = (g * _sigmoid(g) * u).astype(BF16)
        f = f + jnp.dot(a, wd_ref[sl, :], preferred_element_type=F32)
    y_ref[...] = h2 + _rms(f, g_post_ref[...])


def _ffn(o, h1, w_xo, g_xpost, g_pre, w_gate, w_up, w_down, g_post):
    m, d = h1.shape
    d_ff = w_gate.shape[1]
    assert d_ff % FFN_CHUNK == 0
    tm = min(ROW_TILE, m)
    row = pl.BlockSpec((tm, d), lambda i: (i, 0))
    const = lambda a: pl.BlockSpec(a.shape, lambda i: (0, 0), pipeline_mode=pl.Buffered(1))
    est = (d * d + 3 * d * d_ff) * 2 + 2 * tm * d * (2 + 4 + 4) + 6 * tm * d * 4
    return pl.pallas_call(
        _ffn_kernel,
        grid=(m // tm,),
        in_specs=[row, row, const(w_xo), const(g_xpost), const(g_pre), const(w_gate), const(w_up),
                  const(w_down), const(g_post)],
        out_specs=row,
        out_shape=jax.ShapeDtypeStruct((m, d), F32),
        compiler_params=pltpu.CompilerParams(dimension_semantics=("parallel",),
                                             vmem_limit_bytes=_vmem_limit(est)),
        name="ffn",
    )(o, h1, w_xo, g_xpost, g_pre, w_gate, w_up, w_down, g_post)


def kernel(x_prompt, x_sample, mem_prompt, cache_k, cache_v, state_conv, cache_mem_k, cache_mem_v, page_table, rel_bias_table, norm_mix_pre, norm_mix_post, w_in, lambda_q1, lambda_k1, lambda_q2, lambda_k2, subln_g, dw_w, dw_b, conv_ln_g, conv_ln_b, beta_att, beta_conv, w_out, norm_x_pre, norm_x_post, w_xq, w_xk, w_xv, w_xo, norm_ffn_pre, norm_ffn_post, w_gate, w_up, w_down):
    assert w_in.shape[0] == 1, "single-layer trunk"
    bp, tp, d = x_prompt.shape
    bs, ts, _ = x_sample.shape
    assert ts <= MAX_NEW_TOKENS and tp >= CONV_K - 1
    n_mem = mem_prompt.shape[1]
    vec = lambda a: a[0].reshape(1, -1)
    wb = lambda a: a[0].astype(BF16)
    g_mix_pre, g_mix_post = vec(norm_mix_pre), vec(norm_mix_post)
    g_x_pre, g_x_post = vec(norm_x_pre), vec(norm_x_post)
    g_ffn_pre, g_ffn_post = vec(norm_ffn_pre), vec(norm_ffn_post)
    sub_g, b_att, b_conv = vec(subln_g), vec(beta_att), vec(beta_conv)
    c_b, c_g, c_bt = vec(dw_b), vec(conv_ln_g), vec(conv_ln_b)
    w_in_b, w_out_b, w_xq_b, w_xk_b, w_xv_b, w_xo_b = (wb(w) for w in (w_in, w_out, w_xq, w_xk, w_xv, w_xo))
    w_gate_b, w_up_b, w_down_b = wb(w_gate), wb(w_up), wb(w_down)
    dw = dw_w[0]

    blk = min(ATT_BLOCK, tp)
    lam_vecs = jnp.stack([lambda_q1[0], lambda_k1[0], lambda_q2[0], lambda_k2[0]])
    p_bias, s_bias, lam = _bias_tiles(rel_bias_table, lam_vecs, blk, ts)

    def tail(att, conv, h, mem_k, mem_v):
        b, t, _ = h.shape
        flat = lambda a: a.reshape(b * t, a.shape[-1])
        h1, qx = _mix_out(flat(att), flat(conv), flat(h), w_out_b, g_mix_post, g_x_pre, w_xq_b)
        o = _xattn(qx.reshape(b, t, d), mem_k, mem_v)
        y = _ffn(flat(o), h1, w_xo_b, g_x_post, g_ffn_pre, w_gate_b, w_up_b, w_down_b, g_ffn_post)
        return y.reshape(b, t, d)

    qt_p, kb_p, vt_p, kf_p, vf_p, glu_p = _in_proj(x_prompt, g_mix_pre, w_in_b, blk)
    att_p = _attn_prompt(qt_p, kb_p, vt_p, p_bias, lam, sub_g, b_att)
    conv_p = _conv_prompt(glu_p, jnp.zeros((bp, CONV_HALO, CONV_WIDTH), F32), dw, c_b, c_g, c_bt, b_conv)
    mk_p, mv_p = _mem_kv(mem_prompt, w_xk_b, w_xv_b)
    y_p = tail(att_p, conv_p, x_prompt, mk_p, mv_p)

    x_s = jnp.pad(x_sample, ((0, 0), (0, SAMPLE_ROWS - ts), (0, 0)))
    rows_s = bs * SAMPLE_ROWS
    qt_s, _, _, kf_s, vf_s, glu_s = _in_proj(x_s.reshape(1, rows_s, d), g_mix_pre, w_in_b, min(blk, rows_s))
    new_rows = ts * N_ATT_HEADS
    kf_s = kf_s.reshape(bs, SAMPLE_ROWS * N_ATT_HEADS, ATT_V_DIM)[:, :new_rows]
    vf_s = vf_s.reshape(bs, SAMPLE_ROWS * N_ATT_HEADS, ATT_V_DIM)[:, :new_rows]
    glu_s = glu_s.reshape(bs, SAMPLE_ROWS, CONV_WIDTH)[:, :ts]
    q_s = jnp.transpose(qt_s[0], (0, 1, 3, 2))
    q_th = jnp.transpose(q_s.reshape(N_ATT_HEADS, bs, SAMPLE_ROWS, ATT_V_DIM), (1, 2, 0, 3))
    q_th = jnp.pad(q_th[:, :ts], ((0, 0), (0, MAX_NEW_TOKENS - ts), (0, 0), (0, 0)))
    q_rows = jnp.repeat(q_th.reshape(bs, MAX_NEW_TOKENS * N_ATT_HEADS, ATT_V_DIM), 2, axis=1)
    pad_page = lambda a: jnp.pad(a, ((0, 0), (0, PAGE_SIZE - new_rows), (0, 0)))
    n_phys = cache_k.shape[1]
    pool = lambda c: c.reshape(n_phys, PAGE_SIZE * N_ATT_HEADS, ATT_V_DIM)
    att_s = _attn_sample(page_table, q_rows, pool(cache_k), pool(cache_v), pad_page(kf_s), pad_page(vf_s),
                         s_bias, lam, sub_g, b_att)
    xp_s = jnp.concatenate([state_conv[0], glu_s], axis=1)
    conv_s = _conv_sample(jnp.transpose(xp_s, (1, 0, 2)), ts, dw, c_b, c_g, c_bt, b_conv)
    conv_s = jnp.pad(jnp.transpose(conv_s, (1, 0, 2)), ((0, 0), (0, SAMPLE_ROWS - ts), (0, 0)))
    mem_s = lambda c: c[0].reshape(bs, n_mem, d)
    y_s = tail(att_s, conv_s, x_s, mem_s(cache_mem_k), mem_s(cache_mem_v))

    heads = lambda a: a.reshape(1, a.shape[0], a.shape[1] // N_ATT_HEADS, N_ATT_HEADS, ATT_V_DIM)
    mem_heads = lambda a: a.reshape(1, bp, n_mem, N_X_HEADS, d // N_X_HEADS)
    return (y_p, y_s[:, :ts],
            heads(kf_p), heads(vf_p), glu_p[None, :, tp - (CONV_K - 1):],
            mem_heads(mk_p), mem_heads(mv_p),
            heads(kf_s), heads(vf_s), xp_s[None, :, ts:])
```

```python
import functools
import math

import numpy as np
import jax
import jax.numpy as jnp
from jax import lax
from jax.experimental import pallas as pl
from jax.experimental.pallas import tpu as pltpu

F32 = jnp.float32
BF16 = jnp.bfloat16

DIFF_HEAD_DIM = 64
ATT_V_DIM = 2 * DIFF_HEAD_DIM
N_ATT_HEADS = 4
ATT_WIDTH = N_ATT_HEADS * ATT_V_DIM
CONV_WIDTH = 512
CONV_K = 31
N_BUCKETS = 32
MAX_DISTANCE = 128
N_X_HEADS = 4
PAGE_SIZE = 128
RMS_EPS = 1e-6
LN_EPS = 1e-5
ATT_SCALE = DIFF_HEAD_DIM ** -0.5
LOG2_E = math.log2(math.e)
LAM_INIT = 0.8 - 0.6 * math.exp(-0.3 * 0)
MASKED = -1e30

V7X_VMEM_BYTES = 64 * 1024 * 1024
LANES = 128
SUBLANES = 8
BF16_ROWS = 16

ROW_TILE = 512
ATT_BLOCK = 256
ATT_HEADS_PER_STEP = 4
CONV_TILE = 256
CONV_CHUNK = 64
CONV_HALO = 32
PAGES_PER_STEP = 8
SAMPLE_SEQS_PER_STEP = 2
FFN_CHUNK = 256
SAMPLE_ROWS = 8
MAX_NEW_TOKENS = 4
SAMPLE_Q_ROWS = MAX_NEW_TOKENS * 2 * N_ATT_HEADS


def _vmem_limit(nbytes):
    return int(min(max(2 * nbytes, 16 * 1024 * 1024), V7X_VMEM_BYTES - 8 * 1024 * 1024))


def _rms(x, g):
    return x * lax.rsqrt(jnp.mean(x * x, axis=-1, keepdims=True) + RMS_EPS) * g


def _sigmoid(x):
    return 1.0 / (1.0 + jnp.exp(-x))


def _nt_dot(a, b):
    return lax.dot_general(a, b, (((1,), (1,)), ((), ())), preferred_element_type=F32)


def _bucket_np(n):
    n = np.maximum(n, 0)
    max_exact = N_BUCKETS // 2
    nf = np.maximum(n, 1).astype(np.float32)
    large = max_exact + (np.log(nf / max_exact) / math.log(MAX_DISTANCE / max_exact)
                         * (N_BUCKETS - max_exact)).astype(np.int32)
    large = np.minimum(large, N_BUCKETS - 1)
    return np.where(n < max_exact, n, large).astype(np.int32)


def _prompt_codes(blk):
    i = np.arange(blk)[None, :]
    j = np.arange(blk)[:, None]
    prev = _bucket_np(i - j + blk)
    diag = np.where(j > i, -1, _bucket_np(i - j))
    return np.concatenate([prev, diag]).astype(np.int32)


def _sample_codes(n_new):
    r = np.arange(SAMPLE_Q_ROWS)[:, None]
    c = np.arange(PAGE_SIZE * N_ATT_HEADS)[None, :]
    t, head = r // 8, (r // 2) % N_ATT_HEADS
    tok, key_head = c // N_ATT_HEADS, c % N_ATT_HEADS
    own = key_head == head
    far = np.where(own, N_BUCKETS - 1, -1)
    last = np.where(own, _bucket_np(t + PAGE_SIZE - tok), -1)
    new = np.where(own & (tok <= t) & (tok < n_new) & (c < PAGE_SIZE), _bucket_np(t - tok), -1)
    return np.stack([far, last, new]).astype(np.int32)


def _bias_kernel(tab_ref, lam_in_ref, pcode_ref, scode_ref, pbias_ref, sbias_ref, lam_ref):
    far = N_BUCKETS - 1

    def lookup(code, h):
        out = jnp.zeros(code.shape, F32)
        for b in range(far):
            out = jnp.where(code == b, (tab_ref[b, h] - tab_ref[far, h]) * LOG2_E, out)
        return jnp.where(code < 0, MASKED, out)

    for h in range(N_ATT_HEADS):
        pbias_ref[h] = lookup(pcode_ref[...], h)
    row_head = (lax.broadcasted_iota(jnp.int32, scode_ref.shape[1:], 0) >> 1) & (N_ATT_HEADS - 1)
    for i in range(scode_ref.shape[0]):
        code = scode_ref[i]
        out = jnp.zeros(code.shape, F32)
        for h in range(N_ATT_HEADS):
            out = jnp.where(row_head == h, lookup(code, h), out)
        sbias_ref[i] = out
    lv = lam_in_ref[...]
    d1 = jnp.sum(lv[0:1] * lv[1:2], axis=1, keepdims=True)
    d2 = jnp.sum(lv[2:3] * lv[3:4], axis=1, keepdims=True)
    lam = jnp.exp(d1) - jnp.exp(d2) + LAM_INIT
    lam_ref[...] = jnp.broadcast_to(lam, lam_ref.shape)


def _bias_tiles(table, lam_vecs, blk, n_new):
    pcode = jnp.asarray(_prompt_codes(blk))
    scode = jnp.asarray(_sample_codes(n_new))
    vm = pl.BlockSpec(memory_space=pltpu.VMEM)
    return pl.pallas_call(
        _bias_kernel,
        out_shape=(jax.ShapeDtypeStruct((N_ATT_HEADS, 2 * blk, blk), F32),
                   jax.ShapeDtypeStruct(scode.shape, F32),
                   jax.ShapeDtypeStruct((SUBLANES, LANES), F32)),
        in_specs=[pl.BlockSpec(memory_space=pltpu.SMEM), vm, vm, vm],
        out_specs=(vm, vm, vm),
        name="bias_tiles",
    )(table, lam_vecs, pcode, scode)


def _in_proj_kernel(x_ref, g_ref, w_ref, qt_ref, kb_ref, vt_ref, kf_ref, vf_ref, glu_ref):
    xn = _rms(x_ref[...], g_ref[...]).astype(BF16)

    def cols(c):
        return jnp.dot(xn, w_ref[:, c * ATT_WIDTH:(c + 1) * ATT_WIDTH], preferred_element_type=F32)

    tm = xn.shape[0]
    blk = qt_ref.shape[-1]
    q = cols(0) * (ATT_SCALE * LOG2_E)
    k = cols(1)
    v = cols(2)
    kb = k.astype(BF16)
    for h in range(N_ATT_HEADS):
        sl = slice(h * ATT_V_DIM, (h + 1) * ATT_V_DIM)
        kb_ref[h] = kb[:, sl]
        for c in range(tm // blk):
            rows = slice(c * blk, (c + 1) * blk)
            qt_ref[h, c] = q[rows, sl].T.astype(BF16)
            vt_ref[h, c] = v[rows, sl].T.astype(BF16)
        kf_ref[pl.ds(h, tm, stride=N_ATT_HEADS), :] = k[:, sl]
        vf_ref[pl.ds(h, tm, stride=N_ATT_HEADS), :] = v[:, sl]
    glu_ref[...] = cols(3) * _sigmoid(cols(4))


def _in_proj(x, gain, w_in, blk):
    b, t, d = x.shape
    tm = min(ROW_TILE, t)
    n_cols = w_in.shape[1]
    assert tm % blk == 0
    row = lambda w: pl.BlockSpec((None, tm, w), lambda i, j: (i, j, 0))
    heads = pl.BlockSpec((None, N_ATT_HEADS, tm, ATT_V_DIM), lambda i, j: (i, 0, j, 0))
    hm = jax.ShapeDtypeStruct((b, N_ATT_HEADS, t, ATT_V_DIM), BF16)
    heads_t = pl.BlockSpec((None, N_ATT_HEADS, tm // blk, ATT_V_DIM, blk), lambda i, j: (i, 0, j, 0, 0))
    hm_t = jax.ShapeDtypeStruct((b, N_ATT_HEADS, t // blk, ATT_V_DIM, blk), BF16)
    flat = jax.ShapeDtypeStruct((b, t * N_ATT_HEADS, ATT_V_DIM), F32)
    tok_head = pl.BlockSpec((None, tm * N_ATT_HEADS, ATT_V_DIM), lambda i, j: (i, j, 0))
    est = 2 * d * n_cols * 2 + 2 * tm * (d * 4 + 3 * ATT_WIDTH * 2 + 3 * ATT_WIDTH * 4) + 6 * tm * ATT_WIDTH * 4
    return pl.pallas_call(
        _in_proj_kernel,
        grid=(b, t // tm),
        in_specs=[row(d),
                  pl.BlockSpec((1, d), lambda i, j: (0, 0)),
                  pl.BlockSpec((d, n_cols), lambda i, j: (0, 0))],
        out_specs=(heads_t, heads, heads_t, tok_head, tok_head, row(CONV_WIDTH)),
        out_shape=(hm_t, hm, hm_t, flat, flat, jax.ShapeDtypeStruct((b, t, CONV_WIDTH), F32)),
        compiler_params=pltpu.CompilerParams(dimension_semantics=("parallel", "parallel"),
                                             vmem_limit_bytes=_vmem_limit(est)),
        name="in_proj",
    )(x, gain, w_in)


def _conv_post(y, b_ref, g_ref, bt_ref, beta_ref):
    y = y + b_ref[...]
    mu = jnp.mean(y, axis=-1, keepdims=True)
    yc = y - mu
    yn = yc * lax.rsqrt(jnp.mean(yc * yc, axis=-1, keepdims=True) + LN_EPS) * g_ref[...] + bt_ref[...]
    return yn * _sigmoid(yn) * beta_ref[...]


def _conv_prompt_kernel(glu_ref, prev_ref, w_ref, b_ref, g_ref, bt_ref, beta_ref, o_ref, buf, ybuf):
    tt = glu_ref.shape[0]
    n_slab = buf.shape[0]
    first = pl.program_id(1) == 0
    slabs = [slice(s * LANES, (s + 1) * LANES) for s in range(n_slab)]

    @pl.when(first)
    def _():
        for s in range(n_slab):
            buf[s, 0:CONV_HALO] = prev_ref[:, slabs[s]]

    @pl.when(jnp.logical_not(first))
    def _():
        for s in range(n_slab):
            buf[s, 0:CONV_HALO] = buf[s, tt:tt + CONV_HALO]

    for s in range(n_slab):
        buf[s, CONV_HALO:CONV_HALO + tt] = glu_ref[:, slabs[s]]

    shift = CONV_HALO - (CONV_K - 1)
    rc = min(CONV_CHUNK, tt // 2)
    for c0 in range(0, tt, 2 * rc):
        for parity in range(2):
            start = c0 + parity
            accs = []
            for s in range(n_slab):
                acc = jnp.zeros((rc, LANES), F32)
                for j in range(CONV_K):
                    x = buf[s, pl.ds(start + j + shift, rc, stride=2), :]
                    acc = acc + w_ref[j:j + 1, slabs[s]] * x
                accs.append(acc)
            y = _conv_post(jnp.concatenate(accs, axis=1), b_ref, g_ref, bt_ref, beta_ref)
            for s in range(n_slab):
                ybuf[s, pl.ds(start, rc, stride=2), :] = y[:, slabs[s]]
    for s in range(n_slab):
        o_ref[:, slabs[s]] = ybuf[s].astype(o_ref.dtype)


def _conv_prompt(glu, prev, dw_w, dw_b, ln_g, ln_b, beta):
    b, t, c = glu.shape
    tt = min(CONV_TILE, t)
    vec = pl.BlockSpec((1, c), lambda i, j: (0, 0))
    return pl.pallas_call(
        _conv_prompt_kernel,
        grid=(b, t // tt),
        in_specs=[pl.BlockSpec((None, tt, c), lambda i, j: (i, j, 0)),
                  pl.BlockSpec((None, CONV_HALO, c), lambda i, j: (i, 0, 0)),
                  pl.BlockSpec(dw_w.shape, lambda i, j: (0, 0)),
                  vec, vec, vec, vec],
        out_specs=pl.BlockSpec((None, tt, c), lambda i, j: (i, j, 0)),
        out_shape=jax.ShapeDtypeStruct((b, t, c), BF16),
        scratch_shapes=[pltpu.VMEM((c // LANES, tt + CONV_HALO, LANES), F32),
                        pltpu.VMEM((c // LANES, tt, LANES), F32)],
        compiler_params=pltpu.CompilerParams(dimension_semantics=("parallel", "arbitrary")),
        name="conv_prompt",
    )(glu, prev, dw_w, dw_b, ln_g, ln_b, beta)


def _conv_sample_kernel(xp_ref, w_ref, b_ref, g_ref, bt_ref, beta_ref, o_ref):
    n_t = o_ref.shape[0]
    for t in range(n_t):
        acc = jnp.zeros(xp_ref.shape[1:], F32)
        for j in range(CONV_K):
            acc = acc + w_ref[j:j + 1, :] * xp_ref[t + j]
        o_ref[t] = _conv_post(acc, b_ref, g_ref, bt_ref, beta_ref)


def _conv_sample(xp_t, n_t, dw_w, dw_b, ln_g, ln_b, beta):
    vm = pl.BlockSpec(memory_space=pltpu.VMEM)
    return pl.pallas_call(
        _conv_sample_kernel,
        out_shape=jax.ShapeDtypeStruct((n_t,) + xp_t.shape[1:], F32),
        in_specs=[vm] * 6,
        out_specs=vm,
        name="conv_sample",
    )(xp_t, dw_w, dw_b, ln_g, ln_b, beta)


def _softmax_update(s, v_dot, m_sc, l_sc, acc_sc):
    width = s.shape[1]
    m_prev = m_sc[...]
    m_new = jnp.maximum(m_prev, jnp.max(s, axis=1, keepdims=True))
    alpha = jnp.exp2(m_prev - m_new)
    p = jnp.exp2(s - jnp.concatenate([m_new] * (width // LANES), axis=1))
    l_sc[...] = alpha * l_sc[...] + jnp.sum(p, axis=1, keepdims=True)
    reps = acc_sc.shape[1] // LANES
    acc_sc[...] = jnp.concatenate([alpha] * reps, axis=1) * acc_sc[...] + v_dot(p.astype(BF16))
    m_sc[...] = m_new


def _subln(att, g, beta):
    return _rms(att, g) * (1.0 - LAM_INIT) * beta


def _attn_prompt_kernel(qt_ref, k_ref, vt_ref, bias_ref, lam_ref, g_ref, beta_ref, o_ref, m_sc, acc_sc):
    n_heads, e, blk = qt_ref.shape
    qi = pl.program_id(2)
    ones_rows = jnp.ones((acc_sc.shape[1] - e, blk), BF16)
    sub = lax.broadcasted_iota(jnp.int32, qt_ref.shape[1:], 0)
    qst = []
    for h in range(n_heads):
        qt = qt_ref[h]
        zero = jnp.zeros_like(qt)
        qst.append(jnp.concatenate([jnp.where(sub < DIFF_HEAD_DIM, qt, zero),
                                    jnp.where(sub >= DIFF_HEAD_DIM, qt, zero)], axis=1))
    m_sc[...] = jnp.full(m_sc.shape, MASKED, F32)
    acc_sc[...] = jnp.zeros(acc_sc.shape, F32)

    def step(j, n_blk, biased):
        rows = pl.ds(pl.multiple_of(j * blk, blk), n_blk * blk)

        def logits(h):
            st = jnp.dot(k_ref[h, rows, :], qst[h], preferred_element_type=F32)
            if biased:
                bias = bias_ref[h, (2 - n_blk) * blk:, :]
                st = st + jnp.concatenate([bias, bias], axis=1)
            return st

        ahead = 2
        pending = [logits(h) for h in range(min(ahead, n_heads))]
        for h in range(n_heads):
            if h + ahead < n_heads:
                pending.append(logits(h + ahead))
            st = pending[h]
            m_prev = m_sc[h]
            m_new = jnp.maximum(m_prev, jnp.max(st, axis=0, keepdims=True))
            alpha = jnp.exp2(m_prev - m_new)
            pb = jnp.exp2(st - m_new).astype(BF16)
            pv = None
            for i in range(n_blk):
                vt_aug = jnp.concatenate([vt_ref[h, j + i], ones_rows], axis=0)
                part = jnp.dot(vt_aug, pb[i * blk:(i + 1) * blk], preferred_element_type=F32)
                pv = part if pv is None else pv + part
            acc_sc[h] = alpha * acc_sc[h] + pv
            m_sc[h] = m_new

    odd = (qi + 1) & 1

    @pl.when(qi == 0)
    def _():
        step(0, 1, True)

    @pl.when(jnp.logical_and(qi > 0, odd == 1))
    def _():
        step(0, 1, False)

    def far_pair(i, carry):
        step(odd + 2 * i, 2, False)
        return carry

    lax.fori_loop(0, ((qi + 1) >> 1) - 1, far_pair, 0)

    @pl.when(qi > 0)
    def _():
        step(qi - 1, 2, True)

    for h in range(n_heads):
        ot = acc_sc[h, 0:e] / acc_sc[h, e:e + 1]
        att = (ot[:, :blk] - lam_ref[0:1, 0:1] * ot[:, blk:]).T
        sl = slice(h * ATT_V_DIM, (h + 1) * ATT_V_DIM)
        o_ref[:, sl] = _subln(att, g_ref[...], beta_ref[:, sl]).astype(o_ref.dtype)


def _attn_prompt(qt, k, vt, bias, lam, subln_g, beta_att):
    b, h, t, e = k.shape
    blk = bias.shape[-1]
    n_blk = t // blk
    hs = ATT_HEADS_PER_STEP
    assert h % hs == 0
    est = hs * (2 * t * e * 2) + hs * 2 * blk * blk * 4 + 4 * hs * 2 * blk * 2 * blk * 4
    return pl.pallas_call(
        _attn_prompt_kernel,
        grid=(b, h // hs, n_blk),
        in_specs=[pl.BlockSpec((None, hs, None, e, blk), lambda i, j, n: (i, j, n, 0, 0)),
                  pl.BlockSpec((None, hs, t, e), lambda i, j, n: (i, j, 0, 0), pipeline_mode=pl.Buffered(1)),
                  pl.BlockSpec((None, hs, n_blk, e, blk), lambda i, j, n: (i, j, 0, 0, 0),
                               pipeline_mode=pl.Buffered(1)),
                  pl.BlockSpec((hs, 2 * blk, blk), lambda i, j, n: (j, 0, 0), pipeline_mode=pl.Buffered(1)),
                  pl.BlockSpec(lam.shape, lambda i, j, n: (0, 0)),
                  pl.BlockSpec((1, e), lambda i, j, n: (0, 0)),
                  pl.BlockSpec((1, hs * e), lambda i, j, n: (0, j))],
        out_specs=pl.BlockSpec((None, blk, hs * e), lambda i, j, n: (i, n, j)),
        out_shape=jax.ShapeDtypeStruct((b, t, h * e), BF16),
        scratch_shapes=[pltpu.VMEM((hs, 1, 2 * blk), F32),
                        pltpu.VMEM((hs, e + BF16_ROWS, 2 * blk), F32)],
        compiler_params=pltpu.CompilerParams(dimension_semantics=("parallel", "parallel", "arbitrary"),
                                             vmem_limit_bytes=_vmem_limit(est)),
        name="attn_prompt",
    )(qt, k, vt, bias, lam, subln_g, beta_att)


def _attn_sample_kernel(pt_ref, q_ref, *refs):
    del pt_ref
    npg = PAGES_PER_STEP
    n_seq = q_ref.shape[0]
    k_refs, v_refs = refs[:n_seq * npg], refs[n_seq * npg:2 * n_seq * npg]
    (kn_ref, vn_ref, bias_ref, lam_ref, g_ref, beta_ref, o_ref, m_sc, l_sc, acc_sc) = refs[2 * n_seq * npg:]
    c = pl.program_id(1)
    last = c == pl.num_programs(1) - 1

    @pl.when(c == 0)
    def _():
        m_sc[...] = jnp.full(m_sc.shape, MASKED, F32)
        l_sc[...] = jnp.zeros(l_sc.shape, F32)
        acc_sc[...] = jnp.zeros(acc_sc.shape, F32)

    lane = lax.broadcasted_iota(jnp.int32, q_ref.shape[1:], 1)
    row = lax.broadcasted_iota(jnp.int32, q_ref.shape[1:], 0)
    keep = (lane >> 6) == (row & 1)
    qm = [jnp.where(keep, q_ref[s], jnp.zeros(q_ref.shape[1:], q_ref.dtype)) for s in range(n_seq)]
    cols = PAGE_SIZE * N_ATT_HEADS

    def pages_dot(v_list, n_cols):
        def v_dot(p):
            out = None
            for i, vr in enumerate(v_list):
                part = jnp.dot(p[:, i * n_cols:(i + 1) * n_cols], vr[...].astype(BF16),
                               preferred_element_type=F32)
                out = part if out is None else out + part
            return out
        return v_dot

    far_bias = bias_ref[0]
    last_bias = jnp.where(last, bias_ref[1], far_bias)

    def logits(s):
        pages = k_refs[s * npg:(s + 1) * npg]
        return jnp.concatenate(
            [_nt_dot(qm[s], kr[...].astype(BF16)) + (last_bias if i == npg - 1 else far_bias)
             for i, kr in enumerate(pages)], axis=1)

    s_all = [logits(s) for s in range(n_seq)]
    for s in range(n_seq):
        _softmax_update(s_all[s], pages_dot(v_refs[s * npg:(s + 1) * npg], cols),
                        m_sc.at[s], l_sc.at[s], acc_sc.at[s])

    @pl.when(last)
    def _():
        n_new = kn_ref.shape[1]
        sign = jnp.where((row & 1) == 0, 1.0, -lam_ref[0:1, 0:1])
        out_row = lax.broadcasted_iota(jnp.int32, (o_ref.shape[1], ATT_V_DIM), 0)
        for s in range(n_seq):
            s_new = _nt_dot(qm[s], kn_ref[s].astype(BF16)) + bias_ref[2][:, :n_new]
            _softmax_update(s_new, pages_dot([vn_ref.at[s]], n_new), m_sc.at[s], l_sc.at[s], acc_sc.at[s])
            z = acc_sc[s] / l_sc[s] * sign
            for h in range(N_ATT_HEADS):
                att = jnp.zeros(out_row.shape, F32)
                for t in range(MAX_NEW_TOKENS):
                    r0 = t * 8 + h * 2
                    att = jnp.where(out_row == t, jnp.broadcast_to(z[r0:r0 + 1] + z[r0 + 1:r0 + 2], att.shape),
                                    att)
                sl = slice(h * ATT_V_DIM, (h + 1) * ATT_V_DIM)
                o_ref[s, :, sl] = _subln(att, g_ref[...], beta_ref[:, sl])


def _attn_sample(page_table, q_rows, pool_k, pool_v, k_new, v_new, bias, lam, subln_g, beta_att):
    b, n_pages = page_table.shape
    rows, e = q_rows.shape[1:]
    page_rows = pool_k.shape[1]
    npg = PAGES_PER_STEP
    n_seq = SAMPLE_SEQS_PER_STEP
    assert n_pages % npg == 0 and b % n_seq == 0

    def page_spec(s, i):
        return pl.BlockSpec((None, page_rows, e), lambda g, c, pt: (pt[g * n_seq + s, c * npg + i], 0, 0))

    page_specs = [page_spec(s, i) for s in range(n_seq) for i in range(npg)]
    per_seq = lambda a: pl.BlockSpec((n_seq,) + a.shape[1:], lambda g, c, pt: (g, 0, 0))
    whole = lambda a: pl.BlockSpec(a.shape, lambda g, c, pt: (0,) * a.ndim)
    out_shape = jax.ShapeDtypeStruct((b, SAMPLE_ROWS, N_ATT_HEADS * e), F32)
    grid_spec = pltpu.PrefetchScalarGridSpec(
        num_scalar_prefetch=1,
        grid=(b // n_seq, n_pages // npg),
        in_specs=([per_seq(q_rows)] + page_specs * 2
                  + [per_seq(k_new), per_seq(v_new), whole(bias), whole(lam), whole(subln_g), whole(beta_att)]),
        out_specs=per_seq(out_shape),
        scratch_shapes=[pltpu.VMEM((n_seq, rows, LANES), F32), pltpu.VMEM((n_seq, rows, LANES), F32),
                        pltpu.VMEM((n_seq, rows, e), F32)],
    )
    est = n_seq * (2 * 2 * npg * page_rows * e * 4 + 8 * rows * npg * page_rows * 4)
    pages = [pool_k] * (n_seq * npg) + [pool_v] * (n_seq * npg)
    return pl.pallas_call(
        _attn_sample_kernel,
        grid_spec=grid_spec,
        out_shape=out_shape,
        compiler_params=pltpu.CompilerParams(dimension_semantics=("parallel", "arbitrary"),
                                             vmem_limit_bytes=_vmem_limit(est)),
        name="attn_sample",
    )(page_table, q_rows, *pages, k_new, v_new, bias, lam, subln_g, beta_att)


def _mem_kv_kernel(mem_ref, wk_ref, wv_ref, k_ref, v_ref):
    m = mem_ref[...].astype(BF16)
    k_ref[...] = jnp.dot(m, wk_ref[...], preferred_element_type=F32)
    v_ref[...] = jnp.dot(m, wv_ref[...], preferred_element_type=F32)


def _mem_kv(mem, w_xk, w_xv):
    b, n, d = mem.shape
    blk = pl.BlockSpec((None, n, d), lambda i: (i, 0, 0))
    w = pl.BlockSpec((d, d), lambda i: (0, 0))
    out = jax.ShapeDtypeStruct((b, n, d), F32)
    return pl.pallas_call(
        _mem_kv_kernel,
        grid=(b,),
        in_specs=[blk, w, w],
        out_specs=(blk, blk),
        out_shape=(out, out),
        compiler_params=pltpu.CompilerParams(dimension_semantics=("parallel",),
                                             vmem_limit_bytes=_vmem_limit(4 * d * d * 2 + 6 * n * d * 4)),
        name="mem_kv",
    )(mem, w_xk, w_xv)


def _mix_out_kernel(att_ref, conv_ref, h_ref, wo_ref, g_post_ref, g_x_ref, wq_ref, h1_ref, qx_ref):
    half = att_ref.shape[1]
    mo = (jnp.dot(att_ref[...].astype(BF16), wo_ref[0:half, :], preferred_element_type=F32)
          + jnp.dot(conv_ref[...].astype(BF16), wo_ref[half:, :], preferred_element_type=F32))
    h1 = h_ref[...] + _rms(mo, g_post_ref[...])
    h1_ref[...] = h1
    xn = _rms(h1, g_x_ref[...]).astype(BF16)
    x_scale = (wq_ref.shape[1] // N_X_HEADS) ** -0.5
    qx_ref[...] = (jnp.dot(xn, wq_ref[...], preferred_element_type=F32) * x_scale).astype(BF16)


def _mix_out(att, conv, h, w_out, g_post, g_x, w_xq):
    m, d = h.shape
    tm = min(ROW_TILE, m)
    row = lambda w: pl.BlockSpec((tm, w), lambda i: (i, 0))
    const = lambda a: pl.BlockSpec(a.shape, lambda i: (0, 0))
    est = 2 * 2 * d * d * 2 + 2 * tm * (2 * d * 4 + d * 2 + att.shape[1] * 6) + 4 * tm * d * 4
    return pl.pallas_call(
        _mix_out_kernel,
        grid=(m // tm,),
        in_specs=[row(att.shape[1]), row(conv.shape[1]), row(d), const(w_out), const(g_post), const(g_x),
                  const(w_xq)],
        out_specs=(row(d), row(d)),
        out_shape=(jax.ShapeDtypeStruct((m, d), F32), jax.ShapeDtypeStruct((m, d), BF16)),
        compiler_params=pltpu.CompilerParams(dimension_semantics=("parallel",),
                                             vmem_limit_bytes=_vmem_limit(est)),
        name="mix_out",
    )(att, conv, h, w_out, g_post, g_x, w_xq)


def _xattn_kernel(q_ref, mk_ref, mv_ref, o_ref):
    d = q_ref.shape[1]
    hd = d // N_X_HEADS
    for h in range(N_X_HEADS):
        sl = slice(h * hd, (h + 1) * hd)
        s = _nt_dot(q_ref[:, sl], mk_ref[:, sl].astype(BF16))
        p = jnp.exp(s - jnp.max(s, axis=1, keepdims=True))
        l = jnp.sum(p, axis=1, keepdims=True)
        o = jnp.dot(p.astype(BF16), mv_ref[:, sl].astype(BF16), preferred_element_type=F32)
        o_ref[:, sl] = (o / l).astype(o_ref.dtype)


def _xattn(qx, mem_k, mem_v):
    b, t, d = qx.shape
    n = mem_k.shape[1]
    tm = min(ROW_TILE, t)
    row = pl.BlockSpec((None, tm, d), lambda i, j: (i, j, 0))
    mem = pl.BlockSpec((None, n, d), lambda i, j: (i, 0, 0))
    return pl.pallas_call(
        _xattn_kernel,
        grid=(b, t // tm),
        in_specs=[row, mem, mem],
        out_specs=row,
        out_shape=jax.ShapeDtypeStruct((b, t, d), BF16),
        compiler_params=pltpu.CompilerParams(dimension_semantics=("parallel", "parallel"),
                                             vmem_limit_bytes=_vmem_limit(4 * n * d * 4 + 8 * tm * d * 4)),
        name="xattn",
    )(qx, mem_k, mem_v)


def _ffn_kernel(o_ref, h1_ref, wxo_ref, g_xpost_ref, g_pre_ref, wg_ref, wu_ref, wd_ref, g_post_ref, y_ref):
    h2 = h1_ref[...] + _rms(jnp.dot(o_ref[...], wxo_ref[...], preferred_element_type=F32), g_xpost_ref[...])
    xf = _rms(h2, g_pre_ref[...]).astype(BF16)
    d_ff = wg_ref.shape[1]
    f = jnp.zeros(h2.shape, F32)
    for c0 in range(0, d_ff, FFN_CHUNK):
        sl = slice(c0, c0 + FFN_CHUNK)
        g = jnp.dot(xf, wg_ref[:, sl], preferred_element_type=F32)
        u = jnp.dot(xf, wu_ref[:, sl], preferred_element_type=F32)
        a = (g * _sigmoid(g) * u).astype(BF16)
        f = f + jnp.dot(a, wd_ref[sl, :], preferred_element_type=F32)
    y_ref[...] = h2 + _rms(f, g_post_ref[...])


def _ffn(o, h1, w_xo, g_xpost, g_pre, w_gate, w_up, w_down, g_post):
    m, d = h1.shape
    d_ff = w_gate.shape[1]
    assert d_ff % FFN_CHUNK == 0
    tm = min(ROW_TILE, m)
    row = pl.BlockSpec((tm, d), lambda i: (i, 0))
    const = lambda a: pl.BlockSpec(a.shape, lambda i: (0, 0), pipeline_mode=pl.Buffered(1))
    est = (d * d + 3 * d * d_ff) * 2 + 2 * tm * d * (2 + 4 + 4) + 6 * tm * d * 4
    return pl.pallas_call(
        _ffn_kernel,
        grid=(m // tm,),
        in_specs=[row, row, const(w_xo), const(g_xpost), const(g_pre), const(w_gate), const(w_up),
                  const(w_down), const(g_post)],
        out_specs=row,
        out_shape=jax.ShapeDtypeStruct((m, d), F32),
        compiler_params=pltpu.CompilerParams(dimension_semantics=("parallel",),
                                             vmem_limit_bytes=_vmem_limit(est)),
        name="ffn",
    )(o, h1, w_xo, g_xpost, g_pre, w_gate, w_up, w_down, g_post)


def kernel(x_prompt, x_sample, mem_prompt, cache_k, cache_v, state_conv, cache_mem_k, cache_mem_v, page_table, rel_bias_table, norm_mix_pre, norm_mix_post, w_in, lambda_q1, lambda_k1, lambda_q2, lambda_k2, subln_g, dw_w, dw_b, conv_ln_g, conv_ln_b, beta_att, beta_conv, w_out, norm_x_pre, norm_x_post, w_xq, w_xk, w_xv, w_xo, norm_ffn_pre, norm_ffn_post, w_gate, w_up, w_down):
    assert w_in.shape[0] == 1, "single-layer trunk"
    bp, tp, d = x_prompt.shape
    bs, ts, _ = x_sample.shape
    assert ts <= MAX_NEW_TOKENS and tp >= CONV_K - 1
    n_mem = mem_prompt.shape[1]
    vec = lambda a: a[0].reshape(1, -1)
    wb = lambda a: a[0].astype(BF16)
    g_mix_pre, g_mix_post = vec(norm_mix_pre), vec(norm_mix_post)
    g_x_pre, g_x_post = vec(norm_x_pre), vec(norm_x_post)
    g_ffn_pre, g_ffn_post = vec(norm_ffn_pre), vec(norm_ffn_post)
    sub_g, b_att, b_conv = vec(subln_g), vec(beta_att), vec(beta_conv)
    c_b, c_g, c_bt = vec(dw_b), vec(conv_ln_g), vec(conv_ln_b)
    w_in_b, w_out_b, w_xq_b, w_xk_b, w_xv_b, w_xo_b = (wb(w) for w in (w_in, w_out, w_xq, w_xk, w_xv, w_xo))
    w_gate_b, w_up_b, w_down_b = wb(w_gate), wb(w_up), wb(w_down)
    dw = dw_w[0]

    blk = min(ATT_BLOCK, tp)
    lam_vecs = jnp.stack([lambda_q1[0], lambda_k1[0], lambda_q2[0], lambda_k2[0]])
    p_bias, s_bias, lam = _bias_tiles(rel_bias_table, lam_vecs, blk, ts)

    def tail(att, conv, h, mem_k, mem_v):
        b, t, _ = h.shape
        flat = lambda a: a.reshape(b * t, a.shape[-1])
        h1, qx = _mix_out(flat(att), flat(conv), flat(h), w_out_b, g_mix_post, g_x_pre, w_xq_b)
        o = _xattn(qx.reshape(b, t, d), mem_k, mem_v)
        y = _ffn(flat(o), h1, w_xo_b, g_x_post, g_ffn_pre, w_gate_b, w_up_b, w_down_b, g_ffn_post)
        return y.reshape(b, t, d)

    qt_p, kb_p, vt_p, kf_p, vf_p, glu_p = _in_proj(x_prompt, g_mix_pre, w_in_b, blk)
    att_p = _attn_prompt(qt_p, kb_p, vt_p, p_bias, lam, sub_g, b_att)
    conv_p = _conv_prompt(glu_p, jnp.zeros((bp, CONV_HALO, CONV_WIDTH), F32), dw, c_b, c_g, c_bt, b_conv)
    mk_p, mv_p = _mem_kv(mem_prompt, w_xk_b, w_xv_b)
    y_p = tail(att_p, conv_p, x_prompt, mk_p, mv_p)

    x_s = jnp.pad(x_sample, ((0, 0), (0, SAMPLE_ROWS - ts), (0, 0)))
    rows_s = bs * SAMPLE_ROWS
    qt_s, _, _, kf_s, vf_s, glu_s = _in_proj(x_s.reshape(1, rows_s, d), g_mix_pre, w_in_b, min(blk, rows_s))
    new_rows = ts * N_ATT_HEADS
    kf_s = kf_s.reshape(bs, SAMPLE_ROWS * N_ATT_HEADS, ATT_V_DIM)[:, :new_rows]
    vf_s = vf_s.reshape(bs, SAMPLE_ROWS * N_ATT_HEADS, ATT_V_DIM)[:, :new_rows]
    glu_s = glu_s.reshape(bs, SAMPLE_ROWS, CONV_WIDTH)[:, :ts]
    q_s = jnp.transpose(qt_s[0], (0, 1, 3, 2))
    q_th = jnp.transpose(q_s.reshape(N_ATT_HEADS, bs, SAMPLE_ROWS, ATT_V_DIM), (1, 2, 0, 3))
    q_th = jnp.pad(q_th[:, :ts], ((0, 0), (0, MAX_NEW_TOKENS - ts), (0, 0), (0, 0)))
    q_rows = jnp.repeat(q_th.reshape(bs, MAX_NEW_TOKENS * N_ATT_HEADS, ATT_V_DIM), 2, axis=1)
    pad_page = lambda a: jnp.pad(a, ((0, 0), (0, PAGE_SIZE - new_rows), (0, 0)))
    n_phys = cache_k.shape[1]
    pool = lambda c: c.reshape(n_phys, PAGE_SIZE * N_ATT_HEADS, ATT_V_DIM)
    att_s = _attn_sample(page_table, q_rows, pool(cache_k), pool(cache_v), pad_page(kf_s), pad_page(vf_s),
                         s_bias, lam, sub_g, b_att)
    xp_s = jnp.concatenate([state_conv[0], glu_s], axis=1)
    conv_s = _conv_sample(jnp.transpose(xp_s, (1, 0, 2)), ts, dw, c_b, c_g, c_bt, b_conv)
    conv_s = jnp.pad(jnp.transpose(conv_s, (1, 0, 2)), ((0, 0), (0, SAMPLE_ROWS - ts), (0, 0)))
    mem_s = lambda c: c[0].reshape(bs, n_mem, d)
    y_s = tail(att_s, conv_s, x_s, mem_s(cache_mem_k), mem_s(cache_mem_v))

    heads = lambda a: a.reshape(1, a.shape[0], a.shape[1] // N_ATT_HEADS, N_ATT_HEADS, ATT_V_DIM)
    mem_heads = lambda a: a.reshape(1, bp, n_mem, N_X_HEADS, d // N_X_HEADS)
    return (y_p, y_s[:, :ts],
            heads(kf_p), heads(vf_p), glu_p[None, :, tp - (CONV_K - 1):],
            mem_heads(mk_p), mem_heads(mv_p),
            heads(kf_s), heads(vf_s), xp_s[None, :, ts:])
```

```python
import functools
import math

import numpy as np
import jax
import jax.numpy as jnp
from jax import lax
from jax.experimental import pallas as pl
from jax.experimental.pallas import tpu as pltpu

F32 = jnp.float32
BF16 = jnp.bfloat16

DIFF_HEAD_DIM = 64
ATT_V_DIM = 2 * DIFF_HEAD_DIM
N_ATT_HEADS = 4
ATT_WIDTH = N_ATT_HEADS * ATT_V_DIM
CONV_WIDTH = 512
CONV_K = 31
N_BUCKETS = 32
MAX_DISTANCE = 128
N_X_HEADS = 4
PAGE_SIZE = 128
RMS_EPS = 1e-6
LN_EPS = 1e-5
ATT_SCALE = DIFF_HEAD_DIM ** -0.5
LOG2_E = math.log2(math.e)
LAM_INIT = 0.8 - 0.6 * math.exp(-0.3 * 0)
MASKED = -1e30
STALE_MAX_MARGIN = 30.0

V7X_VMEM_BYTES = 64 * 1024 * 1024
LANES = 128
SUBLANES = 8
BF16_ROWS = 16

ROW_TILE = 512
ATT_BLOCK = 256
ATT_HEADS_PER_STEP = 4
CONV_TILE = 256
CONV_CHUNK = 64
CONV_HALO = 32
PAGES_PER_STEP = 8
SAMPLE_SEQS_PER_STEP = 2
FFN_CHUNK = 256
SAMPLE_ROWS = 8
MAX_NEW_TOKENS = 4
SAMPLE_Q_ROWS = MAX_NEW_TOKENS * 2 * N_ATT_HEADS


def _vmem_limit(nbytes):
    return int(min(max(2 * nbytes, 16 * 1024 * 1024), V7X_VMEM_BYTES - 8 * 1024 * 1024))


def _rms(x, g):
    return x * lax.rsqrt(jnp.mean(x * x, axis=-1, keepdims=True) + RMS_EPS) * g


def _sigmoid(x):
    return 1.0 / (1.0 + jnp.exp(-x))


def _nt_dot(a, b):
    return lax.dot_general(a, b, (((1,), (1,)), ((), ())), preferred_element_type=F32)


def _bucket_np(n):
    n = np.maximum(n, 0)
    max_exact = N_BUCKETS // 2
    nf = np.maximum(n, 1).astype(np.float32)
    large = max_exact + (np.log(nf / max_exact) / math.log(MAX_DISTANCE / max_exact)
                         * (N_BUCKETS - max_exact)).astype(np.int32)
    large = np.minimum(large, N_BUCKETS - 1)
    return np.where(n < max_exact, n, large).astype(np.int32)


def _prompt_codes(blk):
    i = np.arange(blk)[None, :]
    j = np.arange(blk)[:, None]
    prev = _bucket_np(i - j + blk)
    diag = np.where(j > i, -1, _bucket_np(i - j))
    return np.concatenate([prev, diag]).astype(np.int32)


def _sample_codes(n_new):
    r = np.arange(SAMPLE_Q_ROWS)[:, None]
    c = np.arange(PAGE_SIZE * N_ATT_HEADS)[None, :]
    t, head = r // 8, (r // 2) % N_ATT_HEADS
    tok, key_head = c // N_ATT_HEADS, c % N_ATT_HEADS
    own = key_head == head
    far = np.where(own, N_BUCKETS - 1, -1)
    last = np.where(own, _bucket_np(t + PAGE_SIZE - tok), -1)
    new = np.where(own & (tok <= t) & (tok < n_new) & (c < PAGE_SIZE), _bucket_np(t - tok), -1)
    return np.stack([far, last, new]).astype(np.int32)


def _bias_kernel(tab_ref, lam_in_ref, pcode_ref, scode_ref, pbias_ref, sbias_ref, lam_ref):
    far = N_BUCKETS - 1

    def lookup(code, h):
        out = jnp.zeros(code.shape, F32)
        for b in range(far):
            out = jnp.where(code == b, (tab_ref[b, h] - tab_ref[far, h]) * LOG2_E, out)
        return jnp.where(code < 0, MASKED, out)

    for h in range(N_ATT_HEADS):
        pbias_ref[h] = lookup(pcode_ref[...], h)
    row_head = (lax.broadcasted_iota(jnp.int32, scode_ref.shape[1:], 0) >> 1) & (N_ATT_HEADS - 1)
    for i in range(scode_ref.shape[0]):
        code = scode_ref[i]
        out = jnp.zeros(code.shape, F32)
        for h in range(N_ATT_HEADS):
            out = jnp.where(row_head == h, lookup(code, h), out)
        sbias_ref[i] = out
    lv = lam_in_ref[...]
    d1 = jnp.sum(lv[0:1] * lv[1:2], axis=1, keepdims=True)
    d2 = jnp.sum(lv[2:3] * lv[3:4], axis=1, keepdims=True)
    lam = jnp.exp(d1) - jnp.exp(d2) + LAM_INIT
    lam_ref[...] = jnp.broadcast_to(lam, lam_ref.shape)


def _bias_tiles(table, lam_vecs, blk, n_new):
    pcode = jnp.asarray(_prompt_codes(blk))
    scode = jnp.asarray(_sample_codes(n_new))
    vm = pl.BlockSpec(memory_space=pltpu.VMEM)
    return pl.pallas_call(
        _bias_kernel,
        out_shape=(jax.ShapeDtypeStruct((N_ATT_HEADS, 2 * blk, blk), F32),
                   jax.ShapeDtypeStruct(scode.shape, F32),
                   jax.ShapeDtypeStruct((SUBLANES, LANES), F32)),
        in_specs=[pl.BlockSpec(memory_space=pltpu.SMEM), vm, vm, vm],
        out_specs=(vm, vm, vm),
        name="bias_tiles",
    )(table, lam_vecs, pcode, scode)


def _in_proj_kernel(x_ref, g_ref, w_ref, qt_ref, kb_ref, vt_ref, kf_ref, vf_ref, glu_ref):
    xn = _rms(x_ref[...], g_ref[...]).astype(BF16)

    def cols(c):
        return jnp.dot(xn, w_ref[:, c * ATT_WIDTH:(c + 1) * ATT_WIDTH], preferred_element_type=F32)

    tm = xn.shape[0]
    blk = qt_ref.shape[-1]
    q = cols(0) * (ATT_SCALE * LOG2_E)
    k = cols(1)
    v = cols(2)
    kb = k.astype(BF16)
    for h in range(N_ATT_HEADS):
        sl = slice(h * ATT_V_DIM, (h + 1) * ATT_V_DIM)
        kb_ref[h] = kb[:, sl]
        for c in range(tm // blk):
            rows = slice(c * blk, (c + 1) * blk)
            qt_ref[h, c] = q[rows, sl].T.astype(BF16)
            vt_ref[h, c] = v[rows, sl].T.astype(BF16)
        kf_ref[pl.ds(h, tm, stride=N_ATT_HEADS), :] = k[:, sl]
        vf_ref[pl.ds(h, tm, stride=N_ATT_HEADS), :] = v[:, sl]
    glu_ref[...] = cols(3) * _sigmoid(cols(4))


def _in_proj(x, gain, w_in, blk):
    b, t, d = x.shape
    tm = min(ROW_TILE, t)
    n_cols = w_in.shape[1]
    assert tm % blk == 0
    row = lambda w: pl.BlockSpec((None, tm, w), lambda i, j: (i, j, 0))
    heads = pl.BlockSpec((None, N_ATT_HEADS, tm, ATT_V_DIM), lambda i, j: (i, 0, j, 0))
    hm = jax.ShapeDtypeStruct((b, N_ATT_HEADS, t, ATT_V_DIM), BF16)
    heads_t = pl.BlockSpec((None, N_ATT_HEADS, tm // blk, ATT_V_DIM, blk), lambda i, j: (i, 0, j, 0, 0))
    hm_t = jax.ShapeDtypeStruct((b, N_ATT_HEADS, t // blk, ATT_V_DIM, blk), BF16)
    flat = jax.ShapeDtypeStruct((b, t * N_ATT_HEADS, ATT_V_DIM), F32)
    tok_head = pl.BlockSpec((None, tm * N_ATT_HEADS, ATT_V_DIM), lambda i, j: (i, j, 0))
    est = 2 * d * n_cols * 2 + 2 * tm * (d * 4 + 3 * ATT_WIDTH * 2 + 3 * ATT_WIDTH * 4) + 6 * tm * ATT_WIDTH * 4
    return pl.pallas_call(
        _in_proj_kernel,
        grid=(b, t // tm),
        in_specs=[row(d),
                  pl.BlockSpec((1, d), lambda i, j: (0, 0)),
                  pl.BlockSpec((d, n_cols), lambda i, j: (0, 0))],
        out_specs=(heads_t, heads, heads_t, tok_head, tok_head, row(CONV_WIDTH)),
        out_shape=(hm_t, hm, hm_t, flat, flat, jax.ShapeDtypeStruct((b, t, CONV_WIDTH), F32)),
        compiler_params=pltpu.CompilerParams(dimension_semantics=("parallel", "parallel"),
                                             vmem_limit_bytes=_vmem_limit(est)),
        name="in_proj",
    )(x, gain, w_in)


def _conv_post(y, b_ref, g_ref, bt_ref, beta_ref):
    y = y + b_ref[...]
    mu = jnp.mean(y, axis=-1, keepdims=True)
    yc = y - mu
    yn = yc * lax.rsqrt(jnp.mean(yc * yc, axis=-1, keepdims=True) + LN_EPS) * g_ref[...] + bt_ref[...]
    return yn * _sigmoid(yn) * beta_ref[...]


def _conv_prompt_kernel(glu_ref, prev_ref, w_ref, b_ref, g_ref, bt_ref, beta_ref, o_ref, buf, ybuf):
    tt = glu_ref.shape[0]
    n_slab = buf.shape[0]
    first = pl.program_id(1) == 0
    slabs = [slice(s * LANES, (s + 1) * LANES) for s in range(n_slab)]

    @pl.when(first)
    def _():
        for s in range(n_slab):
            buf[s, 0:CONV_HALO] = prev_ref[:, slabs[s]]

    @pl.when(jnp.logical_not(first))
    def _():
        for s in range(n_slab):
            buf[s, 0:CONV_HALO] = buf[s, tt:tt + CONV_HALO]

    for s in range(n_slab):
        buf[s, CONV_HALO:CONV_HALO + tt] = glu_ref[:, slabs[s]]

    shift = CONV_HALO - (CONV_K - 1)
    rc = min(CONV_CHUNK, tt // 2)
    for c0 in range(0, tt, 2 * rc):
        for parity in range(2):
            start = c0 + parity
            accs = []
            for s in range(n_slab):
                acc = jnp.zeros((rc, LANES), F32)
                for j in range(CONV_K):
                    x = buf[s, pl.ds(start + j + shift, rc, stride=2), :]
                    acc = acc + w_ref[j:j + 1, slabs[s]] * x
                accs.append(acc)
            y = _conv_post(jnp.concatenate(accs, axis=1), b_ref, g_ref, bt_ref, beta_ref)
            for s in range(n_slab):
                ybuf[s, pl.ds(start, rc, stride=2), :] = y[:, slabs[s]]
    for s in range(n_slab):
        o_ref[:, slabs[s]] = ybuf[s].astype(o_ref.dtype)


def _conv_prompt(glu, prev, dw_w, dw_b, ln_g, ln_b, beta):
    b, t, c = glu.shape
    tt = min(CONV_TILE, t)
    vec = pl.BlockSpec((1, c), lambda i, j: (0, 0))
    return pl.pallas_call(
        _conv_prompt_kernel,
        grid=(b, t // tt),
        in_specs=[pl.BlockSpec((None, tt, c), lambda i, j: (i, j, 0)),
                  pl.BlockSpec((None, CONV_HALO, c), lambda i, j: (i, 0, 0)),
                  pl.BlockSpec(dw_w.shape, lambda i, j: (0, 0)),
                  vec, vec, vec, vec],
        out_specs=pl.BlockSpec((None, tt, c), lambda i, j: (i, j, 0)),
        out_shape=jax.ShapeDtypeStruct((b, t, c), BF16),
        scratch_shapes=[pltpu.VMEM((c // LANES, tt + CONV_HALO, LANES), F32),
                        pltpu.VMEM((c // LANES, tt, LANES), F32)],
        compiler_params=pltpu.CompilerParams(dimension_semantics=("parallel", "arbitrary")),
        name="conv_prompt",
    )(glu, prev, dw_w, dw_b, ln_g, ln_b, beta)


def _conv_sample_kernel(xp_ref, w_ref, b_ref, g_ref, bt_ref, beta_ref, o_ref):
    n_t = o_ref.shape[0]
    for t in range(n_t):
        acc = jnp.zeros(xp_ref.shape[1:], F32)
        for j in range(CONV_K):
            acc = acc + w_ref[j:j + 1, :] * xp_ref[t + j]
        o_ref[t] = _conv_post(acc, b_ref, g_ref, bt_ref, beta_ref)


def _conv_sample(xp_t, n_t, dw_w, dw_b, ln_g, ln_b, beta):
    vm = pl.BlockSpec(memory_space=pltpu.VMEM)
    return pl.pallas_call(
        _conv_sample_kernel,
        out_shape=jax.ShapeDtypeStruct((n_t,) + xp_t.shape[1:], F32),
        in_specs=[vm] * 6,
        out_specs=vm,
        name="conv_sample",
    )(xp_t, dw_w, dw_b, ln_g, ln_b, beta)


def _softmax_update(s, v_dot, m_sc, l_sc, acc_sc):
    width = s.shape[1]
    m_prev = m_sc[...]
    m_new = jnp.maximum(m_prev, jnp.max(s, axis=1, keepdims=True))
    alpha = jnp.exp2(m_prev - m_new)
    p = jnp.exp2(s - jnp.concatenate([m_new] * (width // LANES), axis=1))
    l_sc[...] = alpha * l_sc[...] + jnp.sum(p, axis=1, keepdims=True)
    reps = acc_sc.shape[1] // LANES
    acc_sc[...] = jnp.concatenate([alpha] * reps, axis=1) * acc_sc[...] + v_dot(p.astype(BF16))
    m_sc[...] = m_new


def _subln(att, g, beta):
    return _rms(att, g) * (1.0 - LAM_INIT) * beta


def _attn_prompt_kernel(qt_ref, k_ref, vt_ref, bias_ref, lam_ref, g_ref, beta_ref, o_ref, m_sc, acc_sc):
    n_heads, e, blk = qt_ref.shape
    qi = pl.program_id(2)
    ones_rows = jnp.ones((acc_sc.shape[2] - e, blk), BF16)
    sub = lax.broadcasted_iota(jnp.int32, qt_ref.shape[1:], 0)
    qst = []
    for h in range(n_heads):
        qt = qt_ref[h]
        zero = jnp.zeros_like(qt)
        qst.append(jnp.concatenate([jnp.where(sub < DIFF_HEAD_DIM, qt, zero),
                                    jnp.where(sub >= DIFF_HEAD_DIM, qt, zero)], axis=1))
    m_sc[...] = jnp.full(m_sc.shape, MASKED, F32)
    acc_sc[0] = jnp.zeros(acc_sc.shape[1:], F32)

    def step(j, n_blk, biased, stale_max, src):
        rows = pl.ds(pl.multiple_of(j * blk, blk), n_blk * blk)
        dst = 1 - src

        def logits(h):
            st = jnp.dot(k_ref[h, rows, :], qst[h], preferred_element_type=F32)
            if biased:
                bias = bias_ref[h, (2 - n_blk) * blk:, :]
                st = st + jnp.concatenate([bias, bias], axis=1)
            return st

        def weighted_values(h, pb):
            pv = None
            for i in range(n_blk):
                vt_aug = jnp.concatenate([vt_ref[h, j + i], ones_rows], axis=0)
                part = jnp.dot(vt_aug, pb[i * blk:(i + 1) * blk], preferred_element_type=F32)
                pv = part if pv is None else pv + part
            return pv

        def exact_update(h, st):
            m_prev = m_sc[h]
            m_new = jnp.maximum(m_prev, jnp.max(st, axis=0, keepdims=True))
            alpha = jnp.exp2(m_prev - m_new)
            pv = weighted_values(h, jnp.exp2(st - m_new).astype(BF16))
            acc_sc[dst, h] = alpha * acc_sc[src, h] + pv
            m_sc[h] = m_new

        def stale_update(h, st):
            m_ref = m_sc[h]
            pv = weighted_values(h, jnp.exp2(st - m_ref).astype(BF16))
            acc_sc[dst, h] = acc_sc[src, h] + pv
            return jnp.max(st, axis=0, keepdims=True) - m_ref

        ahead = 2
        pending = [logits(h) for h in range(min(ahead, n_heads))]
        excess = None
        for h in range(n_heads):
            if h + ahead < n_heads:
                pending.append(logits(h + ahead))
            if stale_max:
                over = stale_update(h, pending[h])
                excess = over if excess is None else jnp.maximum(excess, over)
            else:
                exact_update(h, pending[h])

        if stale_max:
            @pl.when(jnp.max(excess) > STALE_MAX_MARGIN)
            def _():
                for h in range(n_heads):
                    exact_update(h, logits(h))

    odd = (qi + 1) & 1
    n_pairs = jnp.maximum(((qi + 1) >> 1) - 1, 0)

    @pl.when(qi == 0)
    def _():
        step(0, 1, True, False, 0)

    @pl.when(qi > 0)
    def _():
        step(qi - 1, 2, True, False, 0)

    def far_pair(i, carry):
        step(odd + 2 * i, 2, False, True, (i + 1) & 1)
        return carry

    lax.fori_loop(0, n_pairs, far_pair, 0)
    single = jnp.logical_and(qi > 0, odd == 1)

    @pl.when(single)
    def _():
        step(0, 1, False, True, (n_pairs + 1) & 1)

    res = (n_pairs + 1 + single.astype(jnp.int32)) & 1
    for h in range(n_heads):
        ot = acc_sc[res, h, 0:e] / acc_sc[res, h, e:e + 1]
        att = (ot[:, :blk] - lam_ref[0:1, 0:1] * ot[:, blk:]).T
        sl = slice(h * ATT_V_DIM, (h + 1) * ATT_V_DIM)
        o_ref[:, sl] = _subln(att, g_ref[...], beta_ref[:, sl]).astype(o_ref.dtype)


def _attn_prompt(qt, k, vt, bias, lam, subln_g, beta_att):
    b, h, t, e = k.shape
    blk = bias.shape[-1]
    n_blk = t // blk
    hs = ATT_HEADS_PER_STEP
    assert h % hs == 0
    est = hs * (2 * t * e * 2) + hs * 2 * blk * blk * 4 + 4 * hs * 2 * blk * 2 * blk * 4
    return pl.pallas_call(
        _attn_prompt_kernel,
        grid=(b, h // hs, n_blk),
        in_specs=[pl.BlockSpec((None, hs, None, e, blk), lambda i, j, n: (i, j, n, 0, 0)),
                  pl.BlockSpec((None, hs, t, e), lambda i, j, n: (i, j, 0, 0), pipeline_mode=pl.Buffered(1)),
                  pl.BlockSpec((None, hs, n_blk, e, blk), lambda i, j, n: (i, j, 0, 0, 0),
                               pipeline_mode=pl.Buffered(1)),
                  pl.BlockSpec((hs, 2 * blk, blk), lambda i, j, n: (j, 0, 0), pipeline_mode=pl.Buffered(1)),
                  pl.BlockSpec(lam.shape, lambda i, j, n: (0, 0)),
                  pl.BlockSpec((1, e), lambda i, j, n: (0, 0)),
                  pl.BlockSpec((1, hs * e), lambda i, j, n: (0, j))],
        out_specs=pl.BlockSpec((None, blk, hs * e), lambda i, j, n: (i, n, j)),
        out_shape=jax.ShapeDtypeStruct((b, t, h * e), BF16),
        scratch_shapes=[pltpu.VMEM((hs, 1, 2 * blk), F32),
                        pltpu.VMEM((2, hs, e + BF16_ROWS, 2 * blk), F32)],
        compiler_params=pltpu.CompilerParams(dimension_semantics=("parallel", "parallel", "arbitrary"),
                                             vmem_limit_bytes=_vmem_limit(est)),
        name="attn_prompt",
    )(qt, k, vt, bias, lam, subln_g, beta_att)


def _attn_sample_kernel(pt_ref, q_ref, *refs):
    del pt_ref
    npg = PAGES_PER_STEP
    n_seq = q_ref.shape[0]
    k_refs, v_refs = refs[:n_seq * npg], refs[n_seq * npg:2 * n_seq * npg]
    (kn_ref, vn_ref, bias_ref, lam_ref, g_ref, beta_ref, o_ref, m_sc, l_sc, acc_sc) = refs[2 * n_seq * npg:]
    c = pl.program_id(1)
    last = c == pl.num_programs(1) - 1

    @pl.when(c == 0)
    def _():
        m_sc[...] = jnp.full(m_sc.shape, MASKED, F32)
        l_sc[...] = jnp.zeros(l_sc.shape, F32)
        acc_sc[...] = jnp.zeros(acc_sc.shape, F32)

    lane = lax.broadcasted_iota(jnp.int32, q_ref.shape[1:], 1)
    row = lax.broadcasted_iota(jnp.int32, q_ref.shape[1:], 0)
    keep = (lane >> 6) == (row & 1)
    qm = [jnp.where(keep, q_ref[s], jnp.zeros(q_ref.shape[1:], q_ref.dtype)) for s in range(n_seq)]
    cols = PAGE_SIZE * N_ATT_HEADS

    def pages_dot(v_list, n_cols):
        def v_dot(p):
            out = None
            for i, vr in enumerate(v_list):
                part = jnp.dot(p[:, i * n_cols:(i + 1) * n_cols], vr[...].astype(BF16),
                               preferred_element_type=F32)
                out = part if out is None else out + part
            return out
        return v_dot

    far_bias = bias_ref[0]
    last_bias = jnp.where(last, bias_ref[1], far_bias)

    def logits(s):
        pages = k_refs[s * npg:(s + 1) * npg]
        return jnp.concatenate(
            [_nt_dot(qm[s], kr[...].astype(BF16)) + (last_bias if i == npg - 1 else far_bias)
             for i, kr in enumerate(pages)], axis=1)

    s_all = [logits(s) for s in range(n_seq)]
    for s in range(n_seq):
        _softmax_update(s_all[s], pages_dot(v_refs[s * npg:(s + 1) * npg], cols),
                        m_sc.at[s], l_sc.at[s], acc_sc.at[s])

    @pl.when(last)
    def _():
        n_new = kn_ref.shape[1]
        sign = jnp.where((row & 1) == 0, 1.0, -lam_ref[0:1, 0:1])
        out_row = lax.broadcasted_iota(jnp.int32, (o_ref.shape[1], ATT_V_DIM), 0)
        for s in range(n_seq):
            s_new = _nt_dot(qm[s], kn_ref[s].astype(BF16)) + bias_ref[2][:, :n_new]
            _softmax_update(s_new, pages_dot([vn_ref.at[s]], n_new), m_sc.at[s], l_sc.at[s], acc_sc.at[s])
            z = acc_sc[s] / l_sc[s] * sign
            for h in range(N_ATT_HEADS):
                att = jnp.zeros(out_row.shape, F32)
                for t in range(MAX_NEW_TOKENS):
                    r0 = t * 8 + h * 2
                    att = jnp.where(out_row == t, jnp.broadcast_to(z[r0:r0 + 1] + z[r0 + 1:r0 + 2], att.shape),
                                    att)
                sl = slice(h * ATT_V_DIM, (h + 1) * ATT_V_DIM)
                o_ref[s, :, sl] = _subln(att, g_ref[...], beta_ref[:, sl])


def _attn_sample(page_table, q_rows, pool_k, pool_v, k_new, v_new, bias, lam, subln_g, beta_att):
    b, n_pages = page_table.shape
    rows, e = q_rows.shape[1:]
    page_rows = pool_k.shape[1]
    npg = PAGES_PER_STEP
    n_seq = SAMPLE_SEQS_PER_STEP
    assert n_pages % npg == 0 and b % n_seq == 0

    def page_spec(s, i):
        return pl.BlockSpec((None, page_rows, e), lambda g, c, pt: (pt[g * n_seq + s, c * npg + i], 0, 0))

    page_specs = [page_spec(s, i) for s in range(n_seq) for i in range(npg)]
    per_seq = lambda a: pl.BlockSpec((n_seq,) + a.shape[1:], lambda g, c, pt: (g, 0, 0))
    whole = lambda a: pl.BlockSpec(a.shape, lambda g, c, pt: (0,) * a.ndim)
    out_shape = jax.ShapeDtypeStruct((b, SAMPLE_ROWS, N_ATT_HEADS * e), F32)
    grid_spec = pltpu.PrefetchScalarGridSpec(
        num_scalar_prefetch=1,
        grid=(b // n_seq, n_pages // npg),
        in_specs=([per_seq(q_rows)] + page_specs * 2
                  + [per_seq(k_new), per_seq(v_new), whole(bias), whole(lam), whole(subln_g), whole(beta_att)]),
        out_specs=per_seq(out_shape),
        scratch_shapes=[pltpu.VMEM((n_seq, rows, LANES), F32), pltpu.VMEM((n_seq, rows, LANES), F32),
                        pltpu.VMEM((n_seq, rows, e), F32)],
    )
    est = n_seq * (2 * 2 * npg * page_rows * e * 4 + 8 * rows * npg * page_rows * 4)
    pages = [pool_k] * (n_seq * npg) + [pool_v] * (n_seq * npg)
    return pl.pallas_call(
        _attn_sample_kernel,
        grid_spec=grid_spec,
        out_shape=out_shape,
        compiler_params=pltpu.CompilerParams(dimension_semantics=("parallel", "arbitrary"),
                                             vmem_limit_bytes=_vmem_limit(est)),
        name="attn_sample",
    )(page_table, q_rows, *pages, k_new, v_new, bias, lam, subln_g, beta_att)


def _mem_kv_kernel(mem_ref, wk_ref, wv_ref, k_ref, v_ref):
    m = mem_ref[...].astype(BF16)
    k_ref[...] = jnp.dot(m, wk_ref[...], preferred_element_type=F32)
    v_ref[...] = jnp.dot(m, wv_ref[...], preferred_element_type=F32)


def _mem_kv(mem, w_xk, w_xv):
    b, n, d = mem.shape
    blk = pl.BlockSpec((None, n, d), lambda i: (i, 0, 0))
    w = pl.BlockSpec((d, d), lambda i: (0, 0))
    out = jax.ShapeDtypeStruct((b, n, d), F32)
    return pl.pallas_call(
        _mem_kv_kernel,
        grid=(b,),
        in_specs=[blk, w, w],
        out_specs=(blk, blk),
        out_shape=(out, out),
        compiler_params=pltpu.CompilerParams(dimension_semantics=("parallel",),
                                             vmem_limit_bytes=_vmem_limit(4 * d * d * 2 + 6 * n * d * 4)),
        name="mem_kv",
    )(mem, w_xk, w_xv)


def _mix_out_kernel(att_ref, conv_ref, h_ref, wo_ref, g_post_ref, g_x_ref, wq_ref, h1_ref, qx_ref):
    half = att_ref.shape[1]
    mo = (jnp.dot(att_ref[...].astype(BF16), wo_ref[0:half, :], preferred_element_type=F32)
          + jnp.dot(conv_ref[...].astype(BF16), wo_ref[half:, :], preferred_element_type=F32))
    h1 = h_ref[...] + _rms(mo, g_post_ref[...])
    h1_ref[...] = h1
    xn = _rms(h1, g_x_ref[...]).astype(BF16)
    x_scale = (wq_ref.shape[1] // N_X_HEADS) ** -0.5
    qx_ref[...] = (jnp.dot(xn, wq_ref[...], preferred_element_type=F32) * x_scale).astype(BF16)


def _mix_out(att, conv, h, w_out, g_post, g_x, w_xq):
    m, d = h.shape
    tm = min(ROW_TILE, m)
    row = lambda w: pl.BlockSpec((tm, w), lambda i: (i, 0))
    const = lambda a: pl.BlockSpec(a.shape, lambda i: (0, 0))
    est = 2 * 2 * d * d * 2 + 2 * tm * (2 * d * 4 + d * 2 + att.shape[1] * 6) + 4 * tm * d * 4
    return pl.pallas_call(
        _mix_out_kernel,
        grid=(m // tm,),
        in_specs=[row(att.shape[1]), row(conv.shape[1]), row(d), const(w_out), const(g_post), const(g_x),
                  const(w_xq)],
        out_specs=(row(d), row(d)),
        out_shape=(jax.ShapeDtypeStruct((m, d), F32), jax.ShapeDtypeStruct((m, d), BF16)),
        compiler_params=pltpu.CompilerParams(dimension_semantics=("parallel",),
                                             vmem_limit_bytes=_vmem_limit(est)),
        name="mix_out",
    )(att, conv, h, w_out, g_post, g_x, w_xq)


def _xattn_kernel(q_ref, mk_ref, mv_ref, o_ref):
    d = q_ref.shape[1]
    hd = d // N_X_HEADS
    for h in range(N_X_HEADS):
        sl = slice(h * hd, (h + 1) * hd)
        s = _nt_dot(q_ref[:, sl], mk_ref[:, sl].astype(BF16))
        p = jnp.exp(s - jnp.max(s, axis=1, keepdims=True))
        l = jnp.sum(p, axis=1, keepdims=True)
        o = jnp.dot(p.astype(BF16), mv_ref[:, sl].astype(BF16), preferred_element_type=F32)
        o_ref[:, sl] = (o / l).astype(o_ref.dtype)


def _xattn(qx, mem_k, mem_v):
    b, t, d = qx.shape
    n = mem_k.shape[1]
    tm = min(ROW_TILE, t)
    row = pl.BlockSpec((None, tm, d), lambda i, j: (i, j, 0))
    mem = pl.BlockSpec((None, n, d), lambda i, j: (i, 0, 0))
    return pl.pallas_call(
        _xattn_kernel,
        grid=(b, t // tm),
        in_specs=[row, mem, mem],
        out_specs=row,
        out_shape=jax.ShapeDtypeStruct((b, t, d), BF16),
        compiler_params=pltpu.CompilerParams(dimension_semantics=("parallel", "parallel"),
                                             vmem_limit_bytes=_vmem_limit(4 * n * d * 4 + 8 * tm * d * 4)),
        name="xattn",
    )(qx, mem_k, mem_v)


def _ffn_kernel(o_ref, h1_ref, wxo_ref, g_xpost_ref, g_pre_ref, wg_ref, wu_ref, wd_ref, g_post_ref, y_ref):
    h2 = h1_ref[...] + _rms(jnp.dot(o_ref[...], wxo_ref[...], preferred_element_type=F32), g_xpost_ref[...])
    xf = _rms(h2, g_pre_ref[...]).astype(BF16)
    d_ff = wg_ref.shape[1]
    f = jnp.zeros(h2.shape, F32)
    for c0 in range(0, d_ff, FFN_CHUNK):
        sl = slice(c0, c0 + FFN_CHUNK)
        g = jnp.dot(xf, wg_ref[:, sl], preferred_element_type=F32)
        u = jnp.dot(xf, wu_ref[:, sl], preferred_element_type=F32)
        a = (g * _sigmoid(g) * u).astype(BF16)
        f = f + jnp.dot(a, wd_ref[sl, :], preferred_element_type=F32)
    y_ref[...] = h2 + _rms(f, g_post_ref[...])


def _ffn(o, h1, w_xo, g_xpost, g_pre, w_gate, w_up, w_down, g_post):
    m, d = h1.shape
    d_ff = w_gate.shape[1]
    assert d_ff % FFN_CHUNK == 0
    tm = min(ROW_TILE, m)
    row = pl.BlockSpec((tm, d), lambda i: (i, 0))
    const = lambda a: pl.BlockSpec(a.shape, lambda i: (0, 0), pipeline_mode=pl.Buffered(1))
    est = (d * d + 3 * d * d_ff) * 2 + 2 * tm * d * (2 + 4 + 4) + 6 * tm * d * 4
    return pl.pallas_call(
        _ffn_kernel,
        grid=(m // tm,),
        in_specs=[row, row, const(w_xo), const(g_xpost), const(g_pre), const(w_gate), const(w_up),
                  const(w_down), const(g_post)],
        out_specs=row,
        out_shape=jax.ShapeDtypeStruct((m, d), F32),
        compiler_params=pltpu.CompilerParams(dimension_semantics=("parallel",),
                                             vmem_limit_bytes=_vmem_limit(est)),
        name="ffn",
    )(o, h1, w_xo, g_xpost, g_pre, w_gate, w_up, w_down, g_post)


def kernel(x_prompt, x_sample, mem_prompt, cache_k, cache_v, state_conv, cache_mem_k, cache_mem_v, page_table, rel_bias_table, norm_mix_pre, norm_mix_post, w_in, lambda_q1, lambda_k1, lambda_q2, lambda_k2, subln_g, dw_w, dw_b, conv_ln_g, conv_ln_b, beta_att, beta_conv, w_out, norm_x_pre, norm_x_post, w_xq, w_xk, w_xv, w_xo, norm_ffn_pre, norm_ffn_post, w_gate, w_up, w_down):
    assert w_in.shape[0] == 1, "single-layer trunk"
    bp, tp, d = x_prompt.shape
    bs, ts, _ = x_sample.shape
    assert ts <= MAX_NEW_TOKENS and tp >= CONV_K - 1
    n_mem = mem_prompt.shape[1]
    vec = lambda a: a[0].reshape(1, -1)
    wb = lambda a: a[0].astype(BF16)
    g_mix_pre, g_mix_post = vec(norm_mix_pre), vec(norm_mix_post)
    g_x_pre, g_x_post = vec(norm_x_pre), vec(norm_x_post)
    g_ffn_pre, g_ffn_post = vec(norm_ffn_pre), vec(norm_ffn_post)
    sub_g, b_att, b_conv = vec(subln_g), vec(beta_att), vec(beta_conv)
    c_b, c_g, c_bt = vec(dw_b), vec(conv_ln_g), vec(conv_ln_b)
    w_in_b, w_out_b, w_xq_b, w_xk_b, w_xv_b, w_xo_b = (wb(w) for w in (w_in, w_out, w_xq, w_xk, w_xv, w_xo))
    w_gate_b, w_up_b, w_down_b = wb(w_gate), wb(w_up), wb(w_down)
    dw = dw_w[0]

    blk = min(ATT_BLOCK, tp)
    lam_vecs = jnp.stack([lambda_q1[0], lambda_k1[0], lambda_q2[0], lambda_k2[0]])
    p_bias, s_bias, lam = _bias_tiles(rel_bias_table, lam_vecs, blk, ts)

    def tail(att, conv, h, mem_k, mem_v):
        b, t, _ = h.shape
        flat = lambda a: a.reshape(b * t, a.shape[-1])
        h1, qx = _mix_out(flat(att), flat(conv), flat(h), w_out_b, g_mix_post, g_x_pre, w_xq_b)
        o = _xattn(qx.reshape(b, t, d), mem_k, mem_v)
        y = _ffn(flat(o), h1, w_xo_b, g_x_post, g_ffn_pre, w_gate_b, w_up_b, w_down_b, g_ffn_post)
        return y.reshape(b, t, d)

    qt_p, kb_p, vt_p, kf_p, vf_p, glu_p = _in_proj(x_prompt, g_mix_pre, w_in_b, blk)
    att_p = _attn_prompt(qt_p, kb_p, vt_p, p_bias, lam, sub_g, b_att)
    conv_p = _conv_prompt(glu_p, jnp.zeros((bp, CONV_HALO, CONV_WIDTH), F32), dw, c_b, c_g, c_bt, b_conv)
    mk_p, mv_p = _mem_kv(mem_prompt, w_xk_b, w_xv_b)
    y_p = tail(att_p, conv_p, x_prompt, mk_p, mv_p)

    x_s = jnp.pad(x_sample, ((0, 0), (0, SAMPLE_ROWS - ts), (0, 0)))
    rows_s = bs * SAMPLE_ROWS
    qt_s, _, _, kf_s, vf_s, glu_s = _in_proj(x_s.reshape(1, rows_s, d), g_mix_pre, w_in_b, min(blk, rows_s))
    new_rows = ts * N_ATT_HEADS
    kf_s = kf_s.reshape(bs, SAMPLE_ROWS * N_ATT_HEADS, ATT_V_DIM)[:, :new_rows]
    vf_s = vf_s.reshape(bs, SAMPLE_ROWS * N_ATT_HEADS, ATT_V_DIM)[:, :new_rows]
    glu_s = glu_s.reshape(bs, SAMPLE_ROWS, CONV_WIDTH)[:, :ts]
    q_s = jnp.transpose(qt_s[0], (0, 1, 3, 2))
    q_th = jnp.transpose(q_s.reshape(N_ATT_HEADS, bs, SAMPLE_ROWS, ATT_V_DIM), (1, 2, 0, 3))
    q_th = jnp.pad(q_th[:, :ts], ((0, 0), (0, MAX_NEW_TOKENS - ts), (0, 0), (0, 0)))
    q_rows = jnp.repeat(q_th.reshape(bs, MAX_NEW_TOKENS * N_ATT_HEADS, ATT_V_DIM), 2, axis=1)
    pad_page = lambda a: jnp.pad(a, ((0, 0), (0, PAGE_SIZE - new_rows), (0, 0)))
    n_phys = cache_k.shape[1]
    pool = lambda c: c.reshape(n_phys, PAGE_SIZE * N_ATT_HEADS, ATT_V_DIM)
    att_s = _attn_sample(page_table, q_rows, pool(cache_k), pool(cache_v), pad_page(kf_s), pad_page(vf_s),
                         s_bias, lam, sub_g, b_att)
    xp_s = jnp.concatenate([state_conv[0], glu_s], axis=1)
    conv_s = _conv_sample(jnp.transpose(xp_s, (1, 0, 2)), ts, dw, c_b, c_g, c_bt, b_conv)
    conv_s = jnp.pad(jnp.transpose(conv_s, (1, 0, 2)), ((0, 0), (0, SAMPLE_ROWS - ts), (0, 0)))
    mem_s = lambda c: c[0].reshape(bs, n_mem, d)
    y_s = tail(att_s, conv_s, x_s, mem_s(cache_mem_k), mem_s(cache_mem_v))

    heads = lambda a: a.reshape(1, a.shape[0], a.shape[1] // N_ATT_HEADS, N_ATT_HEADS, ATT_V_DIM)
    mem_heads = lambda a: a.reshape(1, bp, n_mem, N_X_HEADS, d // N_X_HEADS)
    return (y_p, y_s[:, :ts],
            heads(kf_p), heads(vf_p), glu_p[None, :, tp - (CONV_K - 1):],
            mem_heads(mk_p), mem_heads(mv_p),
            heads(kf_s), heads(vf_s), xp_s[None, :, ts:])
```

```python
import functools
import math

import numpy as np
import jax
import jax.numpy as jnp
from jax import lax
from jax.experimental import pallas as pl
from jax.experimental.pallas import tpu as pltpu

F32 = jnp.float32
BF16 = jnp.bfloat16

DIFF_HEAD_DIM = 64
ATT_V_DIM = 2 * DIFF_HEAD_DIM
N_ATT_HEADS = 4
ATT_WIDTH = N_ATT_HEADS * ATT_V_DIM
CONV_WIDTH = 512
CONV_K = 31
N_BUCKETS = 32
MAX_DISTANCE = 128
N_X_HEADS = 4
PAGE_SIZE = 128
RMS_EPS = 1e-6
LN_EPS = 1e-5
ATT_SCALE = DIFF_HEAD_DIM ** -0.5
LOG2_E = math.log2(math.e)
LAM_INIT = 0.8 - 0.6 * math.exp(-0.3 * 0)
MASKED = -1e30
STALE_MAX_MARGIN = 30.0

V7X_VMEM_BYTES = 64 * 1024 * 1024
LANES = 128
SUBLANES = 8
BF16_ROWS = 16

ROW_TILE = 512
ATT_BLOCK = 256
ATT_HEADS_PER_STEP = 4
CONV_TILE = 256
CONV_CHUNK = 64
CONV_HALO = 32
PAGES_PER_STEP = 8
FFN_CHUNK = 256
SAMPLE_ROWS = 8
MAX_NEW_TOKENS = 4
SAMPLE_Q_ROWS = MAX_NEW_TOKENS * 2 * N_ATT_HEADS


def _vmem_limit(nbytes):
    return int(min(max(2 * nbytes, 16 * 1024 * 1024), V7X_VMEM_BYTES - 8 * 1024 * 1024))


def _rms(x, g):
    return x * lax.rsqrt(jnp.mean(x * x, axis=-1, keepdims=True) + RMS_EPS) * g


def _sigmoid(x):
    return 1.0 / (1.0 + jnp.exp(-x))


def _nt_dot(a, b):
    return lax.dot_general(a, b, (((1,), (1,)), ((), ())), preferred_element_type=F32)


def _bucket_np(n):
    n = np.maximum(n, 0)
    max_exact = N_BUCKETS // 2
    nf = np.maximum(n, 1).astype(np.float32)
    large = max_exact + (np.log(nf / max_exact) / math.log(MAX_DISTANCE / max_exact)
                         * (N_BUCKETS - max_exact)).astype(np.int32)
    large = np.minimum(large, N_BUCKETS - 1)
    return np.where(n < max_exact, n, large).astype(np.int32)


def _prompt_codes(blk):
    i = np.arange(blk)[None, :]
    j = np.arange(blk)[:, None]
    prev = _bucket_np(i - j + blk)
    diag = np.where(j > i, -1, _bucket_np(i - j))
    return np.concatenate([prev, diag]).astype(np.int32)


def _sample_codes(n_new):
    r = np.arange(SAMPLE_Q_ROWS)[:, None]
    c = np.arange(PAGE_SIZE * N_ATT_HEADS)[None, :]
    t, head = r // 8, (r // 2) % N_ATT_HEADS
    tok, key_head = c // N_ATT_HEADS, c % N_ATT_HEADS
    own = key_head == head
    far = np.where(own, N_BUCKETS - 1, -1)
    last = np.where(own, _bucket_np(t + PAGE_SIZE - tok), -1)
    new = np.where(own & (tok <= t) & (tok < n_new) & (c < PAGE_SIZE), _bucket_np(t - tok), -1)
    return np.stack([far, last, new]).astype(np.int32)


def _bias_kernel(tab_ref, lam_in_ref, pcode_ref, scode_ref, pbias_ref, sbias_ref, lam_ref):
    far = N_BUCKETS - 1

    def lookup(code, h):
        out = jnp.zeros(code.shape, F32)
        for b in range(far):
            out = jnp.where(code == b, (tab_ref[b, h] - tab_ref[far, h]) * LOG2_E, out)
        return jnp.where(code < 0, MASKED, out)

    for h in range(N_ATT_HEADS):
        pbias_ref[h] = lookup(pcode_ref[...], h)
    row_head = (lax.broadcasted_iota(jnp.int32, scode_ref.shape[1:], 0) >> 1) & (N_ATT_HEADS - 1)
    for i in range(scode_ref.shape[0]):
        code = scode_ref[i]
        out = jnp.zeros(code.shape, F32)
        for h in range(N_ATT_HEADS):
            out = jnp.where(row_head == h, lookup(code, h), out)
        sbias_ref[i] = out
    lv = lam_in_ref[...]
    d1 = jnp.sum(lv[0:1] * lv[1:2], axis=1, keepdims=True)
    d2 = jnp.sum(lv[2:3] * lv[3:4], axis=1, keepdims=True)
    lam = jnp.exp(d1) - jnp.exp(d2) + LAM_INIT
    lam_ref[...] = jnp.broadcast_to(lam, lam_ref.shape)


def _bias_tiles(table, lam_vecs, blk, n_new):
    pcode = jnp.asarray(_prompt_codes(blk))
    scode = jnp.asarray(_sample_codes(n_new))
    vm = pl.BlockSpec(memory_space=pltpu.VMEM)
    return pl.pallas_call(
        _bias_kernel,
        out_shape=(jax.ShapeDtypeStruct((N_ATT_HEADS, 2 * blk, blk), F32),
                   jax.ShapeDtypeStruct(scode.shape, F32),
                   jax.ShapeDtypeStruct((SUBLANES, LANES), F32)),
        in_specs=[pl.BlockSpec(memory_space=pltpu.SMEM), vm, vm, vm],
        out_specs=(vm, vm, vm),
        name="bias_tiles",
    )(table, lam_vecs, pcode, scode)


def _in_proj_kernel(x_ref, g_ref, w_ref, qt_ref, kb_ref, vt_ref, kf_ref, vf_ref, glu_ref):
    xn = _rms(x_ref[...], g_ref[...]).astype(BF16)

    def cols(c):
        return jnp.dot(xn, w_ref[:, c * ATT_WIDTH:(c + 1) * ATT_WIDTH], preferred_element_type=F32)

    tm = xn.shape[0]
    blk = qt_ref.shape[-1]
    q = cols(0) * (ATT_SCALE * LOG2_E)
    k = cols(1)
    v = cols(2)
    kb = k.astype(BF16)
    for h in range(N_ATT_HEADS):
        sl = slice(h * ATT_V_DIM, (h + 1) * ATT_V_DIM)
        kb_ref[h] = kb[:, sl]
        for c in range(tm // blk):
            rows = slice(c * blk, (c + 1) * blk)
            qt_ref[h, c] = q[rows, sl].T.astype(BF16)
            vt_ref[h, c] = v[rows, sl].T.astype(BF16)
        kf_ref[pl.ds(h, tm, stride=N_ATT_HEADS), :] = k[:, sl]
        vf_ref[pl.ds(h, tm, stride=N_ATT_HEADS), :] = v[:, sl]
    glu_ref[...] = cols(3) * _sigmoid(cols(4))


def _in_proj(x, gain, w_in, blk):
    b, t, d = x.shape
    tm = min(ROW_TILE, t)
    n_cols = w_in.shape[1]
    assert tm % blk == 0
    row = lambda w: pl.BlockSpec((None, tm, w), lambda i, j: (i, j, 0))
    heads = pl.BlockSpec((None, N_ATT_HEADS, tm, ATT_V_DIM), lambda i, j: (i, 0, j, 0))
    hm = jax.ShapeDtypeStruct((b, N_ATT_HEADS, t, ATT_V_DIM), BF16)
    heads_t = pl.BlockSpec((None, N_ATT_HEADS, tm // blk, ATT_V_DIM, blk), lambda i, j: (i, 0, j, 0, 0))
    hm_t = jax.ShapeDtypeStruct((b, N_ATT_HEADS, t // blk, ATT_V_DIM, blk), BF16)
    flat = jax.ShapeDtypeStruct((b, t * N_ATT_HEADS, ATT_V_DIM), F32)
    tok_head = pl.BlockSpec((None, tm * N_ATT_HEADS, ATT_V_DIM), lambda i, j: (i, j, 0))
    est = 2 * d * n_cols * 2 + 2 * tm * (d * 4 + 3 * ATT_WIDTH * 2 + 3 * ATT_WIDTH * 4) + 6 * tm * ATT_WIDTH * 4
    return pl.pallas_call(
        _in_proj_kernel,
        grid=(b, t // tm),
        in_specs=[row(d),
                  pl.BlockSpec((1, d), lambda i, j: (0, 0)),
                  pl.BlockSpec((d, n_cols), lambda i, j: (0, 0))],
        out_specs=(heads_t, heads, heads_t, tok_head, tok_head, row(CONV_WIDTH)),
        out_shape=(hm_t, hm, hm_t, flat, flat, jax.ShapeDtypeStruct((b, t, CONV_WIDTH), F32)),
        compiler_params=pltpu.CompilerParams(dimension_semantics=("parallel", "parallel"),
                                             vmem_limit_bytes=_vmem_limit(est)),
        name="in_proj",
    )(x, gain, w_in)


def _conv_post(y, b_ref, g_ref, bt_ref, beta_ref):
    y = y + b_ref[...]
    mu = jnp.mean(y, axis=-1, keepdims=True)
    yc = y - mu
    yn = yc * lax.rsqrt(jnp.mean(yc * yc, axis=-1, keepdims=True) + LN_EPS) * g_ref[...] + bt_ref[...]
    return yn * _sigmoid(yn) * beta_ref[...]


def _conv_prompt_kernel(glu_ref, prev_ref, w_ref, b_ref, g_ref, bt_ref, beta_ref, o_ref, buf, ybuf):
    tt = glu_ref.shape[0]
    n_slab = buf.shape[0]
    first = pl.program_id(1) == 0
    slabs = [slice(s * LANES, (s + 1) * LANES) for s in range(n_slab)]

    @pl.when(first)
    def _():
        for s in range(n_slab):
            buf[s, 0:CONV_HALO] = prev_ref[:, slabs[s]]

    @pl.when(jnp.logical_not(first))
    def _():
        for s in range(n_slab):
            buf[s, 0:CONV_HALO] = buf[s, tt:tt + CONV_HALO]

    for s in range(n_slab):
        buf[s, CONV_HALO:CONV_HALO + tt] = glu_ref[:, slabs[s]]

    shift = CONV_HALO - (CONV_K - 1)
    rc = min(CONV_CHUNK, tt // 2)
    for c0 in range(0, tt, 2 * rc):
        for parity in range(2):
            start = c0 + parity
            accs = []
            for s in range(n_slab):
                acc = jnp.zeros((rc, LANES), F32)
                for j in range(CONV_K):
                    x = buf[s, pl.ds(start + j + shift, rc, stride=2), :]
                    acc = acc + w_ref[j:j + 1, slabs[s]] * x
                accs.append(acc)
            y = _conv_post(jnp.concatenate(accs, axis=1), b_ref, g_ref, bt_ref, beta_ref)
            for s in range(n_slab):
                ybuf[s, pl.ds(start, rc, stride=2), :] = y[:, slabs[s]]
    for s in range(n_slab):
        o_ref[:, slabs[s]] = ybuf[s].astype(o_ref.dtype)


def _conv_prompt(glu, prev, dw_w, dw_b, ln_g, ln_b, beta):
    b, t, c = glu.shape
    tt = min(CONV_TILE, t)
    vec = pl.BlockSpec((1, c), lambda i, j: (0, 0))
    return pl.pallas_call(
        _conv_prompt_kernel,
        grid=(b, t // tt),
        in_specs=[pl.BlockSpec((None, tt, c), lambda i, j: (i, j, 0)),
                  pl.BlockSpec((None, CONV_HALO, c), lambda i, j: (i, 0, 0)),
                  pl.BlockSpec(dw_w.shape, lambda i, j: (0, 0)),
                  vec, vec, vec, vec],
        out_specs=pl.BlockSpec((None, tt, c), lambda i, j: (i, j, 0)),
        out_shape=jax.ShapeDtypeStruct((b, t, c), BF16),
        scratch_shapes=[pltpu.VMEM((c // LANES, tt + CONV_HALO, LANES), F32),
                        pltpu.VMEM((c // LANES, tt, LANES), F32)],
        compiler_params=pltpu.CompilerParams(dimension_semantics=("parallel", "arbitrary")),
        name="conv_prompt",
    )(glu, prev, dw_w, dw_b, ln_g, ln_b, beta)


def _conv_sample_kernel(xp_ref, w_ref, b_ref, g_ref, bt_ref, beta_ref, o_ref):
    n_t = o_ref.shape[0]
    for t in range(n_t):
        acc = jnp.zeros(xp_ref.shape[1:], F32)
        for j in range(CONV_K):
            acc = acc + w_ref[j:j + 1, :] * xp_ref[t + j]
        o_ref[t] = _conv_post(acc, b_ref, g_ref, bt_ref, beta_ref)


def _conv_sample(xp_t, n_t, dw_w, dw_b, ln_g, ln_b, beta):
    vm = pl.BlockSpec(memory_space=pltpu.VMEM)
    return pl.pallas_call(
        _conv_sample_kernel,
        out_shape=jax.ShapeDtypeStruct((n_t,) + xp_t.shape[1:], F32),
        in_specs=[vm] * 6,
        out_specs=vm,
        name="conv_sample",
    )(xp_t, dw_w, dw_b, ln_g, ln_b, beta)


def _softmax_update(s, v_dot, m_sc, l_sc, acc_sc):
    width = s.shape[1]
    m_prev = m_sc[...]
    m_new = jnp.maximum(m_prev, jnp.max(s, axis=1, keepdims=True))
    alpha = jnp.exp2(m_prev - m_new)
    p = jnp.exp2(s - jnp.concatenate([m_new] * (width // LANES), axis=1))
    l_sc[...] = alpha * l_sc[...] + jnp.sum(p, axis=1, keepdims=True)
    reps = acc_sc.shape[1] // LANES
    acc_sc[...] = jnp.concatenate([alpha] * reps, axis=1) * acc_sc[...] + v_dot(p.astype(BF16))
    m_sc[...] = m_new


def _subln(att, g, beta):
    return _rms(att, g) * (1.0 - LAM_INIT) * beta


def _sample_page_copies(pt_ref, pool_k, pool_v, kbuf, vbuf, sem, chunk, slot, chunks_per_seq):
    seq = chunk // chunks_per_seq
    first_page = (chunk % chunks_per_seq) * PAGES_PER_STEP
    copies = []
    for i in range(PAGES_PER_STEP):
        page = pt_ref[seq, first_page + i]
        copies.append(pltpu.make_async_copy(pool_k.at[page], kbuf.at[slot, i], sem.at[0, slot]))
        copies.append(pltpu.make_async_copy(pool_v.at[page], vbuf.at[slot, i], sem.at[1, slot]))
    return copies


def _attn_kernel(pt_ref, qt_ref, k_ref, vt_ref, bias_ref, lam_ref, g_ref, beta_ref,
                 sq_ref, pool_k, pool_v, kn_ref, vn_ref, sbias_ref, o_ref, so_ref,
                 m_sc, acc_sc, kbuf, vbuf, sem, sm_sc, sl_sc, sacc_sc, cnt_ref):
    n_heads, e, blk = qt_ref.shape
    qi = pl.program_id(2)
    n_seq_s, chunks_per_seq = so_ref.shape[0], pt_ref.shape[1] // PAGES_PER_STEP
    n_chunks = n_seq_s * chunks_per_seq
    first_grid_step = jnp.logical_and(pl.program_id(0) == 0, qi == 0)
    last_grid_step = jnp.logical_and(pl.program_id(0) == pl.num_programs(0) - 1, qi == pl.num_programs(2) - 1)
    ring = (pt_ref, pool_k, pool_v, kbuf, vbuf, sem)

    @pl.when(first_grid_step)
    def _():
        cnt_ref[0] = 0
        for cp in _sample_page_copies(*ring, 0, 0, chunks_per_seq):
            cp.start()

    s_lane = lax.broadcasted_iota(jnp.int32, sq_ref.shape[1:], 1)
    s_row = lax.broadcasted_iota(jnp.int32, sq_ref.shape[1:], 0)

    def sample_fetch():
        n = cnt_ref[0]
        slot = n & 1
        valid = n < n_chunks
        chunk = jnp.minimum(n, n_chunks - 1)
        seq = chunk // chunks_per_seq
        c = chunk % chunks_per_seq
        for cp in _sample_page_copies(*ring, chunk, slot, chunks_per_seq):
            cp.wait()
        for cp in _sample_page_copies(*ring, jnp.minimum(n + 1, n_chunks - 1), 1 - slot, chunks_per_seq):
            cp.start()
        cnt_ref[0] = n + 1
        return slot, seq, c, valid

    def sample_logits(slot, seq, c, valid):
        q = sq_ref[seq]
        qm = jnp.where((s_lane >> 6) == (s_row & 1), q, jnp.zeros_like(q))
        gate = jnp.where(valid, 0.0, MASKED)
        far_bias = sbias_ref[0] + gate
        last_bias = jnp.where(c == chunks_per_seq - 1, sbias_ref[1] + gate, far_bias)
        s = jnp.concatenate(
            [_nt_dot(qm, kbuf[slot, i].astype(BF16)) + (last_bias if i == PAGES_PER_STEP - 1 else far_bias)
             for i in range(PAGES_PER_STEP)], axis=1)
        return qm, s

    def sample_update(slot, seq, c, valid, qm, s):
        fresh = c == 0
        m_prev = jnp.where(fresh, MASKED, sm_sc[...])
        m_new = jnp.maximum(m_prev, jnp.max(s, axis=1, keepdims=True))
        alpha = jnp.exp2(m_prev - m_new)
        p = jnp.exp2(s - jnp.concatenate([m_new] * (s.shape[1] // LANES), axis=1))
        sl_sc[...] = alpha * jnp.where(fresh, 0.0, sl_sc[...]) + jnp.sum(p, axis=1, keepdims=True)
        pb = p.astype(BF16)
        cols = kbuf.shape[2]
        pv = None
        for i in range(PAGES_PER_STEP):
            part = jnp.dot(pb[:, i * cols:(i + 1) * cols], vbuf[slot, i].astype(BF16),
                           preferred_element_type=F32)
            pv = part if pv is None else pv + part
        sacc_sc[...] = alpha * jnp.where(fresh, 0.0, sacc_sc[...]) + pv
        sm_sc[...] = m_new
        return seq, qm, jnp.logical_and(valid, c == chunks_per_seq - 1)

    def sample_finish(seq, qm, seq_done):
        @pl.when(seq_done)
        def _():
            n_new = kn_ref.shape[1]
            s_new = _nt_dot(qm, kn_ref[seq].astype(BF16)) + sbias_ref[2][:, :n_new]
            _softmax_update(s_new, lambda pn: jnp.dot(pn, vn_ref[seq].astype(BF16), preferred_element_type=F32),
                            sm_sc, sl_sc, sacc_sc)
            sign = jnp.where((s_row & 1) == 0, 1.0, -lam_ref[0:1, 0:1])
            z = sacc_sc[...] / sl_sc[...] * sign
            out_row = lax.broadcasted_iota(jnp.int32, (so_ref.shape[1], ATT_V_DIM), 0)
            for h in range(N_ATT_HEADS):
                att = jnp.zeros(out_row.shape, F32)
                for t in range(MAX_NEW_TOKENS):
                    r0 = t * 8 + h * 2
                    att = jnp.where(out_row == t, jnp.broadcast_to(z[r0:r0 + 1] + z[r0 + 1:r0 + 2], att.shape),
                                    att)
                sl = slice(h * ATT_V_DIM, (h + 1) * ATT_V_DIM)
                so_ref[seq, :, sl] = _subln(att, g_ref[...], beta_ref[:, sl])

    ones_rows = jnp.ones((acc_sc.shape[2] - e, blk), BF16)
    sub = lax.broadcasted_iota(jnp.int32, qt_ref.shape[1:], 0)
    qst = []
    for h in range(n_heads):
        qt = qt_ref[h]
        zero = jnp.zeros_like(qt)
        qst.append(jnp.concatenate([jnp.where(sub < DIFF_HEAD_DIM, qt, zero),
                                    jnp.where(sub >= DIFF_HEAD_DIM, qt, zero)], axis=1))
    m_sc[...] = jnp.full(m_sc.shape, MASKED, F32)
    acc_sc[0] = jnp.zeros(acc_sc.shape[1:], F32)

    def step(j, n_blk, biased, stale_max, src):
        rows = pl.ds(pl.multiple_of(j * blk, blk), n_blk * blk)
        dst = 1 - src
        chunk_state = sample_fetch()

        def logits(h):
            st = jnp.dot(k_ref[h, rows, :], qst[h], preferred_element_type=F32)
            if biased:
                bias = bias_ref[h, (2 - n_blk) * blk:, :]
                st = st + jnp.concatenate([bias, bias], axis=1)
            return st

        def weighted_values(h, pb):
            pv = None
            for i in range(n_blk):
                vt_aug = jnp.concatenate([vt_ref[h, j + i], ones_rows], axis=0)
                part = jnp.dot(vt_aug, pb[i * blk:(i + 1) * blk], preferred_element_type=F32)
                pv = part if pv is None else pv + part
            return pv

        def exact_update(h, st):
            m_prev = m_sc[h]
            m_new = jnp.maximum(m_prev, jnp.max(st, axis=0, keepdims=True))
            alpha = jnp.exp2(m_prev - m_new)
            pv = weighted_values(h, jnp.exp2(st - m_new).astype(BF16))
            acc_sc[dst, h] = alpha * acc_sc[src, h] + pv
            m_sc[h] = m_new

        def stale_update(h, st):
            m_ref = m_sc[h]
            pv = weighted_values(h, jnp.exp2(st - m_ref).astype(BF16))
            acc_sc[dst, h] = acc_sc[src, h] + pv
            return jnp.max(st, axis=0, keepdims=True) - m_ref

        ahead = 2
        pending = [logits(h) for h in range(min(ahead, n_heads))]
        chunk_logits = sample_logits(*chunk_state)
        excess = None
        for h in range(n_heads):
            if h + ahead < n_heads:
                pending.append(logits(h + ahead))
            if stale_max:
                over = stale_update(h, pending[h])
                excess = over if excess is None else jnp.maximum(excess, over)
            else:
                exact_update(h, pending[h])
            if h == 0:
                sample_done = sample_update(*chunk_state, *chunk_logits)

        sample_finish(*sample_done)
        if stale_max:
            @pl.when(jnp.max(excess) > STALE_MAX_MARGIN)
            def _():
                for h in range(n_heads):
                    exact_update(h, logits(h))

    odd = (qi + 1) & 1
    n_pairs = jnp.maximum(((qi + 1) >> 1) - 1, 0)

    @pl.when(qi == 0)
    def _():
        step(0, 1, True, False, 0)

    @pl.when(qi > 0)
    def _():
        step(qi - 1, 2, True, False, 0)

    def far_pair(i, carry):
        step(odd + 2 * i, 2, False, True, (i + 1) & 1)
        return carry

    lax.fori_loop(0, n_pairs, far_pair, 0)
    single = jnp.logical_and(qi > 0, odd == 1)

    @pl.when(single)
    def _():
        step(0, 1, False, True, (n_pairs + 1) & 1)

    res = (n_pairs + 1 + single.astype(jnp.int32)) & 1
    for h in range(n_heads):
        ot = acc_sc[res, h, 0:e] / acc_sc[res, h, e:e + 1]
        att = (ot[:, :blk] - lam_ref[0:1, 0:1] * ot[:, blk:]).T
        sl = slice(h * ATT_V_DIM, (h + 1) * ATT_V_DIM)
        o_ref[:, sl] = _subln(att, g_ref[...], beta_ref[:, sl]).astype(o_ref.dtype)

    @pl.when(last_grid_step)
    def _():
        def drain(i, carry):
            chunk_state = sample_fetch()
            sample_finish(*sample_update(*chunk_state, *sample_logits(*chunk_state)))
            return carry

        lax.fori_loop(0, jnp.maximum(n_chunks - cnt_ref[0], 0), drain, 0)
        n = cnt_ref[0]
        for cp in _sample_page_copies(*ring, jnp.minimum(n, n_chunks - 1), n & 1, chunks_per_seq):
            cp.wait()


def _attention(qt, k, vt, bias, lam, subln_g, beta_att, page_table, q_rows, pool_k, pool_v, k_new, v_new,
               s_bias):
    b, h, t, e = k.shape
    blk = bias.shape[-1]
    n_blk = t // blk
    hs = ATT_HEADS_PER_STEP
    assert hs == h, "the sample epilogue reads every head's beta from the prompt block"
    n_seq, n_pages = page_table.shape
    rows = q_rows.shape[1]
    page_rows = pool_k.shape[1]
    npg = PAGES_PER_STEP
    assert n_pages % npg == 0
    once = pl.Buffered(1)
    whole = lambda a: pl.BlockSpec(a.shape, lambda i, j, n, pt: (0,) * a.ndim, pipeline_mode=once)
    sample_out = jax.ShapeDtypeStruct((n_seq, SAMPLE_ROWS, h * e), F32)
    grid_spec = pltpu.PrefetchScalarGridSpec(
        num_scalar_prefetch=1,
        grid=(b, h // hs, n_blk),
        in_specs=[pl.BlockSpec((None, hs, None, e, blk), lambda i, j, n, pt: (i, j, n, 0, 0)),
                  pl.BlockSpec((None, hs, t, e), lambda i, j, n, pt: (i, j, 0, 0), pipeline_mode=once),
                  pl.BlockSpec((None, hs, n_blk, e, blk), lambda i, j, n, pt: (i, j, 0, 0, 0), pipeline_mode=once),
                  pl.BlockSpec((hs, 2 * blk, blk), lambda i, j, n, pt: (j, 0, 0), pipeline_mode=once),
                  whole(lam), whole(subln_g),
                  pl.BlockSpec((1, hs * e), lambda i, j, n, pt: (0, j)),
                  whole(q_rows),
                  pl.BlockSpec(memory_space=pl.ANY), pl.BlockSpec(memory_space=pl.ANY),
                  whole(k_new), whole(v_new), whole(s_bias)],
        out_specs=(pl.BlockSpec((None, blk, hs * e), lambda i, j, n, pt: (i, n, j)),
                   pl.BlockSpec(sample_out.shape, lambda i, j, n, pt: (0, 0, 0))),
        scratch_shapes=[pltpu.VMEM((hs, 1, 2 * blk), F32),
                        pltpu.VMEM((2, hs, e + BF16_ROWS, 2 * blk), F32),
                        pltpu.VMEM((2, npg, page_rows, e), F32),
                        pltpu.VMEM((2, npg, page_rows, e), F32),
                        pltpu.SemaphoreType.DMA((2, 2)),
                        pltpu.VMEM((rows, LANES), F32), pltpu.VMEM((rows, LANES), F32),
                        pltpu.VMEM((rows, e), F32),
                        pltpu.SMEM((1,), jnp.int32)],
    )
    est = (hs * (2 * t * e * 2) + hs * 2 * blk * blk * 4 + 4 * hs * 2 * blk * 2 * blk * 4
           + 2 * 2 * npg * page_rows * e * 4 + 2 * k_new.size * 4 + 8 * rows * npg * page_rows * 4)
    return pl.pallas_call(
        _attn_kernel,
        grid_spec=grid_spec,
        out_shape=(jax.ShapeDtypeStruct((b, t, h * e), BF16), sample_out),
        compiler_params=pltpu.CompilerParams(dimension_semantics=("arbitrary", "arbitrary", "arbitrary"),
                                             vmem_limit_bytes=_vmem_limit(est)),
        name="attention",
    )(page_table, qt, k, vt, bias, lam, subln_g, beta_att, q_rows, pool_k, pool_v, k_new, v_new, s_bias)


def _mem_kv_kernel(mem_ref, wk_ref, wv_ref, k_ref, v_ref):
    m = mem_ref[...].astype(BF16)
    k_ref[...] = jnp.dot(m, wk_ref[...], preferred_element_type=F32)
    v_ref[...] = jnp.dot(m, wv_ref[...], preferred_element_type=F32)


def _mem_kv(mem, w_xk, w_xv):
    b, n, d = mem.shape
    blk = pl.BlockSpec((None, n, d), lambda i: (i, 0, 0))
    w = pl.BlockSpec((d, d), lambda i: (0, 0))
    out = jax.ShapeDtypeStruct((b, n, d), F32)
    return pl.pallas_call(
        _mem_kv_kernel,
        grid=(b,),
        in_specs=[blk, w, w],
        out_specs=(blk, blk),
        out_shape=(out, out),
        compiler_params=pltpu.CompilerParams(dimension_semantics=("parallel",),
                                             vmem_limit_bytes=_vmem_limit(4 * d * d * 2 + 6 * n * d * 4)),
        name="mem_kv",
    )(mem, w_xk, w_xv)


def _mix_out_kernel(att_ref, conv_ref, h_ref, wo_ref, g_post_ref, g_x_ref, wq_ref, h1_ref, qx_ref):
    half = att_ref.shape[1]
    mo = (jnp.dot(att_ref[...].astype(BF16), wo_ref[0:half, :], preferred_element_type=F32)
          + jnp.dot(conv_ref[...].astype(BF16), wo_ref[half:, :], preferred_element_type=F32))
    h1 = h_ref[...] + _rms(mo, g_post_ref[...])
    h1_ref[...] = h1
    xn = _rms(h1, g_x_ref[...]).astype(BF16)
    x_scale = (wq_ref.shape[1] // N_X_HEADS) ** -0.5
    qx_ref[...] = (jnp.dot(xn, wq_ref[...], preferred_element_type=F32) * x_scale).astype(BF16)


def _mix_out(att, conv, h, w_out, g_post, g_x, w_xq):
    m, d = h.shape
    tm = min(ROW_TILE, m)
    row = lambda w: pl.BlockSpec((tm, w), lambda i: (i, 0))
    const = lambda a: pl.BlockSpec(a.shape, lambda i: (0, 0))
    est = 2 * 2 * d * d * 2 + 2 * tm * (2 * d * 4 + d * 2 + att.shape[1] * 6) + 4 * tm * d * 4
    return pl.pallas_call(
        _mix_out_kernel,
        grid=(m // tm,),
        in_specs=[row(att.shape[1]), row(conv.shape[1]), row(d), const(w_out), const(g_post), const(g_x),
                  const(w_xq)],
        out_specs=(row(d), row(d)),
        out_shape=(jax.ShapeDtypeStruct((m, d), F32), jax.ShapeDtypeStruct((m, d), BF16)),
        compiler_params=pltpu.CompilerParams(dimension_semantics=("parallel",),
                                             vmem_limit_bytes=_vmem_limit(est)),
        name="mix_out",
    )(att, conv, h, w_out, g_post, g_x, w_xq)


def _xattn_kernel(q_ref, mk_ref, mv_ref, o_ref):
    d = q_ref.shape[1]
    hd = d // N_X_HEADS
    for h in range(N_X_HEADS):
        sl = slice(h * hd, (h + 1) * hd)
        s = _nt_dot(q_ref[:, sl], mk_ref[:, sl].astype(BF16))
        p = jnp.exp(s - jnp.max(s, axis=1, keepdims=True))
        l = jnp.sum(p, axis=1, keepdims=True)
        o = jnp.dot(p.astype(BF16), mv_ref[:, sl].astype(BF16), preferred_element_type=F32)
        o_ref[:, sl] = (o / l).astype(o_ref.dtype)


def _xattn(qx, mem_k, mem_v):
    b, t, d = qx.shape
    n = mem_k.shape[1]
    tm = min(ROW_TILE, t)
    row = pl.BlockSpec((None, tm, d), lambda i, j: (i, j, 0))
    mem = pl.BlockSpec((None, n, d), lambda i, j: (i, 0, 0))
    return pl.pallas_call(
        _xattn_kernel,
        grid=(b, t // tm),
        in_specs=[row, mem, mem],
        out_specs=row,
        out_shape=jax.ShapeDtypeStruct((b, t, d), BF16),
        compiler_params=pltpu.CompilerParams(dimension_semantics=("parallel", "parallel"),
                                             vmem_limit_bytes=_vmem_limit(4 * n * d * 4 + 8 * tm * d * 4)),
        name="xattn",
    )(qx, mem_k, mem_v)


def _ffn_kernel(o_ref, h1_ref, wxo_ref, g_xpost_ref, g_pre_ref, wg_ref, wu_ref, wd_ref, g_post_ref, y_ref):
    h2 = h1_ref[...] + _rms(jnp.dot(o_ref[...], wxo_ref[...], preferred_element_type=F32), g_xpost_ref[...])
    xf = _rms(h2, g_pre_ref[...]).astype(BF16)
    d_ff = wg_ref.shape[1]
    f = jnp.zeros(h2.shape, F32)
    for c0 in range(0, d_ff, FFN_CHUNK):
        sl = slice(c0, c0 + FFN_CHUNK)
        g = jnp.dot(xf, wg_ref[:, sl], preferred_element_type=F32)
        u = jnp.dot(xf, wu_ref[:, sl], preferred_element_type=F32)
        a = (g * _sigmoid(g) * u).astype(BF16)
        f = f + jnp.dot(a, wd_ref[sl, :], preferred_element_type=F32)
    y_ref[...] = h2 + _rms(f, g_post_ref[...])


def _ffn(o, h1, w_xo, g_xpost, g_pre, w_gate, w_up, w_down, g_post):
    m, d = h1.shape
    d_ff = w_gate.shape[1]
    assert d_ff % FFN_CHUNK == 0
    tm = min(ROW_TILE, m)
    row = pl.BlockSpec((tm, d), lambda i: (i, 0))
    const = lambda a: pl.BlockSpec(a.shape, lambda i: (0, 0), pipeline_mode=pl.Buffered(1))
    est = (d * d + 3 * d * d_ff) * 2 + 2 * tm * d * (2 + 4 + 4) + 6 * tm * d * 4
    return pl.pallas_call(
        _ffn_kernel,
        grid=(m // tm,),
        in_specs=[row, row, const(w_xo), const(g_xpost), const(g_pre), const(w_gate), const(w_up),
                  const(w_down), const(g_post)],
        out_specs=row,
        out_shape=jax.ShapeDtypeStruct((m, d), F32),
        compiler_params=pltpu.CompilerParams(dimension_semantics=("parallel",),
                                             vmem_limit_bytes=_vmem_limit(est)),
        name="ffn",
    )(o, h1, w_xo, g_xpost, g_pre, w_gate, w_up, w_down, g_post)


def kernel(x_prompt, x_sample, mem_prompt, cache_k, cache_v, state_conv, cache_mem_k, cache_mem_v, page_table, rel_bias_table, norm_mix_pre, norm_mix_post, w_in, lambda_q1, lambda_k1, lambda_q2, lambda_k2, subln_g, dw_w, dw_b, conv_ln_g, conv_ln_b, beta_att, beta_conv, w_out, norm_x_pre, norm_x_post, w_xq, w_xk, w_xv, w_xo, norm_ffn_pre, norm_ffn_post, w_gate, w_up, w_down):
    assert w_in.shape[0] == 1, "single-layer trunk"
    bp, tp, d = x_prompt.shape
    bs, ts, _ = x_sample.shape
    assert ts <= MAX_NEW_TOKENS and tp >= CONV_K - 1
    n_mem = mem_prompt.shape[1]
    vec = lambda a: a[0].reshape(1, -1)
    wb = lambda a: a[0].astype(BF16)
    g_mix_pre, g_mix_post = vec(norm_mix_pre), vec(norm_mix_post)
    g_x_pre, g_x_post = vec(norm_x_pre), vec(norm_x_post)
    g_ffn_pre, g_ffn_post = vec(norm_ffn_pre), vec(norm_ffn_post)
    sub_g, b_att, b_conv = vec(subln_g), vec(beta_att), vec(beta_conv)
    c_b, c_g, c_bt = vec(dw_b), vec(conv_ln_g), vec(conv_ln_b)
    w_in_b, w_out_b, w_xq_b, w_xk_b, w_xv_b, w_xo_b = (wb(w) for w in (w_in, w_out, w_xq, w_xk, w_xv, w_xo))
    w_gate_b, w_up_b, w_down_b = wb(w_gate), wb(w_up), wb(w_down)
    dw = dw_w[0]

    blk = min(ATT_BLOCK, tp)
    lam_vecs = jnp.stack([lambda_q1[0], lambda_k1[0], lambda_q2[0], lambda_k2[0]])
    p_bias, s_bias, lam = _bias_tiles(rel_bias_table, lam_vecs, blk, ts)

    def tail(att, conv, h, mem_k, mem_v):
        b, t, _ = h.shape
        flat = lambda a: a.reshape(b * t, a.shape[-1])
        h1, qx = _mix_out(flat(att), flat(conv), flat(h), w_out_b, g_mix_post, g_x_pre, w_xq_b)
        o = _xattn(qx.reshape(b, t, d), mem_k, mem_v)
        y = _ffn(flat(o), h1, w_xo_b, g_x_post, g_ffn_pre, w_gate_b, w_up_b, w_down_b, g_ffn_post)
        return y.reshape(b, t, d)

    qt_p, kb_p, vt_p, kf_p, vf_p, glu_p = _in_proj(x_prompt, g_mix_pre, w_in_b, blk)
    x_s = jnp.pad(x_sample, ((0, 0), (0, SAMPLE_ROWS - ts), (0, 0)))
    rows_s = bs * SAMPLE_ROWS
    qt_s, _, _, kf_s, vf_s, glu_s = _in_proj(x_s.reshape(1, rows_s, d), g_mix_pre, w_in_b, min(blk, rows_s))
    new_rows = ts * N_ATT_HEADS
    kf_s = kf_s.reshape(bs, SAMPLE_ROWS * N_ATT_HEADS, ATT_V_DIM)[:, :new_rows]
    vf_s = vf_s.reshape(bs, SAMPLE_ROWS * N_ATT_HEADS, ATT_V_DIM)[:, :new_rows]
    glu_s = glu_s.reshape(bs, SAMPLE_ROWS, CONV_WIDTH)[:, :ts]
    q_s = jnp.transpose(qt_s[0], (0, 1, 3, 2))
    q_th = jnp.transpose(q_s.reshape(N_ATT_HEADS, bs, SAMPLE_ROWS, ATT_V_DIM), (1, 2, 0, 3))
    q_th = jnp.pad(q_th[:, :ts], ((0, 0), (0, MAX_NEW_TOKENS - ts), (0, 0), (0, 0)))
    q_rows = jnp.repeat(q_th.reshape(bs, MAX_NEW_TOKENS * N_ATT_HEADS, ATT_V_DIM), 2, axis=1)
    pad_page = lambda a: jnp.pad(a, ((0, 0), (0, PAGE_SIZE - new_rows), (0, 0)))
    n_phys = cache_k.shape[1]
    pool = lambda c: c.reshape(n_phys, PAGE_SIZE * N_ATT_HEADS, ATT_V_DIM)

    att_p, att_s = _attention(qt_p, kb_p, vt_p, p_bias, lam, sub_g, b_att, page_table, q_rows,
                              pool(cache_k), pool(cache_v), pad_page(kf_s), pad_page(vf_s), s_bias)

    conv_p = _conv_prompt(glu_p, jnp.zeros((bp, CONV_HALO, CONV_WIDTH), F32), dw, c_b, c_g, c_bt, b_conv)
    mk_p, mv_p = _mem_kv(mem_prompt, w_xk_b, w_xv_b)
    y_p = tail(att_p, conv_p, x_prompt, mk_p, mv_p)

    xp_s = jnp.concatenate([state_conv[0], glu_s], axis=1)
    conv_s = _conv_sample(jnp.transpose(xp_s, (1, 0, 2)), ts, dw, c_b, c_g, c_bt, b_conv)
    conv_s = jnp.pad(jnp.transpose(conv_s, (1, 0, 2)), ((0, 0), (0, SAMPLE_ROWS - ts), (0, 0)))
    mem_s = lambda c: c[0].reshape(bs, n_mem, d)
    y_s = tail(att_s, conv_s, x_s, mem_s(cache_mem_k), mem_s(cache_mem_v))

    heads = lambda a: a.reshape(1, a.shape[0], a.shape[1] // N_ATT_HEADS, N_ATT_HEADS, ATT_V_DIM)
    mem_heads = lambda a: a.reshape(1, bp, n_mem, N_X_HEADS, d // N_X_HEADS)
    return (y_p, y_s[:, :ts],
            heads(kf_p), heads(vf_p), glu_p[None, :, tp - (CONV_K - 1):],
            mem_heads(mk_p), mem_heads(mv_p),
            heads(kf_s), heads(vf_s), xp_s[None, :, ts:])
```

```python
import functools
import math

import numpy as np
import jax
import jax.numpy as jnp
from jax import lax
from jax.experimental import pallas as pl
from jax.experimental.pallas import tpu as pltpu

F32 = jnp.float32
BF16 = jnp.bfloat16

DIFF_HEAD_DIM = 64
ATT_V_DIM = 2 * DIFF_HEAD_DIM
N_ATT_HEADS = 4
ATT_WIDTH = N_ATT_HEADS * ATT_V_DIM
CONV_WIDTH = 512
CONV_K = 31
N_BUCKETS = 32
MAX_DISTANCE = 128
N_X_HEADS = 4
PAGE_SIZE = 128
RMS_EPS = 1e-6
LN_EPS = 1e-5
ATT_SCALE = DIFF_HEAD_DIM ** -0.5
LOG2_E = math.log2(math.e)
LAM_INIT = 0.8 - 0.6 * math.exp(-0.3 * 0)
MASKED = -1e30
STALE_MAX_MARGIN = 30.0

V7X_VMEM_BYTES = 64 * 1024 * 1024
LANES = 128
SUBLANES = 8
BF16_ROWS = 16

ROW_TILE = 512
ATT_BLOCK = 256
ATT_HEADS_PER_STEP = 4
CONV_TILE = 256
CONV_CHUNK = 64
CONV_HALO = 32
PAGES_PER_STEP = 8
SAMPLE_RING_SLOTS = 3
FFN_CHUNK = 256
SAMPLE_ROWS = 8
MAX_NEW_TOKENS = 4
SAMPLE_Q_ROWS = MAX_NEW_TOKENS * 2 * N_ATT_HEADS


def _vmem_limit(nbytes):
    return int(min(max(2 * nbytes, 16 * 1024 * 1024), V7X_VMEM_BYTES - 8 * 1024 * 1024))


def _rms(x, g):
    return x * lax.rsqrt(jnp.mean(x * x, axis=-1, keepdims=True) + RMS_EPS) * g


def _sigmoid(x):
    return 1.0 / (1.0 + jnp.exp(-x))


def _nt_dot(a, b):
    return lax.dot_general(a, b, (((1,), (1,)), ((), ())), preferred_element_type=F32)


def _bucket_np(n):
    n = np.maximum(n, 0)
    max_exact = N_BUCKETS // 2
    nf = np.maximum(n, 1).astype(np.float32)
    large = max_exact + (np.log(nf / max_exact) / math.log(MAX_DISTANCE / max_exact)
                         * (N_BUCKETS - max_exact)).astype(np.int32)
    large = np.minimum(large, N_BUCKETS - 1)
    return np.where(n < max_exact, n, large).astype(np.int32)


def _prompt_codes(blk):
    i = np.arange(blk)[None, :]
    j = np.arange(blk)[:, None]
    prev = _bucket_np(i - j + blk)
    diag = np.where(j > i, -1, _bucket_np(i - j))
    return np.concatenate([prev, diag]).astype(np.int32)


def _sample_codes(n_new):
    r = np.arange(SAMPLE_Q_ROWS)[:, None]
    c = np.arange(PAGE_SIZE * N_ATT_HEADS)[None, :]
    t, head = r // 8, (r // 2) % N_ATT_HEADS
    tok, key_head = c // N_ATT_HEADS, c % N_ATT_HEADS
    own = key_head == head
    far = np.where(own, N_BUCKETS - 1, -1)
    last = np.where(own, _bucket_np(t + PAGE_SIZE - tok), -1)
    new = np.where(own & (tok <= t) & (tok < n_new) & (c < PAGE_SIZE), _bucket_np(t - tok), -1)
    return np.stack([far, last, new]).astype(np.int32)


def _bias_kernel(tab_ref, lam_in_ref, pcode_ref, scode_ref, pbias_ref, sbias_ref, lam_ref):
    far = N_BUCKETS - 1

    def lookup(code, h):
        out = jnp.zeros(code.shape, F32)
        for b in range(far):
            out = jnp.where(code == b, (tab_ref[b, h] - tab_ref[far, h]) * LOG2_E, out)
        return jnp.where(code < 0, MASKED, out)

    for h in range(N_ATT_HEADS):
        pbias_ref[h] = lookup(pcode_ref[...], h)
    row_head = (lax.broadcasted_iota(jnp.int32, scode_ref.shape[1:], 0) >> 1) & (N_ATT_HEADS - 1)
    for i in range(scode_ref.shape[0]):
        code = scode_ref[i]
        out = jnp.zeros(code.shape, F32)
        for h in range(N_ATT_HEADS):
            out = jnp.where(row_head == h, lookup(code, h), out)
        sbias_ref[i] = out
    lv = lam_in_ref[...]
    d1 = jnp.sum(lv[0:1] * lv[1:2], axis=1, keepdims=True)
    d2 = jnp.sum(lv[2:3] * lv[3:4], axis=1, keepdims=True)
    lam = jnp.exp(d1) - jnp.exp(d2) + LAM_INIT
    lam_ref[...] = jnp.broadcast_to(lam, lam_ref.shape)


def _bias_tiles(table, lam_vecs, blk, n_new):
    pcode = jnp.asarray(_prompt_codes(blk))
    scode = jnp.asarray(_sample_codes(n_new))
    vm = pl.BlockSpec(memory_space=pltpu.VMEM)
    return pl.pallas_call(
        _bias_kernel,
        out_shape=(jax.ShapeDtypeStruct((N_ATT_HEADS, 2 * blk, blk), F32),
                   jax.ShapeDtypeStruct(scode.shape, F32),
                   jax.ShapeDtypeStruct((SUBLANES, LANES), F32)),
        in_specs=[pl.BlockSpec(memory_space=pltpu.SMEM), vm, vm, vm],
        out_specs=(vm, vm, vm),
        name="bias_tiles",
    )(table, lam_vecs, pcode, scode)


def _in_proj_kernel(x_ref, g_ref, w_ref, qt_ref, kb_ref, vt_ref, kf_ref, vf_ref, glu_ref):
    xn = _rms(x_ref[...], g_ref[...]).astype(BF16)

    def cols(c):
        return jnp.dot(xn, w_ref[:, c * ATT_WIDTH:(c + 1) * ATT_WIDTH], preferred_element_type=F32)

    tm = xn.shape[0]
    blk = qt_ref.shape[-1]
    q = cols(0) * (ATT_SCALE * LOG2_E)
    k = cols(1)
    v = cols(2)
    kb = k.astype(BF16)
    for h in range(N_ATT_HEADS):
        sl = slice(h * ATT_V_DIM, (h + 1) * ATT_V_DIM)
        kb_ref[h] = kb[:, sl]
        for c in range(tm // blk):
            rows = slice(c * blk, (c + 1) * blk)
            qt_ref[h, c] = q[rows, sl].T.astype(BF16)
            vt_ref[h, c] = v[rows, sl].T.astype(BF16)
        kf_ref[pl.ds(h, tm, stride=N_ATT_HEADS), :] = k[:, sl]
        vf_ref[pl.ds(h, tm, stride=N_ATT_HEADS), :] = v[:, sl]
    glu_ref[...] = cols(3) * _sigmoid(cols(4))


def _in_proj(x, gain, w_in, blk):
    b, t, d = x.shape
    tm = min(ROW_TILE, t)
    n_cols = w_in.shape[1]
    assert tm % blk == 0
    row = lambda w: pl.BlockSpec((None, tm, w), lambda i, j: (i, j, 0))
    heads = pl.BlockSpec((None, N_ATT_HEADS, tm, ATT_V_DIM), lambda i, j: (i, 0, j, 0))
    hm = jax.ShapeDtypeStruct((b, N_ATT_HEADS, t, ATT_V_DIM), BF16)
    heads_t = pl.BlockSpec((None, N_ATT_HEADS, tm // blk, ATT_V_DIM, blk), lambda i, j: (i, 0, j, 0, 0))
    hm_t = jax.ShapeDtypeStruct((b, N_ATT_HEADS, t // blk, ATT_V_DIM, blk), BF16)
    flat = jax.ShapeDtypeStruct((b, t * N_ATT_HEADS, ATT_V_DIM), F32)
    tok_head = pl.BlockSpec((None, tm * N_ATT_HEADS, ATT_V_DIM), lambda i, j: (i, j, 0))
    est = 2 * d * n_cols * 2 + 2 * tm * (d * 4 + 3 * ATT_WIDTH * 2 + 3 * ATT_WIDTH * 4) + 6 * tm * ATT_WIDTH * 4
    return pl.pallas_call(
        _in_proj_kernel,
        grid=(b, t // tm),
        in_specs=[row(d),
                  pl.BlockSpec((1, d), lambda i, j: (0, 0)),
                  pl.BlockSpec((d, n_cols), lambda i, j: (0, 0))],
        out_specs=(heads_t, heads, heads_t, tok_head, tok_head, row(CONV_WIDTH)),
        out_shape=(hm_t, hm, hm_t, flat, flat, jax.ShapeDtypeStruct((b, t, CONV_WIDTH), F32)),
        compiler_params=pltpu.CompilerParams(dimension_semantics=("parallel", "parallel"),
                                             vmem_limit_bytes=_vmem_limit(est)),
        name="in_proj",
    )(x, gain, w_in)


def _conv_post(y, b_ref, g_ref, bt_ref, beta_ref):
    y = y + b_ref[...]
    mu = jnp.mean(y, axis=-1, keepdims=True)
    yc = y - mu
    yn = yc * lax.rsqrt(jnp.mean(yc * yc, axis=-1, keepdims=True) + LN_EPS) * g_ref[...] + bt_ref[...]
    return yn * _sigmoid(yn) * beta_ref[...]


def _conv_prompt_kernel(glu_ref, prev_ref, w_ref, b_ref, g_ref, bt_ref, beta_ref, o_ref, buf, ybuf):
    tt = glu_ref.shape[0]
    n_slab = buf.shape[0]
    first = pl.program_id(1) == 0
    slabs = [slice(s * LANES, (s + 1) * LANES) for s in range(n_slab)]

    @pl.when(first)
    def _():
        for s in range(n_slab):
            buf[s, 0:CONV_HALO] = prev_ref[:, slabs[s]]

    @pl.when(jnp.logical_not(first))
    def _():
        for s in range(n_slab):
            buf[s, 0:CONV_HALO] = buf[s, tt:tt + CONV_HALO]

    for s in range(n_slab):
        buf[s, CONV_HALO:CONV_HALO + tt] = glu_ref[:, slabs[s]]

    shift = CONV_HALO - (CONV_K - 1)
    rc = min(CONV_CHUNK, tt // 2)
    for c0 in range(0, tt, 2 * rc):
        for parity in range(2):
            start = c0 + parity
            accs = []
            for s in range(n_slab):
                acc = jnp.zeros((rc, LANES), F32)
                for j in range(CONV_K):
                    x = buf[s, pl.ds(start + j + shift, rc, stride=2), :]
                    acc = acc + w_ref[j:j + 1, slabs[s]] * x
                accs.append(acc)
            y = _conv_post(jnp.concatenate(accs, axis=1), b_ref, g_ref, bt_ref, beta_ref)
            for s in range(n_slab):
                ybuf[s, pl.ds(start, rc, stride=2), :] = y[:, slabs[s]]
    for s in range(n_slab):
        o_ref[:, slabs[s]] = ybuf[s].astype(o_ref.dtype)


def _conv_prompt(glu, prev, dw_w, dw_b, ln_g, ln_b, beta):
    b, t, c = glu.shape
    tt = min(CONV_TILE, t)
    vec = pl.BlockSpec((1, c), lambda i, j: (0, 0))
    return pl.pallas_call(
        _conv_prompt_kernel,
        grid=(b, t // tt),
        in_specs=[pl.BlockSpec((None, tt, c), lambda i, j: (i, j, 0)),
                  pl.BlockSpec((None, CONV_HALO, c), lambda i, j: (i, 0, 0)),
                  pl.BlockSpec(dw_w.shape, lambda i, j: (0, 0)),
                  vec, vec, vec, vec],
        out_specs=pl.BlockSpec((None, tt, c), lambda i, j: (i, j, 0)),
        out_shape=jax.ShapeDtypeStruct((b, t, c), BF16),
        scratch_shapes=[pltpu.VMEM((c // LANES, tt + CONV_HALO, LANES), F32),
                        pltpu.VMEM((c // LANES, tt, LANES), F32)],
        compiler_params=pltpu.CompilerParams(dimension_semantics=("parallel", "arbitrary")),
        name="conv_prompt",
    )(glu, prev, dw_w, dw_b, ln_g, ln_b, beta)


def _conv_sample_kernel(xp_ref, w_ref, b_ref, g_ref, bt_ref, beta_ref, o_ref):
    n_t = o_ref.shape[0]
    for t in range(n_t):
        acc = jnp.zeros(xp_ref.shape[1:], F32)
        for j in range(CONV_K):
            acc = acc + w_ref[j:j + 1, :] * xp_ref[t + j]
        o_ref[t] = _conv_post(acc, b_ref, g_ref, bt_ref, beta_ref)


def _conv_sample(xp_t, n_t, dw_w, dw_b, ln_g, ln_b, beta):
    vm = pl.BlockSpec(memory_space=pltpu.VMEM)
    return pl.pallas_call(
        _conv_sample_kernel,
        out_shape=jax.ShapeDtypeStruct((n_t,) + xp_t.shape[1:], F32),
        in_specs=[vm] * 6,
        out_specs=vm,
        name="conv_sample",
    )(xp_t, dw_w, dw_b, ln_g, ln_b, beta)


def _softmax_update(s, v_dot, m_sc, l_sc, acc_sc):
    width = s.shape[1]
    m_prev = m_sc[...]
    m_new = jnp.maximum(m_prev, jnp.max(s, axis=1, keepdims=True))
    alpha = jnp.exp2(m_prev - m_new)
    p = jnp.exp2(s - jnp.concatenate([m_new] * (width // LANES), axis=1))
    l_sc[...] = alpha * l_sc[...] + jnp.sum(p, axis=1, keepdims=True)
    reps = acc_sc.shape[1] // LANES
    acc_sc[...] = jnp.concatenate([alpha] * reps, axis=1) * acc_sc[...] + v_dot(p.astype(BF16))
    m_sc[...] = m_new


def _subln(att, g, beta):
    return _rms(att, g) * (1.0 - LAM_INIT) * beta


def _sample_page_copies(pt_ref, pool_k, pool_v, kbuf, vbuf, sem, chunk, slot, chunks_per_seq):
    seq = chunk // chunks_per_seq
    first_page = (chunk % chunks_per_seq) * PAGES_PER_STEP
    copies = []
    for i in range(PAGES_PER_STEP):
        page = pt_ref[seq, first_page + i]
        copies.append(pltpu.make_async_copy(pool_k.at[page], kbuf.at[slot, i], sem.at[0, slot]))
        copies.append(pltpu.make_async_copy(pool_v.at[page], vbuf.at[slot, i], sem.at[1, slot]))
    return copies


def _attn_kernel(pt_ref, qt_ref, k_ref, vt_ref, bias_ref, lam_ref, g_ref, beta_ref,
                 sq_ref, pool_k, pool_v, kn_ref, vn_ref, sbias_ref, o_ref, so_ref,
                 m_sc, acc_sc, kbuf, vbuf, sem, sm_sc, sl_sc, sacc_sc, cnt_ref):
    n_heads, e, blk = qt_ref.shape
    qi = pl.program_id(2)
    n_seq_s, chunks_per_seq = so_ref.shape[0], pt_ref.shape[1] // PAGES_PER_STEP
    n_chunks = n_seq_s * chunks_per_seq
    first_grid_step = jnp.logical_and(pl.program_id(0) == 0, qi == 0)
    last_grid_step = jnp.logical_and(pl.program_id(0) == pl.num_programs(0) - 1, qi == pl.num_programs(2) - 1)
    ring = (pt_ref, pool_k, pool_v, kbuf, vbuf, sem)

    n_slots = kbuf.shape[0]
    lookahead = n_slots - 1
    last_chunk = n_chunks - 1

    @pl.when(first_grid_step)
    def _():
        cnt_ref[0] = 0
        for d in range(lookahead):
            for cp in _sample_page_copies(*ring, min(d, last_chunk), d, chunks_per_seq):
                cp.start()

    s_lane = lax.broadcasted_iota(jnp.int32, sq_ref.shape[1:], 1)
    s_row = lax.broadcasted_iota(jnp.int32, sq_ref.shape[1:], 0)

    def sample_fetch():
        n = cnt_ref[0]
        slot = lax.rem(n, n_slots)
        valid = n < n_chunks
        chunk = jnp.minimum(n, last_chunk)
        seq = chunk // chunks_per_seq
        c = chunk % chunks_per_seq
        for cp in _sample_page_copies(*ring, chunk, slot, chunks_per_seq):
            cp.wait()
        ahead_slot = lax.rem(n + lookahead, n_slots)
        for cp in _sample_page_copies(*ring, jnp.minimum(n + lookahead, last_chunk), ahead_slot, chunks_per_seq):
            cp.start()
        cnt_ref[0] = n + 1
        return slot, seq, c, valid

    def sample_logits(slot, seq, c, valid):
        q = sq_ref[seq]
        qm = jnp.where((s_lane >> 6) == (s_row & 1), q, jnp.zeros_like(q))
        gate = jnp.where(valid, 0.0, MASKED)
        far_bias = sbias_ref[0] + gate
        last_bias = jnp.where(c == chunks_per_seq - 1, sbias_ref[1] + gate, far_bias)
        s = jnp.concatenate(
            [_nt_dot(qm, kbuf[slot, i].astype(BF16)) + (last_bias if i == PAGES_PER_STEP - 1 else far_bias)
             for i in range(PAGES_PER_STEP)], axis=1)
        return qm, s

    def sample_update(slot, seq, c, valid, qm, s):
        fresh = c == 0
        m_prev = jnp.where(fresh, MASKED, sm_sc[...])
        m_new = jnp.maximum(m_prev, jnp.max(s, axis=1, keepdims=True))
        alpha = jnp.exp2(m_prev - m_new)
        p = jnp.exp2(s - jnp.concatenate([m_new] * (s.shape[1] // LANES), axis=1))
        sl_sc[...] = alpha * jnp.where(fresh, 0.0, sl_sc[...]) + jnp.sum(p, axis=1, keepdims=True)
        pb = p.astype(BF16)
        cols = kbuf.shape[2]
        pv = None
        for i in range(PAGES_PER_STEP):
            part = jnp.dot(pb[:, i * cols:(i + 1) * cols], vbuf[slot, i].astype(BF16),
                           preferred_element_type=F32)
            pv = part if pv is None else pv + part
        sacc_sc[...] = alpha * jnp.where(fresh, 0.0, sacc_sc[...]) + pv
        sm_sc[...] = m_new
        return seq, qm, jnp.logical_and(valid, c == chunks_per_seq - 1)

    def sample_finish(seq, qm, seq_done):
        @pl.when(seq_done)
        def _():
            n_new = kn_ref.shape[1]
            s_new = _nt_dot(qm, kn_ref[seq].astype(BF16)) + sbias_ref[2][:, :n_new]
            _softmax_update(s_new, lambda pn: jnp.dot(pn, vn_ref[seq].astype(BF16), preferred_element_type=F32),
                            sm_sc, sl_sc, sacc_sc)
            sign = jnp.where((s_row & 1) == 0, 1.0, -lam_ref[0:1, 0:1])
            z = sacc_sc[...] / sl_sc[...] * sign
            out_row = lax.broadcasted_iota(jnp.int32, (so_ref.shape[1], ATT_V_DIM), 0)
            for h in range(N_ATT_HEADS):
                att = jnp.zeros(out_row.shape, F32)
                for t in range(MAX_NEW_TOKENS):
                    r0 = t * 8 + h * 2
                    att = jnp.where(out_row == t, jnp.broadcast_to(z[r0:r0 + 1] + z[r0 + 1:r0 + 2], att.shape),
                                    att)
                sl = slice(h * ATT_V_DIM, (h + 1) * ATT_V_DIM)
                so_ref[seq, :, sl] = _subln(att, g_ref[...], beta_ref[:, sl])

    ones_rows = jnp.ones((acc_sc.shape[2] - e, blk), BF16)
    sub = lax.broadcasted_iota(jnp.int32, qt_ref.shape[1:], 0)
    qst = []
    for h in range(n_heads):
        qt = qt_ref[h]
        zero = jnp.zeros_like(qt)
        qst.append(jnp.concatenate([jnp.where(sub < DIFF_HEAD_DIM, qt, zero),
                                    jnp.where(sub >= DIFF_HEAD_DIM, qt, zero)], axis=1))
    m_sc[...] = jnp.full(m_sc.shape, MASKED, F32)
    acc_sc[0] = jnp.zeros(acc_sc.shape[1:], F32)

    def step(j, n_blk, biased, stale_max, src):
        rows = pl.ds(pl.multiple_of(j * blk, blk), n_blk * blk)
        dst = 1 - src
        chunk_state = sample_fetch()

        def logits(h):
            st = jnp.dot(k_ref[h, rows, :], qst[h], preferred_element_type=F32)
            if biased:
                bias = bias_ref[h, (2 - n_blk) * blk:, :]
                st = st + jnp.concatenate([bias, bias], axis=1)
            return st

        def weighted_values(h, pb):
            pv = None
            for i in range(n_blk):
                vt_aug = jnp.concatenate([vt_ref[h, j + i], ones_rows], axis=0)
                part = jnp.dot(vt_aug, pb[i * blk:(i + 1) * blk], preferred_element_type=F32)
                pv = part if pv is None else pv + part
            return pv

        def exact_update(h, st):
            m_prev = m_sc[h]
            m_new = jnp.maximum(m_prev, jnp.max(st, axis=0, keepdims=True))
            alpha = jnp.exp2(m_prev - m_new)
            pv = weighted_values(h, jnp.exp2(st - m_new).astype(BF16))
            acc_sc[dst, h] = alpha * acc_sc[src, h] + pv
            m_sc[h] = m_new

        def stale_update(h, st):
            m_ref = m_sc[h]
            pv = weighted_values(h, jnp.exp2(st - m_ref).astype(BF16))
            acc_sc[dst, h] = acc_sc[src, h] + pv
            return jnp.max(st, axis=0, keepdims=True) - m_ref

        ahead = 2
        pending = [logits(h) for h in range(min(ahead, n_heads))]
        chunk_logits = sample_logits(*chunk_state)
        excess = None
        for h in range(n_heads):
            if h + ahead < n_heads:
                pending.append(logits(h + ahead))
            if stale_max:
                over = stale_update(h, pending[h])
                excess = over if excess is None else jnp.maximum(excess, over)
            else:
                exact_update(h, pending[h])
            if h == 0:
                sample_done = sample_update(*chunk_state, *chunk_logits)

        sample_finish(*sample_done)
        if stale_max:
            @pl.when(jnp.max(excess) > STALE_MAX_MARGIN)
            def _():
                for h in range(n_heads):
                    exact_update(h, logits(h))

    odd = (qi + 1) & 1
    n_pairs = jnp.maximum(((qi + 1) >> 1) - 1, 0)

    @pl.when(qi == 0)
    def _():
        step(0, 1, True, False, 0)

    @pl.when(qi > 0)
    def _():
        step(qi - 1, 2, True, False, 0)

    def far_pair(i, carry):
        step(odd + 2 * i, 2, False, True, (i + 1) & 1)
        return carry

    lax.fori_loop(0, n_pairs, far_pair, 0)
    single = jnp.logical_and(qi > 0, odd == 1)

    @pl.when(single)
    def _():
        step(0, 1, False, True, (n_pairs + 1) & 1)

    res = (n_pairs + 1 + single.astype(jnp.int32)) & 1
    for h in range(n_heads):
        ot = acc_sc[res, h, 0:e] / acc_sc[res, h, e:e + 1]
        att = (ot[:, :blk] - lam_ref[0:1, 0:1] * ot[:, blk:]).T
        sl = slice(h * ATT_V_DIM, (h + 1) * ATT_V_DIM)
        o_ref[:, sl] = _subln(att, g_ref[...], beta_ref[:, sl]).astype(o_ref.dtype)

    @pl.when(last_grid_step)
    def _():
        def drain(i, carry):
            chunk_state = sample_fetch()
            sample_finish(*sample_update(*chunk_state, *sample_logits(*chunk_state)))
            return carry

        lax.fori_loop(0, jnp.maximum(n_chunks - cnt_ref[0], 0), drain, 0)
        n = cnt_ref[0]
        for d in range(lookahead):
            for cp in _sample_page_copies(*ring, jnp.minimum(n + d, last_chunk), lax.rem(n + d, n_slots),
                                          chunks_per_seq):
                cp.wait()


def _attention(qt, k, vt, bias, lam, subln_g, beta_att, page_table, q_rows, pool_k, pool_v, k_new, v_new,
               s_bias):
    b, h, t, e = k.shape
    blk = bias.shape[-1]
    n_blk = t // blk
    hs = ATT_HEADS_PER_STEP
    assert hs == h, "the sample epilogue reads every head's beta from the prompt block"
    n_seq, n_pages = page_table.shape
    rows = q_rows.shape[1]
    page_rows = pool_k.shape[1]
    npg = PAGES_PER_STEP
    assert n_pages % npg == 0
    once = pl.Buffered(1)
    whole = lambda a: pl.BlockSpec(a.shape, lambda i, j, n, pt: (0,) * a.ndim, pipeline_mode=once)
    sample_out = jax.ShapeDtypeStruct((n_seq, SAMPLE_ROWS, h * e), F32)
    grid_spec = pltpu.PrefetchScalarGridSpec(
        num_scalar_prefetch=1,
        grid=(b, h // hs, n_blk),
        in_specs=[pl.BlockSpec((None, hs, None, e, blk), lambda i, j, n, pt: (i, j, n, 0, 0)),
                  pl.BlockSpec((None, hs, t, e), lambda i, j, n, pt: (i, j, 0, 0), pipeline_mode=once),
                  pl.BlockSpec((None, hs, n_blk, e, blk), lambda i, j, n, pt: (i, j, 0, 0, 0), pipeline_mode=once),
                  pl.BlockSpec((hs, 2 * blk, blk), lambda i, j, n, pt: (j, 0, 0), pipeline_mode=once),
                  whole(lam), whole(subln_g),
                  pl.BlockSpec((1, hs * e), lambda i, j, n, pt: (0, j)),
                  whole(q_rows),
                  pl.BlockSpec(memory_space=pl.ANY), pl.BlockSpec(memory_space=pl.ANY),
                  whole(k_new), whole(v_new), whole(s_bias)],
        out_specs=(pl.BlockSpec((None, blk, hs * e), lambda i, j, n, pt: (i, n, j)),
                   pl.BlockSpec(sample_out.shape, lambda i, j, n, pt: (0, 0, 0))),
        scratch_shapes=[pltpu.VMEM((hs, 1, 2 * blk), F32),
                        pltpu.VMEM((2, hs, e + BF16_ROWS, 2 * blk), F32),
                        pltpu.VMEM((SAMPLE_RING_SLOTS, npg, page_rows, e), F32),
                        pltpu.VMEM((SAMPLE_RING_SLOTS, npg, page_rows, e), F32),
                        pltpu.SemaphoreType.DMA((2, SAMPLE_RING_SLOTS)),
                        pltpu.VMEM((rows, LANES), F32), pltpu.VMEM((rows, LANES), F32),
                        pltpu.VMEM((rows, e), F32),
                        pltpu.SMEM((1,), jnp.int32)],
    )
    est = (hs * (2 * t * e * 2) + hs * 2 * blk * blk * 4 + 4 * hs * 2 * blk * 2 * blk * 4
           + SAMPLE_RING_SLOTS * 2 * npg * page_rows * e * 4 + 2 * k_new.size * 4 + 8 * rows * npg * page_rows * 4)
    return pl.pallas_call(
        _attn_kernel,
        grid_spec=grid_spec,
        out_shape=(jax.ShapeDtypeStruct((b, t, h * e), BF16), sample_out),
        compiler_params=pltpu.CompilerParams(dimension_semantics=("arbitrary", "arbitrary", "arbitrary"),
                                             vmem_limit_bytes=_vmem_limit(est)),
        name="attention",
    )(page_table, qt, k, vt, bias, lam, subln_g, beta_att, q_rows, pool_k, pool_v, k_new, v_new, s_bias)


def _mem_kv_kernel(mem_ref, wk_ref, wv_ref, k_ref, v_ref):
    m = mem_ref[...].astype(BF16)
    k_ref[...] = jnp.dot(m, wk_ref[...], preferred_element_type=F32)
    v_ref[...] = jnp.dot(m, wv_ref[...], preferred_element_type=F32)


def _mem_kv(mem, w_xk, w_xv):
    b, n, d = mem.shape
    blk = pl.BlockSpec((None, n, d), lambda i: (i, 0, 0))
    w = pl.BlockSpec((d, d), lambda i: (0, 0))
    out = jax.ShapeDtypeStruct((b, n, d), F32)
    return pl.pallas_call(
        _mem_kv_kernel,
        grid=(b,),
        in_specs=[blk, w, w],
        out_specs=(blk, blk),
        out_shape=(out, out),
        compiler_params=pltpu.CompilerParams(dimension_semantics=("parallel",),
                                             vmem_limit_bytes=_vmem_limit(4 * d * d * 2 + 6 * n * d * 4)),
        name="mem_kv",
    )(mem, w_xk, w_xv)


def _mix_out_kernel(att_ref, conv_ref, h_ref, wo_ref, g_post_ref, g_x_ref, wq_ref, h1_ref, qx_ref):
    half = att_ref.shape[1]
    mo = (jnp.dot(att_ref[...].astype(BF16), wo_ref[0:half, :], preferred_element_type=F32)
          + jnp.dot(conv_ref[...].astype(BF16), wo_ref[half:, :], preferred_element_type=F32))
    h1 = h_ref[...] + _rms(mo, g_post_ref[...])
    h1_ref[...] = h1
    xn = _rms(h1, g_x_ref[...]).astype(BF16)
    x_scale = (wq_ref.shape[1] // N_X_HEADS) ** -0.5
    qx_ref[...] = (jnp.dot(xn, wq_ref[...], preferred_element_type=F32) * x_scale).astype(BF16)


def _mix_out(att, conv, h, w_out, g_post, g_x, w_xq):
    m, d = h.shape
    tm = min(ROW_TILE, m)
    row = lambda w: pl.BlockSpec((tm, w), lambda i: (i, 0))
    const = lambda a: pl.BlockSpec(a.shape, lambda i: (0, 0))
    est = 2 * 2 * d * d * 2 + 2 * tm * (2 * d * 4 + d * 2 + att.shape[1] * 6) + 4 * tm * d * 4
    return pl.pallas_call(
        _mix_out_kernel,
        grid=(m // tm,),
        in_specs=[row(att.shape[1]), row(conv.shape[1]), row(d), const(w_out), const(g_post), const(g_x),
                  const(w_xq)],
        out_specs=(row(d), row(d)),
        out_shape=(jax.ShapeDtypeStruct((m, d), F32), jax.ShapeDtypeStruct((m, d), BF16)),
        compiler_params=pltpu.CompilerParams(dimension_semantics=("parallel",),
                                             vmem_limit_bytes=_vmem_limit(est)),
        name="mix_out",
    )(att, conv, h, w_out, g_post, g_x, w_xq)


def _xattn_kernel(q_ref, mk_ref, mv_ref, o_ref):
    d = q_ref.shape[1]
    hd = d // N_X_HEADS
    for h in range(N_X_HEADS):
        sl = slice(h * hd, (h + 1) * hd)
        s = _nt_dot(q_ref[:, sl], mk_ref[:, sl].astype(BF16))
        p = jnp.exp(s - jnp.max(s, axis=1, keepdims=True))
        l = jnp.sum(p, axis=1, keepdims=True)
        o = jnp.dot(p.astype(BF16), mv_ref[:, sl].astype(BF16), preferred_element_type=F32)
        o_ref[:, sl] = (o / l).astype(o_ref.dtype)


def _xattn(qx, mem_k, mem_v):
    b, t, d = qx.shape
    n = mem_k.shape[1]
    tm = min(ROW_TILE, t)
    row = pl.BlockSpec((None, tm, d), lambda i, j: (i, j, 0))
    mem = pl.BlockSpec((None, n, d), lambda i, j: (i, 0, 0))
    return pl.pallas_call(
        _xattn_kernel,
        grid=(b, t // tm),
        in_specs=[row, mem, mem],
        out_specs=row,
        out_shape=jax.ShapeDtypeStruct((b, t, d), BF16),
        compiler_params=pltpu.CompilerParams(dimension_semantics=("parallel", "parallel"),
                                             vmem_limit_bytes=_vmem_limit(4 * n * d * 4 + 8 * tm * d * 4)),
        name="xattn",
    )(qx, mem_k, mem_v)


def _ffn_kernel(o_ref, h1_ref, wxo_ref, g_xpost_ref, g_pre_ref, wg_ref, wu_ref, wd_ref, g_post_ref, y_ref):
    h2 = h1_ref[...] + _rms(jnp.dot(o_ref[...], wxo_ref[...], preferred_element_type=F32), g_xpost_ref[...])
    xf = _rms(h2, g_pre_ref[...]).astype(BF16)
    d_ff = wg_ref.shape[1]
    f = jnp.zeros(h2.shape, F32)
    for c0 in range(0, d_ff, FFN_CHUNK):
        sl = slice(c0, c0 + FFN_CHUNK)
        g = jnp.dot(xf, wg_ref[:, sl], preferred_element_type=F32)
        u = jnp.dot(xf, wu_ref[:, sl], preferred_element_type=F32)
        a = (g * _sigmoid(g) * u).astype(BF16)
        f = f + jnp.dot(a, wd_ref[sl, :], preferred_element_type=F32)
    y_ref[...] = h2 + _rms(f, g_post_ref[...])


def _ffn(o, h1, w_xo, g_xpost, g_pre, w_gate, w_up, w_down, g_post):
    m, d = h1.shape
    d_ff = w_gate.shape[1]
    assert d_ff % FFN_CHUNK == 0
    tm = min(ROW_TILE, m)
    row = pl.BlockSpec((tm, d), lambda i: (i, 0))
    const = lambda a: pl.BlockSpec(a.shape, lambda i: (0, 0), pipeline_mode=pl.Buffered(1))
    est = (d * d + 3 * d * d_ff) * 2 + 2 * tm * d * (2 + 4 + 4) + 6 * tm * d * 4
    return pl.pallas_call(
        _ffn_kernel,
        grid=(m // tm,),
        in_specs=[row, row, const(w_xo), const(g_xpost), const(g_pre), const(w_gate), const(w_up),
                  const(w_down), const(g_post)],
        out_specs=row,
        out_shape=jax.ShapeDtypeStruct((m, d), F32),
        compiler_params=pltpu.CompilerParams(dimension_semantics=("parallel",),
                                             vmem_limit_bytes=_vmem_limit(est)),
        name="ffn",
    )(o, h1, w_xo, g_xpost, g_pre, w_gate, w_up, w_down, g_post)


def kernel(x_prompt, x_sample, mem_prompt, cache_k, cache_v, state_conv, cache_mem_k, cache_mem_v, page_table, rel_bias_table, norm_mix_pre, norm_mix_post, w_in, lambda_q1, lambda_k1, lambda_q2, lambda_k2, subln_g, dw_w, dw_b, conv_ln_g, conv_ln_b, beta_att, beta_conv, w_out, norm_x_pre, norm_x_post, w_xq, w_xk, w_xv, w_xo, norm_ffn_pre, norm_ffn_post, w_gate, w_up, w_down):
    assert w_in.shape[0] == 1, "single-layer trunk"
    bp, tp, d = x_prompt.shape
    bs, ts, _ = x_sample.shape
    assert ts <= MAX_NEW_TOKENS and tp >= CONV_K - 1
    n_mem = mem_prompt.shape[1]
    vec = lambda a: a[0].reshape(1, -1)
    wb = lambda a: a[0].astype(BF16)
    g_mix_pre, g_mix_post = vec(norm_mix_pre), vec(norm_mix_post)
    g_x_pre, g_x_post = vec(norm_x_pre), vec(norm_x_post)
    g_ffn_pre, g_ffn_post = vec(norm_ffn_pre), vec(norm_ffn_post)
    sub_g, b_att, b_conv = vec(subln_g), vec(beta_att), vec(beta_conv)
    c_b, c_g, c_bt = vec(dw_b), vec(conv_ln_g), vec(conv_ln_b)
    w_in_b, w_out_b, w_xq_b, w_xk_b, w_xv_b, w_xo_b = (wb(w) for w in (w_in, w_out, w_xq, w_xk, w_xv, w_xo))
    w_gate_b, w_up_b, w_down_b = wb(w_gate), wb(w_up), wb(w_down)
    dw = dw_w[0]

    blk = min(ATT_BLOCK, tp)
    lam_vecs = jnp.stack([lambda_q1[0], lambda_k1[0], lambda_q2[0], lambda_k2[0]])
    p_bias, s_bias, lam = _bias_tiles(rel_bias_table, lam_vecs, blk, ts)

    def tail(att, conv, h, mem_k, mem_v):
        b, t, _ = h.shape
        flat = lambda a: a.reshape(b * t, a.shape[-1])
        h1, qx = _mix_out(flat(att), flat(conv), flat(h), w_out_b, g_mix_post, g_x_pre, w_xq_b)
        o = _xattn(qx.reshape(b, t, d), mem_k, mem_v)
        y = _ffn(flat(o), h1, w_xo_b, g_x_post, g_ffn_pre, w_gate_b, w_up_b, w_down_b, g_ffn_post)
        return y.reshape(b, t, d)

    qt_p, kb_p, vt_p, kf_p, vf_p, glu_p = _in_proj(x_prompt, g_mix_pre, w_in_b, blk)
    x_s = jnp.pad(x_sample, ((0, 0), (0, SAMPLE_ROWS - ts), (0, 0)))
    rows_s = bs * SAMPLE_ROWS
    qt_s, _, _, kf_s, vf_s, glu_s = _in_proj(x_s.reshape(1, rows_s, d), g_mix_pre, w_in_b, min(blk, rows_s))
    new_rows = ts * N_ATT_HEADS
    kf_s = kf_s.reshape(bs, SAMPLE_ROWS * N_ATT_HEADS, ATT_V_DIM)[:, :new_rows]
    vf_s = vf_s.reshape(bs, SAMPLE_ROWS * N_ATT_HEADS, ATT_V_DIM)[:, :new_rows]
    glu_s = glu_s.reshape(bs, SAMPLE_ROWS, CONV_WIDTH)[:, :ts]
    q_s = jnp.transpose(qt_s[0], (0, 1, 3, 2))
    q_th = jnp.transpose(q_s.reshape(N_ATT_HEADS, bs, SAMPLE_ROWS, ATT_V_DIM), (1, 2, 0, 3))
    q_th = jnp.pad(q_th[:, :ts], ((0, 0), (0, MAX_NEW_TOKENS - ts), (0, 0), (0, 0)))
    q_rows = jnp.repeat(q_th.reshape(bs, MAX_NEW_TOKENS * N_ATT_HEADS, ATT_V_DIM), 2, axis=1)
    pad_page = lambda a: jnp.pad(a, ((0, 0), (0, PAGE_SIZE - new_rows), (0, 0)))
    n_phys = cache_k.shape[1]
    pool = lambda c: c.reshape(n_phys, PAGE_SIZE * N_ATT_HEADS, ATT_V_DIM)

    att_p, att_s = _attention(qt_p, kb_p, vt_p, p_bias, lam, sub_g, b_att, page_table, q_rows,
                              pool(cache_k), pool(cache_v), pad_page(kf_s), pad_page(vf_s), s_bias)

    conv_p = _conv_prompt(glu_p, jnp.zeros((bp, CONV_HALO, CONV_WIDTH), F32), dw, c_b, c_g, c_bt, b_conv)
    mk_p, mv_p = _mem_kv(mem_prompt, w_xk_b, w_xv_b)
    y_p = tail(att_p, conv_p, x_prompt, mk_p, mv_p)

    xp_s = jnp.concatenate([state_conv[0], glu_s], axis=1)
    conv_s = _conv_sample(jnp.transpose(xp_s, (1, 0, 2)), ts, dw, c_b, c_g, c_bt, b_conv)
    conv_s = jnp.pad(jnp.transpose(conv_s, (1, 0, 2)), ((0, 0), (0, SAMPLE_ROWS - ts), (0, 0)))
    mem_s = lambda c: c[0].reshape(bs, n_mem, d)
    y_s = tail(att_s, conv_s, x_s, mem_s(cache_mem_k), mem_s(cache_mem_v))

    heads = lambda a: a.reshape(1, a.shape[0], a.shape[1] // N_ATT_HEADS, N_ATT_HEADS, ATT_V_DIM)
    mem_heads = lambda a: a.reshape(1, bp, n_mem, N_X_HEADS, d // N_X_HEADS)
    return (y_p, y_s[:, :ts],
            heads(kf_p), heads(vf_p), glu_p[None, :, tp - (CONV_K - 1):],
            mem_heads(mk_p), mem_heads(mv_p),
            heads(kf_s), heads(vf_s), xp_s[None, :, ts:])
```

```python
import functools
import math

import numpy as np
import jax
import jax.numpy as jnp
from jax import lax
from jax.experimental import pallas as pl
from jax.experimental.pallas import tpu as pltpu

F32 = jnp.float32
BF16 = jnp.bfloat16

DIFF_HEAD_DIM = 64
ATT_V_DIM = 2 * DIFF_HEAD_DIM
N_ATT_HEADS = 4
ATT_WIDTH = N_ATT_HEADS * ATT_V_DIM
CONV_WIDTH = 512
CONV_K = 31
N_BUCKETS = 32
MAX_DISTANCE = 128
N_X_HEADS = 4
PAGE_SIZE = 128
RMS_EPS = 1e-6
LN_EPS = 1e-5
ATT_SCALE = DIFF_HEAD_DIM ** -0.5
LOG2_E = math.log2(math.e)
LAM_INIT = 0.8 - 0.6 * math.exp(-0.3 * 0)
MASKED = -1e30
STALE_MAX_MARGIN = 30.0

V7X_VMEM_BYTES = 64 * 1024 * 1024
LANES = 128
SUBLANES = 8
BF16_ROWS = 16

ROW_TILE = 512
ATT_BLOCK = 256
ATT_HEADS_PER_STEP = 4
CONV_CHUNK = 64
CONV_HALO = 32
PAGES_PER_STEP = 8
SAMPLE_RING_SLOTS = 3
FFN_CHUNK = 256
SAMPLE_ROWS = 8
MAX_NEW_TOKENS = 4
SAMPLE_Q_ROWS = MAX_NEW_TOKENS * 2 * N_ATT_HEADS


def _vmem_limit(nbytes):
    return int(min(max(2 * nbytes, 16 * 1024 * 1024), V7X_VMEM_BYTES - 8 * 1024 * 1024))


def _rms(x, g):
    return x * lax.rsqrt(jnp.mean(x * x, axis=-1, keepdims=True) + RMS_EPS) * g


def _sigmoid(x):
    return 1.0 / (1.0 + jnp.exp(-x))


def _nt_dot(a, b):
    return lax.dot_general(a, b, (((1,), (1,)), ((), ())), preferred_element_type=F32)


def _bucket_np(n):
    n = np.maximum(n, 0)
    max_exact = N_BUCKETS // 2
    nf = np.maximum(n, 1).astype(np.float32)
    large = max_exact + (np.log(nf / max_exact) / math.log(MAX_DISTANCE / max_exact)
                         * (N_BUCKETS - max_exact)).astype(np.int32)
    large = np.minimum(large, N_BUCKETS - 1)
    return np.where(n < max_exact, n, large).astype(np.int32)


def _prompt_codes(blk):
    i = np.arange(blk)[None, :]
    j = np.arange(blk)[:, None]
    prev = _bucket_np(i - j + blk)
    diag = np.where(j > i, -1, _bucket_np(i - j))
    return np.concatenate([prev, diag]).astype(np.int32)


def _sample_codes(n_new):
    r = np.arange(SAMPLE_Q_ROWS)[:, None]
    c = np.arange(PAGE_SIZE * N_ATT_HEADS)[None, :]
    t, head = r // 8, (r // 2) % N_ATT_HEADS
    tok, key_head = c // N_ATT_HEADS, c % N_ATT_HEADS
    own = key_head == head
    far = np.where(own, N_BUCKETS - 1, -1)
    last = np.where(own, _bucket_np(t + PAGE_SIZE - tok), -1)
    new = np.where(own & (tok <= t) & (tok < n_new) & (c < PAGE_SIZE), _bucket_np(t - tok), -1)
    return np.stack([far, last, new]).astype(np.int32)


def _bias_kernel(tab_ref, lam_in_ref, pcode_ref, scode_ref, pbias_ref, sbias_ref, lam_ref):
    far = N_BUCKETS - 1

    def lookup(code, h):
        out = jnp.zeros(code.shape, F32)
        for b in range(far):
            out = jnp.where(code == b, (tab_ref[b, h] - tab_ref[far, h]) * LOG2_E, out)
        return jnp.where(code < 0, MASKED, out)

    for h in range(N_ATT_HEADS):
        pbias_ref[h] = lookup(pcode_ref[...], h)
    row_head = (lax.broadcasted_iota(jnp.int32, scode_ref.shape[1:], 0) >> 1) & (N_ATT_HEADS - 1)
    for i in range(scode_ref.shape[0]):
        code = scode_ref[i]
        out = jnp.zeros(code.shape, F32)
        for h in range(N_ATT_HEADS):
            out = jnp.where(row_head == h, lookup(code, h), out)
        sbias_ref[i] = out
    lv = lam_in_ref[...]
    d1 = jnp.sum(lv[0:1] * lv[1:2], axis=1, keepdims=True)
    d2 = jnp.sum(lv[2:3] * lv[3:4], axis=1, keepdims=True)
    lam = jnp.exp(d1) - jnp.exp(d2) + LAM_INIT
    lam_ref[...] = jnp.broadcast_to(lam, lam_ref.shape)


def _bias_tiles(table, lam_vecs, blk, n_new):
    pcode = jnp.asarray(_prompt_codes(blk))
    scode = jnp.asarray(_sample_codes(n_new))
    vm = pl.BlockSpec(memory_space=pltpu.VMEM)
    return pl.pallas_call(
        _bias_kernel,
        out_shape=(jax.ShapeDtypeStruct((N_ATT_HEADS, 2 * blk, blk), F32),
                   jax.ShapeDtypeStruct(scode.shape, F32),
                   jax.ShapeDtypeStruct((SUBLANES, LANES), F32)),
        in_specs=[pl.BlockSpec(memory_space=pltpu.SMEM), vm, vm, vm],
        out_specs=(vm, vm, vm),
        name="bias_tiles",
    )(table, lam_vecs, pcode, scode)


def _in_proj_kernel(x_ref, g_ref, w_ref, *refs, with_conv):
    if with_conv:
        conv_in, refs = refs[:6], refs[6:]
        qt_ref, kb_ref, vt_ref, kf_ref, vf_ref, glu_ref, conv_ref, buf, ybuf = refs
    else:
        qt_ref, kb_ref, vt_ref, kf_ref, vf_ref, glu_ref = refs
    xn = _rms(x_ref[...], g_ref[...]).astype(BF16)

    def cols(c):
        return jnp.dot(xn, w_ref[:, c * ATT_WIDTH:(c + 1) * ATT_WIDTH], preferred_element_type=F32)

    tm = xn.shape[0]
    blk = qt_ref.shape[-1]
    glu = cols(3) * _sigmoid(cols(4))
    glu_ref[...] = glu
    if with_conv:
        _conv_tile(glu, *conv_in, conv_ref, buf, ybuf, pl.program_id(1) == 0)
    q = cols(0) * (ATT_SCALE * LOG2_E)
    k = cols(1)
    v = cols(2)
    kb = k.astype(BF16)
    for h in range(N_ATT_HEADS):
        sl = slice(h * ATT_V_DIM, (h + 1) * ATT_V_DIM)
        kb_ref[h] = kb[:, sl]
        for c in range(tm // blk):
            rows = slice(c * blk, (c + 1) * blk)
            qt_ref[h, c] = q[rows, sl].T.astype(BF16)
            vt_ref[h, c] = v[rows, sl].T.astype(BF16)
        kf_ref[pl.ds(h, tm, stride=N_ATT_HEADS), :] = k[:, sl]
        vf_ref[pl.ds(h, tm, stride=N_ATT_HEADS), :] = v[:, sl]


def _in_proj(x, gain, w_in, blk, conv=None):
    b, t, d = x.shape
    tm = min(ROW_TILE, t)
    n_cols = w_in.shape[1]
    assert tm % blk == 0
    c = CONV_WIDTH
    conv = () if conv is None else tuple(conv)
    vec = pl.BlockSpec((1, c), lambda i, j: (0, 0))
    conv_specs = [pl.BlockSpec((None, CONV_HALO, c), lambda i, j: (i, 0, 0)),
                  pl.BlockSpec((CONV_K, c), lambda i, j: (0, 0)), vec, vec, vec, vec] if conv else []
    conv_out = [pl.BlockSpec((None, tm, c), lambda i, j: (i, j, 0))] if conv else []
    conv_shape = [jax.ShapeDtypeStruct((b, t, c), BF16)] if conv else []
    conv_scratch = [pltpu.VMEM((c // LANES, tm + CONV_HALO, LANES), F32),
                    pltpu.VMEM((c // LANES, tm, LANES), F32)] if conv else []
    row = lambda w: pl.BlockSpec((None, tm, w), lambda i, j: (i, j, 0))
    heads = pl.BlockSpec((None, N_ATT_HEADS, tm, ATT_V_DIM), lambda i, j: (i, 0, j, 0))
    hm = jax.ShapeDtypeStruct((b, N_ATT_HEADS, t, ATT_V_DIM), BF16)
    heads_t = pl.BlockSpec((None, N_ATT_HEADS, tm // blk, ATT_V_DIM, blk), lambda i, j: (i, 0, j, 0, 0))
    hm_t = jax.ShapeDtypeStruct((b, N_ATT_HEADS, t // blk, ATT_V_DIM, blk), BF16)
    flat = jax.ShapeDtypeStruct((b, t * N_ATT_HEADS, ATT_V_DIM), F32)
    tok_head = pl.BlockSpec((None, tm * N_ATT_HEADS, ATT_V_DIM), lambda i, j: (i, j, 0))
    est = 2 * d * n_cols * 2 + 2 * tm * (d * 4 + 3 * ATT_WIDTH * 2 + 3 * ATT_WIDTH * 4) + 6 * tm * ATT_WIDTH * 4
    return pl.pallas_call(
        functools.partial(_in_proj_kernel, with_conv=bool(conv)),
        grid=(b, t // tm),
        in_specs=[row(d),
                  pl.BlockSpec((1, d), lambda i, j: (0, 0)),
                  pl.BlockSpec((d, n_cols), lambda i, j: (0, 0))] + conv_specs,
        out_specs=tuple([heads_t, heads, heads_t, tok_head, tok_head, row(CONV_WIDTH)] + conv_out),
        out_shape=tuple([hm_t, hm, hm_t, flat, flat, jax.ShapeDtypeStruct((b, t, CONV_WIDTH), F32)]
                        + conv_shape),
        scratch_shapes=conv_scratch,
        compiler_params=pltpu.CompilerParams(dimension_semantics=("parallel", "arbitrary"),
                                             vmem_limit_bytes=_vmem_limit(est)),
        name="in_proj",
    )(x, gain, w_in, *conv)


def _conv_post(y, b_ref, g_ref, bt_ref, beta_ref):
    y = y + b_ref[...]
    mu = jnp.mean(y, axis=-1, keepdims=True)
    yc = y - mu
    yn = yc * lax.rsqrt(jnp.mean(yc * yc, axis=-1, keepdims=True) + LN_EPS) * g_ref[...] + bt_ref[...]
    return yn * _sigmoid(yn) * beta_ref[...]


def _conv_tile(glu, prev_ref, w_ref, b_ref, g_ref, bt_ref, beta_ref, o_ref, buf, ybuf, first):
    tt = glu.shape[0]
    n_slab = buf.shape[0]
    slabs = [slice(s * LANES, (s + 1) * LANES) for s in range(n_slab)]

    @pl.when(first)
    def _():
        for s in range(n_slab):
            buf[s, 0:CONV_HALO] = prev_ref[:, slabs[s]]

    @pl.when(jnp.logical_not(first))
    def _():
        for s in range(n_slab):
            buf[s, 0:CONV_HALO] = buf[s, tt:tt + CONV_HALO]

    for s in range(n_slab):
        buf[s, CONV_HALO:CONV_HALO + tt] = glu[:, slabs[s]]

    shift = CONV_HALO - (CONV_K - 1)
    rc = min(CONV_CHUNK, tt // 2)
    for c0 in range(0, tt, 2 * rc):
        for parity in range(2):
            start = c0 + parity
            accs = []
            for s in range(n_slab):
                acc = jnp.zeros((rc, LANES), F32)
                for j in range(CONV_K):
                    x = buf[s, pl.ds(start + j + shift, rc, stride=2), :]
                    acc = acc + w_ref[j:j + 1, slabs[s]] * x
                accs.append(acc)
            y = _conv_post(jnp.concatenate(accs, axis=1), b_ref, g_ref, bt_ref, beta_ref)
            for s in range(n_slab):
                ybuf[s, pl.ds(start, rc, stride=2), :] = y[:, slabs[s]]
    for s in range(n_slab):
        o_ref[:, slabs[s]] = ybuf[s].astype(o_ref.dtype)


def _conv_sample_kernel(xp_ref, w_ref, b_ref, g_ref, bt_ref, beta_ref, o_ref):
    n_t = o_ref.shape[0]
    for t in range(n_t):
        acc = jnp.zeros(xp_ref.shape[1:], F32)
        for j in range(CONV_K):
            acc = acc + w_ref[j:j + 1, :] * xp_ref[t + j]
        o_ref[t] = _conv_post(acc, b_ref, g_ref, bt_ref, beta_ref)


def _conv_sample(xp_t, n_t, dw_w, dw_b, ln_g, ln_b, beta):
    vm = pl.BlockSpec(memory_space=pltpu.VMEM)
    return pl.pallas_call(
        _conv_sample_kernel,
        out_shape=jax.ShapeDtypeStruct((n_t,) + xp_t.shape[1:], F32),
        in_specs=[vm] * 6,
        out_specs=vm,
        name="conv_sample",
    )(xp_t, dw_w, dw_b, ln_g, ln_b, beta)


def _softmax_update(s, v_dot, m_sc, l_sc, acc_sc):
    width = s.shape[1]
    m_prev = m_sc[...]
    m_new = jnp.maximum(m_prev, jnp.max(s, axis=1, keepdims=True))
    alpha = jnp.exp2(m_prev - m_new)
    p = jnp.exp2(s - jnp.concatenate([m_new] * (width // LANES), axis=1))
    l_sc[...] = alpha * l_sc[...] + jnp.sum(p, axis=1, keepdims=True)
    reps = acc_sc.shape[1] // LANES
    acc_sc[...] = jnp.concatenate([alpha] * reps, axis=1) * acc_sc[...] + v_dot(p.astype(BF16))
    m_sc[...] = m_new


def _subln(att, g, beta):
    return _rms(att, g) * (1.0 - LAM_INIT) * beta


def _sample_page_copies(pt_ref, pool_k, pool_v, kbuf, vbuf, sem, chunk, slot, chunks_per_seq):
    seq = chunk // chunks_per_seq
    first_page = (chunk % chunks_per_seq) * PAGES_PER_STEP
    copies = []
    for i in range(PAGES_PER_STEP):
        page = pt_ref[seq, first_page + i]
        copies.append(pltpu.make_async_copy(pool_k.at[page], kbuf.at[slot, i], sem.at[0, slot]))
        copies.append(pltpu.make_async_copy(pool_v.at[page], vbuf.at[slot, i], sem.at[1, slot]))
    return copies


def _attn_kernel(pt_ref, qt_ref, k_ref, vt_ref, bias_ref, lam_ref, g_ref, beta_ref,
                 sq_ref, pool_k, pool_v, kn_ref, vn_ref, sbias_ref, o_ref, so_ref,
                 m_sc, acc_sc, kbuf, vbuf, sem, sm_sc, sl_sc, sacc_sc, cnt_ref):
    n_heads, e, blk = qt_ref.shape
    qi = pl.program_id(2)
    n_seq_s, chunks_per_seq = so_ref.shape[0], pt_ref.shape[1] // PAGES_PER_STEP
    n_chunks = n_seq_s * chunks_per_seq
    first_grid_step = jnp.logical_and(pl.program_id(0) == 0, qi == 0)
    last_grid_step = jnp.logical_and(pl.program_id(0) == pl.num_programs(0) - 1, qi == pl.num_programs(2) - 1)
    ring = (pt_ref, pool_k, pool_v, kbuf, vbuf, sem)

    n_slots = kbuf.shape[0]
    lookahead = n_slots - 1
    last_chunk = n_chunks - 1

    @pl.when(first_grid_step)
    def _():
        cnt_ref[0] = 0
        for d in range(lookahead):
            for cp in _sample_page_copies(*ring, min(d, last_chunk), d, chunks_per_seq):
                cp.start()

    s_lane = lax.broadcasted_iota(jnp.int32, sq_ref.shape[1:], 1)
    s_row = lax.broadcasted_iota(jnp.int32, sq_ref.shape[1:], 0)

    def sample_fetch():
        n = cnt_ref[0]
        slot = lax.rem(n, n_slots)
        valid = n < n_chunks
        chunk = jnp.minimum(n, last_chunk)
        seq = chunk // chunks_per_seq
        c = chunk % chunks_per_seq
        for cp in _sample_page_copies(*ring, chunk, slot, chunks_per_seq):
            cp.wait()
        ahead_slot = lax.rem(n + lookahead, n_slots)
        for cp in _sample_page_copies(*ring, jnp.minimum(n + lookahead, last_chunk), ahead_slot, chunks_per_seq):
            cp.start()
        cnt_ref[0] = n + 1
        return slot, seq, c, valid

    def sample_logits(slot, seq, c, valid):
        q = sq_ref[seq]
        qm = jnp.where((s_lane >> 6) == (s_row & 1), q, jnp.zeros_like(q))
        gate = jnp.where(valid, 0.0, MASKED)
        far_bias = sbias_ref[0] + gate
        last_bias = jnp.where(c == chunks_per_seq - 1, sbias_ref[1] + gate, far_bias)
        s = jnp.concatenate(
            [_nt_dot(qm, kbuf[slot, i].astype(BF16)) + (last_bias if i == PAGES_PER_STEP - 1 else far_bias)
             for i in range(PAGES_PER_STEP)], axis=1)
        return qm, s

    def sample_update(slot, seq, c, valid, qm, s):
        fresh = c == 0
        m_prev = jnp.where(fresh, MASKED, sm_sc[...])
        m_new = jnp.maximum(m_prev, jnp.max(s, axis=1, keepdims=True))
        alpha = jnp.exp2(m_prev - m_new)
        p = jnp.exp2(s - jnp.concatenate([m_new] * (s.shape[1] // LANES), axis=1))
        sl_sc[...] = alpha * jnp.where(fresh, 0.0, sl_sc[...]) + jnp.sum(p, axis=1, keepdims=True)
        pb = p.astype(BF16)
        cols = kbuf.shape[2]
        pv = None
        for i in range(PAGES_PER_STEP):
            part = jnp.dot(pb[:, i * cols:(i + 1) * cols], vbuf[slot, i].astype(BF16),
                           preferred_element_type=F32)
            pv = part if pv is None else pv + part
        sacc_sc[...] = alpha * jnp.where(fresh, 0.0, sacc_sc[...]) + pv
        sm_sc[...] = m_new
        return seq, qm, jnp.logical_and(valid, c == chunks_per_seq - 1)

    def sample_finish(seq, qm, seq_done):
        @pl.when(seq_done)
        def _():
            n_new = kn_ref.shape[1]
            s_new = _nt_dot(qm, kn_ref[seq].astype(BF16)) + sbias_ref[2][:, :n_new]
            _softmax_update(s_new, lambda pn: jnp.dot(pn, vn_ref[seq].astype(BF16), preferred_element_type=F32),
                            sm_sc, sl_sc, sacc_sc)
            sign = jnp.where((s_row & 1) == 0, 1.0, -lam_ref[0:1, 0:1])
            z = sacc_sc[...] / sl_sc[...] * sign
            out_row = lax.broadcasted_iota(jnp.int32, (so_ref.shape[1], ATT_V_DIM), 0)
            for h in range(N_ATT_HEADS):
                att = jnp.zeros(out_row.shape, F32)
                for t in range(MAX_NEW_TOKENS):
                    r0 = t * 8 + h * 2
                    att = jnp.where(out_row == t, jnp.broadcast_to(z[r0:r0 + 1] + z[r0 + 1:r0 + 2], att.shape),
                                    att)
                sl = slice(h * ATT_V_DIM, (h + 1) * ATT_V_DIM)
                so_ref[seq, :, sl] = _subln(att, g_ref[...], beta_ref[:, sl])

    ones_rows = jnp.ones((acc_sc.shape[2] - e, blk), BF16)
    sub = lax.broadcasted_iota(jnp.int32, qt_ref.shape[1:], 0)
    qst = []
    for h in range(n_heads):
        qt = qt_ref[h]
        zero = jnp.zeros_like(qt)
        qst.append(jnp.concatenate([jnp.where(sub < DIFF_HEAD_DIM, qt, zero),
                                    jnp.where(sub >= DIFF_HEAD_DIM, qt, zero)], axis=1))
    m_sc[...] = jnp.full(m_sc.shape, MASKED, F32)
    acc_sc[0] = jnp.zeros(acc_sc.shape[1:], F32)

    def step(j, n_blk, biased, stale_max, src):
        rows = pl.ds(pl.multiple_of(j * blk, blk), n_blk * blk)
        dst = 1 - src
        chunk_state = sample_fetch()

        def logits(h):
            st = jnp.dot(k_ref[h, rows, :], qst[h], preferred_element_type=F32)
            if biased:
                bias = bias_ref[h, (2 - n_blk) * blk:, :]
                st = st + jnp.concatenate([bias, bias], axis=1)
            return st

        def weighted_values(h, pb):
            pv = None
            for i in range(n_blk):
                vt_aug = jnp.concatenate([vt_ref[h, j + i], ones_rows], axis=0)
                part = jnp.dot(vt_aug, pb[i * blk:(i + 1) * blk], preferred_element_type=F32)
                pv = part if pv is None else pv + part
            return pv

        def exact_update(h, st):
            m_prev = m_sc[h]
            m_new = jnp.maximum(m_prev, jnp.max(st, axis=0, keepdims=True))
            alpha = jnp.exp2(m_prev - m_new)
            pv = weighted_values(h, jnp.exp2(st - m_new).astype(BF16))
            acc_sc[dst, h] = alpha * acc_sc[src, h] + pv
            m_sc[h] = m_new

        def stale_update(h, st):
            m_ref = m_sc[h]
            pv = weighted_values(h, jnp.exp2(st - m_ref).astype(BF16))
            acc_sc[dst, h] = acc_sc[src, h] + pv
            return jnp.max(st, axis=0, keepdims=True) - m_ref

        ahead = 2
        pending = [logits(h) for h in range(min(ahead, n_heads))]
        chunk_logits = sample_logits(*chunk_state)
        excess = None
        for h in range(n_heads):
            if h + ahead < n_heads:
                pending.append(logits(h + ahead))
            if stale_max:
                over = stale_update(h, pending[h])
                excess = over if excess is None else jnp.maximum(excess, over)
            else:
                exact_update(h, pending[h])
            if h == 0:
                sample_done = sample_update(*chunk_state, *chunk_logits)

        sample_finish(*sample_done)
        if stale_max:
            @pl.when(jnp.max(excess) > STALE_MAX_MARGIN)
            def _():
                for h in range(n_heads):
                    exact_update(h, logits(h))

    odd = (qi + 1) & 1
    n_pairs = jnp.maximum(((qi + 1) >> 1) - 1, 0)

    @pl.when(qi == 0)
    def _():
        step(0, 1, True, False, 0)

    @pl.when(qi > 0)
    def _():
        step(qi - 1, 2, True, False, 0)

    def far_pair(i, carry):
        step(odd + 2 * i, 2, False, True, (i + 1) & 1)
        return carry

    lax.fori_loop(0, n_pairs, far_pair, 0)
    single = jnp.logical_and(qi > 0, odd == 1)

    @pl.when(single)
    def _():
        step(0, 1, False, True, (n_pairs + 1) & 1)

    res = (n_pairs + 1 + single.astype(jnp.int32)) & 1
    for h in range(n_heads):
        ot = acc_sc[res, h, 0:e] / acc_sc[res, h, e:e + 1]
        att = (ot[:, :blk] - lam_ref[0:1, 0:1] * ot[:, blk:]).T
        sl = slice(h * ATT_V_DIM, (h + 1) * ATT_V_DIM)
        o_ref[:, sl] = _subln(att, g_ref[...], beta_ref[:, sl]).astype(o_ref.dtype)

    @pl.when(last_grid_step)
    def _():
        def drain(i, carry):
            chunk_state = sample_fetch()
            sample_finish(*sample_update(*chunk_state, *sample_logits(*chunk_state)))
            return carry

        lax.fori_loop(0, jnp.maximum(n_chunks - cnt_ref[0], 0), drain, 0)
        n = cnt_ref[0]
        for d in range(lookahead):
            for cp in _sample_page_copies(*ring, jnp.minimum(n + d, last_chunk), lax.rem(n + d, n_slots),
                                          chunks_per_seq):
                cp.wait()


def _attention(qt, k, vt, bias, lam, subln_g, beta_att, page_table, q_rows, pool_k, pool_v, k_new, v_new,
               s_bias):
    b, h, t, e = k.shape
    blk = bias.shape[-1]
    n_blk = t // blk
    hs = ATT_HEADS_PER_STEP
    assert hs == h, "the sample epilogue reads every head's beta from the prompt block"
    n_seq, n_pages = page_table.shape
    rows = q_rows.shape[1]
    page_rows = pool_k.shape[1]
    npg = PAGES_PER_STEP
    assert n_pages % npg == 0
    once = pl.Buffered(1)
    whole = lambda a: pl.BlockSpec(a.shape, lambda i, j, n, pt: (0,) * a.ndim, pipeline_mode=once)
    sample_out = jax.ShapeDtypeStruct((n_seq, SAMPLE_ROWS, h * e), F32)
    grid_spec = pltpu.PrefetchScalarGridSpec(
        num_scalar_prefetch=1,
        grid=(b, h // hs, n_blk),
        in_specs=[pl.BlockSpec((None, hs, None, e, blk), lambda i, j, n, pt: (i, j, n, 0, 0)),
                  pl.BlockSpec((None, hs, t, e), lambda i, j, n, pt: (i, j, 0, 0), pipeline_mode=once),
                  pl.BlockSpec((None, hs, n_blk, e, blk), lambda i, j, n, pt: (i, j, 0, 0, 0), pipeline_mode=once),
                  pl.BlockSpec((hs, 2 * blk, blk), lambda i, j, n, pt: (j, 0, 0), pipeline_mode=once),
                  whole(lam), whole(subln_g),
                  pl.BlockSpec((1, hs * e), lambda i, j, n, pt: (0, j)),
                  whole(q_rows),
                  pl.BlockSpec(memory_space=pl.ANY), pl.BlockSpec(memory_space=pl.ANY),
                  whole(k_new), whole(v_new), whole(s_bias)],
        out_specs=(pl.BlockSpec((None, blk, hs * e), lambda i, j, n, pt: (i, n, j)),
                   pl.BlockSpec(sample_out.shape, lambda i, j, n, pt: (0, 0, 0))),
        scratch_shapes=[pltpu.VMEM((hs, 1, 2 * blk), F32),
                        pltpu.VMEM((2, hs, e + BF16_ROWS, 2 * blk), F32),
                        pltpu.VMEM((SAMPLE_RING_SLOTS, npg, page_rows, e), F32),
                        pltpu.VMEM((SAMPLE_RING_SLOTS, npg, page_rows, e), F32),
                        pltpu.SemaphoreType.DMA((2, SAMPLE_RING_SLOTS)),
                        pltpu.VMEM((rows, LANES), F32), pltpu.VMEM((rows, LANES), F32),
                        pltpu.VMEM((rows, e), F32),
                        pltpu.SMEM((1,), jnp.int32)],
    )
    est = (hs * (2 * t * e * 2) + hs * 2 * blk * blk * 4 + 4 * hs * 2 * blk * 2 * blk * 4
           + SAMPLE_RING_SLOTS * 2 * npg * page_rows * e * 4 + 2 * k_new.size * 4 + 8 * rows * npg * page_rows * 4)
    return pl.pallas_call(
        _attn_kernel,
        grid_spec=grid_spec,
        out_shape=(jax.ShapeDtypeStruct((b, t, h * e), BF16), sample_out),
        compiler_params=pltpu.CompilerParams(dimension_semantics=("arbitrary", "arbitrary", "arbitrary"),
                                             vmem_limit_bytes=_vmem_limit(est)),
        name="attention",
    )(page_table, qt, k, vt, bias, lam, subln_g, beta_att, q_rows, pool_k, pool_v, k_new, v_new, s_bias)


def _mem_kv_kernel(mem_ref, wk_ref, wv_ref, k_ref, v_ref):
    m = mem_ref[...].astype(BF16)
    k_ref[...] = jnp.dot(m, wk_ref[...], preferred_element_type=F32)
    v_ref[...] = jnp.dot(m, wv_ref[...], preferred_element_type=F32)


def _mem_kv(mem, w_xk, w_xv):
    b, n, d = mem.shape
    blk = pl.BlockSpec((None, n, d), lambda i: (i, 0, 0))
    w = pl.BlockSpec((d, d), lambda i: (0, 0))
    out = jax.ShapeDtypeStruct((b, n, d), F32)
    return pl.pallas_call(
        _mem_kv_kernel,
        grid=(b,),
        in_specs=[blk, w, w],
        out_specs=(blk, blk),
        out_shape=(out, out),
        compiler_params=pltpu.CompilerParams(dimension_semantics=("parallel",),
                                             vmem_limit_bytes=_vmem_limit(4 * d * d * 2 + 6 * n * d * 4)),
        name="mem_kv",
    )(mem, w_xk, w_xv)


def _mix_out_kernel(att_ref, conv_ref, h_ref, wo_ref, g_post_ref, g_x_ref, wq_ref, h1_ref, qx_ref):
    half = att_ref.shape[1]
    mo = (jnp.dot(att_ref[...].astype(BF16), wo_ref[0:half, :], preferred_element_type=F32)
          + jnp.dot(conv_ref[...].astype(BF16), wo_ref[half:, :], preferred_element_type=F32))
    h1 = h_ref[...] + _rms(mo, g_post_ref[...])
    h1_ref[...] = h1
    xn = _rms(h1, g_x_ref[...]).astype(BF16)
    x_scale = (wq_ref.shape[1] // N_X_HEADS) ** -0.5
    qx_ref[...] = (jnp.dot(xn, wq_ref[...], preferred_element_type=F32) * x_scale).astype(BF16)


def _mix_out(att, conv, h, w_out, g_post, g_x, w_xq):
    m, d = h.shape
    tm = min(ROW_TILE, m)
    row = lambda w: pl.BlockSpec((tm, w), lambda i: (i, 0))
    const = lambda a: pl.BlockSpec(a.shape, lambda i: (0, 0))
    est = 2 * 2 * d * d * 2 + 2 * tm * (2 * d * 4 + d * 2 + att.shape[1] * 6) + 4 * tm * d * 4
    return pl.pallas_call(
        _mix_out_kernel,
        grid=(m // tm,),
        in_specs=[row(att.shape[1]), row(conv.shape[1]), row(d), const(w_out), const(g_post), const(g_x),
                  const(w_xq)],
        out_specs=(row(d), row(d)),
        out_shape=(jax.ShapeDtypeStruct((m, d), F32), jax.ShapeDtypeStruct((m, d), BF16)),
        compiler_params=pltpu.CompilerParams(dimension_semantics=("parallel",),
                                             vmem_limit_bytes=_vmem_limit(est)),
        name="mix_out",
    )(att, conv, h, w_out, g_post, g_x, w_xq)


def _xattn_kernel(q_ref, mk_ref, mv_ref, o_ref):
    d = q_ref.shape[1]
    hd = d // N_X_HEADS
    for h in range(N_X_HEADS):
        sl = slice(h * hd, (h + 1) * hd)
        s = _nt_dot(q_ref[:, sl], mk_ref[:, sl].astype(BF16))
        p = jnp.exp(s - jnp.max(s, axis=1, keepdims=True))
        l = jnp.sum(p, axis=1, keepdims=True)
        o = jnp.dot(p.astype(BF16), mv_ref[:, sl].astype(BF16), preferred_element_type=F32)
        o_ref[:, sl] = (o / l).astype(o_ref.dtype)


def _xattn(qx, mem_k, mem_v):
    b, t, d = qx.shape
    n = mem_k.shape[1]
    tm = min(ROW_TILE, t)
    row = pl.BlockSpec((None, tm, d), lambda i, j: (i, j, 0))
    mem = pl.BlockSpec((None, n, d), lambda i, j: (i, 0, 0))
    return pl.pallas_call(
        _xattn_kernel,
        grid=(b, t // tm),
        in_specs=[row, mem, mem],
        out_specs=row,
        out_shape=jax.ShapeDtypeStruct((b, t, d), BF16),
        compiler_params=pltpu.CompilerParams(dimension_semantics=("parallel", "parallel"),
                                             vmem_limit_bytes=_vmem_limit(4 * n * d * 4 + 8 * tm * d * 4)),
        name="xattn",
    )(qx, mem_k, mem_v)


def _ffn_kernel(o_ref, h1_ref, wxo_ref, g_xpost_ref, g_pre_ref, wg_ref, wu_ref, wd_ref, g_post_ref, y_ref):
    h2 = h1_ref[...] + _rms(jnp.dot(o_ref[...], wxo_ref[...], preferred_element_type=F32), g_xpost_ref[...])
    xf = _rms(h2, g_pre_ref[...]).astype(BF16)
    d_ff = wg_ref.shape[1]
    f = jnp.zeros(h2.shape, F32)
    for c0 in range(0, d_ff, FFN_CHUNK):
        sl = slice(c0, c0 + FFN_CHUNK)
        g = jnp.dot(xf, wg_ref[:, sl], preferred_element_type=F32)
        u = jnp.dot(xf, wu_ref[:, sl], preferred_element_type=F32)
        a = (g * _sigmoid(g) * u).astype(BF16)
        f = f + jnp.dot(a, wd_ref[sl, :], preferred_element_type=F32)
    y_ref[...] = h2 + _rms(f, g_post_ref[...])


def _ffn(o, h1, w_xo, g_xpost, g_pre, w_gate, w_up, w_down, g_post):
    m, d = h1.shape
    d_ff = w_gate.shape[1]
    assert d_ff % FFN_CHUNK == 0
    tm = min(ROW_TILE, m)
    row = pl.BlockSpec((tm, d), lambda i: (i, 0))
    const = lambda a: pl.BlockSpec(a.shape, lambda i: (0, 0), pipeline_mode=pl.Buffered(1))
    est = (d * d + 3 * d * d_ff) * 2 + 2 * tm * d * (2 + 4 + 4) + 6 * tm * d * 4
    return pl.pallas_call(
        _ffn_kernel,
        grid=(m // tm,),
        in_specs=[row, row, const(w_xo), const(g_xpost), const(g_pre), const(w_gate), const(w_up),
                  const(w_down), const(g_post)],
        out_specs=row,
        out_shape=jax.ShapeDtypeStruct((m, d), F32),
        compiler_params=pltpu.CompilerParams(dimension_semantics=("parallel",),
                                             vmem_limit_bytes=_vmem_limit(est)),
        name="ffn",
    )(o, h1, w_xo, g_xpost, g_pre, w_gate, w_up, w_down, g_post)


def kernel(x_prompt, x_sample, mem_prompt, cache_k, cache_v, state_conv, cache_mem_k, cache_mem_v, page_table, rel_bias_table, norm_mix_pre, norm_mix_post, w_in, lambda_q1, lambda_k1, lambda_q2, lambda_k2, subln_g, dw_w, dw_b, conv_ln_g, conv_ln_b, beta_att, beta_conv, w_out, norm_x_pre, norm_x_post, w_xq, w_xk, w_xv, w_xo, norm_ffn_pre, norm_ffn_post, w_gate, w_up, w_down):
    assert w_in.shape[0] == 1, "single-layer trunk"
    bp, tp, d = x_prompt.shape
    bs, ts, _ = x_sample.shape
    assert ts <= MAX_NEW_TOKENS and tp >= CONV_K - 1
    n_mem = mem_prompt.shape[1]
    vec = lambda a: a[0].reshape(1, -1)
    wb = lambda a: a[0].astype(BF16)
    g_mix_pre, g_mix_post = vec(norm_mix_pre), vec(norm_mix_post)
    g_x_pre, g_x_post = vec(norm_x_pre), vec(norm_x_post)
    g_ffn_pre, g_ffn_post = vec(norm_ffn_pre), vec(norm_ffn_post)
    sub_g, b_att, b_conv = vec(subln_g), vec(beta_att), vec(beta_conv)
    c_b, c_g, c_bt = vec(dw_b), vec(conv_ln_g), vec(conv_ln_b)
    w_in_b, w_out_b, w_xq_b, w_xk_b, w_xv_b, w_xo_b = (wb(w) for w in (w_in, w_out, w_xq, w_xk, w_xv, w_xo))
    w_gate_b, w_up_b, w_down_b = wb(w_gate), wb(w_up), wb(w_down)
    dw = dw_w[0]

    blk = min(ATT_BLOCK, tp)
    lam_vecs = jnp.stack([lambda_q1[0], lambda_k1[0], lambda_q2[0], lambda_k2[0]])
    p_bias, s_bias, lam = _bias_tiles(rel_bias_table, lam_vecs, blk, ts)

    def tail(att, conv, h, mem_k, mem_v):
        b, t, _ = h.shape
        flat = lambda a: a.reshape(b * t, a.shape[-1])
        h1, qx = _mix_out(flat(att), flat(conv), flat(h), w_out_b, g_mix_post, g_x_pre, w_xq_b)
        o = _xattn(qx.reshape(b, t, d), mem_k, mem_v)
        y = _ffn(flat(o), h1, w_xo_b, g_x_post, g_ffn_pre, w_gate_b, w_up_b, w_down_b, g_ffn_post)
        return y.reshape(b, t, d)

    conv0 = jnp.zeros((bp, CONV_HALO, CONV_WIDTH), F32)
    qt_p, kb_p, vt_p, kf_p, vf_p, glu_p, conv_p = _in_proj(x_prompt, g_mix_pre, w_in_b, blk,
                                                          conv=(conv0, dw, c_b, c_g, c_bt, b_conv))
    x_s = jnp.pad(x_sample, ((0, 0), (0, SAMPLE_ROWS - ts), (0, 0)))
    rows_s = bs * SAMPLE_ROWS
    qt_s, _, _, kf_s, vf_s, glu_s = _in_proj(x_s.reshape(1, rows_s, d), g_mix_pre, w_in_b, min(blk, rows_s))
    new_rows = ts * N_ATT_HEADS
    kf_s = kf_s.reshape(bs, SAMPLE_ROWS * N_ATT_HEADS, ATT_V_DIM)[:, :new_rows]
    vf_s = vf_s.reshape(bs, SAMPLE_ROWS * N_ATT_HEADS, ATT_V_DIM)[:, :new_rows]
    glu_s = glu_s.reshape(bs, SAMPLE_ROWS, CONV_WIDTH)[:, :ts]
    q_s = jnp.transpose(qt_s[0], (0, 1, 3, 2))
    q_th = jnp.transpose(q_s.reshape(N_ATT_HEADS, bs, SAMPLE_ROWS, ATT_V_DIM), (1, 2, 0, 3))
    q_th = jnp.pad(q_th[:, :ts], ((0, 0), (0, MAX_NEW_TOKENS - ts), (0, 0), (0, 0)))
    q_rows = jnp.repeat(q_th.reshape(bs, MAX_NEW_TOKENS * N_ATT_HEADS, ATT_V_DIM), 2, axis=1)
    pad_page = lambda a: jnp.pad(a, ((0, 0), (0, PAGE_SIZE - new_rows), (0, 0)))
    n_phys = cache_k.shape[1]
    pool = lambda c: c.reshape(n_phys, PAGE_SIZE * N_ATT_HEADS, ATT_V_DIM)

    att_p, att_s = _attention(qt_p, kb_p, vt_p, p_bias, lam, sub_g, b_att, page_table, q_rows,
                              pool(cache_k), pool(cache_v), pad_page(kf_s), pad_page(vf_s), s_bias)

    mk_p, mv_p = _mem_kv(mem_prompt, w_xk_b, w_xv_b)
    y_p = tail(att_p, conv_p, x_prompt, mk_p, mv_p)

    xp_s = jnp.concatenate([state_conv[0], glu_s], axis=1)
    conv_s = _conv_sample(jnp.transpose(xp_s, (1, 0, 2)), ts, dw, c_b, c_g, c_bt, b_conv)
    conv_s = jnp.pad(jnp.transpose(conv_s, (1, 0, 2)), ((0, 0), (0, SAMPLE_ROWS - ts), (0, 0)))
    mem_s = lambda c: c[0].reshape(bs, n_mem, d)
    y_s = tail(att_s, conv_s, x_s, mem_s(cache_mem_k), mem_s(cache_mem_v))

    heads = lambda a: a.reshape(1, a.shape[0], a.shape[1] // N_ATT_HEADS, N_ATT_HEADS, ATT_V_DIM)
    mem_heads = lambda a: a.reshape(1, bp, n_mem, N_X_HEADS, d // N_X_HEADS)
    return (y_p, y_s[:, :ts],
            heads(kf_p), heads(vf_p), glu_p[None, :, tp - (CONV_K - 1):],
            mem_heads(mk_p), mem_heads(mv_p),
            heads(kf_s), heads(vf_s), xp_s[None, :, ts:])
```

```python
import functools
import math

import numpy as np
import jax
import jax.numpy as jnp
from jax import lax
from jax.experimental import pallas as pl
from jax.experimental.pallas import tpu as pltpu

F32 = jnp.float32
BF16 = jnp.bfloat16

DIFF_HEAD_DIM = 64
ATT_V_DIM = 2 * DIFF_HEAD_DIM
N_ATT_HEADS = 4
ATT_WIDTH = N_ATT_HEADS * ATT_V_DIM
CONV_WIDTH = 512
CONV_K = 31
N_BUCKETS = 32
MAX_DISTANCE = 128
N_X_HEADS = 4
PAGE_SIZE = 128
RMS_EPS = 1e-6
LN_EPS = 1e-5
ATT_SCALE = DIFF_HEAD_DIM ** -0.5
LOG2_E = math.log2(math.e)
LAM_INIT = 0.8 - 0.6 * math.exp(-0.3 * 0)
MASKED = -1e30
STALE_MAX_MARGIN = 30.0

V7X_VMEM_BYTES = 64 * 1024 * 1024
LANES = 128
SUBLANES = 8
BF16_ROWS = 16

ROW_TILE = 512
MIX_ROW_TILE = 1024
ATT_BLOCK = 256
ATT_HEADS_PER_STEP = 4
CONV_CHUNK = 64
CONV_HALO = 32
PAGES_PER_STEP = 8
SAMPLE_RING_SLOTS = 3
FFN_CHUNK = 256
SAMPLE_ROWS = 8
MAX_NEW_TOKENS = 4
SAMPLE_Q_ROWS = MAX_NEW_TOKENS * 2 * N_ATT_HEADS


def _vmem_limit(nbytes):
    return int(min(max(2 * nbytes, 16 * 1024 * 1024), V7X_VMEM_BYTES - 8 * 1024 * 1024))


def _rms(x, g):
    return x * lax.rsqrt(jnp.mean(x * x, axis=-1, keepdims=True) + RMS_EPS) * g


def _sigmoid(x):
    return 1.0 / (1.0 + jnp.exp(-x))


def _nt_dot(a, b):
    return lax.dot_general(a, b, (((1,), (1,)), ((), ())), preferred_element_type=F32)


def _bucket_np(n):
    n = np.maximum(n, 0)
    max_exact = N_BUCKETS // 2
    nf = np.maximum(n, 1).astype(np.float32)
    large = max_exact + (np.log(nf / max_exact) / math.log(MAX_DISTANCE / max_exact)
                         * (N_BUCKETS - max_exact)).astype(np.int32)
    large = np.minimum(large, N_BUCKETS - 1)
    return np.where(n < max_exact, n, large).astype(np.int32)


def _prompt_codes(blk):
    i = np.arange(blk)[None, :]
    j = np.arange(blk)[:, None]
    prev = _bucket_np(i - j + blk)
    diag = np.where(j > i, -1, _bucket_np(i - j))
    return np.concatenate([prev, diag]).astype(np.int32)


def _sample_codes(n_new):
    r = np.arange(SAMPLE_Q_ROWS)[:, None]
    c = np.arange(PAGE_SIZE * N_ATT_HEADS)[None, :]
    t, head = r // 8, (r // 2) % N_ATT_HEADS
    tok, key_head = c // N_ATT_HEADS, c % N_ATT_HEADS
    own = key_head == head
    far = np.where(own, N_BUCKETS - 1, -1)
    last = np.where(own, _bucket_np(t + PAGE_SIZE - tok), -1)
    new = np.where(own & (tok <= t) & (tok < n_new) & (c < PAGE_SIZE), _bucket_np(t - tok), -1)
    return np.stack([far, last, new]).astype(np.int32)


def _bias_kernel(tab_ref, lam_in_ref, pcode_ref, scode_ref, pbias_ref, sbias_ref, lam_ref):
    far = N_BUCKETS - 1

    def lookup(code, h):
        out = jnp.zeros(code.shape, F32)
        for b in range(far):
            out = jnp.where(code == b, (tab_ref[b, h] - tab_ref[far, h]) * LOG2_E, out)
        return jnp.where(code < 0, MASKED, out)

    for h in range(N_ATT_HEADS):
        pbias_ref[h] = lookup(pcode_ref[...], h)
    row_head = (lax.broadcasted_iota(jnp.int32, scode_ref.shape[1:], 0) >> 1) & (N_ATT_HEADS - 1)
    for i in range(scode_ref.shape[0]):
        code = scode_ref[i]
        out = jnp.zeros(code.shape, F32)
        for h in range(N_ATT_HEADS):
            out = jnp.where(row_head == h, lookup(code, h), out)
        sbias_ref[i] = out
    lv = lam_in_ref[...]
    d1 = jnp.sum(lv[0:1] * lv[1:2], axis=1, keepdims=True)
    d2 = jnp.sum(lv[2:3] * lv[3:4], axis=1, keepdims=True)
    lam = jnp.exp(d1) - jnp.exp(d2) + LAM_INIT
    lam_ref[...] = jnp.broadcast_to(lam, lam_ref.shape)


def _bias_tiles(table, lam_vecs, blk, n_new):
    pcode = jnp.asarray(_prompt_codes(blk))
    scode = jnp.asarray(_sample_codes(n_new))
    vm = pl.BlockSpec(memory_space=pltpu.VMEM)
    return pl.pallas_call(
        _bias_kernel,
        out_shape=(jax.ShapeDtypeStruct((N_ATT_HEADS, 2 * blk, blk), F32),
                   jax.ShapeDtypeStruct(scode.shape, F32),
                   jax.ShapeDtypeStruct((SUBLANES, LANES), F32)),
        in_specs=[pl.BlockSpec(memory_space=pltpu.SMEM), vm, vm, vm],
        out_specs=(vm, vm, vm),
        name="bias_tiles",
    )(table, lam_vecs, pcode, scode)


def _in_proj_kernel(x_ref, g_ref, w_ref, *refs, with_conv):
    if with_conv:
        conv_in, refs = refs[:6], refs[6:]
        qt_ref, kb_ref, vt_ref, kf_ref, vf_ref, glu_ref, conv_ref, buf, ybuf = refs
    else:
        qt_ref, kb_ref, vt_ref, kf_ref, vf_ref, glu_ref = refs
    xn = _rms(x_ref[...], g_ref[...]).astype(BF16)

    def cols(c):
        return jnp.dot(xn, w_ref[:, c * ATT_WIDTH:(c + 1) * ATT_WIDTH], preferred_element_type=F32)

    tm = xn.shape[0]
    blk = qt_ref.shape[-1]
    glu = cols(3) * _sigmoid(cols(4))
    glu_ref[...] = glu
    if with_conv:
        _conv_tile(glu, *conv_in, conv_ref, buf, ybuf, pl.program_id(1) == 0)
    q = cols(0) * (ATT_SCALE * LOG2_E)
    k = cols(1)
    v = cols(2)
    kb = k.astype(BF16)
    for h in range(N_ATT_HEADS):
        sl = slice(h * ATT_V_DIM, (h + 1) * ATT_V_DIM)
        kb_ref[h] = kb[:, sl]
        for c in range(tm // blk):
            rows = slice(c * blk, (c + 1) * blk)
            qt_ref[h, c] = q[rows, sl].T.astype(BF16)
            vt_ref[h, c] = v[rows, sl].T.astype(BF16)
        kf_ref[pl.ds(h, tm, stride=N_ATT_HEADS), :] = k[:, sl]
        vf_ref[pl.ds(h, tm, stride=N_ATT_HEADS), :] = v[:, sl]


def _in_proj(x, gain, w_in, blk, conv=None):
    b, t, d = x.shape
    tm = min(ROW_TILE, t)
    n_cols = w_in.shape[1]
    assert tm % blk == 0
    c = CONV_WIDTH
    conv = () if conv is None else tuple(conv)
    vec = pl.BlockSpec((1, c), lambda i, j: (0, 0))
    conv_specs = [pl.BlockSpec((None, CONV_HALO, c), lambda i, j: (i, 0, 0)),
                  pl.BlockSpec((CONV_K, c), lambda i, j: (0, 0)), vec, vec, vec, vec] if conv else []
    conv_out = [pl.BlockSpec((None, tm, c), lambda i, j: (i, j, 0))] if conv else []
    conv_shape = [jax.ShapeDtypeStruct((b, t, c), BF16)] if conv else []
    conv_scratch = [pltpu.VMEM((c // LANES, tm + CONV_HALO, LANES), F32),
                    pltpu.VMEM((c // LANES, tm, LANES), F32)] if conv else []
    row = lambda w: pl.BlockSpec((None, tm, w), lambda i, j: (i, j, 0))
    heads = pl.BlockSpec((None, N_ATT_HEADS, tm, ATT_V_DIM), lambda i, j: (i, 0, j, 0))
    hm = jax.ShapeDtypeStruct((b, N_ATT_HEADS, t, ATT_V_DIM), BF16)
    heads_t = pl.BlockSpec((None, N_ATT_HEADS, tm // blk, ATT_V_DIM, blk), lambda i, j: (i, 0, j, 0, 0))
    hm_t = jax.ShapeDtypeStruct((b, N_ATT_HEADS, t // blk, ATT_V_DIM, blk), BF16)
    flat = jax.ShapeDtypeStruct((b, t * N_ATT_HEADS, ATT_V_DIM), F32)
    tok_head = pl.BlockSpec((None, tm * N_ATT_HEADS, ATT_V_DIM), lambda i, j: (i, j, 0))
    est = 2 * d * n_cols * 2 + 2 * tm * (d * 4 + 3 * ATT_WIDTH * 2 + 3 * ATT_WIDTH * 4) + 6 * tm * ATT_WIDTH * 4
    return pl.pallas_call(
        functools.partial(_in_proj_kernel, with_conv=bool(conv)),
        grid=(b, t // tm),
        in_specs=[row(d),
                  pl.BlockSpec((1, d), lambda i, j: (0, 0)),
                  pl.BlockSpec((d, n_cols), lambda i, j: (0, 0))] + conv_specs,
        out_specs=tuple([heads_t, heads, heads_t, tok_head, tok_head, row(CONV_WIDTH)] + conv_out),
        out_shape=tuple([hm_t, hm, hm_t, flat, flat, jax.ShapeDtypeStruct((b, t, CONV_WIDTH), F32)]
                        + conv_shape),
        scratch_shapes=conv_scratch,
        compiler_params=pltpu.CompilerParams(dimension_semantics=("parallel", "arbitrary"),
                                             vmem_limit_bytes=_vmem_limit(est)),
        name="in_proj",
    )(x, gain, w_in, *conv)


def _conv_post(y, b_ref, g_ref, bt_ref, beta_ref):
    y = y + b_ref[...]
    mu = jnp.mean(y, axis=-1, keepdims=True)
    yc = y - mu
    yn = yc * lax.rsqrt(jnp.mean(yc * yc, axis=-1, keepdims=True) + LN_EPS) * g_ref[...] + bt_ref[...]
    return yn * _sigmoid(yn) * beta_ref[...]


def _conv_tile(glu, prev_ref, w_ref, b_ref, g_ref, bt_ref, beta_ref, o_ref, buf, ybuf, first):
    tt = glu.shape[0]
    n_slab = buf.shape[0]
    slabs = [slice(s * LANES, (s + 1) * LANES) for s in range(n_slab)]

    @pl.when(first)
    def _():
        for s in range(n_slab):
            buf[s, 0:CONV_HALO] = prev_ref[:, slabs[s]]

    @pl.when(jnp.logical_not(first))
    def _():
        for s in range(n_slab):
            buf[s, 0:CONV_HALO] = buf[s, tt:tt + CONV_HALO]

    for s in range(n_slab):
        buf[s, CONV_HALO:CONV_HALO + tt] = glu[:, slabs[s]]

    shift = CONV_HALO - (CONV_K - 1)
    rc = min(CONV_CHUNK, tt // 2)
    for c0 in range(0, tt, 2 * rc):
        for parity in range(2):
            start = c0 + parity
            accs = []
            for s in range(n_slab):
                acc = jnp.zeros((rc, LANES), F32)
                for j in range(CONV_K):
                    x = buf[s, pl.ds(start + j + shift, rc, stride=2), :]
                    acc = acc + w_ref[j:j + 1, slabs[s]] * x
                accs.append(acc)
            y = _conv_post(jnp.concatenate(accs, axis=1), b_ref, g_ref, bt_ref, beta_ref)
            for s in range(n_slab):
                ybuf[s, pl.ds(start, rc, stride=2), :] = y[:, slabs[s]]
    for s in range(n_slab):
        o_ref[:, slabs[s]] = ybuf[s].astype(o_ref.dtype)


def _conv_sample_kernel(xp_ref, w_ref, b_ref, g_ref, bt_ref, beta_ref, o_ref):
    n_t = o_ref.shape[0]
    for t in range(n_t):
        acc = jnp.zeros(xp_ref.shape[1:], F32)
        for j in range(CONV_K):
            acc = acc + w_ref[j:j + 1, :] * xp_ref[t + j]
        o_ref[t] = _conv_post(acc, b_ref, g_ref, bt_ref, beta_ref)


def _conv_sample(xp_t, n_t, dw_w, dw_b, ln_g, ln_b, beta):
    vm = pl.BlockSpec(memory_space=pltpu.VMEM)
    return pl.pallas_call(
        _conv_sample_kernel,
        out_shape=jax.ShapeDtypeStruct((n_t,) + xp_t.shape[1:], F32),
        in_specs=[vm] * 6,
        out_specs=vm,
        name="conv_sample",
    )(xp_t, dw_w, dw_b, ln_g, ln_b, beta)


def _softmax_update(s, v_dot, m_sc, l_sc, acc_sc):
    width = s.shape[1]
    m_prev = m_sc[...]
    m_new = jnp.maximum(m_prev, jnp.max(s, axis=1, keepdims=True))
    alpha = jnp.exp2(m_prev - m_new)
    p = jnp.exp2(s - jnp.concatenate([m_new] * (width // LANES), axis=1))
    l_sc[...] = alpha * l_sc[...] + jnp.sum(p, axis=1, keepdims=True)
    reps = acc_sc.shape[1] // LANES
    acc_sc[...] = jnp.concatenate([alpha] * reps, axis=1) * acc_sc[...] + v_dot(p.astype(BF16))
    m_sc[...] = m_new


def _subln(att, g, beta):
    return _rms(att, g) * (1.0 - LAM_INIT) * beta


def _sample_page_copies(pt_ref, pool_k, pool_v, kbuf, vbuf, sem, chunk, slot, chunks_per_seq):
    seq = chunk // chunks_per_seq
    first_page = (chunk % chunks_per_seq) * PAGES_PER_STEP
    copies = []
    for i in range(PAGES_PER_STEP):
        page = pt_ref[seq, first_page + i]
        copies.append(pltpu.make_async_copy(pool_k.at[page], kbuf.at[slot, i], sem.at[0, slot]))
        copies.append(pltpu.make_async_copy(pool_v.at[page], vbuf.at[slot, i], sem.at[1, slot]))
    return copies


def _attn_kernel(pt_ref, qt_ref, k_ref, vt_ref, bias_ref, lam_ref, g_ref, beta_ref,
                 sq_ref, pool_k, pool_v, kn_ref, vn_ref, sbias_ref, o_ref, so_ref,
                 m_sc, acc_sc, kbuf, vbuf, sem, sm_sc, sl_sc, sacc_sc, cnt_ref):
    n_heads, e, blk = qt_ref.shape
    qi = pl.program_id(2)
    n_seq_s, chunks_per_seq = so_ref.shape[0], pt_ref.shape[1] // PAGES_PER_STEP
    n_chunks = n_seq_s * chunks_per_seq
    first_grid_step = jnp.logical_and(pl.program_id(0) == 0, qi == 0)
    last_grid_step = jnp.logical_and(pl.program_id(0) == pl.num_programs(0) - 1, qi == pl.num_programs(2) - 1)
    ring = (pt_ref, pool_k, pool_v, kbuf, vbuf, sem)

    n_slots = kbuf.shape[0]
    lookahead = n_slots - 1
    last_chunk = n_chunks - 1

    @pl.when(first_grid_step)
    def _():
        cnt_ref[0] = 0
        for d in range(lookahead):
            for cp in _sample_page_copies(*ring, min(d, last_chunk), d, chunks_per_seq):
                cp.start()

    s_lane = lax.broadcasted_iota(jnp.int32, sq_ref.shape[1:], 1)
    s_row = lax.broadcasted_iota(jnp.int32, sq_ref.shape[1:], 0)

    def sample_fetch():
        n = cnt_ref[0]
        slot = lax.rem(n, n_slots)
        valid = n < n_chunks
        chunk = jnp.minimum(n, last_chunk)
        seq = chunk // chunks_per_seq
        c = chunk % chunks_per_seq
        for cp in _sample_page_copies(*ring, chunk, slot, chunks_per_seq):
            cp.wait()
        ahead_slot = lax.rem(n + lookahead, n_slots)
        for cp in _sample_page_copies(*ring, jnp.minimum(n + lookahead, last_chunk), ahead_slot, chunks_per_seq):
            cp.start()
        cnt_ref[0] = n + 1
        return slot, seq, c, valid

    def sample_logits(slot, seq, c, valid):
        q = sq_ref[seq]
        qm = jnp.where((s_lane >> 6) == (s_row & 1), q, jnp.zeros_like(q))
        gate = jnp.where(valid, 0.0, MASKED)
        far_bias = sbias_ref[0] + gate
        last_bias = jnp.where(c == chunks_per_seq - 1, sbias_ref[1] + gate, far_bias)
        s = jnp.concatenate(
            [_nt_dot(qm, kbuf[slot, i].astype(BF16)) + (last_bias if i == PAGES_PER_STEP - 1 else far_bias)
             for i in range(PAGES_PER_STEP)], axis=1)
        return qm, s

    def sample_update(slot, seq, c, valid, qm, s):
        fresh = c == 0
        m_prev = jnp.where(fresh, MASKED, sm_sc[...])
        m_new = jnp.maximum(m_prev, jnp.max(s, axis=1, keepdims=True))
        alpha = jnp.exp2(m_prev - m_new)
        p = jnp.exp2(s - jnp.concatenate([m_new] * (s.shape[1] // LANES), axis=1))
        sl_sc[...] = alpha * jnp.where(fresh, 0.0, sl_sc[...]) + jnp.sum(p, axis=1, keepdims=True)
        pb = p.astype(BF16)
        cols = kbuf.shape[2]
        pv = None
        for i in range(PAGES_PER_STEP):
            part = jnp.dot(pb[:, i * cols:(i + 1) * cols], vbuf[slot, i].astype(BF16),
                           preferred_element_type=F32)
            pv = part if pv is None else pv + part
        sacc_sc[...] = alpha * jnp.where(fresh, 0.0, sacc_sc[...]) + pv
        sm_sc[...] = m_new
        return seq, qm, jnp.logical_and(valid, c == chunks_per_seq - 1)

    def sample_finish(seq, qm, seq_done):
        @pl.when(seq_done)
        def _():
            n_new = kn_ref.shape[1]
            s_new = _nt_dot(qm, kn_ref[seq].astype(BF16)) + sbias_ref[2][:, :n_new]
            _softmax_update(s_new, lambda pn: jnp.dot(pn, vn_ref[seq].astype(BF16), preferred_element_type=F32),
                            sm_sc, sl_sc, sacc_sc)
            sign = jnp.where((s_row & 1) == 0, 1.0, -lam_ref[0:1, 0:1])
            z = sacc_sc[...] / sl_sc[...] * sign
            out_row = lax.broadcasted_iota(jnp.int32, (so_ref.shape[1], ATT_V_DIM), 0)
            for h in range(N_ATT_HEADS):
                att = jnp.zeros(out_row.shape, F32)
                for t in range(MAX_NEW_TOKENS):
                    r0 = t * 8 + h * 2
                    att = jnp.where(out_row == t, jnp.broadcast_to(z[r0:r0 + 1] + z[r0 + 1:r0 + 2], att.shape),
                                    att)
                sl = slice(h * ATT_V_DIM, (h + 1) * ATT_V_DIM)
                so_ref[seq, :, sl] = _subln(att, g_ref[...], beta_ref[:, sl])

    ones_rows = jnp.ones((acc_sc.shape[2] - e, blk), BF16)
    sub = lax.broadcasted_iota(jnp.int32, qt_ref.shape[1:], 0)
    qst = []
    for h in range(n_heads):
        qt = qt_ref[h]
        zero = jnp.zeros_like(qt)
        qst.append(jnp.concatenate([jnp.where(sub < DIFF_HEAD_DIM, qt, zero),
                                    jnp.where(sub >= DIFF_HEAD_DIM, qt, zero)], axis=1))
    m_sc[...] = jnp.full(m_sc.shape, MASKED, F32)
    acc_sc[0] = jnp.zeros(acc_sc.shape[1:], F32)

    def step(j, n_blk, biased, stale_max, src):
        rows = pl.ds(pl.multiple_of(j * blk, blk), n_blk * blk)
        dst = 1 - src
        chunk_state = sample_fetch()

        def logits(h):
            st = jnp.dot(k_ref[h, rows, :], qst[h], preferred_element_type=F32)
            if biased:
                bias = bias_ref[h, (2 - n_blk) * blk:, :]
                st = st + jnp.concatenate([bias, bias], axis=1)
            return st

        def weighted_values(h, pb):
            pv = None
            for i in range(n_blk):
                vt_aug = jnp.concatenate([vt_ref[h, j + i], ones_rows], axis=0)
                part = jnp.dot(vt_aug, pb[i * blk:(i + 1) * blk], preferred_element_type=F32)
                pv = part if pv is None else pv + part
            return pv

        def exact_update(h, st):
            m_prev = m_sc[h]
            m_new = jnp.maximum(m_prev, jnp.max(st, axis=0, keepdims=True))
            alpha = jnp.exp2(m_prev - m_new)
            pv = weighted_values(h, jnp.exp2(st - m_new).astype(BF16))
            acc_sc[dst, h] = alpha * acc_sc[src, h] + pv
            m_sc[h] = m_new

        def stale_update(h, st):
            m_ref = m_sc[h]
            pv = weighted_values(h, jnp.exp2(st - m_ref).astype(BF16))
            acc_sc[dst, h] = acc_sc[src, h] + pv
            return jnp.max(st, axis=0, keepdims=True) - m_ref

        ahead = 2
        pending = [logits(h) for h in range(min(ahead, n_heads))]
        chunk_logits = sample_logits(*chunk_state)
        excess = None
        for h in range(n_heads):
            if h + ahead < n_heads:
                pending.append(logits(h + ahead))
            if stale_max:
                over = stale_update(h, pending[h])
                excess = over if excess is None else jnp.maximum(excess, over)
            else:
                exact_update(h, pending[h])
            if h == 0:
                sample_done = sample_update(*chunk_state, *chunk_logits)

        sample_finish(*sample_done)
        if stale_max:
            @pl.when(jnp.max(excess) > STALE_MAX_MARGIN)
            def _():
                for h in range(n_heads):
                    exact_update(h, logits(h))

    odd = (qi + 1) & 1
    n_pairs = jnp.maximum(((qi + 1) >> 1) - 1, 0)

    @pl.when(qi == 0)
    def _():
        step(0, 1, True, False, 0)

    @pl.when(qi > 0)
    def _():
        step(qi - 1, 2, True, False, 0)

    def far_pair(i, carry):
        step(odd + 2 * i, 2, False, True, (i + 1) & 1)
        return carry

    lax.fori_loop(0, n_pairs, far_pair, 0)
    single = jnp.logical_and(qi > 0, odd == 1)

    @pl.when(single)
    def _():
        step(0, 1, False, True, (n_pairs + 1) & 1)

    res = (n_pairs + 1 + single.astype(jnp.int32)) & 1
    for h in range(n_heads):
        ot = acc_sc[res, h, 0:e] / acc_sc[res, h, e:e + 1]
        att = (ot[:, :blk] - lam_ref[0:1, 0:1] * ot[:, blk:]).T
        sl = slice(h * ATT_V_DIM, (h + 1) * ATT_V_DIM)
        o_ref[:, sl] = _subln(att, g_ref[...], beta_ref[:, sl]).astype(o_ref.dtype)

    @pl.when(last_grid_step)
    def _():
        def drain(i, carry):
            chunk_state = sample_fetch()
            sample_finish(*sample_update(*chunk_state, *sample_logits(*chunk_state)))
            return carry

        lax.fori_loop(0, jnp.maximum(n_chunks - cnt_ref[0], 0), drain, 0)
        n = cnt_ref[0]
        for d in range(lookahead):
            for cp in _sample_page_copies(*ring, jnp.minimum(n + d, last_chunk), lax.rem(n + d, n_slots),
                                          chunks_per_seq):
                cp.wait()


def _attention(qt, k, vt, bias, lam, subln_g, beta_att, page_table, q_rows, pool_k, pool_v, k_new, v_new,
               s_bias):
    b, h, t, e = k.shape
    blk = bias.shape[-1]
    n_blk = t // blk
    hs = ATT_HEADS_PER_STEP
    assert hs == h, "the sample epilogue reads every head's beta from the prompt block"
    n_seq, n_pages = page_table.shape
    rows = q_rows.shape[1]
    page_rows = pool_k.shape[1]
    npg = PAGES_PER_STEP
    assert n_pages % npg == 0
    once = pl.Buffered(1)
    whole = lambda a: pl.BlockSpec(a.shape, lambda i, j, n, pt: (0,) * a.ndim, pipeline_mode=once)
    sample_out = jax.ShapeDtypeStruct((n_seq, SAMPLE_ROWS, h * e), F32)
    grid_spec = pltpu.PrefetchScalarGridSpec(
        num_scalar_prefetch=1,
        grid=(b, h // hs, n_blk),
        in_specs=[pl.BlockSpec((None, hs, None, e, blk), lambda i, j, n, pt: (i, j, n, 0, 0)),
                  pl.BlockSpec((None, hs, t, e), lambda i, j, n, pt: (i, j, 0, 0), pipeline_mode=once),
                  pl.BlockSpec((None, hs, n_blk, e, blk), lambda i, j, n, pt: (i, j, 0, 0, 0), pipeline_mode=once),
                  pl.BlockSpec((hs, 2 * blk, blk), lambda i, j, n, pt: (j, 0, 0), pipeline_mode=once),
                  whole(lam), whole(subln_g),
                  pl.BlockSpec((1, hs * e), lambda i, j, n, pt: (0, j)),
                  whole(q_rows),
                  pl.BlockSpec(memory_space=pl.ANY), pl.BlockSpec(memory_space=pl.ANY),
                  whole(k_new), whole(v_new), whole(s_bias)],
        out_specs=(pl.BlockSpec((None, blk, hs * e), lambda i, j, n, pt: (i, n, j)),
                   pl.BlockSpec(sample_out.shape, lambda i, j, n, pt: (0, 0, 0))),
        scratch_shapes=[pltpu.VMEM((hs, 1, 2 * blk), F32),
                        pltpu.VMEM((2, hs, e + BF16_ROWS, 2 * blk), F32),
                        pltpu.VMEM((SAMPLE_RING_SLOTS, npg, page_rows, e), F32),
                        pltpu.VMEM((SAMPLE_RING_SLOTS, npg, page_rows, e), F32),
                        pltpu.SemaphoreType.DMA((2, SAMPLE_RING_SLOTS)),
                        pltpu.VMEM((rows, LANES), F32), pltpu.VMEM((rows, LANES), F32),
                        pltpu.VMEM((rows, e), F32),
                        pltpu.SMEM((1,), jnp.int32)],
    )
    est = (hs * (2 * t * e * 2) + hs * 2 * blk * blk * 4 + 4 * hs * 2 * blk * 2 * blk * 4
           + SAMPLE_RING_SLOTS * 2 * npg * page_rows * e * 4 + 2 * k_new.size * 4 + 8 * rows * npg * page_rows * 4)
    return pl.pallas_call(
        _attn_kernel,
        grid_spec=grid_spec,
        out_shape=(jax.ShapeDtypeStruct((b, t, h * e), BF16), sample_out),
        compiler_params=pltpu.CompilerParams(dimension_semantics=("arbitrary", "arbitrary", "arbitrary"),
                                             vmem_limit_bytes=_vmem_limit(est)),
        name="attention",
    )(page_table, qt, k, vt, bias, lam, subln_g, beta_att, q_rows, pool_k, pool_v, k_new, v_new, s_bias)


def _mem_kv_kernel(mem_ref, wk_ref, wv_ref, k_ref, v_ref):
    m = mem_ref[...].astype(BF16)
    k_ref[...] = jnp.dot(m, wk_ref[...], preferred_element_type=F32)
    v_ref[...] = jnp.dot(m, wv_ref[...], preferred_element_type=F32)


def _mem_kv(mem, w_xk, w_xv):
    b, n, d = mem.shape
    blk = pl.BlockSpec((None, n, d), lambda i: (i, 0, 0))
    w = pl.BlockSpec((d, d), lambda i: (0, 0))
    out = jax.ShapeDtypeStruct((b, n, d), F32)
    return pl.pallas_call(
        _mem_kv_kernel,
        grid=(b,),
        in_specs=[blk, w, w],
        out_specs=(blk, blk),
        out_shape=(out, out),
        compiler_params=pltpu.CompilerParams(dimension_semantics=("parallel",),
                                             vmem_limit_bytes=_vmem_limit(4 * d * d * 2 + 6 * n * d * 4)),
        name="mem_kv",
    )(mem, w_xk, w_xv)


def _mix_out_kernel(att_ref, conv_ref, h_ref, wo_ref, g_post_ref, g_x_ref, wq_ref, h1_ref, qx_ref):
    half = att_ref.shape[1]
    mo = (jnp.dot(att_ref[...].astype(BF16), wo_ref[0:half, :], preferred_element_type=F32)
          + jnp.dot(conv_ref[...].astype(BF16), wo_ref[half:, :], preferred_element_type=F32))
    h1 = h_ref[...] + _rms(mo, g_post_ref[...])
    h1_ref[...] = h1
    xn = _rms(h1, g_x_ref[...]).astype(BF16)
    x_scale = (wq_ref.shape[1] // N_X_HEADS) ** -0.5
    qx_ref[...] = (jnp.dot(xn, wq_ref[...], preferred_element_type=F32) * x_scale).astype(BF16)


def _mix_out(att, conv, h, w_out, g_post, g_x, w_xq):
    m, d = h.shape
    tm = min(MIX_ROW_TILE, m)
    row = lambda w: pl.BlockSpec((tm, w), lambda i: (i, 0))
    const = lambda a: pl.BlockSpec(a.shape, lambda i: (0, 0))
    est = 2 * 2 * d * d * 2 + 2 * tm * (2 * d * 4 + d * 2 + att.shape[1] * 6) + 4 * tm * d * 4
    return pl.pallas_call(
        _mix_out_kernel,
        grid=(m // tm,),
        in_specs=[row(att.shape[1]), row(conv.shape[1]), row(d), const(w_out), const(g_post), const(g_x),
                  const(w_xq)],
        out_specs=(row(d), row(d)),
        out_shape=(jax.ShapeDtypeStruct((m, d), F32), jax.ShapeDtypeStruct((m, d), BF16)),
        compiler_params=pltpu.CompilerParams(dimension_semantics=("parallel",),
                                             vmem_limit_bytes=_vmem_limit(est)),
        name="mix_out",
    )(att, conv, h, w_out, g_post, g_x, w_xq)


def _xattn_kernel(q_ref, mk_ref, mv_ref, o_ref):
    d = q_ref.shape[1]
    hd = d // N_X_HEADS
    for h in range(N_X_HEADS):
        sl = slice(h * hd, (h + 1) * hd)
        s = _nt_dot(q_ref[:, sl], mk_ref[:, sl].astype(BF16))
        p = jnp.exp(s - jnp.max(s, axis=1, keepdims=True))
        l = jnp.sum(p, axis=1, keepdims=True)
        o = jnp.dot(p.astype(BF16), mv_ref[:, sl].astype(BF16), preferred_element_type=F32)
        o_ref[:, sl] = (o / l).astype(o_ref.dtype)


def _xattn(qx, mem_k, mem_v):
    b, t, d = qx.shape
    n = mem_k.shape[1]
    tm = min(ROW_TILE, t)
    row = pl.BlockSpec((None, tm, d), lambda i, j: (i, j, 0))
    mem = pl.BlockSpec((None, n, d), lambda i, j: (i, 0, 0))
    return pl.pallas_call(
        _xattn_kernel,
        grid=(b, t // tm),
        in_specs=[row, mem, mem],
        out_specs=row,
        out_shape=jax.ShapeDtypeStruct((b, t, d), BF16),
        compiler_params=pltpu.CompilerParams(dimension_semantics=("parallel", "parallel"),
                                             vmem_limit_bytes=_vmem_limit(4 * n * d * 4 + 8 * tm * d * 4)),
        name="xattn",
    )(qx, mem_k, mem_v)


def _ffn_kernel(o_ref, h1_ref, wxo_ref, g_xpost_ref, g_pre_ref, wg_ref, wu_ref, wd_ref, g_post_ref, y_ref):
    h2 = h1_ref[...] + _rms(jnp.dot(o_ref[...], wxo_ref[...], preferred_element_type=F32), g_xpost_ref[...])
    xf = _rms(h2, g_pre_ref[...]).astype(BF16)
    d_ff = wg_ref.shape[1]
    f = jnp.zeros(h2.shape, F32)
    for c0 in range(0, d_ff, FFN_CHUNK):
        sl = slice(c0, c0 + FFN_CHUNK)
        g = jnp.dot(xf, wg_ref[:, sl], preferred_element_type=F32)
        u = jnp.dot(xf, wu_ref[:, sl], preferred_element_type=F32)
        a = (g * _sigmoid(g) * u).astype(BF16)
        f = f + jnp.dot(a, wd_ref[sl, :], preferred_element_type=F32)
    y_ref[...] = h2 + _rms(f, g_post_ref[...])


def _ffn(o, h1, w_xo, g_xpost, g_pre, w_gate, w_up, w_down, g_post):
    m, d = h1.shape
    d_ff = w_gate.shape[1]
    assert d_ff % FFN_CHUNK == 0
    tm = min(ROW_TILE, m)
    row = pl.BlockSpec((tm, d), lambda i: (i, 0))
    const = lambda a: pl.BlockSpec(a.shape, lambda i: (0, 0), pipeline_mode=pl.Buffered(1))
    est = (d * d + 3 * d * d_ff) * 2 + 2 * tm * d * (2 + 4 + 4) + 6 * tm * d * 4
    return pl.pallas_call(
        _ffn_kernel,
        grid=(m // tm,),
        in_specs=[row, row, const(w_xo), const(g_xpost), const(g_pre), const(w_gate), const(w_up),
                  const(w_down), const(g_post)],
        out_specs=row,
        out_shape=jax.ShapeDtypeStruct((m, d), F32),
        compiler_params=pltpu.CompilerParams(dimension_semantics=("parallel",),
                                             vmem_limit_bytes=_vmem_limit(est)),
        name="ffn",
    )(o, h1, w_xo, g_xpost, g_pre, w_gate, w_up, w_down, g_post)


def kernel(x_prompt, x_sample, mem_prompt, cache_k, cache_v, state_conv, cache_mem_k, cache_mem_v, page_table, rel_bias_table, norm_mix_pre, norm_mix_post, w_in, lambda_q1, lambda_k1, lambda_q2, lambda_k2, subln_g, dw_w, dw_b, conv_ln_g, conv_ln_b, beta_att, beta_conv, w_out, norm_x_pre, norm_x_post, w_xq, w_xk, w_xv, w_xo, norm_ffn_pre, norm_ffn_post, w_gate, w_up, w_down):
    assert w_in.shape[0] == 1, "single-layer trunk"
    bp, tp, d = x_prompt.shape
    bs, ts, _ = x_sample.shape
    assert ts <= MAX_NEW_TOKENS and tp >= CONV_K - 1
    n_mem = mem_prompt.shape[1]
    vec = lambda a: a[0].reshape(1, -1)
    wb = lambda a: a[0].astype(BF16)
    g_mix_pre, g_mix_post = vec(norm_mix_pre), vec(norm_mix_post)
    g_x_pre, g_x_post = vec(norm_x_pre), vec(norm_x_post)
    g_ffn_pre, g_ffn_post = vec(norm_ffn_pre), vec(norm_ffn_post)
    sub_g, b_att, b_conv = vec(subln_g), vec(beta_att), vec(beta_conv)
    c_b, c_g, c_bt = vec(dw_b), vec(conv_ln_g), vec(conv_ln_b)
    w_in_b, w_out_b, w_xq_b, w_xk_b, w_xv_b, w_xo_b = (wb(w) for w in (w_in, w_out, w_xq, w_xk, w_xv, w_xo))
    w_gate_b, w_up_b, w_down_b = wb(w_gate), wb(w_up), wb(w_down)
    dw = dw_w[0]

    blk = min(ATT_BLOCK, tp)
    lam_vecs = jnp.stack([lambda_q1[0], lambda_k1[0], lambda_q2[0], lambda_k2[0]])
    p_bias, s_bias, lam = _bias_tiles(rel_bias_table, lam_vecs, blk, ts)

    def tail(att, conv, h, mem_k, mem_v):
        b, t, _ = h.shape
        flat = lambda a: a.reshape(b * t, a.shape[-1])
        h1, qx = _mix_out(flat(att), flat(conv), flat(h), w_out_b, g_mix_post, g_x_pre, w_xq_b)
        o = _xattn(qx.reshape(b, t, d), mem_k, mem_v)
        y = _ffn(flat(o), h1, w_xo_b, g_x_post, g_ffn_pre, w_gate_b, w_up_b, w_down_b, g_ffn_post)
        return y.reshape(b, t, d)

    conv0 = jnp.zeros((bp, CONV_HALO, CONV_WIDTH), F32)
    qt_p, kb_p, vt_p, kf_p, vf_p, glu_p, conv_p = _in_proj(x_prompt, g_mix_pre, w_in_b, blk,
                                                          conv=(conv0, dw, c_b, c_g, c_bt, b_conv))
    x_s = jnp.pad(x_sample, ((0, 0), (0, SAMPLE_ROWS - ts), (0, 0)))
    rows_s = bs * SAMPLE_ROWS
    qt_s, _, _, kf_s, vf_s, glu_s = _in_proj(x_s.reshape(1, rows_s, d), g_mix_pre, w_in_b, min(blk, rows_s))
    new_rows = ts * N_ATT_HEADS
    kf_s = kf_s.reshape(bs, SAMPLE_ROWS * N_ATT_HEADS, ATT_V_DIM)[:, :new_rows]
    vf_s = vf_s.reshape(bs, SAMPLE_ROWS * N_ATT_HEADS, ATT_V_DIM)[:, :new_rows]
    glu_s = glu_s.reshape(bs, SAMPLE_ROWS, CONV_WIDTH)[:, :ts]
    q_s = jnp.transpose(qt_s[0], (0, 1, 3, 2))
    q_th = jnp.transpose(q_s.reshape(N_ATT_HEADS, bs, SAMPLE_ROWS, ATT_V_DIM), (1, 2, 0, 3))
    q_th = jnp.pad(q_th[:, :ts], ((0, 0), (0, MAX_NEW_TOKENS - ts), (0, 0), (0, 0)))
    q_rows = jnp.repeat(q_th.reshape(bs, MAX_NEW_TOKENS * N_ATT_HEADS, ATT_V_DIM), 2, axis=1)
    pad_page = lambda a: jnp.pad(a, ((0, 0), (0, PAGE_SIZE - new_rows), (0, 0)))
    n_phys = cache_k.shape[1]
    pool = lambda c: c.reshape(n_phys, PAGE_SIZE * N_ATT_HEADS, ATT_V_DIM)

    att_p, att_s = _attention(qt_p, kb_p, vt_p, p_bias, lam, sub_g, b_att, page_table, q_rows,
                              pool(cache_k), pool(cache_v), pad_page(kf_s), pad_page(vf_s), s_bias)

    mk_p, mv_p = _mem_kv(mem_prompt, w_xk_b, w_xv_b)
    y_p = tail(att_p, conv_p, x_prompt, mk_p, mv_p)

    xp_s = jnp.concatenate([state_conv[0], glu_s], axis=1)
    conv_s = _conv_sample(jnp.transpose(xp_s, (1, 0, 2)), ts, dw, c_b, c_g, c_bt, b_conv)
    conv_s = jnp.pad(jnp.transpose(conv_s, (1, 0, 2)), ((0, 0), (0, SAMPLE_ROWS - ts), (0, 0)))
    mem_s = lambda c: c[0].astype(BF16).reshape(bs, n_mem, d)
    y_s = tail(att_s, conv_s, x_s, mem_s(cache_mem_k), mem_s(cache_mem_v))

    heads = lambda a: a.reshape(1, a.shape[0], a.shape[1] // N_ATT_HEADS, N_ATT_HEADS, ATT_V_DIM)
    mem_heads = lambda a: a.reshape(1, bp, n_mem, N_X_HEADS, d // N_X_HEADS)
    return (y_p, y_s[:, :ts],
            heads(kf_p), heads(vf_p), glu_p[None, :, tp - (CONV_K - 1):],
            mem_heads(mk_p), mem_heads(mv_p),
            heads(kf_s), heads(vf_s), xp_s[None, :, ts:])
```

```python
import functools
import math

import numpy as np
import jax
import jax.numpy as jnp
from jax import lax
from jax.experimental import pallas as pl
from jax.experimental.pallas import tpu as pltpu

F32 = jnp.float32
BF16 = jnp.bfloat16

DIFF_HEAD_DIM = 64
ATT_V_DIM = 2 * DIFF_HEAD_DIM
N_ATT_HEADS = 4
ATT_WIDTH = N_ATT_HEADS * ATT_V_DIM
CONV_WIDTH = 512
CONV_K = 31
N_BUCKETS = 32
MAX_DISTANCE = 128
N_X_HEADS = 4
PAGE_SIZE = 128
RMS_EPS = 1e-6
LN_EPS = 1e-5
ATT_SCALE = DIFF_HEAD_DIM ** -0.5
LOG2_E = math.log2(math.e)
LAM_INIT = 0.8 - 0.6 * math.exp(-0.3 * 0)
MASKED = -1e30
STALE_MAX_MARGIN = 30.0

V7X_VMEM_BYTES = 64 * 1024 * 1024
LANES = 128
SUBLANES = 8
BF16_ROWS = 16

ROW_TILE = 512
MIX_ROW_TILE = 1024
ATT_BLOCK = 256
ATT_HEADS_PER_STEP = 4
CONV_CHUNK = 64
CONV_HALO = 32
PAGES_PER_STEP = 8
SAMPLE_RING_SLOTS = 3
FFN_CHUNK = 256
SAMPLE_ROWS = 8
MAX_NEW_TOKENS = 4
SAMPLE_Q_ROWS = MAX_NEW_TOKENS * 2 * N_ATT_HEADS


def _vmem_limit(nbytes):
    return int(min(max(2 * nbytes, 16 * 1024 * 1024), V7X_VMEM_BYTES - 8 * 1024 * 1024))


def _rms(x, g):
    return x * lax.rsqrt(jnp.mean(x * x, axis=-1, keepdims=True) + RMS_EPS) * g


def _sigmoid(x):
    return 1.0 / (1.0 + jnp.exp(-x))


def _nt_dot(a, b):
    return lax.dot_general(a, b, (((1,), (1,)), ((), ())), preferred_element_type=F32)


def _bucket_np(n):
    n = np.maximum(n, 0)
    max_exact = N_BUCKETS // 2
    nf = np.maximum(n, 1).astype(np.float32)
    large = max_exact + (np.log(nf / max_exact) / math.log(MAX_DISTANCE / max_exact)
                         * (N_BUCKETS - max_exact)).astype(np.int32)
    large = np.minimum(large, N_BUCKETS - 1)
    return np.where(n < max_exact, n, large).astype(np.int32)


def _prompt_codes(blk):
    i = np.arange(blk)[None, :]
    j = np.arange(blk)[:, None]
    prev = _bucket_np(i - j + blk)
    diag = np.where(j > i, -1, _bucket_np(i - j))
    return np.concatenate([prev, diag]).astype(np.int32)


def _sample_codes(n_new):
    r = np.arange(SAMPLE_Q_ROWS)[:, None]
    c = np.arange(PAGE_SIZE * N_ATT_HEADS)[None, :]
    t, head = r // 8, (r // 2) % N_ATT_HEADS
    tok, key_head = c // N_ATT_HEADS, c % N_ATT_HEADS
    own = key_head == head
    far = np.where(own, N_BUCKETS - 1, -1)
    last = np.where(own, _bucket_np(t + PAGE_SIZE - tok), -1)
    new = np.where(own & (tok <= t) & (tok < n_new) & (c < PAGE_SIZE), _bucket_np(t - tok), -1)
    return np.stack([far, last, new]).astype(np.int32)


def _bias_kernel(tab_ref, lam_in_ref, pcode_ref, scode_ref, pbias_ref, sbias_ref, lam_ref):
    far = N_BUCKETS - 1

    def lookup(code, h):
        out = jnp.zeros(code.shape, F32)
        for b in range(far):
            out = jnp.where(code == b, (tab_ref[b, h] - tab_ref[far, h]) * LOG2_E, out)
        return jnp.where(code < 0, MASKED, out)

    for h in range(N_ATT_HEADS):
        pbias_ref[h] = lookup(pcode_ref[...], h)
    row_head = (lax.broadcasted_iota(jnp.int32, scode_ref.shape[1:], 0) >> 1) & (N_ATT_HEADS - 1)
    for i in range(scode_ref.shape[0]):
        code = scode_ref[i]
        out = jnp.zeros(code.shape, F32)
        for h in range(N_ATT_HEADS):
            out = jnp.where(row_head == h, lookup(code, h), out)
        sbias_ref[i] = out
    lv = lam_in_ref[...]
    d1 = jnp.sum(lv[0:1] * lv[1:2], axis=1, keepdims=True)
    d2 = jnp.sum(lv[2:3] * lv[3:4], axis=1, keepdims=True)
    lam = jnp.exp(d1) - jnp.exp(d2) + LAM_INIT
    lam_ref[...] = jnp.broadcast_to(lam, lam_ref.shape)


def _bias_tiles(table, lam_vecs, blk, n_new):
    pcode = jnp.asarray(_prompt_codes(blk))
    scode = jnp.asarray(_sample_codes(n_new))
    vm = pl.BlockSpec(memory_space=pltpu.VMEM)
    return pl.pallas_call(
        _bias_kernel,
        out_shape=(jax.ShapeDtypeStruct((N_ATT_HEADS, 2 * blk, blk), F32),
                   jax.ShapeDtypeStruct(scode.shape, F32),
                   jax.ShapeDtypeStruct((SUBLANES, LANES), F32)),
        in_specs=[pl.BlockSpec(memory_space=pltpu.SMEM), vm, vm, vm],
        out_specs=(vm, vm, vm),
        name="bias_tiles",
    )(table, lam_vecs, pcode, scode)


def _in_proj_kernel(x_ref, g_ref, w_ref, *refs, with_conv):
    if with_conv:
        conv_in, refs = refs[:6], refs[6:]
        qt_ref, kb_ref, vt_ref, kf_ref, vf_ref, glu_ref, conv_ref, buf, ybuf = refs
    else:
        qt_ref, kb_ref, vt_ref, kf_ref, vf_ref, glu_ref = refs
    xn = _rms(x_ref[...], g_ref[...]).astype(BF16)

    def cols(c):
        return jnp.dot(xn, w_ref[:, c * ATT_WIDTH:(c + 1) * ATT_WIDTH], preferred_element_type=F32)

    tm = xn.shape[0]
    blk = qt_ref.shape[-1]
    glu = cols(3) * _sigmoid(cols(4))
    glu_ref[...] = glu
    if with_conv:
        _conv_tile(glu, *conv_in, conv_ref, buf, ybuf, pl.program_id(1) == 0)
    q = cols(0) * (ATT_SCALE * LOG2_E)
    k = cols(1)
    v = cols(2)
    kb = k.astype(BF16)
    for h in range(N_ATT_HEADS):
        sl = slice(h * ATT_V_DIM, (h + 1) * ATT_V_DIM)
        kb_ref[h] = kb[:, sl]
        for c in range(tm // blk):
            rows = slice(c * blk, (c + 1) * blk)
            qt_ref[h, c] = q[rows, sl].T.astype(BF16)
            vt_ref[h, c] = v[rows, sl].T.astype(BF16)
        kf_ref[pl.ds(h, tm, stride=N_ATT_HEADS), :] = k[:, sl]
        vf_ref[pl.ds(h, tm, stride=N_ATT_HEADS), :] = v[:, sl]


def _in_proj(x, gain, w_in, blk, conv=None):
    b, t, d = x.shape
    tm = min(ROW_TILE, t)
    n_cols = w_in.shape[1]
    assert tm % blk == 0
    c = CONV_WIDTH
    conv = () if conv is None else tuple(conv)
    vec = pl.BlockSpec((1, c), lambda i, j: (0, 0))
    conv_specs = [pl.BlockSpec((None, CONV_HALO, c), lambda i, j: (i, 0, 0)),
                  pl.BlockSpec((CONV_K, c), lambda i, j: (0, 0)), vec, vec, vec, vec] if conv else []
    conv_out = [pl.BlockSpec((None, tm, c), lambda i, j: (i, j, 0))] if conv else []
    conv_shape = [jax.ShapeDtypeStruct((b, t, c), BF16)] if conv else []
    conv_scratch = [pltpu.VMEM((c // LANES, tm + CONV_HALO, LANES), F32),
                    pltpu.VMEM((c // LANES, tm, LANES), F32)] if conv else []
    row = lambda w: pl.BlockSpec((None, tm, w), lambda i, j: (i, j, 0))
    heads = pl.BlockSpec((None, N_ATT_HEADS, tm, ATT_V_DIM), lambda i, j: (i, 0, j, 0))
    hm = jax.ShapeDtypeStruct((b, N_ATT_HEADS, t, ATT_V_DIM), BF16)
    heads_t = pl.BlockSpec((None, N_ATT_HEADS, tm // blk, ATT_V_DIM, blk), lambda i, j: (i, 0, j, 0, 0))
    hm_t = jax.ShapeDtypeStruct((b, N_ATT_HEADS, t // blk, ATT_V_DIM, blk), BF16)
    flat = jax.ShapeDtypeStruct((b, t * N_ATT_HEADS, ATT_V_DIM), F32)
    tok_head = pl.BlockSpec((None, tm * N_ATT_HEADS, ATT_V_DIM), lambda i, j: (i, j, 0))
    est = 2 * d * n_cols * 2 + 2 * tm * (d * 4 + 3 * ATT_WIDTH * 2 + 3 * ATT_WIDTH * 4) + 6 * tm * ATT_WIDTH * 4
    return pl.pallas_call(
        functools.partial(_in_proj_kernel, with_conv=bool(conv)),
        grid=(b, t // tm),
        in_specs=[row(d),
                  pl.BlockSpec((1, d), lambda i, j: (0, 0)),
                  pl.BlockSpec((d, n_cols), lambda i, j: (0, 0))] + conv_specs,
        out_specs=tuple([heads_t, heads, heads_t, tok_head, tok_head, row(CONV_WIDTH)] + conv_out),
        out_shape=tuple([hm_t, hm, hm_t, flat, flat, jax.ShapeDtypeStruct((b, t, CONV_WIDTH), F32)]
                        + conv_shape),
        scratch_shapes=conv_scratch,
        compiler_params=pltpu.CompilerParams(dimension_semantics=("parallel", "arbitrary"),
                                             vmem_limit_bytes=_vmem_limit(est)),
        name="in_proj",
    )(x, gain, w_in, *conv)


def _conv_post(y, b_ref, g_ref, bt_ref, beta_ref):
    y = y + b_ref[...]
    mu = jnp.mean(y, axis=-1, keepdims=True)
    yc = y - mu
    yn = yc * lax.rsqrt(jnp.mean(yc * yc, axis=-1, keepdims=True) + LN_EPS) * g_ref[...] + bt_ref[...]
    return yn * _sigmoid(yn) * beta_ref[...]


def _conv_tile(glu, prev_ref, w_ref, b_ref, g_ref, bt_ref, beta_ref, o_ref, buf, ybuf, first):
    tt = glu.shape[0]
    n_slab = buf.shape[0]
    slabs = [slice(s * LANES, (s + 1) * LANES) for s in range(n_slab)]

    @pl.when(first)
    def _():
        for s in range(n_slab):
            buf[s, 0:CONV_HALO] = prev_ref[:, slabs[s]]

    @pl.when(jnp.logical_not(first))
    def _():
        for s in range(n_slab):
            buf[s, 0:CONV_HALO] = buf[s, tt:tt + CONV_HALO]

    for s in range(n_slab):
        buf[s, CONV_HALO:CONV_HALO + tt] = glu[:, slabs[s]]

    shift = CONV_HALO - (CONV_K - 1)
    rc = min(CONV_CHUNK, tt // 2)
    for c0 in range(0, tt, 2 * rc):
        for parity in range(2):
            start = c0 + parity
            accs = []
            for s in range(n_slab):
                acc = jnp.zeros((rc, LANES), F32)
                for j in range(CONV_K):
                    x = buf[s, pl.ds(start + j + shift, rc, stride=2), :]
                    acc = acc + w_ref[j:j + 1, slabs[s]] * x
                accs.append(acc)
            y = _conv_post(jnp.concatenate(accs, axis=1), b_ref, g_ref, bt_ref, beta_ref)
            for s in range(n_slab):
                ybuf[s, pl.ds(start, rc, stride=2), :] = y[:, slabs[s]]
    for s in range(n_slab):
        o_ref[:, slabs[s]] = ybuf[s].astype(o_ref.dtype)


def _conv_sample_kernel(xp_ref, w_ref, b_ref, g_ref, bt_ref, beta_ref, o_ref):
    n_t = o_ref.shape[0]
    for t in range(n_t):
        acc = jnp.zeros(xp_ref.shape[1:], F32)
        for j in range(CONV_K):
            acc = acc + w_ref[j:j + 1, :] * xp_ref[t + j]
        o_ref[t] = _conv_post(acc, b_ref, g_ref, bt_ref, beta_ref)


def _conv_sample(xp_t, n_t, dw_w, dw_b, ln_g, ln_b, beta):
    vm = pl.BlockSpec(memory_space=pltpu.VMEM)
    return pl.pallas_call(
        _conv_sample_kernel,
        out_shape=jax.ShapeDtypeStruct((n_t,) + xp_t.shape[1:], F32),
        in_specs=[vm] * 6,
        out_specs=vm,
        name="conv_sample",
    )(xp_t, dw_w, dw_b, ln_g, ln_b, beta)


def _softmax_update(s, v_dot, m_sc, l_sc, acc_sc):
    width = s.shape[1]
    m_prev = m_sc[...]
    m_new = jnp.maximum(m_prev, jnp.max(s, axis=1, keepdims=True))
    alpha = jnp.exp2(m_prev - m_new)
    p = jnp.exp2(s - jnp.concatenate([m_new] * (width // LANES), axis=1))
    l_sc[...] = alpha * l_sc[...] + jnp.sum(p, axis=1, keepdims=True)
    reps = acc_sc.shape[1] // LANES
    acc_sc[...] = jnp.concatenate([alpha] * reps, axis=1) * acc_sc[...] + v_dot(p.astype(BF16))
    m_sc[...] = m_new


def _subln(att, g, beta):
    return _rms(att, g) * (1.0 - LAM_INIT) * beta


def _sample_page_copies(pt_ref, pool_k, pool_v, kbuf, vbuf, sem, chunk, slot, chunks_per_seq):
    seq = chunk // chunks_per_seq
    first_page = (chunk % chunks_per_seq) * PAGES_PER_STEP
    copies = []
    for i in range(PAGES_PER_STEP):
        page = pt_ref[seq, first_page + i]
        copies.append(pltpu.make_async_copy(pool_k.at[page], kbuf.at[slot, i], sem.at[0, slot]))
        copies.append(pltpu.make_async_copy(pool_v.at[page], vbuf.at[slot, i], sem.at[1, slot]))
    return copies


def _attn_kernel(pt_ref, qt_ref, k_ref, vt_ref, bias_ref, lam_ref, g_ref, beta_ref,
                 sq_ref, pool_k, pool_v, kn_ref, vn_ref, sbias_ref, o_ref, so_ref,
                 m_sc, acc_sc, kbuf, vbuf, sem, sm_sc, sl_sc, sacc_sc, cnt_ref):
    n_heads, e, blk = qt_ref.shape
    qi = pl.program_id(2)
    n_seq_s, chunks_per_seq = so_ref.shape[0], pt_ref.shape[1] // PAGES_PER_STEP
    n_chunks = n_seq_s * chunks_per_seq
    first_grid_step = jnp.logical_and(pl.program_id(0) == 0, qi == 0)
    last_grid_step = jnp.logical_and(pl.program_id(0) == pl.num_programs(0) - 1, qi == pl.num_programs(2) - 1)
    ring = (pt_ref, pool_k, pool_v, kbuf, vbuf, sem)

    n_slots = kbuf.shape[0]
    lookahead = n_slots - 1
    last_chunk = n_chunks - 1

    @pl.when(first_grid_step)
    def _():
        cnt_ref[0] = 0
        for d in range(lookahead):
            for cp in _sample_page_copies(*ring, min(d, last_chunk), d, chunks_per_seq):
                cp.start()

    s_lane = lax.broadcasted_iota(jnp.int32, sq_ref.shape[1:], 1)
    s_row = lax.broadcasted_iota(jnp.int32, sq_ref.shape[1:], 0)

    def sample_fetch():
        n = cnt_ref[0]
        slot = lax.rem(n, n_slots)
        valid = n < n_chunks
        chunk = jnp.minimum(n, last_chunk)
        seq = chunk // chunks_per_seq
        c = chunk % chunks_per_seq
        for cp in _sample_page_copies(*ring, chunk, slot, chunks_per_seq):
            cp.wait()
        ahead_slot = lax.rem(n + lookahead, n_slots)
        for cp in _sample_page_copies(*ring, jnp.minimum(n + lookahead, last_chunk), ahead_slot, chunks_per_seq):
            cp.start()
        cnt_ref[0] = n + 1
        return slot, seq, c, valid

    def sample_logits(slot, seq, c, valid):
        q = sq_ref[seq]
        qm = jnp.where((s_lane >> 6) == (s_row & 1), q, jnp.zeros_like(q))
        gate = jnp.where(valid, 0.0, MASKED)
        far_bias = sbias_ref[0] + gate
        last_bias = jnp.where(c == chunks_per_seq - 1, sbias_ref[1] + gate, far_bias)
        s = jnp.concatenate(
            [_nt_dot(qm, kbuf[slot, i].astype(BF16)) + (last_bias if i == PAGES_PER_STEP - 1 else far_bias)
             for i in range(PAGES_PER_STEP)], axis=1)
        return qm, s

    def sample_update(slot, seq, c, valid, qm, s):
        fresh = c == 0
        m_prev = jnp.where(fresh, MASKED, sm_sc[...])
        m_new = jnp.maximum(m_prev, jnp.max(s, axis=1, keepdims=True))
        alpha = jnp.exp2(m_prev - m_new)
        p = jnp.exp2(s - jnp.concatenate([m_new] * (s.shape[1] // LANES), axis=1))
        sl_sc[...] = alpha * jnp.where(fresh, 0.0, sl_sc[...]) + jnp.sum(p, axis=1, keepdims=True)
        pb = p.astype(BF16)
        cols = kbuf.shape[2]
        pv = None
        for i in range(PAGES_PER_STEP):
            part = jnp.dot(pb[:, i * cols:(i + 1) * cols], vbuf[slot, i].astype(BF16),
                           preferred_element_type=F32)
            pv = part if pv is None else pv + part
        sacc_sc[...] = alpha * jnp.where(fresh, 0.0, sacc_sc[...]) + pv
        sm_sc[...] = m_new
        return seq, qm, jnp.logical_and(valid, c == chunks_per_seq - 1)

    def sample_finish(seq, qm, seq_done):
        @pl.when(seq_done)
        def _():
            n_new = kn_ref.shape[1]
            s_new = _nt_dot(qm, kn_ref[seq].astype(BF16)) + sbias_ref[2][:, :n_new]
            _softmax_update(s_new, lambda pn: jnp.dot(pn, vn_ref[seq].astype(BF16), preferred_element_type=F32),
                            sm_sc, sl_sc, sacc_sc)
            sign = jnp.where((s_row & 1) == 0, 1.0, -lam_ref[0:1, 0:1])
            z = sacc_sc[...] / sl_sc[...] * sign
            out_row = lax.broadcasted_iota(jnp.int32, (so_ref.shape[1], ATT_V_DIM), 0)
            for h in range(N_ATT_HEADS):
                att = jnp.zeros(out_row.shape, F32)
                for t in range(MAX_NEW_TOKENS):
                    r0 = t * 8 + h * 2
                    att = jnp.where(out_row == t, jnp.broadcast_to(z[r0:r0 + 1] + z[r0 + 1:r0 + 2], att.shape),
                                    att)
                sl = slice(h * ATT_V_DIM, (h + 1) * ATT_V_DIM)
                so_ref[seq, :, sl] = _subln(att, g_ref[...], beta_ref[:, sl])

    ones_rows = jnp.ones((acc_sc.shape[2] - e, blk), BF16)
    sub = lax.broadcasted_iota(jnp.int32, qt_ref.shape[1:], 0)
    qst = []
    for h in range(n_heads):
        qt = qt_ref[h]
        zero = jnp.zeros_like(qt)
        qst.append(jnp.concatenate([jnp.where(sub < DIFF_HEAD_DIM, qt, zero),
                                    jnp.where(sub >= DIFF_HEAD_DIM, qt, zero)], axis=1))
    m_sc[...] = jnp.full(m_sc.shape, MASKED, F32)
    acc_sc[0] = jnp.zeros(acc_sc.shape[1:], F32)

    def step(j, n_blk, biased, stale_max, src):
        rows = pl.ds(pl.multiple_of(j * blk, blk), n_blk * blk)
        dst = 1 - src
        chunk_state = sample_fetch()

        def logits(h):
            st = jnp.dot(k_ref[h, rows, :], qst[h], preferred_element_type=F32)
            if biased:
                bias = bias_ref[h, (2 - n_blk) * blk:, :]
                st = st + jnp.concatenate([bias, bias], axis=1)
            return st

        def weighted_values(h, pb):
            pv = None
            for i in range(n_blk):
                vt_aug = jnp.concatenate([vt_ref[h, j + i], ones_rows], axis=0)
                part = jnp.dot(vt_aug, pb[i * blk:(i + 1) * blk], preferred_element_type=F32)
                pv = part if pv is None else pv + part
            return pv

        def exact_update(h, st):
            m_prev = m_sc[h]
            m_new = jnp.maximum(m_prev, jnp.max(st, axis=0, keepdims=True))
            alpha = jnp.exp2(m_prev - m_new)
            pv = weighted_values(h, jnp.exp2(st - m_new).astype(BF16))
            acc_sc[dst, h] = alpha * acc_sc[src, h] + pv
            m_sc[h] = m_new

        def stale_update(h, st):
            m_ref = m_sc[h]
            pv = weighted_values(h, jnp.exp2(st - m_ref).astype(BF16))
            acc_sc[dst, h] = acc_sc[src, h] + pv
            return jnp.max(st, axis=0, keepdims=True) - m_ref

        ahead = 2
        pending = [logits(h) for h in range(min(ahead, n_heads))]
        chunk_logits = sample_logits(*chunk_state)
        excess = None
        for h in range(n_heads):
            if h + ahead < n_heads:
                pending.append(logits(h + ahead))
            if stale_max:
                over = stale_update(h, pending[h])
                excess = over if excess is None else jnp.maximum(excess, over)
            else:
                exact_update(h, pending[h])
            if h == 0:
                sample_done = sample_update(*chunk_state, *chunk_logits)

        sample_finish(*sample_done)
        if stale_max:
            @pl.when(jnp.max(excess) > STALE_MAX_MARGIN)
            def _():
                for h in range(n_heads):
                    exact_update(h, logits(h))

    odd = (qi + 1) & 1
    n_pairs = jnp.maximum(((qi + 1) >> 1) - 1, 0)

    @pl.when(qi == 0)
    def _():
        step(0, 1, True, False, 0)

    @pl.when(qi > 0)
    def _():
        step(qi - 1, 2, True, False, 0)

    def far_pair(i, carry):
        step(odd + 2 * i, 2, False, True, (i + 1) & 1)
        return carry

    lax.fori_loop(0, n_pairs, far_pair, 0)
    single = jnp.logical_and(qi > 0, odd == 1)

    @pl.when(single)
    def _():
        step(0, 1, False, True, (n_pairs + 1) & 1)

    res = (n_pairs + 1 + single.astype(jnp.int32)) & 1
    for h in range(n_heads):
        ot = acc_sc[res, h, 0:e] / acc_sc[res, h, e:e + 1]
        att = (ot[:, :blk] - lam_ref[0:1, 0:1] * ot[:, blk:]).T
        sl = slice(h * ATT_V_DIM, (h + 1) * ATT_V_DIM)
        o_ref[:, sl] = _subln(att, g_ref[...], beta_ref[:, sl]).astype(o_ref.dtype)

    @pl.when(last_grid_step)
    def _():
        def drain(i, carry):
            chunk_state = sample_fetch()
            sample_finish(*sample_update(*chunk_state, *sample_logits(*chunk_state)))
            return carry

        lax.fori_loop(0, jnp.maximum(n_chunks - cnt_ref[0], 0), drain, 0)
        n = cnt_ref[0]
        for d in range(lookahead):
            for cp in _sample_page_copies(*ring, jnp.minimum(n + d, last_chunk), lax.rem(n + d, n_slots),
                                          chunks_per_seq):
                cp.wait()


def _attention(qt, k, vt, bias, lam, subln_g, beta_att, page_table, q_rows, pool_k, pool_v, k_new, v_new,
               s_bias):
    b, h, t, e = k.shape
    blk = bias.shape[-1]
    n_blk = t // blk
    hs = ATT_HEADS_PER_STEP
    assert hs == h, "the sample epilogue reads every head's beta from the prompt block"
    n_seq, n_pages = page_table.shape
    rows = q_rows.shape[1]
    page_rows = pool_k.shape[1]
    npg = PAGES_PER_STEP
    assert n_pages % npg == 0
    once = pl.Buffered(1)
    whole = lambda a: pl.BlockSpec(a.shape, lambda i, j, n, pt: (0,) * a.ndim, pipeline_mode=once)
    sample_out = jax.ShapeDtypeStruct((n_seq, SAMPLE_ROWS, h * e), F32)
    grid_spec = pltpu.PrefetchScalarGridSpec(
        num_scalar_prefetch=1,
        grid=(b, h // hs, n_blk),
        in_specs=[pl.BlockSpec((None, hs, None, e, blk), lambda i, j, n, pt: (i, j, n, 0, 0)),
                  pl.BlockSpec((None, hs, t, e), lambda i, j, n, pt: (i, j, 0, 0), pipeline_mode=once),
                  pl.BlockSpec((None, hs, n_blk, e, blk), lambda i, j, n, pt: (i, j, 0, 0, 0), pipeline_mode=once),
                  pl.BlockSpec((hs, 2 * blk, blk), lambda i, j, n, pt: (j, 0, 0), pipeline_mode=once),
                  whole(lam), whole(subln_g),
                  pl.BlockSpec((1, hs * e), lambda i, j, n, pt: (0, j)),
                  whole(q_rows),
                  pl.BlockSpec(memory_space=pl.ANY), pl.BlockSpec(memory_space=pl.ANY),
                  whole(k_new), whole(v_new), whole(s_bias)],
        out_specs=(pl.BlockSpec((None, blk, hs * e), lambda i, j, n, pt: (i, n, j)),
                   pl.BlockSpec(sample_out.shape, lambda i, j, n, pt: (0, 0, 0))),
        scratch_shapes=[pltpu.VMEM((hs, 1, 2 * blk), F32),
                        pltpu.VMEM((2, hs, e + BF16_ROWS, 2 * blk), F32),
                        pltpu.VMEM((SAMPLE_RING_SLOTS, npg, page_rows, e), F32),
                        pltpu.VMEM((SAMPLE_RING_SLOTS, npg, page_rows, e), F32),
                        pltpu.SemaphoreType.DMA((2, SAMPLE_RING_SLOTS)),
                        pltpu.VMEM((rows, LANES), F32), pltpu.VMEM((rows, LANES), F32),
                        pltpu.VMEM((rows, e), F32),
                        pltpu.SMEM((1,), jnp.int32)],
    )
    est = (hs * (2 * t * e * 2) + hs * 2 * blk * blk * 4 + 4 * hs * 2 * blk * 2 * blk * 4
           + SAMPLE_RING_SLOTS * 2 * npg * page_rows * e * 4 + 2 * k_new.size * 4 + 8 * rows * npg * page_rows * 4)
    return pl.pallas_call(
        _attn_kernel,
        grid_spec=grid_spec,
        out_shape=(jax.ShapeDtypeStruct((b, t, h * e), BF16), sample_out),
        compiler_params=pltpu.CompilerParams(dimension_semantics=("arbitrary", "arbitrary", "arbitrary"),
                                             vmem_limit_bytes=_vmem_limit(est)),
        name="attention",
    )(page_table, qt, k, vt, bias, lam, subln_g, beta_att, q_rows, pool_k, pool_v, k_new, v_new, s_bias)


def _mem_kv_kernel(mem_ref, wk_ref, wv_ref, k_ref, v_ref):
    m = mem_ref[...].astype(BF16)
    k_ref[...] = jnp.dot(m, wk_ref[...], preferred_element_type=F32)
    v_ref[...] = jnp.dot(m, wv_ref[...], preferred_element_type=F32)


def _mem_kv(mem, w_xk, w_xv):
    b, n, d = mem.shape
    blk = pl.BlockSpec((None, n, d), lambda i: (i, 0, 0))
    w = pl.BlockSpec((d, d), lambda i: (0, 0))
    out = jax.ShapeDtypeStruct((b, n, d), F32)
    return pl.pallas_call(
        _mem_kv_kernel,
        grid=(b,),
        in_specs=[blk, w, w],
        out_specs=(blk, blk),
        out_shape=(out, out),
        compiler_params=pltpu.CompilerParams(dimension_semantics=("parallel",),
                                             vmem_limit_bytes=_vmem_limit(4 * d * d * 2 + 6 * n * d * 4)),
        name="mem_kv",
    )(mem, w_xk, w_xv)


def _mix_out_kernel(att_ref, conv_ref, h_ref, wo_ref, g_post_ref, g_x_ref, wq_ref, h1_ref, qx_ref):
    half = att_ref.shape[1]
    mo = (jnp.dot(att_ref[...].astype(BF16), wo_ref[0:half, :], preferred_element_type=F32)
          + jnp.dot(conv_ref[...].astype(BF16), wo_ref[half:, :], preferred_element_type=F32))
    h1 = h_ref[...] + _rms(mo, g_post_ref[...])
    h1_ref[...] = h1
    xn = _rms(h1, g_x_ref[...]).astype(BF16)
    x_scale = (wq_ref.shape[1] // N_X_HEADS) ** -0.5
    qx_ref[...] = (jnp.dot(xn, wq_ref[...], preferred_element_type=F32) * x_scale).astype(BF16)


def _mix_out(att, conv, h, w_out, g_post, g_x, w_xq):
    m, d = h.shape
    tm = min(MIX_ROW_TILE, m)
    row = lambda w: pl.BlockSpec((tm, w), lambda i: (i, 0))
    const = lambda a: pl.BlockSpec(a.shape, lambda i: (0, 0))
    est = 2 * 2 * d * d * 2 + 2 * tm * (2 * d * 4 + d * 2 + att.shape[1] * 6) + 4 * tm * d * 4
    return pl.pallas_call(
        _mix_out_kernel,
        grid=(m // tm,),
        in_specs=[row(att.shape[1]), row(conv.shape[1]), row(d), const(w_out), const(g_post), const(g_x),
                  const(w_xq)],
        out_specs=(row(d), row(d)),
        out_shape=(jax.ShapeDtypeStruct((m, d), F32), jax.ShapeDtypeStruct((m, d), BF16)),
        compiler_params=pltpu.CompilerParams(dimension_semantics=("parallel",),
                                             vmem_limit_bytes=_vmem_limit(est)),
        name="mix_out",
    )(att, conv, h, w_out, g_post, g_x, w_xq)


def _xattn_kernel(q_ref, mk_ref, mv_ref, o_ref):
    d = q_ref.shape[1]
    hd = d // N_X_HEADS
    by_piece = mk_ref.shape[1] != d
    n_piece = hd // LANES
    stride = n_piece * N_X_HEADS
    n_tok = mk_ref.shape[0] // stride if by_piece else mk_ref.shape[0]

    def piece(ref, h, i):
        if by_piece:
            return ref[pl.ds(i * N_X_HEADS + h, n_tok, stride=stride), :].astype(BF16)
        return ref[:, h * hd + i * LANES:h * hd + (i + 1) * LANES].astype(BF16)

    for h in range(N_X_HEADS):
        s = None
        for i in range(n_piece):
            cols = slice(h * hd + i * LANES, h * hd + (i + 1) * LANES)
            part = _nt_dot(q_ref[:, cols], piece(mk_ref, h, i))
            s = part if s is None else s + part
        p = jnp.exp(s - jnp.max(s, axis=1, keepdims=True))
        l = jnp.sum(p, axis=1, keepdims=True)
        pb = p.astype(BF16)
        for i in range(n_piece):
            cols = slice(h * hd + i * LANES, h * hd + (i + 1) * LANES)
            o = jnp.dot(pb, piece(mv_ref, h, i), preferred_element_type=F32)
            o_ref[:, cols] = (o / l).astype(o_ref.dtype)


def _xattn(qx, mem_k, mem_v):
    b, t, d = qx.shape
    n = mem_k.shape[1]
    tm = min(ROW_TILE, t)
    row = pl.BlockSpec((None, tm, d), lambda i, j: (i, j, 0))
    mem = pl.BlockSpec((None,) + mem_k.shape[1:], lambda i, j: (i, 0, 0))
    return pl.pallas_call(
        _xattn_kernel,
        grid=(b, t // tm),
        in_specs=[row, mem, mem],
        out_specs=row,
        out_shape=jax.ShapeDtypeStruct((b, t, d), BF16),
        compiler_params=pltpu.CompilerParams(dimension_semantics=("parallel", "parallel"),
                                             vmem_limit_bytes=_vmem_limit(4 * n * d * 4 + 8 * tm * d * 4)),
        name="xattn",
    )(qx, mem_k, mem_v)


def _ffn_kernel(o_ref, h1_ref, wxo_ref, g_xpost_ref, g_pre_ref, wg_ref, wu_ref, wd_ref, g_post_ref, y_ref):
    h2 = h1_ref[...] + _rms(jnp.dot(o_ref[...], wxo_ref[...], preferred_element_type=F32), g_xpost_ref[...])
    xf = _rms(h2, g_pre_ref[...]).astype(BF16)
    d_ff = wg_ref.shape[1]
    f = jnp.zeros(h2.shape, F32)
    for c0 in range(0, d_ff, FFN_CHUNK):
        sl = slice(c0, c0 + FFN_CHUNK)
        g = jnp.dot(xf, wg_ref[:, sl], preferred_element_type=F32)
        u = jnp.dot(xf, wu_ref[:, sl], preferred_element_type=F32)
        a = (g * _sigmoid(g) * u).astype(BF16)
        f = f + jnp.dot(a, wd_ref[sl, :], preferred_element_type=F32)
    y_ref[...] = h2 + _rms(f, g_post_ref[...])


def _ffn(o, h1, w_xo, g_xpost, g_pre, w_gate, w_up, w_down, g_post):
    m, d = h1.shape
    d_ff = w_gate.shape[1]
    assert d_ff % FFN_CHUNK == 0
    tm = min(ROW_TILE, m)
    row = pl.BlockSpec((tm, d), lambda i: (i, 0))
    const = lambda a: pl.BlockSpec(a.shape, lambda i: (0, 0), pipeline_mode=pl.Buffered(1))
    est = (d * d + 3 * d * d_ff) * 2 + 2 * tm * d * (2 + 4 + 4) + 6 * tm * d * 4
    return pl.pallas_call(
        _ffn_kernel,
        grid=(m // tm,),
        in_specs=[row, row, const(w_xo), const(g_xpost), const(g_pre), const(w_gate), const(w_up),
                  const(w_down), const(g_post)],
        out_specs=row,
        out_shape=jax.ShapeDtypeStruct((m, d), F32),
        compiler_params=pltpu.CompilerParams(dimension_semantics=("parallel",),
                                             vmem_limit_bytes=_vmem_limit(est)),
        name="ffn",
    )(o, h1, w_xo, g_xpost, g_pre, w_gate, w_up, w_down, g_post)


def kernel(x_prompt, x_sample, mem_prompt, cache_k, cache_v, state_conv, cache_mem_k, cache_mem_v, page_table, rel_bias_table, norm_mix_pre, norm_mix_post, w_in, lambda_q1, lambda_k1, lambda_q2, lambda_k2, subln_g, dw_w, dw_b, conv_ln_g, conv_ln_b, beta_att, beta_conv, w_out, norm_x_pre, norm_x_post, w_xq, w_xk, w_xv, w_xo, norm_ffn_pre, norm_ffn_post, w_gate, w_up, w_down):
    assert w_in.shape[0] == 1, "single-layer trunk"
    bp, tp, d = x_prompt.shape
    bs, ts, _ = x_sample.shape
    assert ts <= MAX_NEW_TOKENS and tp >= CONV_K - 1
    n_mem = mem_prompt.shape[1]
    vec = lambda a: a[0].reshape(1, -1)
    wb = lambda a: a[0].astype(BF16)
    g_mix_pre, g_mix_post = vec(norm_mix_pre), vec(norm_mix_post)
    g_x_pre, g_x_post = vec(norm_x_pre), vec(norm_x_post)
    g_ffn_pre, g_ffn_post = vec(norm_ffn_pre), vec(norm_ffn_post)
    sub_g, b_att, b_conv = vec(subln_g), vec(beta_att), vec(beta_conv)
    c_b, c_g, c_bt = vec(dw_b), vec(conv_ln_g), vec(conv_ln_b)
    w_in_b, w_out_b, w_xq_b, w_xk_b, w_xv_b, w_xo_b = (wb(w) for w in (w_in, w_out, w_xq, w_xk, w_xv, w_xo))
    w_gate_b, w_up_b, w_down_b = wb(w_gate), wb(w_up), wb(w_down)
    dw = dw_w[0]

    blk = min(ATT_BLOCK, tp)
    lam_vecs = jnp.stack([lambda_q1[0], lambda_k1[0], lambda_q2[0], lambda_k2[0]])
    p_bias, s_bias, lam = _bias_tiles(rel_bias_table, lam_vecs, blk, ts)

    def tail(att, conv, h, mem_k, mem_v):
        b, t, _ = h.shape
        flat = lambda a: a.reshape(b * t, a.shape[-1])
        h1, qx = _mix_out(flat(att), flat(conv), flat(h), w_out_b, g_mix_post, g_x_pre, w_xq_b)
        o = _xattn(qx.reshape(b, t, d), mem_k, mem_v)
        y = _ffn(flat(o), h1, w_xo_b, g_x_post, g_ffn_pre, w_gate_b, w_up_b, w_down_b, g_ffn_post)
        return y.reshape(b, t, d)

    conv0 = jnp.zeros((bp, CONV_HALO, CONV_WIDTH), F32)
    qt_p, kb_p, vt_p, kf_p, vf_p, glu_p, conv_p = _in_proj(x_prompt, g_mix_pre, w_in_b, blk,
                                                          conv=(conv0, dw, c_b, c_g, c_bt, b_conv))
    x_s = jnp.pad(x_sample, ((0, 0), (0, SAMPLE_ROWS - ts), (0, 0)))
    rows_s = bs * SAMPLE_ROWS
    qt_s, _, _, kf_s, vf_s, glu_s = _in_proj(x_s.reshape(1, rows_s, d), g_mix_pre, w_in_b, min(blk, rows_s))
    new_rows = ts * N_ATT_HEADS
    kf_s = kf_s.reshape(bs, SAMPLE_ROWS * N_ATT_HEADS, ATT_V_DIM)[:, :new_rows]
    vf_s = vf_s.reshape(bs, SAMPLE_ROWS * N_ATT_HEADS, ATT_V_DIM)[:, :new_rows]
    glu_s = glu_s.reshape(bs, SAMPLE_ROWS, CONV_WIDTH)[:, :ts]
    q_s = jnp.transpose(qt_s[0], (0, 1, 3, 2))
    q_th = jnp.transpose(q_s.reshape(N_ATT_HEADS, bs, SAMPLE_ROWS, ATT_V_DIM), (1, 2, 0, 3))
    q_th = jnp.pad(q_th[:, :ts], ((0, 0), (0, MAX_NEW_TOKENS - ts), (0, 0), (0, 0)))
    q_rows = jnp.repeat(q_th.reshape(bs, MAX_NEW_TOKENS * N_ATT_HEADS, ATT_V_DIM), 2, axis=1)
    pad_page = lambda a: jnp.pad(a, ((0, 0), (0, PAGE_SIZE - new_rows), (0, 0)))
    n_phys = cache_k.shape[1]
    pool = lambda c: c.reshape(n_phys, PAGE_SIZE * N_ATT_HEADS, ATT_V_DIM)

    att_p, att_s = _attention(qt_p, kb_p, vt_p, p_bias, lam, sub_g, b_att, page_table, q_rows,
                              pool(cache_k), pool(cache_v), pad_page(kf_s), pad_page(vf_s), s_bias)

    mk_p, mv_p = _mem_kv(mem_prompt, w_xk_b, w_xv_b)
    y_p = tail(att_p, conv_p, x_prompt, mk_p, mv_p)

    xp_s = jnp.concatenate([state_conv[0], glu_s], axis=1)
    conv_s = _conv_sample(jnp.transpose(xp_s, (1, 0, 2)), ts, dw, c_b, c_g, c_bt, b_conv)
    conv_s = jnp.pad(jnp.transpose(conv_s, (1, 0, 2)), ((0, 0), (0, SAMPLE_ROWS - ts), (0, 0)))
    hd_x = d // N_X_HEADS
    mem_s = lambda c: jnp.transpose(c[0].reshape(bs, n_mem, N_X_HEADS, hd_x // LANES, LANES),
                                    (0, 1, 3, 2, 4)).reshape(bs, n_mem * d // LANES, LANES)
    y_s = tail(att_s, conv_s, x_s, mem_s(cache_mem_k), mem_s(cache_mem_v))

    heads = lambda a: a.reshape(1, a.shape[0], a.shape[1] // N_ATT_HEADS, N_ATT_HEADS, ATT_V_DIM)
    mem_heads = lambda a: a.reshape(1, bp, n_mem, N_X_HEADS, d // N_X_HEADS)
    return (y_p, y_s[:, :ts],
            heads(kf_p), heads(vf_p), glu_p[None, :, tp - (CONV_K - 1):],
            mem_heads(mk_p), mem_heads(mv_p),
            heads(kf_s), heads(vf_s), xp_s[None, :, ts:])
```

```python
import functools
import math

import numpy as np
import jax
import jax.numpy as jnp
from jax import lax
from jax.experimental import pallas as pl
from jax.experimental.pallas import tpu as pltpu

F32 = jnp.float32
BF16 = jnp.bfloat16

DIFF_HEAD_DIM = 64
DIFF_HEAD_SHIFT = DIFF_HEAD_DIM.bit_length() - 1
ATT_V_DIM = 2 * DIFF_HEAD_DIM
N_ATT_HEADS = 4
ATT_WIDTH = N_ATT_HEADS * ATT_V_DIM
CONV_WIDTH = 512
CONV_K = 31
N_BUCKETS = 32
MAX_DISTANCE = 128
N_X_HEADS = 4
PAGE_SIZE = 128
RMS_EPS = 1e-6
LN_EPS = 1e-5
ATT_SCALE = DIFF_HEAD_DIM ** -0.5
LOG2_E = math.log2(math.e)
LAM_INIT = 0.8 - 0.6 * math.exp(-0.3 * 0)
MASKED = -1e30
STALE_MAX_MARGIN = 30.0

V7X_VMEM_BYTES = 64 * 1024 * 1024
LANES = 128
SUBLANES = 8
BF16_ROWS = 16

ROW_TILE = 512
MIX_ROW_TILE = 1024
ATT_BLOCK = 256
ATT_HEADS_PER_STEP = 4
CONV_CHUNK = 64
CONV_HALO = 32
PAGES_PER_STEP = 8
SAMPLE_RING_SLOTS = 3
FFN_CHUNK = 256
SAMPLE_ROWS = 8
MAX_NEW_TOKENS = 4
ROWS_PER_TOKEN = 2 * N_ATT_HEADS
SAMPLE_Q_ROWS = MAX_NEW_TOKENS * ROWS_PER_TOKEN


VMEM_TEMPORARIES_FACTOR = 2
VMEM_SMALL_CALL_BYTES = 16 * 1024 * 1024
VMEM_LEFT_FREE_BYTES = 8 * 1024 * 1024


def _vmem_limit(block_bytes):
    wanted = max(VMEM_TEMPORARIES_FACTOR * block_bytes, VMEM_SMALL_CALL_BYTES)
    return int(min(wanted, V7X_VMEM_BYTES - VMEM_LEFT_FREE_BYTES))


def _rms(x, g):
    return x * lax.rsqrt(jnp.mean(x * x, axis=-1, keepdims=True) + RMS_EPS) * g


def _sigmoid(x):
    return 1.0 / (1.0 + jnp.exp(-x))


def _nt_dot(a, b):
    return lax.dot_general(a, b, (((1,), (1,)), ((), ())), preferred_element_type=F32)


def _bucket_np(n):
    n = np.maximum(n, 0)
    max_exact = N_BUCKETS // 2
    nf = np.maximum(n, 1).astype(np.float32)
    large = max_exact + (np.log(nf / max_exact) / math.log(MAX_DISTANCE / max_exact)
                         * (N_BUCKETS - max_exact)).astype(np.int32)
    large = np.minimum(large, N_BUCKETS - 1)
    return np.where(n < max_exact, n, large).astype(np.int32)


def _prompt_codes(blk):
    i = np.arange(blk)[None, :]
    j = np.arange(blk)[:, None]
    prev = _bucket_np(i - j + blk)
    diag = np.where(j > i, -1, _bucket_np(i - j))
    return np.concatenate([prev, diag]).astype(np.int32)


def _sample_codes(n_new):
    r = np.arange(SAMPLE_Q_ROWS)[:, None]
    c = np.arange(PAGE_SIZE * N_ATT_HEADS)[None, :]
    t, head = r // ROWS_PER_TOKEN, (r // 2) % N_ATT_HEADS
    tok, key_head = c // N_ATT_HEADS, c % N_ATT_HEADS
    own = key_head == head
    far = np.where(own, N_BUCKETS - 1, -1)
    last = np.where(own, _bucket_np(t + PAGE_SIZE - tok), -1)
    new = np.where(own & (tok <= t) & (tok < n_new) & (c < PAGE_SIZE), _bucket_np(t - tok), -1)
    return np.stack([far, last, new]).astype(np.int32)


def _bias_kernel(tab_ref, lam_in_ref, pcode_ref, scode_ref, pbias_ref, sbias_ref, lam_ref):
    far = N_BUCKETS - 1

    def lookup(code, h):
        out = jnp.zeros(code.shape, F32)
        for b in range(far):
            out = jnp.where(code == b, (tab_ref[b, h] - tab_ref[far, h]) * LOG2_E, out)
        return jnp.where(code < 0, MASKED, out)

    for h in range(N_ATT_HEADS):
        pbias_ref[h] = lookup(pcode_ref[...], h)
    row_head = (lax.broadcasted_iota(jnp.int32, scode_ref.shape[1:], 0) >> 1) & (N_ATT_HEADS - 1)
    for i in range(scode_ref.shape[0]):
        code = scode_ref[i]
        out = jnp.zeros(code.shape, F32)
        for h in range(N_ATT_HEADS):
            out = jnp.where(row_head == h, lookup(code, h), out)
        sbias_ref[i] = out
    lv = lam_in_ref[...]
    d1 = jnp.sum(lv[0:1] * lv[1:2], axis=1, keepdims=True)
    d2 = jnp.sum(lv[2:3] * lv[3:4], axis=1, keepdims=True)
    lam = jnp.exp(d1) - jnp.exp(d2) + LAM_INIT
    lam_ref[...] = jnp.broadcast_to(lam, lam_ref.shape)


def _bias_tiles(table, lam_vecs, blk, n_new):
    pcode = jnp.asarray(_prompt_codes(blk))
    scode = jnp.asarray(_sample_codes(n_new))
    vm = pl.BlockSpec(memory_space=pltpu.VMEM)
    return pl.pallas_call(
        _bias_kernel,
        out_shape=(jax.ShapeDtypeStruct((N_ATT_HEADS, 2 * blk, blk), F32),
                   jax.ShapeDtypeStruct(scode.shape, F32),
                   jax.ShapeDtypeStruct((SUBLANES, LANES), F32)),
        in_specs=[pl.BlockSpec(memory_space=pltpu.SMEM), vm, vm, vm],
        out_specs=(vm, vm, vm),
        name="bias_tiles",
    )(table, lam_vecs, pcode, scode)


def _in_proj_kernel(x_ref, g_ref, w_ref, *refs, with_conv):
    if with_conv:
        conv_in, refs = refs[:6], refs[6:]
        qt_ref, kb_ref, vt_ref, kf_ref, vf_ref, glu_ref, conv_ref, buf, ybuf = refs
    else:
        qt_ref, kb_ref, vt_ref, kf_ref, vf_ref, glu_ref = refs
    xn = _rms(x_ref[...], g_ref[...]).astype(BF16)

    def cols(c):
        return jnp.dot(xn, w_ref[:, c * ATT_WIDTH:(c + 1) * ATT_WIDTH], preferred_element_type=F32)

    tm = xn.shape[0]
    blk = qt_ref.shape[-1]
    glu = cols(3) * _sigmoid(cols(4))
    glu_ref[...] = glu
    if with_conv:
        _conv_tile(glu, *conv_in, conv_ref, buf, ybuf, pl.program_id(1) == 0)
    q = cols(0) * (ATT_SCALE * LOG2_E)
    k = cols(1)
    v = cols(2)
    kb = k.astype(BF16)
    for h in range(N_ATT_HEADS):
        sl = slice(h * ATT_V_DIM, (h + 1) * ATT_V_DIM)
        kb_ref[h] = kb[:, sl]
        for c in range(tm // blk):
            rows = slice(c * blk, (c + 1) * blk)
            qt_ref[h, c] = q[rows, sl].T.astype(BF16)
            vt_ref[h, c] = v[rows, sl].T.astype(BF16)
        kf_ref[pl.ds(h, tm, stride=N_ATT_HEADS), :] = k[:, sl]
        vf_ref[pl.ds(h, tm, stride=N_ATT_HEADS), :] = v[:, sl]


def _in_proj(x, gain, w_in, blk, conv=None):
    b, t, d = x.shape
    tm = min(ROW_TILE, t)
    n_cols = w_in.shape[1]
    assert tm % blk == 0
    c = CONV_WIDTH
    conv = () if conv is None else tuple(conv)
    vec = pl.BlockSpec((1, c), lambda i, j: (0, 0))
    conv_specs = [pl.BlockSpec((None, CONV_HALO, c), lambda i, j: (i, 0, 0)),
                  pl.BlockSpec((CONV_K, c), lambda i, j: (0, 0)), vec, vec, vec, vec] if conv else []
    conv_out = [pl.BlockSpec((None, tm, c), lambda i, j: (i, j, 0))] if conv else []
    conv_shape = [jax.ShapeDtypeStruct((b, t, c), BF16)] if conv else []
    conv_scratch = [pltpu.VMEM((c // LANES, tm + CONV_HALO, LANES), F32),
                    pltpu.VMEM((c // LANES, tm, LANES), F32)] if conv else []
    row = lambda w: pl.BlockSpec((None, tm, w), lambda i, j: (i, j, 0))
    heads = pl.BlockSpec((None, N_ATT_HEADS, tm, ATT_V_DIM), lambda i, j: (i, 0, j, 0))
    hm = jax.ShapeDtypeStruct((b, N_ATT_HEADS, t, ATT_V_DIM), BF16)
    heads_t = pl.BlockSpec((None, N_ATT_HEADS, tm // blk, ATT_V_DIM, blk), lambda i, j: (i, 0, j, 0, 0))
    hm_t = jax.ShapeDtypeStruct((b, N_ATT_HEADS, t // blk, ATT_V_DIM, blk), BF16)
    flat = jax.ShapeDtypeStruct((b, t * N_ATT_HEADS, ATT_V_DIM), F32)
    tok_head = pl.BlockSpec((None, tm * N_ATT_HEADS, ATT_V_DIM), lambda i, j: (i, j, 0))
    est = 2 * d * n_cols * 2 + 2 * tm * (d * 4 + 3 * ATT_WIDTH * 2 + 3 * ATT_WIDTH * 4) + 6 * tm * ATT_WIDTH * 4
    return pl.pallas_call(
        functools.partial(_in_proj_kernel, with_conv=bool(conv)),
        grid=(b, t // tm),
        in_specs=[row(d),
                  pl.BlockSpec((1, d), lambda i, j: (0, 0)),
                  pl.BlockSpec((d, n_cols), lambda i, j: (0, 0))] + conv_specs,
        out_specs=tuple([heads_t, heads, heads_t, tok_head, tok_head, row(CONV_WIDTH)] + conv_out),
        out_shape=tuple([hm_t, hm, hm_t, flat, flat, jax.ShapeDtypeStruct((b, t, CONV_WIDTH), F32)]
                        + conv_shape),
        scratch_shapes=conv_scratch,
        compiler_params=pltpu.CompilerParams(dimension_semantics=("parallel", "arbitrary"),
                                             vmem_limit_bytes=_vmem_limit(est)),
        name="in_proj",
    )(x, gain, w_in, *conv)


def _conv_post(y, b_ref, g_ref, bt_ref, beta_ref):
    y = y + b_ref[...]
    mu = jnp.mean(y, axis=-1, keepdims=True)
    yc = y - mu
    yn = yc * lax.rsqrt(jnp.mean(yc * yc, axis=-1, keepdims=True) + LN_EPS) * g_ref[...] + bt_ref[...]
    return yn * _sigmoid(yn) * beta_ref[...]


def _conv_tile(glu, prev_ref, w_ref, b_ref, g_ref, bt_ref, beta_ref, o_ref, buf, ybuf, first):
    tt = glu.shape[0]
    n_slab = buf.shape[0]
    slabs = [slice(s * LANES, (s + 1) * LANES) for s in range(n_slab)]

    @pl.when(first)
    def _():
        for s in range(n_slab):
            buf[s, 0:CONV_HALO] = prev_ref[:, slabs[s]]

    @pl.when(jnp.logical_not(first))
    def _():
        for s in range(n_slab):
            buf[s, 0:CONV_HALO] = buf[s, tt:tt + CONV_HALO]

    for s in range(n_slab):
        buf[s, CONV_HALO:CONV_HALO + tt] = glu[:, slabs[s]]

    shift = CONV_HALO - (CONV_K - 1)
    rc = min(CONV_CHUNK, tt // 2)
    for c0 in range(0, tt, 2 * rc):
        for parity in range(2):
            start = c0 + parity
            accs = []
            for s in range(n_slab):
                acc = jnp.zeros((rc, LANES), F32)
                for j in range(CONV_K):
                    x = buf[s, pl.ds(start + j + shift, rc, stride=2), :]
                    acc = acc + w_ref[j:j + 1, slabs[s]] * x
                accs.append(acc)
            y = _conv_post(jnp.concatenate(accs, axis=1), b_ref, g_ref, bt_ref, beta_ref)
            for s in range(n_slab):
                ybuf[s, pl.ds(start, rc, stride=2), :] = y[:, slabs[s]]
    for s in range(n_slab):
        o_ref[:, slabs[s]] = ybuf[s].astype(o_ref.dtype)


def _conv_sample_kernel(xp_ref, w_ref, b_ref, g_ref, bt_ref, beta_ref, o_ref):
    n_t = o_ref.shape[0]
    for t in range(n_t):
        acc = jnp.zeros(xp_ref.shape[1:], F32)
        for j in range(CONV_K):
            acc = acc + w_ref[j:j + 1, :] * xp_ref[t + j]
        o_ref[t] = _conv_post(acc, b_ref, g_ref, bt_ref, beta_ref)


def _conv_sample(xp_t, n_t, dw_w, dw_b, ln_g, ln_b, beta):
    vm = pl.BlockSpec(memory_space=pltpu.VMEM)
    return pl.pallas_call(
        _conv_sample_kernel,
        out_shape=jax.ShapeDtypeStruct((n_t,) + xp_t.shape[1:], F32),
        in_specs=[vm] * 6,
        out_specs=vm,
        name="conv_sample",
    )(xp_t, dw_w, dw_b, ln_g, ln_b, beta)


def _softmax_update(s, v_dot, m_sc, l_sc, acc_sc):
    width = s.shape[1]
    m_prev = m_sc[...]
    m_new = jnp.maximum(m_prev, jnp.max(s, axis=1, keepdims=True))
    alpha = jnp.exp2(m_prev - m_new)
    p = jnp.exp2(s - jnp.concatenate([m_new] * (width // LANES), axis=1))
    l_sc[...] = alpha * l_sc[...] + jnp.sum(p, axis=1, keepdims=True)
    reps = acc_sc.shape[1] // LANES
    acc_sc[...] = jnp.concatenate([alpha] * reps, axis=1) * acc_sc[...] + v_dot(p.astype(BF16))
    m_sc[...] = m_new


def _subln(att, g, beta):
    return _rms(att, g) * (1.0 - LAM_INIT) * beta


def _sample_page_copies(pt_ref, pool_k, pool_v, kbuf, vbuf, sem, chunk, slot, chunks_per_seq):
    seq = chunk // chunks_per_seq
    first_page = (chunk % chunks_per_seq) * PAGES_PER_STEP
    copies = []
    for i in range(PAGES_PER_STEP):
        page = pt_ref[seq, first_page + i]
        copies.append(pltpu.make_async_copy(pool_k.at[page], kbuf.at[slot, i], sem.at[0, slot]))
        copies.append(pltpu.make_async_copy(pool_v.at[page], vbuf.at[slot, i], sem.at[1, slot]))
    return copies


def _attn_kernel(pt_ref, qt_ref, k_ref, vt_ref, bias_ref, lam_ref, g_ref, beta_ref,
                 sq_ref, pool_k, pool_v, kn_ref, vn_ref, sbias_ref, o_ref, so_ref,
                 m_sc, acc_sc, kbuf, vbuf, sem, sm_sc, sl_sc, sacc_sc, cnt_ref):
    n_heads, e, blk = qt_ref.shape
    qi = pl.program_id(2)
    n_seq_s, chunks_per_seq = so_ref.shape[0], pt_ref.shape[1] // PAGES_PER_STEP
    n_chunks = n_seq_s * chunks_per_seq
    first_grid_step = jnp.logical_and(pl.program_id(0) == 0, qi == 0)
    last_grid_step = jnp.logical_and(pl.program_id(0) == pl.num_programs(0) - 1, qi == pl.num_programs(2) - 1)
    ring = (pt_ref, pool_k, pool_v, kbuf, vbuf, sem)

    n_slots = kbuf.shape[0]
    lookahead = n_slots - 1
    last_chunk = n_chunks - 1

    @pl.when(first_grid_step)
    def _():
        cnt_ref[0] = 0
        for d in range(lookahead):
            for cp in _sample_page_copies(*ring, min(d, last_chunk), d, chunks_per_seq):
                cp.start()

    s_lane = lax.broadcasted_iota(jnp.int32, sq_ref.shape[1:], 1)
    s_row = lax.broadcasted_iota(jnp.int32, sq_ref.shape[1:], 0)

    def sample_fetch():
        n = cnt_ref[0]
        slot = lax.rem(n, n_slots)
        valid = n < n_chunks
        chunk = jnp.minimum(n, last_chunk)
        seq = chunk // chunks_per_seq
        c = chunk % chunks_per_seq
        for cp in _sample_page_copies(*ring, chunk, slot, chunks_per_seq):
            cp.wait()
        ahead_slot = lax.rem(n + lookahead, n_slots)
        for cp in _sample_page_copies(*ring, jnp.minimum(n + lookahead, last_chunk), ahead_slot, chunks_per_seq):
            cp.start()
        cnt_ref[0] = n + 1
        return slot, seq, c, valid

    def sample_logits(slot, seq, c, valid):
        q = sq_ref[seq]
        qm = jnp.where((s_lane >> DIFF_HEAD_SHIFT) == (s_row & 1), q, jnp.zeros_like(q))
        gate = jnp.where(valid, 0.0, MASKED)
        far_bias = sbias_ref[0] + gate
        last_bias = jnp.where(c == chunks_per_seq - 1, sbias_ref[1] + gate, far_bias)
        s = jnp.concatenate(
            [_nt_dot(qm, kbuf[slot, i].astype(BF16)) + (last_bias if i == PAGES_PER_STEP - 1 else far_bias)
             for i in range(PAGES_PER_STEP)], axis=1)
        return qm, s

    def sample_update(slot, seq, c, valid, qm, s):
        fresh = c == 0
        m_prev = jnp.where(fresh, MASKED, sm_sc[...])
        m_new = jnp.maximum(m_prev, jnp.max(s, axis=1, keepdims=True))
        alpha = jnp.exp2(m_prev - m_new)
        p = jnp.exp2(s - jnp.concatenate([m_new] * (s.shape[1] // LANES), axis=1))
        sl_sc[...] = alpha * jnp.where(fresh, 0.0, sl_sc[...]) + jnp.sum(p, axis=1, keepdims=True)
        pb = p.astype(BF16)
        cols = kbuf.shape[2]
        pv = None
        for i in range(PAGES_PER_STEP):
            part = jnp.dot(pb[:, i * cols:(i + 1) * cols], vbuf[slot, i].astype(BF16),
                           preferred_element_type=F32)
            pv = part if pv is None else pv + part
        sacc_sc[...] = alpha * jnp.where(fresh, 0.0, sacc_sc[...]) + pv
        sm_sc[...] = m_new
        return seq, qm, jnp.logical_and(valid, c == chunks_per_seq - 1)

    def sample_finish(seq, qm, seq_done):
        @pl.when(seq_done)
        def _():
            n_new = kn_ref.shape[1]
            s_new = _nt_dot(qm, kn_ref[seq].astype(BF16)) + sbias_ref[2][:, :n_new]
            _softmax_update(s_new, lambda pn: jnp.dot(pn, vn_ref[seq].astype(BF16), preferred_element_type=F32),
                            sm_sc, sl_sc, sacc_sc)
            sign = jnp.where((s_row & 1) == 0, 1.0, -lam_ref[0:1, 0:1])
            z = sacc_sc[...] / sl_sc[...] * sign
            out_row = lax.broadcasted_iota(jnp.int32, (so_ref.shape[1], ATT_V_DIM), 0)
            for h in range(N_ATT_HEADS):
                att = jnp.zeros(out_row.shape, F32)
                for t in range(MAX_NEW_TOKENS):
                    r0 = t * ROWS_PER_TOKEN + h * 2
                    att = jnp.where(out_row == t, jnp.broadcast_to(z[r0:r0 + 1] + z[r0 + 1:r0 + 2], att.shape),
                                    att)
                sl = slice(h * ATT_V_DIM, (h + 1) * ATT_V_DIM)
                so_ref[seq, :, sl] = _subln(att, g_ref[...], beta_ref[:, sl])

    ones_rows = jnp.ones((acc_sc.shape[2] - e, blk), BF16)
    sub = lax.broadcasted_iota(jnp.int32, qt_ref.shape[1:], 0)
    qst = []
    for h in range(n_heads):
        qt = qt_ref[h]
        zero = jnp.zeros_like(qt)
        qst.append(jnp.concatenate([jnp.where(sub < DIFF_HEAD_DIM, qt, zero),
                                    jnp.where(sub >= DIFF_HEAD_DIM, qt, zero)], axis=1))
    m_sc[...] = jnp.full(m_sc.shape, MASKED, F32)
    acc_sc[0] = jnp.zeros(acc_sc.shape[1:], F32)

    def step(j, n_blk, biased, stale_max, src):
        rows = pl.ds(pl.multiple_of(j * blk, blk), n_blk * blk)
        dst = 1 - src
        chunk_state = sample_fetch()

        def logits(h):
            st = jnp.dot(k_ref[h, rows, :], qst[h], preferred_element_type=F32)
            if biased:
                bias = bias_ref[h, (2 - n_blk) * blk:, :]
                st = st + jnp.concatenate([bias, bias], axis=1)
            return st

        def weighted_values(h, pb):
            pv = None
            for i in range(n_blk):
                vt_aug = jnp.concatenate([vt_ref[h, j + i], ones_rows], axis=0)
                part = jnp.dot(vt_aug, pb[i * blk:(i + 1) * blk], preferred_element_type=F32)
                pv = part if pv is None else pv + part
            return pv

        def exact_update(h, st):
            m_prev = m_sc[h]
            m_new = jnp.maximum(m_prev, jnp.max(st, axis=0, keepdims=True))
            alpha = jnp.exp2(m_prev - m_new)
            pv = weighted_values(h, jnp.exp2(st - m_new).astype(BF16))
            acc_sc[dst, h] = alpha * acc_sc[src, h] + pv
            m_sc[h] = m_new

        def stale_update(h, st):
            m_ref = m_sc[h]
            pv = weighted_values(h, jnp.exp2(st - m_ref).astype(BF16))
            acc_sc[dst, h] = acc_sc[src, h] + pv
            return jnp.max(st, axis=0, keepdims=True) - m_ref

        ahead = 2
        pending = [logits(h) for h in range(min(ahead, n_heads))]
        chunk_logits = sample_logits(*chunk_state)
        excess = None
        for h in range(n_heads):
            if h + ahead < n_heads:
                pending.append(logits(h + ahead))
            if stale_max:
                over = stale_update(h, pending[h])
                excess = over if excess is None else jnp.maximum(excess, over)
            else:
                exact_update(h, pending[h])
            if h == 0:
                sample_done = sample_update(*chunk_state, *chunk_logits)

        sample_finish(*sample_done)
        if stale_max:
            @pl.when(jnp.max(excess) > STALE_MAX_MARGIN)
            def _():
                for h in range(n_heads):
                    exact_update(h, logits(h))

    odd = (qi + 1) & 1
    n_pairs = jnp.maximum(((qi + 1) >> 1) - 1, 0)

    @pl.when(qi == 0)
    def _():
        step(0, 1, True, False, 0)

    @pl.when(qi > 0)
    def _():
        step(qi - 1, 2, True, False, 0)

    def far_pair(i, carry):
        step(odd + 2 * i, 2, False, True, (i + 1) & 1)
        return carry

    lax.fori_loop(0, n_pairs, far_pair, 0)
    single = jnp.logical_and(qi > 0, odd == 1)

    @pl.when(single)
    def _():
        step(0, 1, False, True, (n_pairs + 1) & 1)

    res = (n_pairs + 1 + single.astype(jnp.int32)) & 1
    for h in range(n_heads):
        ot = acc_sc[res, h, 0:e] / acc_sc[res, h, e:e + 1]
        att = (ot[:, :blk] - lam_ref[0:1, 0:1] * ot[:, blk:]).T
        sl = slice(h * ATT_V_DIM, (h + 1) * ATT_V_DIM)
        o_ref[:, sl] = _subln(att, g_ref[...], beta_ref[:, sl]).astype(o_ref.dtype)

    @pl.when(last_grid_step)
    def _():
        def drain(i, carry):
            chunk_state = sample_fetch()
            sample_finish(*sample_update(*chunk_state, *sample_logits(*chunk_state)))
            return carry

        lax.fori_loop(0, jnp.maximum(n_chunks - cnt_ref[0], 0), drain, 0)
        n = cnt_ref[0]
        for d in range(lookahead):
            for cp in _sample_page_copies(*ring, jnp.minimum(n + d, last_chunk), lax.rem(n + d, n_slots),
                                          chunks_per_seq):
                cp.wait()


def _attention(qt, k, vt, bias, lam, subln_g, beta_att, page_table, q_rows, pool_k, pool_v, k_new, v_new,
               s_bias):
    b, h, t, e = k.shape
    blk = bias.shape[-1]
    n_blk = t // blk
    hs = ATT_HEADS_PER_STEP
    assert hs == h, "the sample epilogue reads every head's beta from the prompt block"
    n_seq, n_pages = page_table.shape
    rows = q_rows.shape[1]
    page_rows = pool_k.shape[1]
    npg = PAGES_PER_STEP
    assert n_pages % npg == 0
    once = pl.Buffered(1)
    whole = lambda a: pl.BlockSpec(a.shape, lambda i, j, n, pt: (0,) * a.ndim, pipeline_mode=once)
    sample_out = jax.ShapeDtypeStruct((n_seq, SAMPLE_ROWS, h * e), F32)
    grid_spec = pltpu.PrefetchScalarGridSpec(
        num_scalar_prefetch=1,
        grid=(b, h // hs, n_blk),
        in_specs=[pl.BlockSpec((None, hs, None, e, blk), lambda i, j, n, pt: (i, j, n, 0, 0)),
                  pl.BlockSpec((None, hs, t, e), lambda i, j, n, pt: (i, j, 0, 0), pipeline_mode=once),
                  pl.BlockSpec((None, hs, n_blk, e, blk), lambda i, j, n, pt: (i, j, 0, 0, 0), pipeline_mode=once),
                  pl.BlockSpec((hs, 2 * blk, blk), lambda i, j, n, pt: (j, 0, 0), pipeline_mode=once),
                  whole(lam), whole(subln_g),
                  pl.BlockSpec((1, hs * e), lambda i, j, n, pt: (0, j)),
                  whole(q_rows),
                  pl.BlockSpec(memory_space=pl.ANY), pl.BlockSpec(memory_space=pl.ANY),
                  whole(k_new), whole(v_new), whole(s_bias)],
        out_specs=(pl.BlockSpec((None, blk, hs * e), lambda i, j, n, pt: (i, n, j)),
                   pl.BlockSpec(sample_out.shape, lambda i, j, n, pt: (0, 0, 0))),
        scratch_shapes=[pltpu.VMEM((hs, 1, 2 * blk), F32),
                        pltpu.VMEM((2, hs, e + BF16_ROWS, 2 * blk), F32),
                        pltpu.VMEM((SAMPLE_RING_SLOTS, npg, page_rows, e), F32),
                        pltpu.VMEM((SAMPLE_RING_SLOTS, npg, page_rows, e), F32),
                        pltpu.SemaphoreType.DMA((2, SAMPLE_RING_SLOTS)),
                        pltpu.VMEM((rows, LANES), F32), pltpu.VMEM((rows, LANES), F32),
                        pltpu.VMEM((rows, e), F32),
                        pltpu.SMEM((1,), jnp.int32)],
    )
    est = (hs * (2 * t * e * 2) + hs * 2 * blk * blk * 4 + 4 * hs * 2 * blk * 2 * blk * 4
           + SAMPLE_RING_SLOTS * 2 * npg * page_rows * e * 4 + 2 * k_new.size * 4 + 8 * rows * npg * page_rows * 4)
    return pl.pallas_call(
        _attn_kernel,
        grid_spec=grid_spec,
        out_shape=(jax.ShapeDtypeStruct((b, t, h * e), BF16), sample_out),
        compiler_params=pltpu.CompilerParams(dimension_semantics=("arbitrary", "arbitrary", "arbitrary"),
                                             vmem_limit_bytes=_vmem_limit(est)),
        name="attention",
    )(page_table, qt, k, vt, bias, lam, subln_g, beta_att, q_rows, pool_k, pool_v, k_new, v_new, s_bias)


def _mem_kv_kernel(mem_ref, wk_ref, wv_ref, k_ref, v_ref):
    m = mem_ref[...].astype(BF16)
    k_ref[...] = jnp.dot(m, wk_ref[...], preferred_element_type=F32)
    v_ref[...] = jnp.dot(m, wv_ref[...], preferred_element_type=F32)


def _mem_kv(mem, w_xk, w_xv):
    b, n, d = mem.shape
    blk = pl.BlockSpec((None, n, d), lambda i: (i, 0, 0))
    w = pl.BlockSpec((d, d), lambda i: (0, 0))
    out = jax.ShapeDtypeStruct((b, n, d), F32)
    return pl.pallas_call(
        _mem_kv_kernel,
        grid=(b,),
        in_specs=[blk, w, w],
        out_specs=(blk, blk),
        out_shape=(out, out),
        compiler_params=pltpu.CompilerParams(dimension_semantics=("parallel",),
                                             vmem_limit_bytes=_vmem_limit(4 * d * d * 2 + 6 * n * d * 4)),
        name="mem_kv",
    )(mem, w_xk, w_xv)


def _mix_out_kernel(att_ref, conv_ref, h_ref, wo_ref, g_post_ref, g_x_ref, wq_ref, h1_ref, qx_ref):
    half = att_ref.shape[1]
    mo = (jnp.dot(att_ref[...].astype(BF16), wo_ref[0:half, :], preferred_element_type=F32)
          + jnp.dot(conv_ref[...].astype(BF16), wo_ref[half:, :], preferred_element_type=F32))
    h1 = h_ref[...] + _rms(mo, g_post_ref[...])
    h1_ref[...] = h1
    xn = _rms(h1, g_x_ref[...]).astype(BF16)
    x_scale = (wq_ref.shape[1] // N_X_HEADS) ** -0.5
    qx_ref[...] = (jnp.dot(xn, wq_ref[...], preferred_element_type=F32) * x_scale).astype(BF16)


def _mix_out(att, conv, h, w_out, g_post, g_x, w_xq):
    m, d = h.shape
    tm = min(MIX_ROW_TILE, m)
    row = lambda w: pl.BlockSpec((tm, w), lambda i: (i, 0))
    const = lambda a: pl.BlockSpec(a.shape, lambda i: (0, 0))
    est = 2 * 2 * d * d * 2 + 2 * tm * (2 * d * 4 + d * 2 + att.shape[1] * 6) + 4 * tm * d * 4
    return pl.pallas_call(
        _mix_out_kernel,
        grid=(m // tm,),
        in_specs=[row(att.shape[1]), row(conv.shape[1]), row(d), const(w_out), const(g_post), const(g_x),
                  const(w_xq)],
        out_specs=(row(d), row(d)),
        out_shape=(jax.ShapeDtypeStruct((m, d), F32), jax.ShapeDtypeStruct((m, d), BF16)),
        compiler_params=pltpu.CompilerParams(dimension_semantics=("parallel",),
                                             vmem_limit_bytes=_vmem_limit(est)),
        name="mix_out",
    )(att, conv, h, w_out, g_post, g_x, w_xq)


def _xattn_kernel(q_ref, mk_ref, mv_ref, o_ref):
    d = q_ref.shape[1]
    hd = d // N_X_HEADS
    by_piece = mk_ref.shape[1] != d
    width = LANES if by_piece else hd
    n_piece = hd // width
    stride = n_piece * N_X_HEADS
    n_tok = mk_ref.shape[0] // stride if by_piece else mk_ref.shape[0]

    def piece(ref, h, i):
        if by_piece:
            return ref[pl.ds(i * N_X_HEADS + h, n_tok, stride=stride), :].astype(BF16)
        return ref[:, h * hd:(h + 1) * hd].astype(BF16)

    for h in range(N_X_HEADS):
        s = None
        for i in range(n_piece):
            cols = slice(h * hd + i * width, h * hd + (i + 1) * width)
            part = _nt_dot(q_ref[:, cols], piece(mk_ref, h, i))
            s = part if s is None else s + part
        p = jnp.exp(s - jnp.max(s, axis=1, keepdims=True))
        l = jnp.sum(p, axis=1, keepdims=True)
        pb = p.astype(BF16)
        for i in range(n_piece):
            cols = slice(h * hd + i * width, h * hd + (i + 1) * width)
            o = jnp.dot(pb, piece(mv_ref, h, i), preferred_element_type=F32)
            o_ref[:, cols] = (o / l).astype(o_ref.dtype)


def _xattn(qx, mem_k, mem_v):
    b, t, d = qx.shape
    n = mem_k.shape[1]
    tm = min(ROW_TILE, t)
    row = pl.BlockSpec((None, tm, d), lambda i, j: (i, j, 0))
    mem = pl.BlockSpec((None,) + mem_k.shape[1:], lambda i, j: (i, 0, 0))
    return pl.pallas_call(
        _xattn_kernel,
        grid=(b, t // tm),
        in_specs=[row, mem, mem],
        out_specs=row,
        out_shape=jax.ShapeDtypeStruct((b, t, d), BF16),
        compiler_params=pltpu.CompilerParams(dimension_semantics=("parallel", "parallel"),
                                             vmem_limit_bytes=_vmem_limit(4 * n * d * 4 + 8 * tm * d * 4)),
        name="xattn",
    )(qx, mem_k, mem_v)


def _ffn_kernel(o_ref, h1_ref, wxo_ref, g_xpost_ref, g_pre_ref, wg_ref, wu_ref, wd_ref, g_post_ref, y_ref):
    h2 = h1_ref[...] + _rms(jnp.dot(o_ref[...], wxo_ref[...], preferred_element_type=F32), g_xpost_ref[...])
    xf = _rms(h2, g_pre_ref[...]).astype(BF16)
    d_ff = wg_ref.shape[1]
    f = jnp.zeros(h2.shape, F32)
    for c0 in range(0, d_ff, FFN_CHUNK):
        sl = slice(c0, c0 + FFN_CHUNK)
        g = jnp.dot(xf, wg_ref[:, sl], preferred_element_type=F32)
        u = jnp.dot(xf, wu_ref[:, sl], preferred_element_type=F32)
        a = (g * _sigmoid(g) * u).astype(BF16)
        f = f + jnp.dot(a, wd_ref[sl, :], preferred_element_type=F32)
    y_ref[...] = h2 + _rms(f, g_post_ref[...])


def _ffn(o, h1, w_xo, g_xpost, g_pre, w_gate, w_up, w_down, g_post):
    m, d = h1.shape
    d_ff = w_gate.shape[1]
    assert d_ff % FFN_CHUNK == 0
    tm = min(ROW_TILE, m)
    row = pl.BlockSpec((tm, d), lambda i: (i, 0))
    const = lambda a: pl.BlockSpec(a.shape, lambda i: (0, 0), pipeline_mode=pl.Buffered(1))
    est = (d * d + 3 * d * d_ff) * 2 + 2 * tm * d * (2 + 4 + 4) + 6 * tm * d * 4
    return pl.pallas_call(
        _ffn_kernel,
        grid=(m // tm,),
        in_specs=[row, row, const(w_xo), const(g_xpost), const(g_pre), const(w_gate), const(w_up),
                  const(w_down), const(g_post)],
        out_specs=row,
        out_shape=jax.ShapeDtypeStruct((m, d), F32),
        compiler_params=pltpu.CompilerParams(dimension_semantics=("parallel",),
                                             vmem_limit_bytes=_vmem_limit(est)),
        name="ffn",
    )(o, h1, w_xo, g_xpost, g_pre, w_gate, w_up, w_down, g_post)


def kernel(x_prompt, x_sample, mem_prompt, cache_k, cache_v, state_conv, cache_mem_k, cache_mem_v, page_table, rel_bias_table, norm_mix_pre, norm_mix_post, w_in, lambda_q1, lambda_k1, lambda_q2, lambda_k2, subln_g, dw_w, dw_b, conv_ln_g, conv_ln_b, beta_att, beta_conv, w_out, norm_x_pre, norm_x_post, w_xq, w_xk, w_xv, w_xo, norm_ffn_pre, norm_ffn_post, w_gate, w_up, w_down):
    assert w_in.shape[0] == 1, "single-layer trunk"
    bp, tp, d = x_prompt.shape
    bs, ts, _ = x_sample.shape
    assert ts <= MAX_NEW_TOKENS and tp >= CONV_K - 1
    n_mem = mem_prompt.shape[1]
    vec = lambda a: a[0].reshape(1, -1)
    wb = lambda a: a[0].astype(BF16)
    g_mix_pre, g_mix_post = vec(norm_mix_pre), vec(norm_mix_post)
    g_x_pre, g_x_post = vec(norm_x_pre), vec(norm_x_post)
    g_ffn_pre, g_ffn_post = vec(norm_ffn_pre), vec(norm_ffn_post)
    sub_g, b_att, b_conv = vec(subln_g), vec(beta_att), vec(beta_conv)
    c_b, c_g, c_bt = vec(dw_b), vec(conv_ln_g), vec(conv_ln_b)
    w_in_b, w_out_b, w_xq_b, w_xk_b, w_xv_b, w_xo_b = (wb(w) for w in (w_in, w_out, w_xq, w_xk, w_xv, w_xo))
    w_gate_b, w_up_b, w_down_b = wb(w_gate), wb(w_up), wb(w_down)
    dw = dw_w[0]

    blk = min(ATT_BLOCK, tp)
    lam_vecs = jnp.stack([lambda_q1[0], lambda_k1[0], lambda_q2[0], lambda_k2[0]])
    p_bias, s_bias, lam = _bias_tiles(rel_bias_table, lam_vecs, blk, ts)

    def tail(att, conv, h, mem_k, mem_v):
        b, t, _ = h.shape
        flat = lambda a: a.reshape(b * t, a.shape[-1])
        h1, qx = _mix_out(flat(att), flat(conv), flat(h), w_out_b, g_mix_post, g_x_pre, w_xq_b)
        o = _xattn(qx.reshape(b, t, d), mem_k, mem_v)
        y = _ffn(flat(o), h1, w_xo_b, g_x_post, g_ffn_pre, w_gate_b, w_up_b, w_down_b, g_ffn_post)
        return y.reshape(b, t, d)

    conv0 = jnp.zeros((bp, CONV_HALO, CONV_WIDTH), F32)
    qt_p, kb_p, vt_p, kf_p, vf_p, glu_p, conv_p = _in_proj(x_prompt, g_mix_pre, w_in_b, blk,
                                                          conv=(conv0, dw, c_b, c_g, c_bt, b_conv))
    x_s = jnp.pad(x_sample, ((0, 0), (0, SAMPLE_ROWS - ts), (0, 0)))
    rows_s = bs * SAMPLE_ROWS
    qt_s, _, _, kf_s, vf_s, glu_s = _in_proj(x_s.reshape(1, rows_s, d), g_mix_pre, w_in_b, min(blk, rows_s))
    new_rows = ts * N_ATT_HEADS
    kf_s = kf_s.reshape(bs, SAMPLE_ROWS * N_ATT_HEADS, ATT_V_DIM)[:, :new_rows]
    vf_s = vf_s.reshape(bs, SAMPLE_ROWS * N_ATT_HEADS, ATT_V_DIM)[:, :new_rows]
    glu_s = glu_s.reshape(bs, SAMPLE_ROWS, CONV_WIDTH)[:, :ts]
    q_s = jnp.transpose(qt_s[0], (0, 1, 3, 2))
    q_th = jnp.transpose(q_s.reshape(N_ATT_HEADS, bs, SAMPLE_ROWS, ATT_V_DIM), (1, 2, 0, 3))
    q_th = jnp.pad(q_th[:, :ts], ((0, 0), (0, MAX_NEW_TOKENS - ts), (0, 0), (0, 0)))
    q_rows = jnp.repeat(q_th.reshape(bs, MAX_NEW_TOKENS * N_ATT_HEADS, ATT_V_DIM), 2, axis=1)
    pad_page = lambda a: jnp.pad(a, ((0, 0), (0, PAGE_SIZE - new_rows), (0, 0)))
    n_phys = cache_k.shape[1]
    pool = lambda c: c.reshape(n_phys, PAGE_SIZE * N_ATT_HEADS, ATT_V_DIM)

    att_p, att_s = _attention(qt_p, kb_p, vt_p, p_bias, lam, sub_g, b_att, page_table, q_rows,
                              pool(cache_k), pool(cache_v), pad_page(kf_s), pad_page(vf_s), s_bias)

    mk_p, mv_p = _mem_kv(mem_prompt, w_xk_b, w_xv_b)
    y_p = tail(att_p, conv_p, x_prompt, mk_p, mv_p)

    xp_s = jnp.concatenate([state_conv[0], glu_s], axis=1)
    conv_s = _conv_sample(jnp.transpose(xp_s, (1, 0, 2)), ts, dw, c_b, c_g, c_bt, b_conv)
    conv_s = jnp.pad(jnp.transpose(conv_s, (1, 0, 2)), ((0, 0), (0, SAMPLE_ROWS - ts), (0, 0)))
    hd_x = d // N_X_HEADS
    mem_s = lambda c: jnp.transpose(c[0].reshape(bs, n_mem, N_X_HEADS, hd_x // LANES, LANES),
                                    (0, 1, 3, 2, 4)).reshape(bs, n_mem * d // LANES, LANES)
    y_s = tail(att_s, conv_s, x_s, mem_s(cache_mem_k), mem_s(cache_mem_v))

    heads = lambda a: a.reshape(1, a.shape[0], a.shape[1] // N_ATT_HEADS, N_ATT_HEADS, ATT_V_DIM)
    mem_heads = lambda a: a.reshape(1, bp, n_mem, N_X_HEADS, d // N_X_HEADS)
    return (y_p, y_s[:, :ts],
            heads(kf_p), heads(vf_p), glu_p[None, :, tp - (CONV_K - 1):],
            mem_heads(mk_p), mem_heads(mv_p),
            heads(kf_s), heads(vf_s), xp_s[None, :, ts:])
```

```python
import functools
import math

import numpy as np
import jax
import jax.numpy as jnp
from jax import lax
from jax.experimental import pallas as pl
from jax.experimental.pallas import tpu as pltpu

F32 = jnp.float32
BF16 = jnp.bfloat16

DIFF_HEAD_DIM = 64
DIFF_HEAD_SHIFT = DIFF_HEAD_DIM.bit_length() - 1
ATT_V_DIM = 2 * DIFF_HEAD_DIM
N_ATT_HEADS = 4
ATT_WIDTH = N_ATT_HEADS * ATT_V_DIM
CONV_WIDTH = 512
CONV_K = 31
N_BUCKETS = 32
MAX_DISTANCE = 128
N_X_HEADS = 4
PAGE_SIZE = 128
RMS_EPS = 1e-6
LN_EPS = 1e-5
ATT_SCALE = DIFF_HEAD_DIM ** -0.5
LOG2_E = math.log2(math.e)
LAM_INIT = 0.8 - 0.6 * math.exp(-0.3 * 0)
MASKED = -1e30
STALE_MAX_MARGIN = 30.0

V7X_VMEM_BYTES = 64 * 1024 * 1024
LANES = 128
SUBLANES = 8
BF16_ROWS = 16

ROW_TILE = 512
MIX_ROW_TILE = 1024
ATT_BLOCK = 256
ATT_HEADS_PER_STEP = 4
CONV_CHUNK = 64
CONV_HALO = 32
PAGES_PER_STEP = 8
SAMPLE_RING_SLOTS = 3
FFN_CHUNK = 256
SAMPLE_ROWS = 8
MAX_NEW_TOKENS = 4
ROWS_PER_TOKEN = 2 * N_ATT_HEADS
SAMPLE_Q_ROWS = MAX_NEW_TOKENS * ROWS_PER_TOKEN


VMEM_TEMPORARIES_FACTOR = 2
VMEM_SMALL_CALL_BYTES = 16 * 1024 * 1024
VMEM_LEFT_FREE_BYTES = 8 * 1024 * 1024


def _vmem_limit(block_bytes):
    wanted = max(VMEM_TEMPORARIES_FACTOR * block_bytes, VMEM_SMALL_CALL_BYTES)
    return int(min(wanted, V7X_VMEM_BYTES - VMEM_LEFT_FREE_BYTES))


def _rms(x, g):
    return x * lax.rsqrt(jnp.mean(x * x, axis=-1, keepdims=True) + RMS_EPS) * g


def _sigmoid(x):
    return 1.0 / (1.0 + jnp.exp(-x))


def _nt_dot(a, b):
    return lax.dot_general(a, b, (((1,), (1,)), ((), ())), preferred_element_type=F32)


def _bucket_np(n):
    n = np.maximum(n, 0)
    max_exact = N_BUCKETS // 2
    nf = np.maximum(n, 1).astype(np.float32)
    large = max_exact + (np.log(nf / max_exact) / math.log(MAX_DISTANCE / max_exact)
                         * (N_BUCKETS - max_exact)).astype(np.int32)
    large = np.minimum(large, N_BUCKETS - 1)
    return np.where(n < max_exact, n, large).astype(np.int32)


def _prompt_codes(blk):
    i = np.arange(blk)[None, :]
    j = np.arange(blk)[:, None]
    prev = _bucket_np(i - j + blk)
    diag = np.where(j > i, -1, _bucket_np(i - j))
    return np.concatenate([prev, diag]).astype(np.int32)


def _sample_codes(n_new):
    r = np.arange(SAMPLE_Q_ROWS)[:, None]
    c = np.arange(PAGE_SIZE * N_ATT_HEADS)[None, :]
    t, head = r // ROWS_PER_TOKEN, (r // 2) % N_ATT_HEADS
    tok, key_head = c // N_ATT_HEADS, c % N_ATT_HEADS
    own = key_head == head
    far = np.where(own, N_BUCKETS - 1, -1)
    last = np.where(own, _bucket_np(t + PAGE_SIZE - tok), -1)
    new = np.where(own & (tok <= t) & (tok < n_new) & (c < PAGE_SIZE), _bucket_np(t - tok), -1)
    return np.stack([far, last, new]).astype(np.int32)


def _bias_kernel(tab_ref, lam_in_ref, pcode_ref, scode_ref, pbias_ref, sbias_ref, lam_ref):
    far = N_BUCKETS - 1

    def lookup(code, h):
        out = jnp.zeros(code.shape, F32)
        for b in range(far):
            out = jnp.where(code == b, (tab_ref[b, h] - tab_ref[far, h]) * LOG2_E, out)
        return jnp.where(code < 0, MASKED, out)

    for h in range(N_ATT_HEADS):
        pbias_ref[h] = lookup(pcode_ref[...], h)
    row_head = (lax.broadcasted_iota(jnp.int32, scode_ref.shape[1:], 0) >> 1) & (N_ATT_HEADS - 1)
    for i in range(scode_ref.shape[0]):
        code = scode_ref[i]
        out = jnp.zeros(code.shape, F32)
        for h in range(N_ATT_HEADS):
            out = jnp.where(row_head == h, lookup(code, h), out)
        sbias_ref[i] = out
    lv = lam_in_ref[...]
    d1 = jnp.sum(lv[0:1] * lv[1:2], axis=1, keepdims=True)
    d2 = jnp.sum(lv[2:3] * lv[3:4], axis=1, keepdims=True)
    lam = jnp.exp(d1) - jnp.exp(d2) + LAM_INIT
    lam_ref[...] = jnp.broadcast_to(lam, lam_ref.shape)


def _bias_tiles(table, lam_vecs, blk, n_new):
    pcode = jnp.asarray(_prompt_codes(blk))
    scode = jnp.asarray(_sample_codes(n_new))
    vm = pl.BlockSpec(memory_space=pltpu.VMEM)
    return pl.pallas_call(
        _bias_kernel,
        out_shape=(jax.ShapeDtypeStruct((N_ATT_HEADS, 2 * blk, blk), F32),
                   jax.ShapeDtypeStruct(scode.shape, F32),
                   jax.ShapeDtypeStruct((SUBLANES, LANES), F32)),
        in_specs=[pl.BlockSpec(memory_space=pltpu.SMEM), vm, vm, vm],
        out_specs=(vm, vm, vm),
        name="bias_tiles",
    )(table, lam_vecs, pcode, scode)


def _in_proj_kernel(x_ref, g_ref, w_ref, *refs, with_conv):
    if with_conv:
        conv_in, refs = refs[:6], refs[6:]
        qt_ref, kb_ref, vt_ref, kf_ref, vf_ref, glu_ref, conv_ref, buf, ybuf = refs
    else:
        qt_ref, kb_ref, vt_ref, kf_ref, vf_ref, glu_ref = refs
    xn = _rms(x_ref[...], g_ref[...]).astype(BF16)

    def cols(c):
        return jnp.dot(xn, w_ref[:, c * ATT_WIDTH:(c + 1) * ATT_WIDTH], preferred_element_type=F32)

    tm = xn.shape[0]
    blk = qt_ref.shape[-1]
    glu = cols(3) * _sigmoid(cols(4))
    glu_ref[...] = glu
    if with_conv:
        _conv_tile(glu, *conv_in, conv_ref, buf, ybuf, pl.program_id(1) == 0)
    q = cols(0) * (ATT_SCALE * LOG2_E)
    k = cols(1)
    v = cols(2)
    kb = k.astype(BF16)
    for h in range(N_ATT_HEADS):
        sl = slice(h * ATT_V_DIM, (h + 1) * ATT_V_DIM)
        kb_ref[h] = kb[:, sl]
        for c in range(tm // blk):
            rows = slice(c * blk, (c + 1) * blk)
            qt_ref[h, c] = q[rows, sl].T.astype(BF16)
            vt_ref[h, c] = v[rows, sl].T.astype(BF16)
        kf_ref[pl.ds(h, tm, stride=N_ATT_HEADS), :] = k[:, sl]
        vf_ref[pl.ds(h, tm, stride=N_ATT_HEADS), :] = v[:, sl]


def _in_proj(x, gain, w_in, blk, conv=None):
    b, t, d = x.shape
    tm = min(ROW_TILE, t)
    n_cols = w_in.shape[1]
    assert tm % blk == 0
    c = CONV_WIDTH
    conv = () if conv is None else tuple(conv)
    vec = pl.BlockSpec((1, c), lambda i, j: (0, 0))
    conv_specs = [pl.BlockSpec((None, CONV_HALO, c), lambda i, j: (i, 0, 0)),
                  pl.BlockSpec((CONV_K, c), lambda i, j: (0, 0)), vec, vec, vec, vec] if conv else []
    conv_out = [pl.BlockSpec((None, tm, c), lambda i, j: (i, j, 0))] if conv else []
    conv_shape = [jax.ShapeDtypeStruct((b, t, c), BF16)] if conv else []
    conv_scratch = [pltpu.VMEM((c // LANES, tm + CONV_HALO, LANES), F32),
                    pltpu.VMEM((c // LANES, tm, LANES), F32)] if conv else []
    row = lambda w: pl.BlockSpec((None, tm, w), lambda i, j: (i, j, 0))
    heads = pl.BlockSpec((None, N_ATT_HEADS, tm, ATT_V_DIM), lambda i, j: (i, 0, j, 0))
    hm = jax.ShapeDtypeStruct((b, N_ATT_HEADS, t, ATT_V_DIM), BF16)
    heads_t = pl.BlockSpec((None, N_ATT_HEADS, tm // blk, ATT_V_DIM, blk), lambda i, j: (i, 0, j, 0, 0))
    hm_t = jax.ShapeDtypeStruct((b, N_ATT_HEADS, t // blk, ATT_V_DIM, blk), BF16)
    flat = jax.ShapeDtypeStruct((b, t * N_ATT_HEADS, ATT_V_DIM), F32)
    tok_head = pl.BlockSpec((None, tm * N_ATT_HEADS, ATT_V_DIM), lambda i, j: (i, j, 0))
    est = 2 * d * n_cols * 2 + 2 * tm * (d * 4 + 3 * ATT_WIDTH * 2 + 3 * ATT_WIDTH * 4) + 6 * tm * ATT_WIDTH * 4
    return pl.pallas_call(
        functools.partial(_in_proj_kernel, with_conv=bool(conv)),
        grid=(b, t // tm),
        in_specs=[row(d),
                  pl.BlockSpec((1, d), lambda i, j: (0, 0)),
                  pl.BlockSpec((d, n_cols), lambda i, j: (0, 0))] + conv_specs,
        out_specs=tuple([heads_t, heads, heads_t, tok_head, tok_head, row(CONV_WIDTH)] + conv_out),
        out_shape=tuple([hm_t, hm, hm_t, flat, flat, jax.ShapeDtypeStruct((b, t, CONV_WIDTH), F32)]
                        + conv_shape),
        scratch_shapes=conv_scratch,
        compiler_params=pltpu.CompilerParams(dimension_semantics=("parallel", "arbitrary"),
                                             vmem_limit_bytes=_vmem_limit(est)),
        name="in_proj",
    )(x, gain, w_in, *conv)


def _conv_post(y, b_ref, g_ref, bt_ref, beta_ref):
    y = y + b_ref[...]
    mu = jnp.mean(y, axis=-1, keepdims=True)
    yc = y - mu
    yn = yc * lax.rsqrt(jnp.mean(yc * yc, axis=-1, keepdims=True) + LN_EPS) * g_ref[...] + bt_ref[...]
    return yn * _sigmoid(yn) * beta_ref[...]


def _conv_tile(glu, prev_ref, w_ref, b_ref, g_ref, bt_ref, beta_ref, o_ref, buf, ybuf, first):
    tt = glu.shape[0]
    n_slab = buf.shape[0]
    slabs = [slice(s * LANES, (s + 1) * LANES) for s in range(n_slab)]

    @pl.when(first)
    def _():
        for s in range(n_slab):
            buf[s, 0:CONV_HALO] = prev_ref[:, slabs[s]]

    @pl.when(jnp.logical_not(first))
    def _():
        for s in range(n_slab):
            buf[s, 0:CONV_HALO] = buf[s, tt:tt + CONV_HALO]

    for s in range(n_slab):
        buf[s, CONV_HALO:CONV_HALO + tt] = glu[:, slabs[s]]

    shift = CONV_HALO - (CONV_K - 1)
    rc = min(CONV_CHUNK, tt // 2)
    for c0 in range(0, tt, 2 * rc):
        for parity in range(2):
            start = c0 + parity
            accs = []
            for s in range(n_slab):
                acc = jnp.zeros((rc, LANES), F32)
                for j in range(CONV_K):
                    x = buf[s, pl.ds(start + j + shift, rc, stride=2), :]
                    acc = acc + w_ref[j:j + 1, slabs[s]] * x
                accs.append(acc)
            y = _conv_post(jnp.concatenate(accs, axis=1), b_ref, g_ref, bt_ref, beta_ref)
            for s in range(n_slab):
                ybuf[s, pl.ds(start, rc, stride=2), :] = y[:, slabs[s]]
    for s in range(n_slab):
        o_ref[:, slabs[s]] = ybuf[s].astype(o_ref.dtype)


def _conv_sample_kernel(xp_ref, w_ref, b_ref, g_ref, bt_ref, beta_ref, o_ref):
    n_t = o_ref.shape[0]
    for t in range(n_t):
        acc = jnp.zeros(xp_ref.shape[1:], F32)
        for j in range(CONV_K):
            acc = acc + w_ref[j:j + 1, :] * xp_ref[t + j]
        o_ref[t] = _conv_post(acc, b_ref, g_ref, bt_ref, beta_ref)


def _conv_sample(xp_t, n_t, dw_w, dw_b, ln_g, ln_b, beta):
    vm = pl.BlockSpec(memory_space=pltpu.VMEM)
    return pl.pallas_call(
        _conv_sample_kernel,
        out_shape=jax.ShapeDtypeStruct((n_t,) + xp_t.shape[1:], F32),
        in_specs=[vm] * 6,
        out_specs=vm,
        name="conv_sample",
    )(xp_t, dw_w, dw_b, ln_g, ln_b, beta)


def _softmax_update(s, v_dot, m_sc, l_sc, acc_sc):
    width = s.shape[1]
    m_prev = m_sc[...]
    m_new = jnp.maximum(m_prev, jnp.max(s, axis=1, keepdims=True))
    alpha = jnp.exp2(m_prev - m_new)
    p = jnp.exp2(s - jnp.concatenate([m_new] * (width // LANES), axis=1))
    l_sc[...] = alpha * l_sc[...] + jnp.sum(p, axis=1, keepdims=True)
    reps = acc_sc.shape[1] // LANES
    acc_sc[...] = jnp.concatenate([alpha] * reps, axis=1) * acc_sc[...] + v_dot(p.astype(BF16))
    m_sc[...] = m_new


def _subln(att, g, beta):
    return _rms(att, g) * (1.0 - LAM_INIT) * beta


def _sample_page_copies(pt_ref, pool_k, pool_v, kbuf, vbuf, sem, chunk, slot, chunks_per_seq):
    seq = chunk // chunks_per_seq
    first_page = (chunk % chunks_per_seq) * PAGES_PER_STEP
    copies = []
    for i in range(PAGES_PER_STEP):
        page = pt_ref[seq, first_page + i]
        copies.append(pltpu.make_async_copy(pool_k.at[page], kbuf.at[slot, i], sem.at[0, slot]))
        copies.append(pltpu.make_async_copy(pool_v.at[page], vbuf.at[slot, i], sem.at[1, slot]))
    return copies


def _attn_kernel(pt_ref, qt_ref, k_ref, vt_ref, bias_ref, lam_ref, g_ref, beta_ref,
                 sq_ref, pool_k, pool_v, kn_ref, vn_ref, sbias_ref, o_ref, so_ref,
                 m_sc, acc_sc, kbuf, vbuf, sem, sm_sc, sl_sc, sacc_sc, cnt_ref):
    n_heads, e, blk = qt_ref.shape
    qi = pl.program_id(2)
    n_seq_s, chunks_per_seq = so_ref.shape[0], pt_ref.shape[1] // PAGES_PER_STEP
    n_chunks = n_seq_s * chunks_per_seq
    first_grid_step = jnp.logical_and(pl.program_id(0) == 0, qi == 0)
    last_grid_step = jnp.logical_and(pl.program_id(0) == pl.num_programs(0) - 1, qi == pl.num_programs(2) - 1)
    ring = (pt_ref, pool_k, pool_v, kbuf, vbuf, sem)

    n_slots = kbuf.shape[0]
    lookahead = n_slots - 1
    last_chunk = n_chunks - 1

    @pl.when(first_grid_step)
    def _():
        cnt_ref[0] = 0
        for d in range(lookahead):
            for cp in _sample_page_copies(*ring, min(d, last_chunk), d, chunks_per_seq):
                cp.start()

    s_lane = lax.broadcasted_iota(jnp.int32, sq_ref.shape[1:], 1)
    s_row = lax.broadcasted_iota(jnp.int32, sq_ref.shape[1:], 0)

    def sample_fetch():
        n = cnt_ref[0]
        slot = lax.rem(n, n_slots)
        valid = n < n_chunks
        chunk = jnp.minimum(n, last_chunk)
        seq = chunk // chunks_per_seq
        c = chunk % chunks_per_seq
        for cp in _sample_page_copies(*ring, chunk, slot, chunks_per_seq):
            cp.wait()
        ahead_slot = lax.rem(n + lookahead, n_slots)
        for cp in _sample_page_copies(*ring, jnp.minimum(n + lookahead, last_chunk), ahead_slot, chunks_per_seq):
            cp.start()
        cnt_ref[0] = n + 1
        return slot, seq, c, valid

    def sample_logits(slot, seq, c, valid):
        q = sq_ref[seq]
        qm = jnp.where((s_lane >> DIFF_HEAD_SHIFT) == (s_row & 1), q, jnp.zeros_like(q))
        gate = jnp.where(valid, 0.0, MASKED)
        far_bias = sbias_ref[0] + gate
        last_bias = jnp.where(c == chunks_per_seq - 1, sbias_ref[1] + gate, far_bias)
        s = jnp.concatenate(
            [_nt_dot(qm, kbuf[slot, i].astype(BF16)) + (last_bias if i == PAGES_PER_STEP - 1 else far_bias)
             for i in range(PAGES_PER_STEP)], axis=1)
        return qm, s

    def sample_update(slot, seq, c, valid, qm, s):
        fresh = c == 0
        m_prev = jnp.where(fresh, MASKED, sm_sc[...])
        m_new = jnp.maximum(m_prev, jnp.max(s, axis=1, keepdims=True))
        alpha = jnp.exp2(m_prev - m_new)
        p = jnp.exp2(s - jnp.concatenate([m_new] * (s.shape[1] // LANES), axis=1))
        sl_sc[...] = alpha * jnp.where(fresh, 0.0, sl_sc[...]) + jnp.sum(p, axis=1, keepdims=True)
        pb = p.astype(BF16)
        cols = kbuf.shape[2]
        pv = None
        for i in range(PAGES_PER_STEP):
            part = jnp.dot(pb[:, i * cols:(i + 1) * cols], vbuf[slot, i].astype(BF16),
                           preferred_element_type=F32)
            pv = part if pv is None else pv + part
        sacc_sc[...] = alpha * jnp.where(fresh, 0.0, sacc_sc[...]) + pv
        sm_sc[...] = m_new
        return seq, qm, jnp.logical_and(valid, c == chunks_per_seq - 1)

    def sample_finish(seq, qm, seq_done):
        @pl.when(seq_done)
        def _():
            n_new = kn_ref.shape[1]
            s_new = _nt_dot(qm, kn_ref[seq].astype(BF16)) + sbias_ref[2][:, :n_new]
            _softmax_update(s_new, lambda pn: jnp.dot(pn, vn_ref[seq].astype(BF16), preferred_element_type=F32),
                            sm_sc, sl_sc, sacc_sc)
            sign = jnp.where((s_row & 1) == 0, 1.0, -lam_ref[0:1, 0:1])
            z = sacc_sc[...] / sl_sc[...] * sign
            out_row = lax.broadcasted_iota(jnp.int32, (so_ref.shape[1], ATT_V_DIM), 0)
            for h in range(N_ATT_HEADS):
                att = jnp.zeros(out_row.shape, F32)
                for t in range(MAX_NEW_TOKENS):
                    r0 = t * ROWS_PER_TOKEN + h * 2
                    att = jnp.where(out_row == t, jnp.broadcast_to(z[r0:r0 + 1] + z[r0 + 1:r0 + 2], att.shape),
                                    att)
                sl = slice(h * ATT_V_DIM, (h + 1) * ATT_V_DIM)
                so_ref[seq, :, sl] = _subln(att, g_ref[...], beta_ref[:, sl])

    ones_rows = jnp.ones((acc_sc.shape[2] - e, blk), BF16)
    sub = lax.broadcasted_iota(jnp.int32, qt_ref.shape[1:], 0)
    qst = []
    for h in range(n_heads):
        qt = qt_ref[h]
        zero = jnp.zeros_like(qt)
        qst.append(jnp.concatenate([jnp.where(sub < DIFF_HEAD_DIM, qt, zero),
                                    jnp.where(sub >= DIFF_HEAD_DIM, qt, zero)], axis=1))
    m_sc[...] = jnp.full(m_sc.shape, MASKED, F32)
    acc_sc[0] = jnp.zeros(acc_sc.shape[1:], F32)

    def step(j, n_blk, biased, stale_max, src):
        rows = pl.ds(pl.multiple_of(j * blk, blk), n_blk * blk)
        dst = 1 - src
        chunk_state = sample_fetch()

        def logits(h):
            st = jnp.dot(k_ref[h, rows, :], qst[h], preferred_element_type=F32)
            if biased:
                bias = bias_ref[h, (2 - n_blk) * blk:, :]
                st = st + jnp.concatenate([bias, bias], axis=1)
            return st

        def weighted_values(h, pb):
            pv = None
            for i in range(n_blk):
                vt_aug = jnp.concatenate([vt_ref[h, j + i], ones_rows], axis=0)
                part = jnp.dot(vt_aug, pb[i * blk:(i + 1) * blk], preferred_element_type=F32)
                pv = part if pv is None else pv + part
            return pv

        def exact_update(h, st):
            m_prev = m_sc[h]
            m_new = jnp.maximum(m_prev, jnp.max(st, axis=0, keepdims=True))
            alpha = jnp.exp2(m_prev - m_new)
            pv = weighted_values(h, jnp.exp2(st - m_new).astype(BF16))
            acc_sc[dst, h] = alpha * acc_sc[src, h] + pv
            m_sc[h] = m_new

        def stale_update(h, st):
            m_ref = m_sc[h]
            pv = weighted_values(h, jnp.exp2(st - m_ref).astype(BF16))
            acc_sc[dst, h] = acc_sc[src, h] + pv
            return jnp.max(st, axis=0, keepdims=True) - m_ref

        ahead = 2
        chunk_logits = sample_logits(*chunk_state)
        pending = [logits(h) for h in range(min(ahead, n_heads))]
        excess = None
        for h in range(n_heads):
            if h + ahead < n_heads:
                pending.append(logits(h + ahead))
            if stale_max:
                over = stale_update(h, pending[h])
                excess = over if excess is None else jnp.maximum(excess, over)
            else:
                exact_update(h, pending[h])
        sample_finish(*sample_update(*chunk_state, *chunk_logits))
        if stale_max:
            @pl.when(jnp.max(excess) > STALE_MAX_MARGIN)
            def _():
                for h in range(n_heads):
                    exact_update(h, logits(h))

    odd = (qi + 1) & 1
    n_pairs = jnp.maximum(((qi + 1) >> 1) - 1, 0)

    @pl.when(qi == 0)
    def _():
        step(0, 1, True, False, 0)

    @pl.when(qi > 0)
    def _():
        step(qi - 1, 2, True, False, 0)

    def far_pair(i, carry):
        step(odd + 2 * i, 2, False, True, (i + 1) & 1)
        return carry

    lax.fori_loop(0, n_pairs, far_pair, 0)
    single = jnp.logical_and(qi > 0, odd == 1)

    @pl.when(single)
    def _():
        step(0, 1, False, True, (n_pairs + 1) & 1)

    res = (n_pairs + 1 + single.astype(jnp.int32)) & 1
    for h in range(n_heads):
        ot = acc_sc[res, h, 0:e] / acc_sc[res, h, e:e + 1]
        att = (ot[:, :blk] - lam_ref[0:1, 0:1] * ot[:, blk:]).T
        sl = slice(h * ATT_V_DIM, (h + 1) * ATT_V_DIM)
        o_ref[:, sl] = _subln(att, g_ref[...], beta_ref[:, sl]).astype(o_ref.dtype)

    @pl.when(last_grid_step)
    def _():
        def drain(i, carry):
            chunk_state = sample_fetch()
            sample_finish(*sample_update(*chunk_state, *sample_logits(*chunk_state)))
            return carry

        lax.fori_loop(0, jnp.maximum(n_chunks - cnt_ref[0], 0), drain, 0)
        n = cnt_ref[0]
        for d in range(lookahead):
            for cp in _sample_page_copies(*ring, jnp.minimum(n + d, last_chunk), lax.rem(n + d, n_slots),
                                          chunks_per_seq):
                cp.wait()


def _attention(qt, k, vt, bias, lam, subln_g, beta_att, page_table, q_rows, pool_k, pool_v, k_new, v_new,
               s_bias):
    b, h, t, e = k.shape
    blk = bias.shape[-1]
    n_blk = t // blk
    hs = ATT_HEADS_PER_STEP
    assert hs == h, "the sample epilogue reads every head's beta from the prompt block"
    n_seq, n_pages = page_table.shape
    rows = q_rows.shape[1]
    page_rows = pool_k.shape[1]
    npg = PAGES_PER_STEP
    assert n_pages % npg == 0
    once = pl.Buffered(1)
    whole = lambda a: pl.BlockSpec(a.shape, lambda i, j, n, pt: (0,) * a.ndim, pipeline_mode=once)
    sample_out = jax.ShapeDtypeStruct((n_seq, SAMPLE_ROWS, h * e), F32)
    grid_spec = pltpu.PrefetchScalarGridSpec(
        num_scalar_prefetch=1,
        grid=(b, h // hs, n_blk),
        in_specs=[pl.BlockSpec((None, hs, None, e, blk), lambda i, j, n, pt: (i, j, n, 0, 0)),
                  pl.BlockSpec((None, hs, t, e), lambda i, j, n, pt: (i, j, 0, 0), pipeline_mode=once),
                  pl.BlockSpec((None, hs, n_blk, e, blk), lambda i, j, n, pt: (i, j, 0, 0, 0), pipeline_mode=once),
                  pl.BlockSpec((hs, 2 * blk, blk), lambda i, j, n, pt: (j, 0, 0), pipeline_mode=once),
                  whole(lam), whole(subln_g),
                  pl.BlockSpec((1, hs * e), lambda i, j, n, pt: (0, j)),
                  whole(q_rows),
                  pl.BlockSpec(memory_space=pl.ANY), pl.BlockSpec(memory_space=pl.ANY),
                  whole(k_new), whole(v_new), whole(s_bias)],
        out_specs=(pl.BlockSpec((None, blk, hs * e), lambda i, j, n, pt: (i, n, j)),
                   pl.BlockSpec(sample_out.shape, lambda i, j, n, pt: (0, 0, 0))),
        scratch_shapes=[pltpu.VMEM((hs, 1, 2 * blk), F32),
                        pltpu.VMEM((2, hs, e + BF16_ROWS, 2 * blk), F32),
                        pltpu.VMEM((SAMPLE_RING_SLOTS, npg, page_rows, e), F32),
                        pltpu.VMEM((SAMPLE_RING_SLOTS, npg, page_rows, e), F32),
                        pltpu.SemaphoreType.DMA((2, SAMPLE_RING_SLOTS)),
                        pltpu.VMEM((rows, LANES), F32), pltpu.VMEM((rows, LANES), F32),
                        pltpu.VMEM((rows, e), F32),
                        pltpu.SMEM((1,), jnp.int32)],
    )
    est = (hs * (2 * t * e * 2) + hs * 2 * blk * blk * 4 + 4 * hs * 2 * blk * 2 * blk * 4
           + SAMPLE_RING_SLOTS * 2 * npg * page_rows * e * 4 + 2 * k_new.size * 4 + 8 * rows * npg * page_rows * 4)
    return pl.pallas_call(
        _attn_kernel,
        grid_spec=grid_spec,
        out_shape=(jax.ShapeDtypeStruct((b, t, h * e), BF16), sample_out),
        compiler_params=pltpu.CompilerParams(dimension_semantics=("arbitrary", "arbitrary", "arbitrary"),
                                             vmem_limit_bytes=_vmem_limit(est)),
        name="attention",
    )(page_table, qt, k, vt, bias, lam, subln_g, beta_att, q_rows, pool_k, pool_v, k_new, v_new, s_bias)


def _mem_kv_kernel(mem_ref, wk_ref, wv_ref, k_ref, v_ref):
    m = mem_ref[...].astype(BF16)
    k_ref[...] = jnp.dot(m, wk_ref[...], preferred_element_type=F32)
    v_ref[...] = jnp.dot(m, wv_ref[...], preferred_element_type=F32)


def _mem_kv(mem, w_xk, w_xv):
    b, n, d = mem.shape
    blk = pl.BlockSpec((None, n, d), lambda i: (i, 0, 0))
    w = pl.BlockSpec((d, d), lambda i: (0, 0))
    out = jax.ShapeDtypeStruct((b, n, d), F32)
    return pl.pallas_call(
        _mem_kv_kernel,
        grid=(b,),
        in_specs=[blk, w, w],
        out_specs=(blk, blk),
        out_shape=(out, out),
        compiler_params=pltpu.CompilerParams(dimension_semantics=("parallel",),
                                             vmem_limit_bytes=_vmem_limit(4 * d * d * 2 + 6 * n * d * 4)),
        name="mem_kv",
    )(mem, w_xk, w_xv)


def _mix_out_kernel(att_ref, conv_ref, h_ref, wo_ref, g_post_ref, g_x_ref, wq_ref, h1_ref, qx_ref):
    half = att_ref.shape[1]
    mo = (jnp.dot(att_ref[...].astype(BF16), wo_ref[0:half, :], preferred_element_type=F32)
          + jnp.dot(conv_ref[...].astype(BF16), wo_ref[half:, :], preferred_element_type=F32))
    h1 = h_ref[...] + _rms(mo, g_post_ref[...])
    h1_ref[...] = h1
    xn = _rms(h1, g_x_ref[...]).astype(BF16)
    x_scale = (wq_ref.shape[1] // N_X_HEADS) ** -0.5
    qx_ref[...] = (jnp.dot(xn, wq_ref[...], preferred_element_type=F32) * x_scale).astype(BF16)


def _mix_out(att, conv, h, w_out, g_post, g_x, w_xq):
    m, d = h.shape
    tm = min(MIX_ROW_TILE, m)
    row = lambda w: pl.BlockSpec((tm, w), lambda i: (i, 0))
    const = lambda a: pl.BlockSpec(a.shape, lambda i: (0, 0))
    est = 2 * 2 * d * d * 2 + 2 * tm * (2 * d * 4 + d * 2 + att.shape[1] * 6) + 4 * tm * d * 4
    return pl.pallas_call(
        _mix_out_kernel,
        grid=(m // tm,),
        in_specs=[row(att.shape[1]), row(conv.shape[1]), row(d), const(w_out), const(g_post), const(g_x),
                  const(w_xq)],
        out_specs=(row(d), row(d)),
        out_shape=(jax.ShapeDtypeStruct((m, d), F32), jax.ShapeDtypeStruct((m, d), BF16)),
        compiler_params=pltpu.CompilerParams(dimension_semantics=("parallel",),
                                             vmem_limit_bytes=_vmem_limit(est)),
        name="mix_out",
    )(att, conv, h, w_out, g_post, g_x, w_xq)


def _xattn_kernel(q_ref, mk_ref, mv_ref, o_ref):
    d = q_ref.shape[1]
    hd = d // N_X_HEADS
    by_piece = mk_ref.shape[1] != d
    width = LANES if by_piece else hd
    n_piece = hd // width
    stride = n_piece * N_X_HEADS
    n_tok = mk_ref.shape[0] // stride if by_piece else mk_ref.shape[0]

    def piece(ref, h, i):
        if by_piece:
            return ref[pl.ds(i * N_X_HEADS + h, n_tok, stride=stride), :].astype(BF16)
        return ref[:, h * hd:(h + 1) * hd].astype(BF16)

    for h in range(N_X_HEADS):
        s = None
        for i in range(n_piece):
            cols = slice(h * hd + i * width, h * hd + (i + 1) * width)
            part = _nt_dot(q_ref[:, cols], piece(mk_ref, h, i))
            s = part if s is None else s + part
        p = jnp.exp(s - jnp.max(s, axis=1, keepdims=True))
        l = jnp.sum(p, axis=1, keepdims=True)
        pb = p.astype(BF16)
        for i in range(n_piece):
            cols = slice(h * hd + i * width, h * hd + (i + 1) * width)
            o = jnp.dot(pb, piece(mv_ref, h, i), preferred_element_type=F32)
            o_ref[:, cols] = (o / l).astype(o_ref.dtype)


def _xattn(qx, mem_k, mem_v):
    b, t, d = qx.shape
    n = mem_k.shape[1]
    tm = min(ROW_TILE, t)
    row = pl.BlockSpec((None, tm, d), lambda i, j: (i, j, 0))
    mem = pl.BlockSpec((None,) + mem_k.shape[1:], lambda i, j: (i, 0, 0))
    return pl.pallas_call(
        _xattn_kernel,
        grid=(b, t // tm),
        in_specs=[row, mem, mem],
        out_specs=row,
        out_shape=jax.ShapeDtypeStruct((b, t, d), BF16),
        compiler_params=pltpu.CompilerParams(dimension_semantics=("parallel", "parallel"),
                                             vmem_limit_bytes=_vmem_limit(4 * n * d * 4 + 8 * tm * d * 4)),
        name="xattn",
    )(qx, mem_k, mem_v)


def _ffn_kernel(o_ref, h1_ref, wxo_ref, g_xpost_ref, g_pre_ref, wg_ref, wu_ref, wd_ref, g_post_ref, y_ref):
    h2 = h1_ref[...] + _rms(jnp.dot(o_ref[...], wxo_ref[...], preferred_element_type=F32), g_xpost_ref[...])
    xf = _rms(h2, g_pre_ref[...]).astype(BF16)
    d_ff = wg_ref.shape[1]
    f = jnp.zeros(h2.shape, F32)
    for c0 in range(0, d_ff, FFN_CHUNK):
        sl = slice(c0, c0 + FFN_CHUNK)
        g = jnp.dot(xf, wg_ref[:, sl], preferred_element_type=F32)
        u = jnp.dot(xf, wu_ref[:, sl], preferred_element_type=F32)
        a = (g * _sigmoid(g) * u).astype(BF16)
        f = f + jnp.dot(a, wd_ref[sl, :], preferred_element_type=F32)
    y_ref[...] = h2 + _rms(f, g_post_ref[...])


def _ffn(o, h1, w_xo, g_xpost, g_pre, w_gate, w_up, w_down, g_post):
    m, d = h1.shape
    d_ff = w_gate.shape[1]
    assert d_ff % FFN_CHUNK == 0
    tm = min(ROW_TILE, m)
    row = pl.BlockSpec((tm, d), lambda i: (i, 0))
    const = lambda a: pl.BlockSpec(a.shape, lambda i: (0, 0), pipeline_mode=pl.Buffered(1))
    est = (d * d + 3 * d * d_ff) * 2 + 2 * tm * d * (2 + 4 + 4) + 6 * tm * d * 4
    return pl.pallas_call(
        _ffn_kernel,
        grid=(m // tm,),
        in_specs=[row, row, const(w_xo), const(g_xpost), const(g_pre), const(w_gate), const(w_up),
                  const(w_down), const(g_post)],
        out_specs=row,
        out_shape=jax.ShapeDtypeStruct((m, d), F32),
        compiler_params=pltpu.CompilerParams(dimension_semantics=("parallel",),
                                             vmem_limit_bytes=_vmem_limit(est)),
        name="ffn",
    )(o, h1, w_xo, g_xpost, g_pre, w_gate, w_up, w_down, g_post)


def kernel(x_prompt, x_sample, mem_prompt, cache_k, cache_v, state_conv, cache_mem_k, cache_mem_v, page_table, rel_bias_table, norm_mix_pre, norm_mix_post, w_in, lambda_q1, lambda_k1, lambda_q2, lambda_k2, subln_g, dw_w, dw_b, conv_ln_g, conv_ln_b, beta_att, beta_conv, w_out, norm_x_pre, norm_x_post, w_xq, w_xk, w_xv, w_xo, norm_ffn_pre, norm_ffn_post, w_gate, w_up, w_down):
    assert w_in.shape[0] == 1, "single-layer trunk"
    bp, tp, d = x_prompt.shape
    bs, ts, _ = x_sample.shape
    assert ts <= MAX_NEW_TOKENS and tp >= CONV_K - 1
    n_mem = mem_prompt.shape[1]
    vec = lambda a: a[0].reshape(1, -1)
    wb = lambda a: a[0].astype(BF16)
    g_mix_pre, g_mix_post = vec(norm_mix_pre), vec(norm_mix_post)
    g_x_pre, g_x_post = vec(norm_x_pre), vec(norm_x_post)
    g_ffn_pre, g_ffn_post = vec(norm_ffn_pre), vec(norm_ffn_post)
    sub_g, b_att, b_conv = vec(subln_g), vec(beta_att), vec(beta_conv)
    c_b, c_g, c_bt = vec(dw_b), vec(conv_ln_g), vec(conv_ln_b)
    w_in_b, w_out_b, w_xq_b, w_xk_b, w_xv_b, w_xo_b = (wb(w) for w in (w_in, w_out, w_xq, w_xk, w_xv, w_xo))
    w_gate_b, w_up_b, w_down_b = wb(w_gate), wb(w_up), wb(w_down)
    dw = dw_w[0]

    blk = min(ATT_BLOCK, tp)
    lam_vecs = jnp.stack([lambda_q1[0], lambda_k1[0], lambda_q2[0], lambda_k2[0]])
    p_bias, s_bias, lam = _bias_tiles(rel_bias_table, lam_vecs, blk, ts)

    def tail(att, conv, h, mem_k, mem_v):
        b, t, _ = h.shape
        flat = lambda a: a.reshape(b * t, a.shape[-1])
        h1, qx = _mix_out(flat(att), flat(conv), flat(h), w_out_b, g_mix_post, g_x_pre, w_xq_b)
        o = _xattn(qx.reshape(b, t, d), mem_k, mem_v)
        y = _ffn(flat(o), h1, w_xo_b, g_x_post, g_ffn_pre, w_gate_b, w_up_b, w_down_b, g_ffn_post)
        return y.reshape(b, t, d)

    conv0 = jnp.zeros((bp, CONV_HALO, CONV_WIDTH), F32)
    qt_p, kb_p, vt_p, kf_p, vf_p, glu_p, conv_p = _in_proj(x_prompt, g_mix_pre, w_in_b, blk,
                                                          conv=(conv0, dw, c_b, c_g, c_bt, b_conv))
    x_s = jnp.pad(x_sample, ((0, 0), (0, SAMPLE_ROWS - ts), (0, 0)))
    rows_s = bs * SAMPLE_ROWS
    qt_s, _, _, kf_s, vf_s, glu_s = _in_proj(x_s.reshape(1, rows_s, d), g_mix_pre, w_in_b, min(blk, rows_s))
    new_rows = ts * N_ATT_HEADS
    kf_s = kf_s.reshape(bs, SAMPLE_ROWS * N_ATT_HEADS, ATT_V_DIM)[:, :new_rows]
    vf_s = vf_s.reshape(bs, SAMPLE_ROWS * N_ATT_HEADS, ATT_V_DIM)[:, :new_rows]
    glu_s = glu_s.reshape(bs, SAMPLE_ROWS, CONV_WIDTH)[:, :ts]
    q_s = jnp.transpose(qt_s[0], (0, 1, 3, 2))
    q_th = jnp.transpose(q_s.reshape(N_ATT_HEADS, bs, SAMPLE_ROWS, ATT_V_DIM), (1, 2, 0, 3))
    q_th = jnp.pad(q_th[:, :ts], ((0, 0), (0, MAX_NEW_TOKENS - ts), (0, 0), (0, 0)))
    q_rows = jnp.repeat(q_th.reshape(bs, MAX_NEW_TOKENS * N_ATT_HEADS, ATT_V_DIM), 2, axis=1)
    pad_page = lambda a: jnp.pad(a, ((0, 0), (0, PAGE_SIZE - new_rows), (0, 0)))
    n_phys = cache_k.shape[1]
    pool = lambda c: c.reshape(n_phys, PAGE_SIZE * N_ATT_HEADS, ATT_V_DIM)

    att_p, att_s = _attention(qt_p, kb_p, vt_p, p_bias, lam, sub_g, b_att, page_table, q_rows,
                              pool(cache_k), pool(cache_v), pad_page(kf_s), pad_page(vf_s), s_bias)

    mk_p, mv_p = _mem_kv(mem_prompt, w_xk_b, w_xv_b)
    y_p = tail(att_p, conv_p, x_prompt, mk_p, mv_p)

    xp_s = jnp.concatenate([state_conv[0], glu_s], axis=1)
    conv_s = _conv_sample(jnp.transpose(xp_s, (1, 0, 2)), ts, dw, c_b, c_g, c_bt, b_conv)
    conv_s = jnp.pad(jnp.transpose(conv_s, (1, 0, 2)), ((0, 0), (0, SAMPLE_ROWS - ts), (0, 0)))
    hd_x = d // N_X_HEADS
    mem_s = lambda c: jnp.transpose(c[0].reshape(bs, n_mem, N_X_HEADS, hd_x // LANES, LANES),
                                    (0, 1, 3, 2, 4)).reshape(bs, n_mem * d // LANES, LANES)
    y_s = tail(att_s, conv_s, x_s, mem_s(cache_mem_k), mem_s(cache_mem_v))

    heads = lambda a: a.reshape(1, a.shape[0], a.shape[1] // N_ATT_HEADS, N_ATT_HEADS, ATT_V_DIM)
    mem_heads = lambda a: a.reshape(1, bp, n_mem, N_X_HEADS, d // N_X_HEADS)
    return (y_p, y_s[:, :ts],
            heads(kf_p), heads(vf_p), glu_p[None, :, tp - (CONV_K - 1):],
            mem_heads(mk_p), mem_heads(mv_p),
            heads(kf_s), heads(vf_s), xp_s[None, :, ts:])
```

```python
import functools
import math

import numpy as np
import jax
import jax.numpy as jnp
from jax import lax
from jax.experimental import pallas as pl
from jax.experimental.pallas import tpu as pltpu

F32 = jnp.float32
BF16 = jnp.bfloat16

DIFF_HEAD_DIM = 64
DIFF_HEAD_SHIFT = DIFF_HEAD_DIM.bit_length() - 1
ATT_V_DIM = 2 * DIFF_HEAD_DIM
N_ATT_HEADS = 4
ATT_WIDTH = N_ATT_HEADS * ATT_V_DIM
CONV_WIDTH = 512
CONV_K = 31
N_BUCKETS = 32
MAX_DISTANCE = 128
N_X_HEADS = 4
PAGE_SIZE = 128
RMS_EPS = 1e-6
LN_EPS = 1e-5
ATT_SCALE = DIFF_HEAD_DIM ** -0.5
LOG2_E = math.log2(math.e)
LAM_INIT = 0.8 - 0.6 * math.exp(-0.3 * 0)
MASKED = -1e30
STALE_MAX_MARGIN = 30.0

V7X_VMEM_BYTES = 64 * 1024 * 1024
LANES = 128
SUBLANES = 8
BF16_ROWS = 16

ROW_TILE = 512
MIX_ROW_TILE = 1024
ATT_BLOCK = 256
ATT_HEADS_PER_STEP = 4
CONV_CHUNK = 64
CONV_HALO = 32
PAGES_PER_STEP = 8
SAMPLE_RING_SLOTS = 3
FFN_CHUNK = 256
SAMPLE_ROWS = 8
MAX_NEW_TOKENS = 4
ROWS_PER_TOKEN = 2 * N_ATT_HEADS
SAMPLE_Q_ROWS = MAX_NEW_TOKENS * ROWS_PER_TOKEN


VMEM_TEMPORARIES_FACTOR = 2
VMEM_SMALL_CALL_BYTES = 16 * 1024 * 1024
VMEM_LEFT_FREE_BYTES = 8 * 1024 * 1024


def _vmem_limit(block_bytes):
    wanted = max(VMEM_TEMPORARIES_FACTOR * block_bytes, VMEM_SMALL_CALL_BYTES)
    return int(min(wanted, V7X_VMEM_BYTES - VMEM_LEFT_FREE_BYTES))


def _rms(x, g):
    return x * lax.rsqrt(jnp.mean(x * x, axis=-1, keepdims=True) + RMS_EPS) * g


def _sigmoid(x):
    return 1.0 / (1.0 + jnp.exp(-x))


def _nt_dot(a, b):
    return lax.dot_general(a, b, (((1,), (1,)), ((), ())), preferred_element_type=F32)


def _bucket_np(n):
    n = np.maximum(n, 0)
    max_exact = N_BUCKETS // 2
    nf = np.maximum(n, 1).astype(np.float32)
    large = max_exact + (np.log(nf / max_exact) / math.log(MAX_DISTANCE / max_exact)
                         * (N_BUCKETS - max_exact)).astype(np.int32)
    large = np.minimum(large, N_BUCKETS - 1)
    return np.where(n < max_exact, n, large).astype(np.int32)


def _prompt_codes(blk):
    i = np.arange(blk)[None, :]
    j = np.arange(blk)[:, None]
    prev = _bucket_np(i - j + blk)
    diag = np.where(j > i, -1, _bucket_np(i - j))
    return np.concatenate([prev, diag]).astype(np.int32)


def _sample_codes(n_new):
    r = np.arange(SAMPLE_Q_ROWS)[:, None]
    c = np.arange(PAGE_SIZE * N_ATT_HEADS)[None, :]
    t, head = r // ROWS_PER_TOKEN, (r // 2) % N_ATT_HEADS
    tok, key_head = c // N_ATT_HEADS, c % N_ATT_HEADS
    own = key_head == head
    far = np.where(own, N_BUCKETS - 1, -1)
    last = np.where(own, _bucket_np(t + PAGE_SIZE - tok), -1)
    new = np.where(own & (tok <= t) & (tok < n_new) & (c < PAGE_SIZE), _bucket_np(t - tok), -1)
    return np.stack([far, last, new]).astype(np.int32)


def _bias_kernel(tab_ref, lam_in_ref, pcode_ref, scode_ref, pbias_ref, sbias_ref, lam_ref):
    far = N_BUCKETS - 1

    def lookup(code, h):
        out = jnp.zeros(code.shape, F32)
        for b in range(far):
            out = jnp.where(code == b, (tab_ref[b, h] - tab_ref[far, h]) * LOG2_E, out)
        return jnp.where(code < 0, MASKED, out)

    for h in range(N_ATT_HEADS):
        pbias_ref[h] = lookup(pcode_ref[...], h)
    row_head = (lax.broadcasted_iota(jnp.int32, scode_ref.shape[1:], 0) >> 1) & (N_ATT_HEADS - 1)
    for i in range(scode_ref.shape[0]):
        code = scode_ref[i]
        out = jnp.zeros(code.shape, F32)
        for h in range(N_ATT_HEADS):
            out = jnp.where(row_head == h, lookup(code, h), out)
        sbias_ref[i] = out
    lv = lam_in_ref[...]
    d1 = jnp.sum(lv[0:1] * lv[1:2], axis=1, keepdims=True)
    d2 = jnp.sum(lv[2:3] * lv[3:4], axis=1, keepdims=True)
    lam = jnp.exp(d1) - jnp.exp(d2) + LAM_INIT
    lam_ref[...] = jnp.broadcast_to(lam, lam_ref.shape)


def _bias_tiles(table, lam_vecs, blk, n_new):
    pcode = jnp.asarray(_prompt_codes(blk))
    scode = jnp.asarray(_sample_codes(n_new))
    vm = pl.BlockSpec(memory_space=pltpu.VMEM)
    return pl.pallas_call(
        _bias_kernel,
        out_shape=(jax.ShapeDtypeStruct((N_ATT_HEADS, 2 * blk, blk), F32),
                   jax.ShapeDtypeStruct(scode.shape, F32),
                   jax.ShapeDtypeStruct((SUBLANES, LANES), F32)),
        in_specs=[pl.BlockSpec(memory_space=pltpu.SMEM), vm, vm, vm],
        out_specs=(vm, vm, vm),
        name="bias_tiles",
    )(table, lam_vecs, pcode, scode)


def _in_proj_kernel(x_ref, g_ref, w_ref, *refs, with_conv):
    if with_conv:
        conv_in, refs = refs[:6], refs[6:]
        qt_ref, kb_ref, vt_ref, kf_ref, vf_ref, glu_ref, conv_ref, buf, ybuf = refs
    else:
        qt_ref, kb_ref, vt_ref, kf_ref, vf_ref, glu_ref = refs
    xn = _rms(x_ref[...], g_ref[...]).astype(BF16)

    def cols(c):
        return jnp.dot(xn, w_ref[:, c * ATT_WIDTH:(c + 1) * ATT_WIDTH], preferred_element_type=F32)

    tm = xn.shape[0]
    blk = qt_ref.shape[-1]
    glu = cols(3) * _sigmoid(cols(4))
    glu_ref[...] = glu
    if with_conv:
        _conv_tile(glu, *conv_in, conv_ref, buf, ybuf, pl.program_id(1) == 0)
    q = cols(0) * (ATT_SCALE * LOG2_E)
    k = cols(1)
    v = cols(2)
    kb = k.astype(BF16)
    for h in range(N_ATT_HEADS):
        sl = slice(h * ATT_V_DIM, (h + 1) * ATT_V_DIM)
        kb_ref[h] = kb[:, sl]
        for c in range(tm // blk):
            rows = slice(c * blk, (c + 1) * blk)
            qt_ref[h, c] = q[rows, sl].T.astype(BF16)
            vt_ref[h, c] = v[rows, sl].T.astype(BF16)
        kf_ref[pl.ds(h, tm, stride=N_ATT_HEADS), :] = k[:, sl]
        vf_ref[pl.ds(h, tm, stride=N_ATT_HEADS), :] = v[:, sl]


def _in_proj(x, gain, w_in, blk, conv=None):
    b, t, d = x.shape
    tm = min(ROW_TILE, t)
    n_cols = w_in.shape[1]
    assert tm % blk == 0
    c = CONV_WIDTH
    conv = () if conv is None else tuple(conv)
    vec = pl.BlockSpec((1, c), lambda i, j: (0, 0))
    conv_specs = [pl.BlockSpec((None, CONV_HALO, c), lambda i, j: (i, 0, 0)),
                  pl.BlockSpec((CONV_K, c), lambda i, j: (0, 0)), vec, vec, vec, vec] if conv else []
    conv_out = [pl.BlockSpec((None, tm, c), lambda i, j: (i, j, 0))] if conv else []
    conv_shape = [jax.ShapeDtypeStruct((b, t, c), BF16)] if conv else []
    conv_scratch = [pltpu.VMEM((c // LANES, tm + CONV_HALO, LANES), F32),
                    pltpu.VMEM((c // LANES, tm, LANES), F32)] if conv else []
    row = lambda w: pl.BlockSpec((None, tm, w), lambda i, j: (i, j, 0))
    heads = pl.BlockSpec((None, N_ATT_HEADS, tm, ATT_V_DIM), lambda i, j: (i, 0, j, 0))
    hm = jax.ShapeDtypeStruct((b, N_ATT_HEADS, t, ATT_V_DIM), BF16)
    heads_t = pl.BlockSpec((None, N_ATT_HEADS, tm // blk, ATT_V_DIM, blk), lambda i, j: (i, 0, j, 0, 0))
    hm_t = jax.ShapeDtypeStruct((b, N_ATT_HEADS, t // blk, ATT_V_DIM, blk), BF16)
    flat = jax.ShapeDtypeStruct((b, t * N_ATT_HEADS, ATT_V_DIM), F32)
    tok_head = pl.BlockSpec((None, tm * N_ATT_HEADS, ATT_V_DIM), lambda i, j: (i, j, 0))
    est = 2 * d * n_cols * 2 + 2 * tm * (d * 4 + 3 * ATT_WIDTH * 2 + 3 * ATT_WIDTH * 4) + 6 * tm * ATT_WIDTH * 4
    return pl.pallas_call(
        functools.partial(_in_proj_kernel, with_conv=bool(conv)),
        grid=(b, t // tm),
        in_specs=[row(d),
                  pl.BlockSpec((1, d), lambda i, j: (0, 0)),
                  pl.BlockSpec((d, n_cols), lambda i, j: (0, 0))] + conv_specs,
        out_specs=tuple([heads_t, heads, heads_t, tok_head, tok_head, row(CONV_WIDTH)] + conv_out),
        out_shape=tuple([hm_t, hm, hm_t, flat, flat, jax.ShapeDtypeStruct((b, t, CONV_WIDTH), F32)]
                        + conv_shape),
        scratch_shapes=conv_scratch,
        compiler_params=pltpu.CompilerParams(dimension_semantics=("parallel", "arbitrary"),
                                             vmem_limit_bytes=_vmem_limit(est)),
        name="in_proj",
    )(x, gain, w_in, *conv)


def _conv_post(y, b_ref, g_ref, bt_ref, beta_ref):
    y = y + b_ref[...]
    mu = jnp.mean(y, axis=-1, keepdims=True)
    yc = y - mu
    yn = yc * lax.rsqrt(jnp.mean(yc * yc, axis=-1, keepdims=True) + LN_EPS) * g_ref[...] + bt_ref[...]
    return yn * _sigmoid(yn) * beta_ref[...]


def _conv_tile(glu, prev_ref, w_ref, b_ref, g_ref, bt_ref, beta_ref, o_ref, buf, ybuf, first):
    tt = glu.shape[0]
    n_slab = buf.shape[0]
    slabs = [slice(s * LANES, (s + 1) * LANES) for s in range(n_slab)]

    @pl.when(first)
    def _():
        for s in range(n_slab):
            buf[s, 0:CONV_HALO] = prev_ref[:, slabs[s]]

    @pl.when(jnp.logical_not(first))
    def _():
        for s in range(n_slab):
            buf[s, 0:CONV_HALO] = buf[s, tt:tt + CONV_HALO]

    for s in range(n_slab):
        buf[s, CONV_HALO:CONV_HALO + tt] = glu[:, slabs[s]]

    shift = CONV_HALO - (CONV_K - 1)
    rc = min(CONV_CHUNK, tt // 2)
    for c0 in range(0, tt, 2 * rc):
        for parity in range(2):
            start = c0 + parity
            accs = []
            for s in range(n_slab):
                acc = jnp.zeros((rc, LANES), F32)
                for j in range(CONV_K):
                    x = buf[s, pl.ds(start + j + shift, rc, stride=2), :]
                    acc = acc + w_ref[j:j + 1, slabs[s]] * x
                accs.append(acc)
            y = _conv_post(jnp.concatenate(accs, axis=1), b_ref, g_ref, bt_ref, beta_ref)
            for s in range(n_slab):
                ybuf[s, pl.ds(start, rc, stride=2), :] = y[:, slabs[s]]
    for s in range(n_slab):
        o_ref[:, slabs[s]] = ybuf[s].astype(o_ref.dtype)


def _conv_sample_kernel(xp_ref, w_ref, b_ref, g_ref, bt_ref, beta_ref, o_ref):
    n_t = o_ref.shape[0]
    for t in range(n_t):
        acc = jnp.zeros(xp_ref.shape[1:], F32)
        for j in range(CONV_K):
            acc = acc + w_ref[j:j + 1, :] * xp_ref[t + j]
        o_ref[t] = _conv_post(acc, b_ref, g_ref, bt_ref, beta_ref)


def _conv_sample(xp_t, n_t, dw_w, dw_b, ln_g, ln_b, beta):
    vm = pl.BlockSpec(memory_space=pltpu.VMEM)
    return pl.pallas_call(
        _conv_sample_kernel,
        out_shape=jax.ShapeDtypeStruct((n_t,) + xp_t.shape[1:], F32),
        in_specs=[vm] * 6,
        out_specs=vm,
        name="conv_sample",
    )(xp_t, dw_w, dw_b, ln_g, ln_b, beta)


def _softmax_update(s, v_dot, m_sc, l_sc, acc_sc):
    width = s.shape[1]
    m_prev = m_sc[...]
    m_new = jnp.maximum(m_prev, jnp.max(s, axis=1, keepdims=True))
    alpha = jnp.exp2(m_prev - m_new)
    p = jnp.exp2(s - jnp.concatenate([m_new] * (width // LANES), axis=1))
    l_sc[...] = alpha * l_sc[...] + jnp.sum(p, axis=1, keepdims=True)
    reps = acc_sc.shape[1] // LANES
    acc_sc[...] = jnp.concatenate([alpha] * reps, axis=1) * acc_sc[...] + v_dot(p.astype(BF16))
    m_sc[...] = m_new


def _subln(att, g, beta):
    return _rms(att, g) * (1.0 - LAM_INIT) * beta


def _sample_page_copies(pt_ref, pool_k, pool_v, kbuf, vbuf, sem, chunk, slot, chunks_per_seq):
    seq = chunk // chunks_per_seq
    first_page = (chunk % chunks_per_seq) * PAGES_PER_STEP
    copies = []
    for i in range(PAGES_PER_STEP):
        page = pt_ref[seq, first_page + i]
        copies.append(pltpu.make_async_copy(pool_k.at[page], kbuf.at[slot, i], sem.at[0, slot]))
        copies.append(pltpu.make_async_copy(pool_v.at[page], vbuf.at[slot, i], sem.at[1, slot]))
    return copies


def _attn_kernel(pt_ref, qt_ref, k_ref, vt_ref, bias_ref, lam_ref, g_ref, beta_ref,
                 sq_ref, pool_k, pool_v, kn_ref, vn_ref, sbias_ref, o_ref, so_ref,
                 m_sc, acc_sc, kbuf, vbuf, sem, sm_sc, sl_sc, sacc_sc, cnt_ref):
    n_heads, e, blk = qt_ref.shape
    qi = pl.program_id(2)
    n_seq_s, chunks_per_seq = so_ref.shape[0], pt_ref.shape[1] // PAGES_PER_STEP
    n_chunks = n_seq_s * chunks_per_seq
    first_grid_step = jnp.logical_and(pl.program_id(0) == 0, qi == 0)
    last_grid_step = jnp.logical_and(pl.program_id(0) == pl.num_programs(0) - 1, qi == pl.num_programs(2) - 1)
    ring = (pt_ref, pool_k, pool_v, kbuf, vbuf, sem)

    n_slots = kbuf.shape[0]
    lookahead = n_slots - 1
    last_chunk = n_chunks - 1

    @pl.when(first_grid_step)
    def _():
        cnt_ref[0] = 0
        for d in range(lookahead):
            for cp in _sample_page_copies(*ring, min(d, last_chunk), d, chunks_per_seq):
                cp.start()

    s_lane = lax.broadcasted_iota(jnp.int32, sq_ref.shape[1:], 1)
    s_row = lax.broadcasted_iota(jnp.int32, sq_ref.shape[1:], 0)

    def sample_fetch():
        n = cnt_ref[0]
        slot = lax.rem(n, n_slots)
        valid = n < n_chunks
        chunk = jnp.minimum(n, last_chunk)
        seq = chunk // chunks_per_seq
        c = chunk % chunks_per_seq
        for cp in _sample_page_copies(*ring, chunk, slot, chunks_per_seq):
            cp.wait()
        ahead_slot = lax.rem(n + lookahead, n_slots)
        for cp in _sample_page_copies(*ring, jnp.minimum(n + lookahead, last_chunk), ahead_slot, chunks_per_seq):
            cp.start()
        cnt_ref[0] = n + 1
        return slot, seq, c, valid

    def sample_logits(slot, seq, c, valid):
        q = sq_ref[seq]
        qm = jnp.where((s_lane >> DIFF_HEAD_SHIFT) == (s_row & 1), q, jnp.zeros_like(q))
        gate = jnp.where(valid, 0.0, MASKED)
        far_bias = sbias_ref[0] + gate
        last_bias = jnp.where(c == chunks_per_seq - 1, sbias_ref[1] + gate, far_bias)
        s = jnp.concatenate(
            [_nt_dot(qm, kbuf[slot, i].astype(BF16)) + (last_bias if i == PAGES_PER_STEP - 1 else far_bias)
             for i in range(PAGES_PER_STEP)], axis=1)
        return qm, s

    def sample_update(slot, seq, c, valid, qm, s):
        fresh = c == 0
        m_prev = jnp.where(fresh, MASKED, sm_sc[...])
        m_new = jnp.maximum(m_prev, jnp.max(s, axis=1, keepdims=True))
        alpha = jnp.exp2(m_prev - m_new)
        p = jnp.exp2(s - jnp.concatenate([m_new] * (s.shape[1] // LANES), axis=1))
        sl_sc[...] = alpha * jnp.where(fresh, 0.0, sl_sc[...]) + jnp.sum(p, axis=1, keepdims=True)
        pb = p.astype(BF16)
        cols = kbuf.shape[2]
        pv = None
        for i in range(PAGES_PER_STEP):
            part = jnp.dot(pb[:, i * cols:(i + 1) * cols], vbuf[slot, i].astype(BF16),
                           preferred_element_type=F32)
            pv = part if pv is None else pv + part
        sacc_sc[...] = alpha * jnp.where(fresh, 0.0, sacc_sc[...]) + pv
        sm_sc[...] = m_new
        return seq, qm, jnp.logical_and(valid, c == chunks_per_seq - 1)

    def sample_finish(seq, qm, seq_done):
        @pl.when(seq_done)
        def _():
            n_new = kn_ref.shape[1]
            s_new = _nt_dot(qm, kn_ref[seq].astype(BF16)) + sbias_ref[2][:, :n_new]
            _softmax_update(s_new, lambda pn: jnp.dot(pn, vn_ref[seq].astype(BF16), preferred_element_type=F32),
                            sm_sc, sl_sc, sacc_sc)
            sign = jnp.where((s_row & 1) == 0, 1.0, -lam_ref[0:1, 0:1])
            z = sacc_sc[...] / sl_sc[...] * sign
            out_row = lax.broadcasted_iota(jnp.int32, (so_ref.shape[1], ATT_V_DIM), 0)
            for h in range(N_ATT_HEADS):
                att = jnp.zeros(out_row.shape, F32)
                for t in range(MAX_NEW_TOKENS):
                    r0 = t * ROWS_PER_TOKEN + h * 2
                    att = jnp.where(out_row == t, jnp.broadcast_to(z[r0:r0 + 1] + z[r0 + 1:r0 + 2], att.shape),
                                    att)
                sl = slice(h * ATT_V_DIM, (h + 1) * ATT_V_DIM)
                so_ref[seq, :, sl] = _subln(att, g_ref[...], beta_ref[:, sl])

    ones_rows = jnp.ones((acc_sc.shape[2] - e, blk), BF16)
    sub = lax.broadcasted_iota(jnp.int32, qt_ref.shape[1:], 0)
    qst = []
    for h in range(n_heads):
        qt = qt_ref[h]
        zero = jnp.zeros_like(qt)
        qst.append(jnp.concatenate([jnp.where(sub < DIFF_HEAD_DIM, qt, zero),
                                    jnp.where(sub >= DIFF_HEAD_DIM, qt, zero)], axis=1))
    m_sc[...] = jnp.full(m_sc.shape, MASKED, F32)
    acc_sc[0] = jnp.zeros(acc_sc.shape[1:], F32)

    def step(j, n_blk, biased, stale_max, src, with_sample):
        rows = pl.ds(pl.multiple_of(j * blk, blk), n_blk * blk)
        dst = 1 - src
        if with_sample:
            chunk_state = sample_fetch()

        def logits(h):
            st = jnp.dot(k_ref[h, rows, :], qst[h], preferred_element_type=F32)
            if biased:
                bias = bias_ref[h, (2 - n_blk) * blk:, :]
                st = st + jnp.concatenate([bias, bias], axis=1)
            return st

        def weighted_values(h, pb):
            pv = None
            for i in range(n_blk):
                vt_aug = jnp.concatenate([vt_ref[h, j + i], ones_rows], axis=0)
                part = jnp.dot(vt_aug, pb[i * blk:(i + 1) * blk], preferred_element_type=F32)
                pv = part if pv is None else pv + part
            return pv

        def exact_update(h, st):
            m_prev = m_sc[h]
            m_new = jnp.maximum(m_prev, jnp.max(st, axis=0, keepdims=True))
            alpha = jnp.exp2(m_prev - m_new)
            pv = weighted_values(h, jnp.exp2(st - m_new).astype(BF16))
            acc_sc[dst, h] = alpha * acc_sc[src, h] + pv
            m_sc[h] = m_new

        def stale_update(h, st):
            m_ref = m_sc[h]
            pv = weighted_values(h, jnp.exp2(st - m_ref).astype(BF16))
            acc_sc[dst, h] = acc_sc[src, h] + pv
            return jnp.max(st, axis=0, keepdims=True) - m_ref

        ahead = 2
        if with_sample:
            chunk_logits = sample_logits(*chunk_state)
        pending = [logits(h) for h in range(min(ahead, n_heads))]
        excess = None
        for h in range(n_heads):
            if h + ahead < n_heads:
                pending.append(logits(h + ahead))
            if stale_max:
                over = stale_update(h, pending[h])
                excess = over if excess is None else jnp.maximum(excess, over)
            else:
                exact_update(h, pending[h])
        if with_sample:
            sample_finish(*sample_update(*chunk_state, *chunk_logits))
        if stale_max:
            @pl.when(jnp.max(excess) > STALE_MAX_MARGIN)
            def _():
                for h in range(n_heads):
                    exact_update(h, logits(h))

    odd = (qi + 1) & 1
    n_pairs = jnp.maximum(((qi + 1) >> 1) - 1, 0)
    single = jnp.logical_and(qi > 0, odd == 1)

    def key_blocks(with_sample):
        @pl.when(qi == 0)
        def _():
            step(0, 1, True, False, 0, with_sample)

        @pl.when(qi > 0)
        def _():
            step(qi - 1, 2, True, False, 0, with_sample)

        def far_pair(i, carry):
            step(odd + 2 * i, 2, False, True, (i + 1) & 1, with_sample)
            return carry

        lax.fori_loop(0, n_pairs, far_pair, 0)

        @pl.when(single)
        def _():
            step(0, 1, False, True, (n_pairs + 1) & 1, with_sample)

    chunks_left = cnt_ref[0] < n_chunks

    @pl.when(chunks_left)
    def _():
        key_blocks(True)

    @pl.when(jnp.logical_not(chunks_left))
    def _():
        key_blocks(False)

    res = (n_pairs + 1 + single.astype(jnp.int32)) & 1
    for h in range(n_heads):
        ot = acc_sc[res, h, 0:e] / acc_sc[res, h, e:e + 1]
        att = (ot[:, :blk] - lam_ref[0:1, 0:1] * ot[:, blk:]).T
        sl = slice(h * ATT_V_DIM, (h + 1) * ATT_V_DIM)
        o_ref[:, sl] = _subln(att, g_ref[...], beta_ref[:, sl]).astype(o_ref.dtype)

    @pl.when(last_grid_step)
    def _():
        def drain(i, carry):
            chunk_state = sample_fetch()
            sample_finish(*sample_update(*chunk_state, *sample_logits(*chunk_state)))
            return carry

        lax.fori_loop(0, jnp.maximum(n_chunks - cnt_ref[0], 0), drain, 0)
        n = cnt_ref[0]
        for d in range(lookahead):
            for cp in _sample_page_copies(*ring, jnp.minimum(n + d, last_chunk), lax.rem(n + d, n_slots),
                                          chunks_per_seq):
                cp.wait()


def _attention(qt, k, vt, bias, lam, subln_g, beta_att, page_table, q_rows, pool_k, pool_v, k_new, v_new,
               s_bias):
    b, h, t, e = k.shape
    blk = bias.shape[-1]
    n_blk = t // blk
    hs = ATT_HEADS_PER_STEP
    assert hs == h, "the sample epilogue reads every head's beta from the prompt block"
    n_seq, n_pages = page_table.shape
    rows = q_rows.shape[1]
    page_rows = pool_k.shape[1]
    npg = PAGES_PER_STEP
    assert n_pages % npg == 0
    once = pl.Buffered(1)
    whole = lambda a: pl.BlockSpec(a.shape, lambda i, j, n, pt: (0,) * a.ndim, pipeline_mode=once)
    sample_out = jax.ShapeDtypeStruct((n_seq, SAMPLE_ROWS, h * e), F32)
    grid_spec = pltpu.PrefetchScalarGridSpec(
        num_scalar_prefetch=1,
        grid=(b, h // hs, n_blk),
        in_specs=[pl.BlockSpec((None, hs, None, e, blk), lambda i, j, n, pt: (i, j, n, 0, 0)),
                  pl.BlockSpec((None, hs, t, e), lambda i, j, n, pt: (i, j, 0, 0), pipeline_mode=once),
                  pl.BlockSpec((None, hs, n_blk, e, blk), lambda i, j, n, pt: (i, j, 0, 0, 0), pipeline_mode=once),
                  pl.BlockSpec((hs, 2 * blk, blk), lambda i, j, n, pt: (j, 0, 0), pipeline_mode=once),
                  whole(lam), whole(subln_g),
                  pl.BlockSpec((1, hs * e), lambda i, j, n, pt: (0, j)),
                  whole(q_rows),
                  pl.BlockSpec(memory_space=pl.ANY), pl.BlockSpec(memory_space=pl.ANY),
                  whole(k_new), whole(v_new), whole(s_bias)],
        out_specs=(pl.BlockSpec((None, blk, hs * e), lambda i, j, n, pt: (i, n, j)),
                   pl.BlockSpec(sample_out.shape, lambda i, j, n, pt: (0, 0, 0))),
        scratch_shapes=[pltpu.VMEM((hs, 1, 2 * blk), F32),
                        pltpu.VMEM((2, hs, e + BF16_ROWS, 2 * blk), F32),
                        pltpu.VMEM((SAMPLE_RING_SLOTS, npg, page_rows, e), F32),
                        pltpu.VMEM((SAMPLE_RING_SLOTS, npg, page_rows, e), F32),
                        pltpu.SemaphoreType.DMA((2, SAMPLE_RING_SLOTS)),
                        pltpu.VMEM((rows, LANES), F32), pltpu.VMEM((rows, LANES), F32),
                        pltpu.VMEM((rows, e), F32),
                        pltpu.SMEM((1,), jnp.int32)],
    )
    est = (hs * (2 * t * e * 2) + hs * 2 * blk * blk * 4 + 4 * hs * 2 * blk * 2 * blk * 4
           + SAMPLE_RING_SLOTS * 2 * npg * page_rows * e * 4 + 2 * k_new.size * 4 + 8 * rows * npg * page_rows * 4)
    return pl.pallas_call(
        _attn_kernel,
        grid_spec=grid_spec,
        out_shape=(jax.ShapeDtypeStruct((b, t, h * e), BF16), sample_out),
        compiler_params=pltpu.CompilerParams(dimension_semantics=("arbitrary", "arbitrary", "arbitrary"),
                                             vmem_limit_bytes=_vmem_limit(est)),
        name="attention",
    )(page_table, qt, k, vt, bias, lam, subln_g, beta_att, q_rows, pool_k, pool_v, k_new, v_new, s_bias)


def _mem_kv_kernel(mem_ref, wk_ref, wv_ref, k_ref, v_ref):
    m = mem_ref[...].astype(BF16)
    k_ref[...] = jnp.dot(m, wk_ref[...], preferred_element_type=F32)
    v_ref[...] = jnp.dot(m, wv_ref[...], preferred_element_type=F32)


def _mem_kv(mem, w_xk, w_xv):
    b, n, d = mem.shape
    blk = pl.BlockSpec((None, n, d), lambda i: (i, 0, 0))
    w = pl.BlockSpec((d, d), lambda i: (0, 0))
    out = jax.ShapeDtypeStruct((b, n, d), F32)
    return pl.pallas_call(
        _mem_kv_kernel,
        grid=(b,),
        in_specs=[blk, w, w],
        out_specs=(blk, blk),
        out_shape=(out, out),
        compiler_params=pltpu.CompilerParams(dimension_semantics=("parallel",),
                                             vmem_limit_bytes=_vmem_limit(4 * d * d * 2 + 6 * n * d * 4)),
        name="mem_kv",
    )(mem, w_xk, w_xv)


def _mix_out_kernel(att_ref, conv_ref, h_ref, wo_ref, g_post_ref, g_x_ref, wq_ref, h1_ref, qx_ref):
    half = att_ref.shape[1]
    mo = (jnp.dot(att_ref[...].astype(BF16), wo_ref[0:half, :], preferred_element_type=F32)
          + jnp.dot(conv_ref[...].astype(BF16), wo_ref[half:, :], preferred_element_type=F32))
    h1 = h_ref[...] + _rms(mo, g_post_ref[...])
    h1_ref[...] = h1
    xn = _rms(h1, g_x_ref[...]).astype(BF16)
    x_scale = (wq_ref.shape[1] // N_X_HEADS) ** -0.5
    qx_ref[...] = (jnp.dot(xn, wq_ref[...], preferred_element_type=F32) * x_scale).astype(BF16)


def _mix_out(att, conv, h, w_out, g_post, g_x, w_xq):
    m, d = h.shape
    tm = min(MIX_ROW_TILE, m)
    row = lambda w: pl.BlockSpec((tm, w), lambda i: (i, 0))
    const = lambda a: pl.BlockSpec(a.shape, lambda i: (0, 0))
    est = 2 * 2 * d * d * 2 + 2 * tm * (2 * d * 4 + d * 2 + att.shape[1] * 6) + 4 * tm * d * 4
    return pl.pallas_call(
        _mix_out_kernel,
        grid=(m // tm,),
        in_specs=[row(att.shape[1]), row(conv.shape[1]), row(d), const(w_out), const(g_post), const(g_x),
                  const(w_xq)],
        out_specs=(row(d), row(d)),
        out_shape=(jax.ShapeDtypeStruct((m, d), F32), jax.ShapeDtypeStruct((m, d), BF16)),
        compiler_params=pltpu.CompilerParams(dimension_semantics=("parallel",),
                                             vmem_limit_bytes=_vmem_limit(est)),
        name="mix_out",
    )(att, conv, h, w_out, g_post, g_x, w_xq)


def _xattn_kernel(q_ref, mk_ref, mv_ref, o_ref):
    d = q_ref.shape[1]
    hd = d // N_X_HEADS
    by_piece = mk_ref.shape[1] != d
    width = LANES if by_piece else hd
    n_piece = hd // width
    stride = n_piece * N_X_HEADS
    n_tok = mk_ref.shape[0] // stride if by_piece else mk_ref.shape[0]

    def piece(ref, h, i):
        if by_piece:
            return ref[pl.ds(i * N_X_HEADS + h, n_tok, stride=stride), :].astype(BF16)
        return ref[:, h * hd:(h + 1) * hd].astype(BF16)

    for h in range(N_X_HEADS):
        s = None
        for i in range(n_piece):
            cols = slice(h * hd + i * width, h * hd + (i + 1) * width)
            part = _nt_dot(q_ref[:, cols], piece(mk_ref, h, i))
            s = part if s is None else s + part
        p = jnp.exp(s - jnp.max(s, axis=1, keepdims=True))
        l = jnp.sum(p, axis=1, keepdims=True)
        pb = p.astype(BF16)
        for i in range(n_piece):
            cols = slice(h * hd + i * width, h * hd + (i + 1) * width)
            o = jnp.dot(pb, piece(mv_ref, h, i), preferred_element_type=F32)
            o_ref[:, cols] = (o / l).astype(o_ref.dtype)


def _xattn(qx, mem_k, mem_v):
    b, t, d = qx.shape
    n = mem_k.shape[1]
    tm = min(ROW_TILE, t)
    row = pl.BlockSpec((None, tm, d), lambda i, j: (i, j, 0))
    mem = pl.BlockSpec((None,) + mem_k.shape[1:], lambda i, j: (i, 0, 0))
    return pl.pallas_call(
        _xattn_kernel,
        grid=(b, t // tm),
        in_specs=[row, mem, mem],
        out_specs=row,
        out_shape=jax.ShapeDtypeStruct((b, t, d), BF16),
        compiler_params=pltpu.CompilerParams(dimension_semantics=("parallel", "parallel"),
                                             vmem_limit_bytes=_vmem_limit(4 * n * d * 4 + 8 * tm * d * 4)),
        name="xattn",
    )(qx, mem_k, mem_v)


def _ffn_kernel(o_ref, h1_ref, wxo_ref, g_xpost_ref, g_pre_ref, wg_ref, wu_ref, wd_ref, g_post_ref, y_ref):
    h2 = h1_ref[...] + _rms(jnp.dot(o_ref[...], wxo_ref[...], preferred_element_type=F32), g_xpost_ref[...])
    xf = _rms(h2, g_pre_ref[...]).astype(BF16)
    d_ff = wg_ref.shape[1]
    f = jnp.zeros(h2.shape, F32)
    for c0 in range(0, d_ff, FFN_CHUNK):
        sl = slice(c0, c0 + FFN_CHUNK)
        g = jnp.dot(xf, wg_ref[:, sl], preferred_element_type=F32)
        u = jnp.dot(xf, wu_ref[:, sl], preferred_element_type=F32)
        a = (g * _sigmoid(g) * u).astype(BF16)
        f = f + jnp.dot(a, wd_ref[sl, :], preferred_element_type=F32)
    y_ref[...] = h2 + _rms(f, g_post_ref[...])


def _ffn(o, h1, w_xo, g_xpost, g_pre, w_gate, w_up, w_down, g_post):
    m, d = h1.shape
    d_ff = w_gate.shape[1]
    assert d_ff % FFN_CHUNK == 0
    tm = min(ROW_TILE, m)
    row = pl.BlockSpec((tm, d), lambda i: (i, 0))
    const = lambda a: pl.BlockSpec(a.shape, lambda i: (0, 0), pipeline_mode=pl.Buffered(1))
    est = (d * d + 3 * d * d_ff) * 2 + 2 * tm * d * (2 + 4 + 4) + 6 * tm * d * 4
    return pl.pallas_call(
        _ffn_kernel,
        grid=(m // tm,),
        in_specs=[row, row, const(w_xo), const(g_xpost), const(g_pre), const(w_gate), const(w_up),
                  const(w_down), const(g_post)],
        out_specs=row,
        out_shape=jax.ShapeDtypeStruct((m, d), F32),
        compiler_params=pltpu.CompilerParams(dimension_semantics=("parallel",),
                                             vmem_limit_bytes=_vmem_limit(est)),
        name="ffn",
    )(o, h1, w_xo, g_xpost, g_pre, w_gate, w_up, w_down, g_post)


def kernel(x_prompt, x_sample, mem_prompt, cache_k, cache_v, state_conv, cache_mem_k, cache_mem_v, page_table, rel_bias_table, norm_mix_pre, norm_mix_post, w_in, lambda_q1, lambda_k1, lambda_q2, lambda_k2, subln_g, dw_w, dw_b, conv_ln_g, conv_ln_b, beta_att, beta_conv, w_out, norm_x_pre, norm_x_post, w_xq, w_xk, w_xv, w_xo, norm_ffn_pre, norm_ffn_post, w_gate, w_up, w_down):
    assert w_in.shape[0] == 1, "single-layer trunk"
    bp, tp, d = x_prompt.shape
    bs, ts, _ = x_sample.shape
    assert ts <= MAX_NEW_TOKENS and tp >= CONV_K - 1
    n_mem = mem_prompt.shape[1]
    vec = lambda a: a[0].reshape(1, -1)
    wb = lambda a: a[0].astype(BF16)
    g_mix_pre, g_mix_post = vec(norm_mix_pre), vec(norm_mix_post)
    g_x_pre, g_x_post = vec(norm_x_pre), vec(norm_x_post)
    g_ffn_pre, g_ffn_post = vec(norm_ffn_pre), vec(norm_ffn_post)
    sub_g, b_att, b_conv = vec(subln_g), vec(beta_att), vec(beta_conv)
    c_b, c_g, c_bt = vec(dw_b), vec(conv_ln_g), vec(conv_ln_b)
    w_in_b, w_out_b, w_xq_b, w_xk_b, w_xv_b, w_xo_b = (wb(w) for w in (w_in, w_out, w_xq, w_xk, w_xv, w_xo))
    w_gate_b, w_up_b, w_down_b = wb(w_gate), wb(w_up), wb(w_down)
    dw = dw_w[0]

    blk = min(ATT_BLOCK, tp)
    lam_vecs = jnp.stack([lambda_q1[0], lambda_k1[0], lambda_q2[0], lambda_k2[0]])
    p_bias, s_bias, lam = _bias_tiles(rel_bias_table, lam_vecs, blk, ts)

    def tail(att, conv, h, mem_k, mem_v):
        b, t, _ = h.shape
        flat = lambda a: a.reshape(b * t, a.shape[-1])
        h1, qx = _mix_out(flat(att), flat(conv), flat(h), w_out_b, g_mix_post, g_x_pre, w_xq_b)
        o = _xattn(qx.reshape(b, t, d), mem_k, mem_v)
        y = _ffn(flat(o), h1, w_xo_b, g_x_post, g_ffn_pre, w_gate_b, w_up_b, w_down_b, g_ffn_post)
        return y.reshape(b, t, d)

    conv0 = jnp.zeros((bp, CONV_HALO, CONV_WIDTH), F32)
    qt_p, kb_p, vt_p, kf_p, vf_p, glu_p, conv_p = _in_proj(x_prompt, g_mix_pre, w_in_b, blk,
                                                          conv=(conv0, dw, c_b, c_g, c_bt, b_conv))
    x_s = jnp.pad(x_sample, ((0, 0), (0, SAMPLE_ROWS - ts), (0, 0)))
    rows_s = bs * SAMPLE_ROWS
    qt_s, _, _, kf_s, vf_s, glu_s = _in_proj(x_s.reshape(1, rows_s, d), g_mix_pre, w_in_b, min(blk, rows_s))
    new_rows = ts * N_ATT_HEADS
    kf_s = kf_s.reshape(bs, SAMPLE_ROWS * N_ATT_HEADS, ATT_V_DIM)[:, :new_rows]
    vf_s = vf_s.reshape(bs, SAMPLE_ROWS * N_ATT_HEADS, ATT_V_DIM)[:, :new_rows]
    glu_s = glu_s.reshape(bs, SAMPLE_ROWS, CONV_WIDTH)[:, :ts]
    q_s = jnp.transpose(qt_s[0], (0, 1, 3, 2))
    q_th = jnp.transpose(q_s.reshape(N_ATT_HEADS, bs, SAMPLE_ROWS, ATT_V_DIM), (1, 2, 0, 3))
    q_th = jnp.pad(q_th[:, :ts], ((0, 0), (0, MAX_NEW_TOKENS - ts), (0, 0), (0, 0)))
    q_rows = jnp.repeat(q_th.reshape(bs, MAX_NEW_TOKENS * N_ATT_HEADS, ATT_V_DIM), 2, axis=1)
    pad_page = lambda a: jnp.pad(a, ((0, 0), (0, PAGE_SIZE - new_rows), (0, 0)))
    n_phys = cache_k.shape[1]
    pool = lambda c: c.reshape(n_phys, PAGE_SIZE * N_ATT_HEADS, ATT_V_DIM)

    att_p, att_s = _attention(qt_p, kb_p, vt_p, p_bias, lam, sub_g, b_att, page_table, q_rows,
                              pool(cache_k), pool(cache_v), pad_page(kf_s), pad_page(vf_s), s_bias)

    mk_p, mv_p = _mem_kv(mem_prompt, w_xk_b, w_xv_b)
    y_p = tail(att_p, conv_p, x_prompt, mk_p, mv_p)

    xp_s = jnp.concatenate([state_conv[0], glu_s], axis=1)
    conv_s = _conv_sample(jnp.transpose(xp_s, (1, 0, 2)), ts, dw, c_b, c_g, c_bt, b_conv)
    conv_s = jnp.pad(jnp.transpose(conv_s, (1, 0, 2)), ((0, 0), (0, SAMPLE_ROWS - ts), (0, 0)))
    hd_x = d // N_X_HEADS
    mem_s = lambda c: jnp.transpose(c[0].reshape(bs, n_mem, N_X_HEADS, hd_x // LANES, LANES),
                                    (0, 1, 3, 2, 4)).reshape(bs, n_mem * d // LANES, LANES)
    y_s = tail(att_s, conv_s, x_s, mem_s(cache_mem_k), mem_s(cache_mem_v))

    heads = lambda a: a.reshape(1, a.shape[0], a.shape[1] // N_ATT_HEADS, N_ATT_HEADS, ATT_V_DIM)
    mem_heads = lambda a: a.reshape(1, bp, n_mem, N_X_HEADS, d // N_X_HEADS)
    return (y_p, y_s[:, :ts],
            heads(kf_p), heads(vf_p), glu_p[None, :, tp - (CONV_K - 1):],
            mem_heads(mk_p), mem_heads(mv_p),
            heads(kf_s), heads(vf_s), xp_s[None, :, ts:])
```

```python
import functools
import math

import numpy as np
import jax
import jax.numpy as jnp
from jax import lax
from jax.experimental import pallas as pl
from jax.experimental.pallas import tpu as pltpu

F32 = jnp.float32
BF16 = jnp.bfloat16

DIFF_HEAD_DIM = 64
DIFF_HEAD_SHIFT = DIFF_HEAD_DIM.bit_length() - 1
ATT_V_DIM = 2 * DIFF_HEAD_DIM
N_ATT_HEADS = 4
ATT_WIDTH = N_ATT_HEADS * ATT_V_DIM
CONV_WIDTH = 512
CONV_K = 31
N_BUCKETS = 32
MAX_DISTANCE = 128
N_X_HEADS = 4
PAGE_SIZE = 128
RMS_EPS = 1e-6
LN_EPS = 1e-5
ATT_SCALE = DIFF_HEAD_DIM ** -0.5
LOG2_E = math.log2(math.e)
LAM_INIT = 0.8 - 0.6 * math.exp(-0.3 * 0)
MASKED = -1e30
STALE_MAX_MARGIN = 30.0

V7X_VMEM_BYTES = 64 * 1024 * 1024
LANES = 128
SUBLANES = 8
BF16_ROWS = 16

ROW_TILE = 512
IN_ROW_TILE = 1024
MIX_ROW_TILE = 1024
XATTN_ROW_TILE = 1024
ATT_BLOCK = 256
ATT_HEADS_PER_STEP = 4
CONV_CHUNK = 64
CONV_HALO = 32
PAGES_PER_STEP = 8
SAMPLE_RING_SLOTS = 3
FFN_CHUNK = 256
SAMPLE_ROWS = 8
MAX_NEW_TOKENS = 4
ROWS_PER_TOKEN = 2 * N_ATT_HEADS
SAMPLE_Q_ROWS = MAX_NEW_TOKENS * ROWS_PER_TOKEN


VMEM_TEMPORARIES_FACTOR = 2
VMEM_SMALL_CALL_BYTES = 16 * 1024 * 1024
VMEM_LEFT_FREE_BYTES = 8 * 1024 * 1024


def _vmem_limit(block_bytes):
    wanted = max(VMEM_TEMPORARIES_FACTOR * block_bytes, VMEM_SMALL_CALL_BYTES)
    return int(min(wanted, V7X_VMEM_BYTES - VMEM_LEFT_FREE_BYTES))


def _rms(x, g):
    return x * lax.rsqrt(jnp.mean(x * x, axis=-1, keepdims=True) + RMS_EPS) * g


def _sigmoid(x):
    return 1.0 / (1.0 + jnp.exp(-x))


def _nt_dot(a, b):
    return lax.dot_general(a, b, (((1,), (1,)), ((), ())), preferred_element_type=F32)


def _bucket_np(n):
    n = np.maximum(n, 0)
    max_exact = N_BUCKETS // 2
    nf = np.maximum(n, 1).astype(np.float32)
    large = max_exact + (np.log(nf / max_exact) / math.log(MAX_DISTANCE / max_exact)
                         * (N_BUCKETS - max_exact)).astype(np.int32)
    large = np.minimum(large, N_BUCKETS - 1)
    return np.where(n < max_exact, n, large).astype(np.int32)


def _prompt_codes(blk):
    i = np.arange(blk)[None, :]
    j = np.arange(blk)[:, None]
    prev = _bucket_np(i - j + blk)
    diag = np.where(j > i, -1, _bucket_np(i - j))
    return np.concatenate([prev, diag]).astype(np.int32)


def _sample_codes(n_new):
    r = np.arange(SAMPLE_Q_ROWS)[:, None]
    c = np.arange(PAGE_SIZE * N_ATT_HEADS)[None, :]
    t, head = r // ROWS_PER_TOKEN, (r // 2) % N_ATT_HEADS
    tok, key_head = c // N_ATT_HEADS, c % N_ATT_HEADS
    own = key_head == head
    far = np.where(own, N_BUCKETS - 1, -1)
    last = np.where(own, _bucket_np(t + PAGE_SIZE - tok), -1)
    new = np.where(own & (tok <= t) & (tok < n_new) & (c < PAGE_SIZE), _bucket_np(t - tok), -1)
    return np.stack([far, last, new]).astype(np.int32)


def _bias_kernel(tab_ref, lam_in_ref, pcode_ref, scode_ref, pbias_ref, sbias_ref, lam_ref):
    far = N_BUCKETS - 1

    def lookup(code, h):
        out = jnp.zeros(code.shape, F32)
        for b in range(far):
            out = jnp.where(code == b, (tab_ref[b, h] - tab_ref[far, h]) * LOG2_E, out)
        return jnp.where(code < 0, MASKED, out)

    for h in range(N_ATT_HEADS):
        pbias_ref[h] = lookup(pcode_ref[...], h)
    row_head = (lax.broadcasted_iota(jnp.int32, scode_ref.shape[1:], 0) >> 1) & (N_ATT_HEADS - 1)
    for i in range(scode_ref.shape[0]):
        code = scode_ref[i]
        out = jnp.zeros(code.shape, F32)
        for h in range(N_ATT_HEADS):
            out = jnp.where(row_head == h, lookup(code, h), out)
        sbias_ref[i] = out
    lv = lam_in_ref[...]
    d1 = jnp.sum(lv[0:1] * lv[1:2], axis=1, keepdims=True)
    d2 = jnp.sum(lv[2:3] * lv[3:4], axis=1, keepdims=True)
    lam = jnp.exp(d1) - jnp.exp(d2) + LAM_INIT
    lam_ref[...] = jnp.broadcast_to(lam, lam_ref.shape)


def _bias_tiles(table, lam_vecs, blk, n_new):
    pcode = jnp.asarray(_prompt_codes(blk))
    scode = jnp.asarray(_sample_codes(n_new))
    vm = pl.BlockSpec(memory_space=pltpu.VMEM)
    return pl.pallas_call(
        _bias_kernel,
        out_shape=(jax.ShapeDtypeStruct((N_ATT_HEADS, 2 * blk, blk), F32),
                   jax.ShapeDtypeStruct(scode.shape, F32),
                   jax.ShapeDtypeStruct((SUBLANES, LANES), F32)),
        in_specs=[pl.BlockSpec(memory_space=pltpu.SMEM), vm, vm, vm],
        out_specs=(vm, vm, vm),
        name="bias_tiles",
    )(table, lam_vecs, pcode, scode)


def _in_proj_kernel(x_ref, g_ref, w_ref, *refs, with_conv):
    if with_conv:
        conv_in, refs = refs[:6], refs[6:]
        qt_ref, kb_ref, vt_ref, kf_ref, vf_ref, glu_ref, conv_ref, buf, ybuf = refs
    else:
        qt_ref, kb_ref, vt_ref, kf_ref, vf_ref, glu_ref = refs
    xn = _rms(x_ref[...], g_ref[...]).astype(BF16)

    def cols(c):
        return jnp.dot(xn, w_ref[:, c * ATT_WIDTH:(c + 1) * ATT_WIDTH], preferred_element_type=F32)

    tm = xn.shape[0]
    blk = qt_ref.shape[-1]
    glu = cols(3) * _sigmoid(cols(4))
    glu_ref[...] = glu
    if with_conv:
        _conv_tile(glu, *conv_in, conv_ref, buf, ybuf, pl.program_id(1) == 0)
    q = cols(0) * (ATT_SCALE * LOG2_E)
    k = cols(1)
    v = cols(2)
    kb = k.astype(BF16)
    for h in range(N_ATT_HEADS):
        sl = slice(h * ATT_V_DIM, (h + 1) * ATT_V_DIM)
        kb_ref[h] = kb[:, sl]
        for c in range(tm // blk):
            rows = slice(c * blk, (c + 1) * blk)
            qt_ref[h, c] = q[rows, sl].T.astype(BF16)
            vt_ref[h, c] = v[rows, sl].T.astype(BF16)
        kf_ref[pl.ds(h, tm, stride=N_ATT_HEADS), :] = k[:, sl]
        vf_ref[pl.ds(h, tm, stride=N_ATT_HEADS), :] = v[:, sl]


def _in_proj(x, gain, w_in, blk, conv=None):
    b, t, d = x.shape
    tm = min(IN_ROW_TILE, t)
    n_cols = w_in.shape[1]
    assert tm % blk == 0
    c = CONV_WIDTH
    conv = () if conv is None else tuple(conv)
    vec = pl.BlockSpec((1, c), lambda i, j: (0, 0))
    conv_specs = [pl.BlockSpec((None, CONV_HALO, c), lambda i, j: (i, 0, 0)),
                  pl.BlockSpec((CONV_K, c), lambda i, j: (0, 0)), vec, vec, vec, vec] if conv else []
    conv_out = [pl.BlockSpec((None, tm, c), lambda i, j: (i, j, 0))] if conv else []
    conv_shape = [jax.ShapeDtypeStruct((b, t, c), BF16)] if conv else []
    conv_scratch = [pltpu.VMEM((c // LANES, tm + CONV_HALO, LANES), F32),
                    pltpu.VMEM((c // LANES, tm, LANES), F32)] if conv else []
    row = lambda w: pl.BlockSpec((None, tm, w), lambda i, j: (i, j, 0))
    heads = pl.BlockSpec((None, N_ATT_HEADS, tm, ATT_V_DIM), lambda i, j: (i, 0, j, 0))
    hm = jax.ShapeDtypeStruct((b, N_ATT_HEADS, t, ATT_V_DIM), BF16)
    heads_t = pl.BlockSpec((None, N_ATT_HEADS, tm // blk, ATT_V_DIM, blk), lambda i, j: (i, 0, j, 0, 0))
    hm_t = jax.ShapeDtypeStruct((b, N_ATT_HEADS, t // blk, ATT_V_DIM, blk), BF16)
    flat = jax.ShapeDtypeStruct((b, t * N_ATT_HEADS, ATT_V_DIM), F32)
    tok_head = pl.BlockSpec((None, tm * N_ATT_HEADS, ATT_V_DIM), lambda i, j: (i, j, 0))
    est = 2 * d * n_cols * 2 + 2 * tm * (d * 4 + 3 * ATT_WIDTH * 2 + 3 * ATT_WIDTH * 4) + 6 * tm * ATT_WIDTH * 4
    return pl.pallas_call(
        functools.partial(_in_proj_kernel, with_conv=bool(conv)),
        grid=(b, t // tm),
        in_specs=[row(d),
                  pl.BlockSpec((1, d), lambda i, j: (0, 0)),
                  pl.BlockSpec((d, n_cols), lambda i, j: (0, 0))] + conv_specs,
        out_specs=tuple([heads_t, heads, heads_t, tok_head, tok_head, row(CONV_WIDTH)] + conv_out),
        out_shape=tuple([hm_t, hm, hm_t, flat, flat, jax.ShapeDtypeStruct((b, t, CONV_WIDTH), F32)]
                        + conv_shape),
        scratch_shapes=conv_scratch,
        compiler_params=pltpu.CompilerParams(dimension_semantics=("parallel", "arbitrary"),
                                             vmem_limit_bytes=_vmem_limit(est)),
        name="in_proj",
    )(x, gain, w_in, *conv)


def _conv_post(y, b_ref, g_ref, bt_ref, beta_ref):
    y = y + b_ref[...]
    mu = jnp.mean(y, axis=-1, keepdims=True)
    yc = y - mu
    yn = yc * lax.rsqrt(jnp.mean(yc * yc, axis=-1, keepdims=True) + LN_EPS) * g_ref[...] + bt_ref[...]
    return yn * _sigmoid(yn) * beta_ref[...]


def _conv_tile(glu, prev_ref, w_ref, b_ref, g_ref, bt_ref, beta_ref, o_ref, buf, ybuf, first):
    tt = glu.shape[0]
    n_slab = buf.shape[0]
    slabs = [slice(s * LANES, (s + 1) * LANES) for s in range(n_slab)]

    @pl.when(first)
    def _():
        for s in range(n_slab):
            buf[s, 0:CONV_HALO] = prev_ref[:, slabs[s]]

    @pl.when(jnp.logical_not(first))
    def _():
        for s in range(n_slab):
            buf[s, 0:CONV_HALO] = buf[s, tt:tt + CONV_HALO]

    for s in range(n_slab):
        buf[s, CONV_HALO:CONV_HALO + tt] = glu[:, slabs[s]]

    shift = CONV_HALO - (CONV_K - 1)
    rc = min(CONV_CHUNK, tt // 2)
    for c0 in range(0, tt, 2 * rc):
        for parity in range(2):
            start = c0 + parity
            accs = []
            for s in range(n_slab):
                acc = jnp.zeros((rc, LANES), F32)
                for j in range(CONV_K):
                    x = buf[s, pl.ds(start + j + shift, rc, stride=2), :]
                    acc = acc + w_ref[j:j + 1, slabs[s]] * x
                accs.append(acc)
            y = _conv_post(jnp.concatenate(accs, axis=1), b_ref, g_ref, bt_ref, beta_ref)
            for s in range(n_slab):
                ybuf[s, pl.ds(start, rc, stride=2), :] = y[:, slabs[s]]
    for s in range(n_slab):
        o_ref[:, slabs[s]] = ybuf[s].astype(o_ref.dtype)


def _conv_sample_kernel(xp_ref, w_ref, b_ref, g_ref, bt_ref, beta_ref, o_ref):
    n_t = o_ref.shape[0]
    for t in range(n_t):
        acc = jnp.zeros(xp_ref.shape[1:], F32)
        for j in range(CONV_K):
            acc = acc + w_ref[j:j + 1, :] * xp_ref[t + j]
        o_ref[t] = _conv_post(acc, b_ref, g_ref, bt_ref, beta_ref)


def _conv_sample(xp_t, n_t, dw_w, dw_b, ln_g, ln_b, beta):
    vm = pl.BlockSpec(memory_space=pltpu.VMEM)
    return pl.pallas_call(
        _conv_sample_kernel,
        out_shape=jax.ShapeDtypeStruct((n_t,) + xp_t.shape[1:], F32),
        in_specs=[vm] * 6,
        out_specs=vm,
        name="conv_sample",
    )(xp_t, dw_w, dw_b, ln_g, ln_b, beta)


def _softmax_update(s, v_dot, m_sc, l_sc, acc_sc):
    width = s.shape[1]
    m_prev = m_sc[...]
    m_new = jnp.maximum(m_prev, jnp.max(s, axis=1, keepdims=True))
    alpha = jnp.exp2(m_prev - m_new)
    p = jnp.exp2(s - jnp.concatenate([m_new] * (width // LANES), axis=1))
    l_sc[...] = alpha * l_sc[...] + jnp.sum(p, axis=1, keepdims=True)
    reps = acc_sc.shape[1] // LANES
    acc_sc[...] = jnp.concatenate([alpha] * reps, axis=1) * acc_sc[...] + v_dot(p.astype(BF16))
    m_sc[...] = m_new


def _subln(att, g, beta):
    return _rms(att, g) * (1.0 - LAM_INIT) * beta


def _sample_page_copies(pt_ref, pool_k, pool_v, kbuf, vbuf, sem, chunk, slot, chunks_per_seq):
    seq = chunk // chunks_per_seq
    first_page = (chunk % chunks_per_seq) * PAGES_PER_STEP
    copies = []
    for i in range(PAGES_PER_STEP):
        page = pt_ref[seq, first_page + i]
        copies.append(pltpu.make_async_copy(pool_k.at[page], kbuf.at[slot, i], sem.at[0, slot]))
        copies.append(pltpu.make_async_copy(pool_v.at[page], vbuf.at[slot, i], sem.at[1, slot]))
    return copies


def _attn_kernel(pt_ref, qt_ref, k_ref, vt_ref, bias_ref, lam_ref, g_ref, beta_ref,
                 sq_ref, pool_k, pool_v, kn_ref, vn_ref, sbias_ref, o_ref, so_ref,
                 m_sc, acc_sc, kbuf, vbuf, sem, sm_sc, sl_sc, sacc_sc, cnt_ref):
    n_heads, e, blk = qt_ref.shape
    qi = pl.program_id(2)
    n_seq_s, chunks_per_seq = so_ref.shape[0], pt_ref.shape[1] // PAGES_PER_STEP
    n_chunks = n_seq_s * chunks_per_seq
    first_grid_step = jnp.logical_and(pl.program_id(0) == 0, qi == 0)
    last_grid_step = jnp.logical_and(pl.program_id(0) == pl.num_programs(0) - 1, qi == pl.num_programs(2) - 1)
    ring = (pt_ref, pool_k, pool_v, kbuf, vbuf, sem)

    n_slots = kbuf.shape[0]
    lookahead = n_slots - 1
    last_chunk = n_chunks - 1

    @pl.when(first_grid_step)
    def _():
        cnt_ref[0] = 0
        for d in range(lookahead):
            for cp in _sample_page_copies(*ring, min(d, last_chunk), d, chunks_per_seq):
                cp.start()

    s_lane = lax.broadcasted_iota(jnp.int32, sq_ref.shape[1:], 1)
    s_row = lax.broadcasted_iota(jnp.int32, sq_ref.shape[1:], 0)

    def sample_fetch():
        n = cnt_ref[0]
        slot = lax.rem(n, n_slots)
        valid = n < n_chunks
        chunk = jnp.minimum(n, last_chunk)
        seq = chunk // chunks_per_seq
        c = chunk % chunks_per_seq
        for cp in _sample_page_copies(*ring, chunk, slot, chunks_per_seq):
            cp.wait()
        ahead_slot = lax.rem(n + lookahead, n_slots)
        for cp in _sample_page_copies(*ring, jnp.minimum(n + lookahead, last_chunk), ahead_slot, chunks_per_seq):
            cp.start()
        cnt_ref[0] = n + 1
        return slot, seq, c, valid

    def sample_logits(slot, seq, c, valid):
        q = sq_ref[seq]
        qm = jnp.where((s_lane >> DIFF_HEAD_SHIFT) == (s_row & 1), q, jnp.zeros_like(q))
        gate = jnp.where(valid, 0.0, MASKED)
        far_bias = sbias_ref[0] + gate
        last_bias = jnp.where(c == chunks_per_seq - 1, sbias_ref[1] + gate, far_bias)
        s = jnp.concatenate(
            [_nt_dot(qm, kbuf[slot, i].astype(BF16)) + (last_bias if i == PAGES_PER_STEP - 1 else far_bias)
             for i in range(PAGES_PER_STEP)], axis=1)
        return qm, s

    def sample_update(slot, seq, c, valid, qm, s):
        fresh = c == 0
        m_prev = jnp.where(fresh, MASKED, sm_sc[...])
        m_new = jnp.maximum(m_prev, jnp.max(s, axis=1, keepdims=True))
        alpha = jnp.exp2(m_prev - m_new)
        p = jnp.exp2(s - jnp.concatenate([m_new] * (s.shape[1] // LANES), axis=1))
        sl_sc[...] = alpha * jnp.where(fresh, 0.0, sl_sc[...]) + jnp.sum(p, axis=1, keepdims=True)
        pb = p.astype(BF16)
        cols = kbuf.shape[2]
        pv = None
        for i in range(PAGES_PER_STEP):
            part = jnp.dot(pb[:, i * cols:(i + 1) * cols], vbuf[slot, i].astype(BF16),
                           preferred_element_type=F32)
            pv = part if pv is None else pv + part
        sacc_sc[...] = alpha * jnp.where(fresh, 0.0, sacc_sc[...]) + pv
        sm_sc[...] = m_new
        return seq, qm, jnp.logical_and(valid, c == chunks_per_seq - 1)

    def sample_finish(seq, qm, seq_done):
        @pl.when(seq_done)
        def _():
            n_new = kn_ref.shape[1]
            s_new = _nt_dot(qm, kn_ref[seq].astype(BF16)) + sbias_ref[2][:, :n_new]
            _softmax_update(s_new, lambda pn: jnp.dot(pn, vn_ref[seq].astype(BF16), preferred_element_type=F32),
                            sm_sc, sl_sc, sacc_sc)
            sign = jnp.where((s_row & 1) == 0, 1.0, -lam_ref[0:1, 0:1])
            z = sacc_sc[...] / sl_sc[...] * sign
            out_row = lax.broadcasted_iota(jnp.int32, (so_ref.shape[1], ATT_V_DIM), 0)
            for h in range(N_ATT_HEADS):
                att = jnp.zeros(out_row.shape, F32)
                for t in range(MAX_NEW_TOKENS):
                    r0 = t * ROWS_PER_TOKEN + h * 2
                    att = jnp.where(out_row == t, jnp.broadcast_to(z[r0:r0 + 1] + z[r0 + 1:r0 + 2], att.shape),
                                    att)
                sl = slice(h * ATT_V_DIM, (h + 1) * ATT_V_DIM)
                so_ref[seq, :, sl] = _subln(att, g_ref[...], beta_ref[:, sl])

    ones_rows = jnp.ones((acc_sc.shape[2] - e, blk), BF16)
    sub = lax.broadcasted_iota(jnp.int32, qt_ref.shape[1:], 0)
    qst = []
    for h in range(n_heads):
        qt = qt_ref[h]
        zero = jnp.zeros_like(qt)
        qst.append(jnp.concatenate([jnp.where(sub < DIFF_HEAD_DIM, qt, zero),
                                    jnp.where(sub >= DIFF_HEAD_DIM, qt, zero)], axis=1))
    m_sc[...] = jnp.full(m_sc.shape, MASKED, F32)
    acc_sc[0] = jnp.zeros(acc_sc.shape[1:], F32)

    def step(j, n_blk, biased, stale_max, src, with_sample):
        rows = pl.ds(pl.multiple_of(j * blk, blk), n_blk * blk)
        dst = 1 - src
        if with_sample:
            chunk_state = sample_fetch()

        def logits(h):
            st = jnp.dot(k_ref[h, rows, :], qst[h], preferred_element_type=F32)
            if biased:
                bias = bias_ref[h, (2 - n_blk) * blk:, :]
                st = st + jnp.concatenate([bias, bias], axis=1)
            return st

        def weighted_values(h, pb):
            pv = None
            for i in range(n_blk):
                vt_aug = jnp.concatenate([vt_ref[h, j + i], ones_rows], axis=0)
                part = jnp.dot(vt_aug, pb[i * blk:(i + 1) * blk], preferred_element_type=F32)
                pv = part if pv is None else pv + part
            return pv

        def exact_update(h, st):
            m_prev = m_sc[h]
            m_new = jnp.maximum(m_prev, jnp.max(st, axis=0, keepdims=True))
            alpha = jnp.exp2(m_prev - m_new)
            pv = weighted_values(h, jnp.exp2(st - m_new).astype(BF16))
            acc_sc[dst, h] = alpha * acc_sc[src, h] + pv
            m_sc[h] = m_new

        def stale_update(h, st):
            m_ref = m_sc[h]
            pv = weighted_values(h, jnp.exp2(st - m_ref).astype(BF16))
            acc_sc[dst, h] = acc_sc[src, h] + pv
            return jnp.max(st, axis=0, keepdims=True) - m_ref

        ahead = 2
        if with_sample:
            chunk_logits = sample_logits(*chunk_state)
        pending = [logits(h) for h in range(min(ahead, n_heads))]
        excess = None
        for h in range(n_heads):
            if h + ahead < n_heads:
                pending.append(logits(h + ahead))
            if stale_max:
                over = stale_update(h, pending[h])
                excess = over if excess is None else jnp.maximum(excess, over)
            else:
                exact_update(h, pending[h])
        if with_sample:
            sample_finish(*sample_update(*chunk_state, *chunk_logits))
        if stale_max:
            @pl.when(jnp.max(excess) > STALE_MAX_MARGIN)
            def _():
                for h in range(n_heads):
                    exact_update(h, logits(h))

    odd = (qi + 1) & 1
    n_pairs = jnp.maximum(((qi + 1) >> 1) - 1, 0)
    single = jnp.logical_and(qi > 0, odd == 1)

    def key_blocks(with_sample):
        @pl.when(qi == 0)
        def _():
            step(0, 1, True, False, 0, with_sample)

        @pl.when(qi > 0)
        def _():
            step(qi - 1, 2, True, False, 0, with_sample)

        def far_pair(i, carry):
            step(odd + 2 * i, 2, False, True, (i + 1) & 1, with_sample)
            return carry

        lax.fori_loop(0, n_pairs, far_pair, 0)

        @pl.when(single)
        def _():
            step(0, 1, False, True, (n_pairs + 1) & 1, with_sample)

    chunks_left = cnt_ref[0] < n_chunks

    @pl.when(chunks_left)
    def _():
        key_blocks(True)

    @pl.when(jnp.logical_not(chunks_left))
    def _():
        key_blocks(False)

    res = (n_pairs + 1 + single.astype(jnp.int32)) & 1
    for h in range(n_heads):
        ot = acc_sc[res, h, 0:e] / acc_sc[res, h, e:e + 1]
        att = (ot[:, :blk] - lam_ref[0:1, 0:1] * ot[:, blk:]).T
        sl = slice(h * ATT_V_DIM, (h + 1) * ATT_V_DIM)
        o_ref[:, sl] = _subln(att, g_ref[...], beta_ref[:, sl]).astype(o_ref.dtype)

    @pl.when(last_grid_step)
    def _():
        def drain(i, carry):
            chunk_state = sample_fetch()
            sample_finish(*sample_update(*chunk_state, *sample_logits(*chunk_state)))
            return carry

        lax.fori_loop(0, jnp.maximum(n_chunks - cnt_ref[0], 0), drain, 0)
        n = cnt_ref[0]
        for d in range(lookahead):
            for cp in _sample_page_copies(*ring, jnp.minimum(n + d, last_chunk), lax.rem(n + d, n_slots),
                                          chunks_per_seq):
                cp.wait()


def _attention(qt, k, vt, bias, lam, subln_g, beta_att, page_table, q_rows, pool_k, pool_v, k_new, v_new,
               s_bias):
    b, h, t, e = k.shape
    blk = bias.shape[-1]
    n_blk = t // blk
    hs = ATT_HEADS_PER_STEP
    assert hs == h, "the sample epilogue reads every head's beta from the prompt block"
    n_seq, n_pages = page_table.shape
    rows = q_rows.shape[1]
    page_rows = pool_k.shape[1]
    npg = PAGES_PER_STEP
    assert n_pages % npg == 0
    once = pl.Buffered(1)
    whole = lambda a: pl.BlockSpec(a.shape, lambda i, j, n, pt: (0,) * a.ndim, pipeline_mode=once)
    sample_out = jax.ShapeDtypeStruct((n_seq, SAMPLE_ROWS, h * e), F32)
    grid_spec = pltpu.PrefetchScalarGridSpec(
        num_scalar_prefetch=1,
        grid=(b, h // hs, n_blk),
        in_specs=[pl.BlockSpec((None, hs, None, e, blk), lambda i, j, n, pt: (i, j, n, 0, 0)),
                  pl.BlockSpec((None, hs, t, e), lambda i, j, n, pt: (i, j, 0, 0), pipeline_mode=once),
                  pl.BlockSpec((None, hs, n_blk, e, blk), lambda i, j, n, pt: (i, j, 0, 0, 0), pipeline_mode=once),
                  pl.BlockSpec((hs, 2 * blk, blk), lambda i, j, n, pt: (j, 0, 0), pipeline_mode=once),
                  whole(lam), whole(subln_g),
                  pl.BlockSpec((1, hs * e), lambda i, j, n, pt: (0, j)),
                  whole(q_rows),
                  pl.BlockSpec(memory_space=pl.ANY), pl.BlockSpec(memory_space=pl.ANY),
                  whole(k_new), whole(v_new), whole(s_bias)],
        out_specs=(pl.BlockSpec((None, blk, hs * e), lambda i, j, n, pt: (i, n, j)),
                   pl.BlockSpec(sample_out.shape, lambda i, j, n, pt: (0, 0, 0))),
        scratch_shapes=[pltpu.VMEM((hs, 1, 2 * blk), F32),
                        pltpu.VMEM((2, hs, e + BF16_ROWS, 2 * blk), F32),
                        pltpu.VMEM((SAMPLE_RING_SLOTS, npg, page_rows, e), F32),
                        pltpu.VMEM((SAMPLE_RING_SLOTS, npg, page_rows, e), F32),
                        pltpu.SemaphoreType.DMA((2, SAMPLE_RING_SLOTS)),
                        pltpu.VMEM((rows, LANES), F32), pltpu.VMEM((rows, LANES), F32),
                        pltpu.VMEM((rows, e), F32),
                        pltpu.SMEM((1,), jnp.int32)],
    )
    est = (hs * (2 * t * e * 2) + hs * 2 * blk * blk * 4 + 4 * hs * 2 * blk * 2 * blk * 4
           + SAMPLE_RING_SLOTS * 2 * npg * page_rows * e * 4 + 2 * k_new.size * 4 + 8 * rows * npg * page_rows * 4)
    return pl.pallas_call(
        _attn_kernel,
        grid_spec=grid_spec,
        out_shape=(jax.ShapeDtypeStruct((b, t, h * e), BF16), sample_out),
        compiler_params=pltpu.CompilerParams(dimension_semantics=("arbitrary", "arbitrary", "arbitrary"),
                                             vmem_limit_bytes=_vmem_limit(est)),
        name="attention",
    )(page_table, qt, k, vt, bias, lam, subln_g, beta_att, q_rows, pool_k, pool_v, k_new, v_new, s_bias)


def _mem_kv_kernel(mem_ref, wk_ref, wv_ref, k_ref, v_ref):
    m = mem_ref[...].astype(BF16)
    k_ref[...] = jnp.dot(m, wk_ref[...], preferred_element_type=F32)
    v_ref[...] = jnp.dot(m, wv_ref[...], preferred_element_type=F32)


def _mem_kv(mem, w_xk, w_xv):
    b, n, d = mem.shape
    blk = pl.BlockSpec((None, n, d), lambda i: (i, 0, 0))
    w = pl.BlockSpec((d, d), lambda i: (0, 0))
    out = jax.ShapeDtypeStruct((b, n, d), F32)
    return pl.pallas_call(
        _mem_kv_kernel,
        grid=(b,),
        in_specs=[blk, w, w],
        out_specs=(blk, blk),
        out_shape=(out, out),
        compiler_params=pltpu.CompilerParams(dimension_semantics=("parallel",),
                                             vmem_limit_bytes=_vmem_limit(4 * d * d * 2 + 6 * n * d * 4)),
        name="mem_kv",
    )(mem, w_xk, w_xv)


def _mix_out_kernel(att_ref, conv_ref, h_ref, wo_ref, g_post_ref, g_x_ref, wq_ref, h1_ref, qx_ref):
    half = att_ref.shape[1]
    mo = (jnp.dot(att_ref[...].astype(BF16), wo_ref[0:half, :], preferred_element_type=F32)
          + jnp.dot(conv_ref[...].astype(BF16), wo_ref[half:, :], preferred_element_type=F32))
    h1 = h_ref[...] + _rms(mo, g_post_ref[...])
    h1_ref[...] = h1
    xn = _rms(h1, g_x_ref[...]).astype(BF16)
    x_scale = (wq_ref.shape[1] // N_X_HEADS) ** -0.5
    qx_ref[...] = (jnp.dot(xn, wq_ref[...], preferred_element_type=F32) * x_scale).astype(BF16)


def _mix_out(att, conv, h, w_out, g_post, g_x, w_xq):
    m, d = h.shape
    tm = min(MIX_ROW_TILE, m)
    row = lambda w: pl.BlockSpec((tm, w), lambda i: (i, 0))
    const = lambda a: pl.BlockSpec(a.shape, lambda i: (0, 0))
    est = 2 * 2 * d * d * 2 + 2 * tm * (2 * d * 4 + d * 2 + att.shape[1] * 6) + 4 * tm * d * 4
    return pl.pallas_call(
        _mix_out_kernel,
        grid=(m // tm,),
        in_specs=[row(att.shape[1]), row(conv.shape[1]), row(d), const(w_out), const(g_post), const(g_x),
                  const(w_xq)],
        out_specs=(row(d), row(d)),
        out_shape=(jax.ShapeDtypeStruct((m, d), F32), jax.ShapeDtypeStruct((m, d), BF16)),
        compiler_params=pltpu.CompilerParams(dimension_semantics=("parallel",),
                                             vmem_limit_bytes=_vmem_limit(est)),
        name="mix_out",
    )(att, conv, h, w_out, g_post, g_x, w_xq)


def _xattn_kernel(q_ref, mk_ref, mv_ref, o_ref):
    d = q_ref.shape[1]
    hd = d // N_X_HEADS
    by_piece = mk_ref.shape[1] != d
    width = LANES if by_piece else hd
    n_piece = hd // width
    stride = n_piece * N_X_HEADS
    n_tok = mk_ref.shape[0] // stride if by_piece else mk_ref.shape[0]

    def piece(ref, h, i):
        if by_piece:
            return ref[pl.ds(i * N_X_HEADS + h, n_tok, stride=stride), :].astype(BF16)
        return ref[:, h * hd:(h + 1) * hd].astype(BF16)

    for h in range(N_X_HEADS):
        s = None
        for i in range(n_piece):
            cols = slice(h * hd + i * width, h * hd + (i + 1) * width)
            part = _nt_dot(q_ref[:, cols], piece(mk_ref, h, i))
            s = part if s is None else s + part
        p = jnp.exp(s - jnp.max(s, axis=1, keepdims=True))
        l = jnp.sum(p, axis=1, keepdims=True)
        pb = p.astype(BF16)
        for i in range(n_piece):
            cols = slice(h * hd + i * width, h * hd + (i + 1) * width)
            o = jnp.dot(pb, piece(mv_ref, h, i), preferred_element_type=F32)
            o_ref[:, cols] = (o / l).astype(o_ref.dtype)


def _xattn(qx, mem_k, mem_v):
    b, t, d = qx.shape
    n = mem_k.shape[1]
    tm = min(XATTN_ROW_TILE, t)
    row = pl.BlockSpec((None, tm, d), lambda i, j: (i, j, 0))
    mem = pl.BlockSpec((None,) + mem_k.shape[1:], lambda i, j: (i, 0, 0))
    return pl.pallas_call(
        _xattn_kernel,
        grid=(b, t // tm),
        in_specs=[row, mem, mem],
        out_specs=row,
        out_shape=jax.ShapeDtypeStruct((b, t, d), BF16),
        compiler_params=pltpu.CompilerParams(dimension_semantics=("parallel", "parallel"),
                                             vmem_limit_bytes=_vmem_limit(4 * n * d * 4 + 8 * tm * d * 4)),
        name="xattn",
    )(qx, mem_k, mem_v)


def _ffn_kernel(o_ref, h1_ref, wxo_ref, g_xpost_ref, g_pre_ref, wg_ref, wu_ref, wd_ref, g_post_ref, y_ref):
    h2 = h1_ref[...] + _rms(jnp.dot(o_ref[...], wxo_ref[...], preferred_element_type=F32), g_xpost_ref[...])
    xf = _rms(h2, g_pre_ref[...]).astype(BF16)
    d_ff = wg_ref.shape[1]
    f = jnp.zeros(h2.shape, F32)
    for c0 in range(0, d_ff, FFN_CHUNK):
        sl = slice(c0, c0 + FFN_CHUNK)
        g = jnp.dot(xf, wg_ref[:, sl], preferred_element_type=F32)
        u = jnp.dot(xf, wu_ref[:, sl], preferred_element_type=F32)
        a = (g * _sigmoid(g) * u).astype(BF16)
        f = f + jnp.dot(a, wd_ref[sl, :], preferred_element_type=F32)
    y_ref[...] = h2 + _rms(f, g_post_ref[...])


def _ffn(o, h1, w_xo, g_xpost, g_pre, w_gate, w_up, w_down, g_post):
    m, d = h1.shape
    d_ff = w_gate.shape[1]
    assert d_ff % FFN_CHUNK == 0
    tm = min(ROW_TILE, m)
    row = pl.BlockSpec((tm, d), lambda i: (i, 0))
    const = lambda a: pl.BlockSpec(a.shape, lambda i: (0, 0), pipeline_mode=pl.Buffered(1))
    est = (d * d + 3 * d * d_ff) * 2 + 2 * tm * d * (2 + 4 + 4) + 6 * tm * d * 4
    return pl.pallas_call(
        _ffn_kernel,
        grid=(m // tm,),
        in_specs=[row, row, const(w_xo), const(g_xpost), const(g_pre), const(w_gate), const(w_up),
                  const(w_down), const(g_post)],
        out_specs=row,
        out_shape=jax.ShapeDtypeStruct((m, d), F32),
        compiler_params=pltpu.CompilerParams(dimension_semantics=("parallel",),
                                             vmem_limit_bytes=_vmem_limit(est)),
        name="ffn",
    )(o, h1, w_xo, g_xpost, g_pre, w_gate, w_up, w_down, g_post)


def kernel(x_prompt, x_sample, mem_prompt, cache_k, cache_v, state_conv, cache_mem_k, cache_mem_v, page_table, rel_bias_table, norm_mix_pre, norm_mix_post, w_in, lambda_q1, lambda_k1, lambda_q2, lambda_k2, subln_g, dw_w, dw_b, conv_ln_g, conv_ln_b, beta_att, beta_conv, w_out, norm_x_pre, norm_x_post, w_xq, w_xk, w_xv, w_xo, norm_ffn_pre, norm_ffn_post, w_gate, w_up, w_down):
    assert w_in.shape[0] == 1, "single-layer trunk"
    bp, tp, d = x_prompt.shape
    bs, ts, _ = x_sample.shape
    assert ts <= MAX_NEW_TOKENS and tp >= CONV_K - 1
    n_mem = mem_prompt.shape[1]
    vec = lambda a: a[0].reshape(1, -1)
    wb = lambda a: a[0].astype(BF16)
    g_mix_pre, g_mix_post = vec(norm_mix_pre), vec(norm_mix_post)
    g_x_pre, g_x_post = vec(norm_x_pre), vec(norm_x_post)
    g_ffn_pre, g_ffn_post = vec(norm_ffn_pre), vec(norm_ffn_post)
    sub_g, b_att, b_conv = vec(subln_g), vec(beta_att), vec(beta_conv)
    c_b, c_g, c_bt = vec(dw_b), vec(conv_ln_g), vec(conv_ln_b)
    w_in_b, w_out_b, w_xq_b, w_xk_b, w_xv_b, w_xo_b = (wb(w) for w in (w_in, w_out, w_xq, w_xk, w_xv, w_xo))
    w_gate_b, w_up_b, w_down_b = wb(w_gate), wb(w_up), wb(w_down)
    dw = dw_w[0]

    blk = min(ATT_BLOCK, tp)
    lam_vecs = jnp.stack([lambda_q1[0], lambda_k1[0], lambda_q2[0], lambda_k2[0]])
    p_bias, s_bias, lam = _bias_tiles(rel_bias_table, lam_vecs, blk, ts)

    def tail(att, conv, h, mem_k, mem_v):
        b, t, _ = h.shape
        flat = lambda a: a.reshape(b * t, a.shape[-1])
        h1, qx = _mix_out(flat(att), flat(conv), flat(h), w_out_b, g_mix_post, g_x_pre, w_xq_b)
        o = _xattn(qx.reshape(b, t, d), mem_k, mem_v)
        y = _ffn(flat(o), h1, w_xo_b, g_x_post, g_ffn_pre, w_gate_b, w_up_b, w_down_b, g_ffn_post)
        return y.reshape(b, t, d)

    conv0 = jnp.zeros((bp, CONV_HALO, CONV_WIDTH), F32)
    qt_p, kb_p, vt_p, kf_p, vf_p, glu_p, conv_p = _in_proj(x_prompt, g_mix_pre, w_in_b, blk,
                                                          conv=(conv0, dw, c_b, c_g, c_bt, b_conv))
    x_s = jnp.pad(x_sample, ((0, 0), (0, SAMPLE_ROWS - ts), (0, 0)))
    rows_s = bs * SAMPLE_ROWS
    qt_s, _, _, kf_s, vf_s, glu_s = _in_proj(x_s.reshape(1, rows_s, d), g_mix_pre, w_in_b, min(blk, rows_s))
    new_rows = ts * N_ATT_HEADS
    kf_s = kf_s.reshape(bs, SAMPLE_ROWS * N_ATT_HEADS, ATT_V_DIM)[:, :new_rows]
    vf_s = vf_s.reshape(bs, SAMPLE_ROWS * N_ATT_HEADS, ATT_V_DIM)[:, :new_rows]
    glu_s = glu_s.reshape(bs, SAMPLE_ROWS, CONV_WIDTH)[:, :ts]
    q_s = jnp.transpose(qt_s[0], (0, 1, 3, 2))
    q_th = jnp.transpose(q_s.reshape(N_ATT_HEADS, bs, SAMPLE_ROWS, ATT_V_DIM), (1, 2, 0, 3))
    q_th = jnp.pad(q_th[:, :ts], ((0, 0), (0, MAX_NEW_TOKENS - ts), (0, 0), (0, 0)))
    q_rows = jnp.repeat(q_th.reshape(bs, MAX_NEW_TOKENS * N_ATT_HEADS, ATT_V_DIM), 2, axis=1)
    pad_page = lambda a: jnp.pad(a, ((0, 0), (0, PAGE_SIZE - new_rows), (0, 0)))
    n_phys = cache_k.shape[1]
    pool = lambda c: c.reshape(n_phys, PAGE_SIZE * N_ATT_HEADS, ATT_V_DIM)

    att_p, att_s = _attention(qt_p, kb_p, vt_p, p_bias, lam, sub_g, b_att, page_table, q_rows,
                              pool(cache_k), pool(cache_v), pad_page(kf_s), pad_page(vf_s), s_bias)

    mk_p, mv_p = _mem_kv(mem_prompt, w_xk_b, w_xv_b)
    y_p = tail(att_p, conv_p, x_prompt, mk_p, mv_p)

    xp_s = jnp.concatenate([state_conv[0], glu_s], axis=1)
    conv_s = _conv_sample(jnp.transpose(xp_s, (1, 0, 2)), ts, dw, c_b, c_g, c_bt, b_conv)
    conv_s = jnp.pad(jnp.transpose(conv_s, (1, 0, 2)), ((0, 0), (0, SAMPLE_ROWS - ts), (0, 0)))
    hd_x = d // N_X_HEADS
    mem_s = lambda c: jnp.transpose(c[0].reshape(bs, n_mem, N_X_HEADS, hd_x // LANES, LANES),
                                    (0, 1, 3, 2, 4)).reshape(bs, n_mem * d // LANES, LANES)
    y_s = tail(att_s, conv_s, x_s, mem_s(cache_mem_k), mem_s(cache_mem_v))

    heads = lambda a: a.reshape(1, a.shape[0], a.shape[1] // N_ATT_HEADS, N_ATT_HEADS, ATT_V_DIM)
    mem_heads = lambda a: a.reshape(1, bp, n_mem, N_X_HEADS, d // N_X_HEADS)
    return (y_p, y_s[:, :ts],
            heads(kf_p), heads(vf_p), glu_p[None, :, tp - (CONV_K - 1):],
            mem_heads(mk_p), mem_heads(mv_p),
            heads(kf_s), heads(vf_s), xp_s[None, :, ts:])
```

```python
import functools
import math

import numpy as np
import jax
import jax.numpy as jnp
from jax import lax
from jax.experimental import pallas as pl
from jax.experimental.pallas import tpu as pltpu

F32 = jnp.float32
BF16 = jnp.bfloat16

DIFF_HEAD_DIM = 64
DIFF_HEAD_SHIFT = DIFF_HEAD_DIM.bit_length() - 1
ATT_V_DIM = 2 * DIFF_HEAD_DIM
N_ATT_HEADS = 4
ATT_WIDTH = N_ATT_HEADS * ATT_V_DIM
CONV_WIDTH = 512
CONV_K = 31
N_BUCKETS = 32
MAX_DISTANCE = 128
N_X_HEADS = 4
PAGE_SIZE = 128
RMS_EPS = 1e-6
LN_EPS = 1e-5
ATT_SCALE = DIFF_HEAD_DIM ** -0.5
LOG2_E = math.log2(math.e)
LAM_INIT = 0.8 - 0.6 * math.exp(-0.3 * 0)
MASKED = -1e30
STALE_MAX_MARGIN = 30.0

V7X_VMEM_BYTES = 64 * 1024 * 1024
LANES = 128
SUBLANES = 8
BF16_ROWS = 16

ROW_TILE = 512
IN_ROW_TILE = 1024
MIX_ROW_TILE = 1024
XATTN_ROW_TILE = 1024
ATT_BLOCK = 256
ATT_HEADS_PER_STEP = 4
CONV_CHUNK = 64
CONV_HALO = 32
PAGES_PER_STEP = 8
SAMPLE_RING_SLOTS = 3
FFN_CHUNK = 256
SAMPLE_ROWS = 8
MAX_NEW_TOKENS = 4
ROWS_PER_TOKEN = 2 * N_ATT_HEADS
SAMPLE_Q_ROWS = MAX_NEW_TOKENS * ROWS_PER_TOKEN


VMEM_TEMPORARIES_FACTOR = 2
VMEM_SMALL_CALL_BYTES = 16 * 1024 * 1024
VMEM_LEFT_FREE_BYTES = 8 * 1024 * 1024


def _vmem_limit(block_bytes):
    wanted = max(VMEM_TEMPORARIES_FACTOR * block_bytes, VMEM_SMALL_CALL_BYTES)
    return int(min(wanted, V7X_VMEM_BYTES - VMEM_LEFT_FREE_BYTES))


def _rms(x, g):
    return x * lax.rsqrt(jnp.mean(x * x, axis=-1, keepdims=True) + RMS_EPS) * g


def _sigmoid(x):
    return 1.0 / (1.0 + jnp.exp(-x))


def _nt_dot(a, b):
    return lax.dot_general(a, b, (((1,), (1,)), ((), ())), preferred_element_type=F32)


def _bucket_np(n):
    n = np.maximum(n, 0)
    max_exact = N_BUCKETS // 2
    nf = np.maximum(n, 1).astype(np.float32)
    large = max_exact + (np.log(nf / max_exact) / math.log(MAX_DISTANCE / max_exact)
                         * (N_BUCKETS - max_exact)).astype(np.int32)
    large = np.minimum(large, N_BUCKETS - 1)
    return np.where(n < max_exact, n, large).astype(np.int32)


def _prompt_codes(blk):
    i = np.arange(blk)[None, :]
    j = np.arange(blk)[:, None]
    prev = _bucket_np(i - j + blk)
    diag = np.where(j > i, -1, _bucket_np(i - j))
    return np.concatenate([prev, diag]).astype(np.int32)


def _sample_codes(n_new):
    r = np.arange(SAMPLE_Q_ROWS)[:, None]
    c = np.arange(PAGE_SIZE * N_ATT_HEADS)[None, :]
    t, head = r // ROWS_PER_TOKEN, (r // 2) % N_ATT_HEADS
    tok, key_head = c // N_ATT_HEADS, c % N_ATT_HEADS
    own = key_head == head
    far = np.where(own, N_BUCKETS - 1, -1)
    last = np.where(own, _bucket_np(t + PAGE_SIZE - tok), -1)
    new = np.where(own & (tok <= t) & (tok < n_new) & (c < PAGE_SIZE), _bucket_np(t - tok), -1)
    return np.stack([far, last, new]).astype(np.int32)


def _bias_kernel(tab_ref, lam_in_ref, pcode_ref, scode_ref, pbias_ref, sbias_ref, lam_ref):
    far = N_BUCKETS - 1

    def lookup(code, h):
        out = jnp.zeros(code.shape, F32)
        for b in range(far):
            out = jnp.where(code == b, (tab_ref[b, h] - tab_ref[far, h]) * LOG2_E, out)
        return jnp.where(code < 0, MASKED, out)

    for h in range(N_ATT_HEADS):
        pbias_ref[h] = lookup(pcode_ref[...], h)
    row_head = (lax.broadcasted_iota(jnp.int32, scode_ref.shape[1:], 0) >> 1) & (N_ATT_HEADS - 1)
    for i in range(scode_ref.shape[0]):
        code = scode_ref[i]
        out = jnp.zeros(code.shape, F32)
        for h in range(N_ATT_HEADS):
            out = jnp.where(row_head == h, lookup(code, h), out)
        sbias_ref[i] = out
    lv = lam_in_ref[...]
    d1 = jnp.sum(lv[0:1] * lv[1:2], axis=1, keepdims=True)
    d2 = jnp.sum(lv[2:3] * lv[3:4], axis=1, keepdims=True)
    lam = jnp.exp(d1) - jnp.exp(d2) + LAM_INIT
    lam_ref[...] = jnp.broadcast_to(lam, lam_ref.shape)


def _bias_tiles(table, lam_vecs, blk, n_new):
    pcode = jnp.asarray(_prompt_codes(blk))
    scode = jnp.asarray(_sample_codes(n_new))
    vm = pl.BlockSpec(memory_space=pltpu.VMEM)
    return pl.pallas_call(
        _bias_kernel,
        out_shape=(jax.ShapeDtypeStruct((N_ATT_HEADS, 2 * blk, blk), F32),
                   jax.ShapeDtypeStruct(scode.shape, F32),
                   jax.ShapeDtypeStruct((SUBLANES, LANES), F32)),
        in_specs=[pl.BlockSpec(memory_space=pltpu.SMEM), vm, vm, vm],
        out_specs=(vm, vm, vm),
        name="bias_tiles",
    )(table, lam_vecs, pcode, scode)


def _in_proj_kernel(x_ref, g_ref, w_ref, *refs, with_conv):
    if with_conv:
        conv_in, refs = refs[:6], refs[6:]
        qt_ref, kb_ref, vt_ref, kf_ref, vf_ref, glu_ref, conv_ref, buf, ybuf = refs
    else:
        qt_ref, kb_ref, vt_ref, kf_ref, vf_ref, glu_ref = refs
    xn = _rms(x_ref[...], g_ref[...]).astype(BF16)

    def cols(c):
        return jnp.dot(xn, w_ref[:, c * ATT_WIDTH:(c + 1) * ATT_WIDTH], preferred_element_type=F32)

    tm = xn.shape[0]
    blk = qt_ref.shape[-1]
    glu = cols(3) * _sigmoid(cols(4))
    glu_ref[...] = glu
    if with_conv:
        _conv_tile(glu, *conv_in, conv_ref, buf, ybuf, pl.program_id(1) == 0)
    q = cols(0) * (ATT_SCALE * LOG2_E)
    k = cols(1)
    v = cols(2)
    kb = k.astype(BF16)
    for h in range(N_ATT_HEADS):
        sl = slice(h * ATT_V_DIM, (h + 1) * ATT_V_DIM)
        kb_ref[h] = kb[:, sl]
        for c in range(tm // blk):
            rows = slice(c * blk, (c + 1) * blk)
            qt_ref[h, c] = q[rows, sl].T.astype(BF16)
            vt_ref[h, c] = v[rows, sl].T.astype(BF16)
        kf_ref[pl.ds(h, tm, stride=N_ATT_HEADS), :] = k[:, sl]
        vf_ref[pl.ds(h, tm, stride=N_ATT_HEADS), :] = v[:, sl]


def _in_proj(x, gain, w_in, blk, conv=None):
    b, t, d = x.shape
    tm = min(IN_ROW_TILE, t)
    n_cols = w_in.shape[1]
    assert tm % blk == 0
    c = CONV_WIDTH
    conv = () if conv is None else tuple(conv)
    vec = pl.BlockSpec((1, c), lambda i, j: (0, 0))
    conv_specs = [pl.BlockSpec((None, CONV_HALO, c), lambda i, j: (i, 0, 0)),
                  pl.BlockSpec((CONV_K, c), lambda i, j: (0, 0)), vec, vec, vec, vec] if conv else []
    conv_out = [pl.BlockSpec((None, tm, c), lambda i, j: (i, j, 0))] if conv else []
    conv_shape = [jax.ShapeDtypeStruct((b, t, c), BF16)] if conv else []
    conv_scratch = [pltpu.VMEM((c // LANES, tm + CONV_HALO, LANES), F32),
                    pltpu.VMEM((c // LANES, tm, LANES), F32)] if conv else []
    row = lambda w: pl.BlockSpec((None, tm, w), lambda i, j: (i, j, 0))
    heads = pl.BlockSpec((None, N_ATT_HEADS, tm, ATT_V_DIM), lambda i, j: (i, 0, j, 0))
    hm = jax.ShapeDtypeStruct((b, N_ATT_HEADS, t, ATT_V_DIM), BF16)
    heads_t = pl.BlockSpec((None, N_ATT_HEADS, tm // blk, ATT_V_DIM, blk), lambda i, j: (i, 0, j, 0, 0))
    hm_t = jax.ShapeDtypeStruct((b, N_ATT_HEADS, t // blk, ATT_V_DIM, blk), BF16)
    flat = jax.ShapeDtypeStruct((b, t * N_ATT_HEADS, ATT_V_DIM), F32)
    tok_head = pl.BlockSpec((None, tm * N_ATT_HEADS, ATT_V_DIM), lambda i, j: (i, j, 0))
    est = 2 * d * n_cols * 2 + 2 * tm * (d * 4 + 3 * ATT_WIDTH * 2 + 3 * ATT_WIDTH * 4) + 6 * tm * ATT_WIDTH * 4
    return pl.pallas_call(
        functools.partial(_in_proj_kernel, with_conv=bool(conv)),
        grid=(b, t // tm),
        in_specs=[row(d),
                  pl.BlockSpec((1, d), lambda i, j: (0, 0)),
                  pl.BlockSpec((d, n_cols), lambda i, j: (0, 0))] + conv_specs,
        out_specs=tuple([heads_t, heads, heads_t, tok_head, tok_head, row(CONV_WIDTH)] + conv_out),
        out_shape=tuple([hm_t, hm, hm_t, flat, flat, jax.ShapeDtypeStruct((b, t, CONV_WIDTH), F32)]
                        + conv_shape),
        scratch_shapes=conv_scratch,
        compiler_params=pltpu.CompilerParams(dimension_semantics=("parallel", "arbitrary"),
                                             vmem_limit_bytes=_vmem_limit(est)),
        name="in_proj",
    )(x, gain, w_in, *conv)


def _conv_post(y, b_ref, g_ref, bt_ref, beta_ref):
    y = y + b_ref[...]
    mu = jnp.mean(y, axis=-1, keepdims=True)
    yc = y - mu
    yn = yc * lax.rsqrt(jnp.mean(yc * yc, axis=-1, keepdims=True) + LN_EPS) * g_ref[...] + bt_ref[...]
    return yn * _sigmoid(yn) * beta_ref[...]


def _conv_tile(glu, prev_ref, w_ref, b_ref, g_ref, bt_ref, beta_ref, o_ref, buf, ybuf, first):
    tt = glu.shape[0]
    n_slab = buf.shape[0]
    slabs = [slice(s * LANES, (s + 1) * LANES) for s in range(n_slab)]

    @pl.when(first)
    def _():
        for s in range(n_slab):
            buf[s, 0:CONV_HALO] = prev_ref[:, slabs[s]]

    @pl.when(jnp.logical_not(first))
    def _():
        for s in range(n_slab):
            buf[s, 0:CONV_HALO] = buf[s, tt:tt + CONV_HALO]

    for s in range(n_slab):
        buf[s, CONV_HALO:CONV_HALO + tt] = glu[:, slabs[s]]

    shift = CONV_HALO - (CONV_K - 1)
    rc = min(CONV_CHUNK, tt // 2)
    for c0 in range(0, tt, 2 * rc):
        for parity in range(2):
            start = c0 + parity
            accs = []
            for s in range(n_slab):
                acc = jnp.zeros((rc, LANES), F32)
                for j in range(CONV_K):
                    x = buf[s, pl.ds(start + j + shift, rc, stride=2), :]
                    acc = acc + w_ref[j:j + 1, slabs[s]] * x
                accs.append(acc)
            y = _conv_post(jnp.concatenate(accs, axis=1), b_ref, g_ref, bt_ref, beta_ref)
            for s in range(n_slab):
                ybuf[s, pl.ds(start, rc, stride=2), :] = y[:, slabs[s]]
    for s in range(n_slab):
        o_ref[:, slabs[s]] = ybuf[s].astype(o_ref.dtype)


def _conv_sample_kernel(xp_ref, w_ref, b_ref, g_ref, bt_ref, beta_ref, o_ref):
    n_t = o_ref.shape[0]
    for t in range(n_t):
        acc = jnp.zeros(xp_ref.shape[1:], F32)
        for j in range(CONV_K):
            acc = acc + w_ref[j:j + 1, :] * xp_ref[t + j]
        o_ref[t] = _conv_post(acc, b_ref, g_ref, bt_ref, beta_ref)


def _conv_sample(xp_t, n_t, dw_w, dw_b, ln_g, ln_b, beta):
    vm = pl.BlockSpec(memory_space=pltpu.VMEM)
    return pl.pallas_call(
        _conv_sample_kernel,
        out_shape=jax.ShapeDtypeStruct((n_t,) + xp_t.shape[1:], F32),
        in_specs=[vm] * 6,
        out_specs=vm,
        name="conv_sample",
    )(xp_t, dw_w, dw_b, ln_g, ln_b, beta)


def _softmax_update(s, v_dot, m_sc, l_sc, acc_sc):
    width = s.shape[1]
    m_prev = m_sc[...]
    m_new = jnp.maximum(m_prev, jnp.max(s, axis=1, keepdims=True))
    alpha = jnp.exp2(m_prev - m_new)
    p = jnp.exp2(s - jnp.concatenate([m_new] * (width // LANES), axis=1))
    l_sc[...] = alpha * l_sc[...] + jnp.sum(p, axis=1, keepdims=True)
    reps = acc_sc.shape[1] // LANES
    acc_sc[...] = jnp.concatenate([alpha] * reps, axis=1) * acc_sc[...] + v_dot(p.astype(BF16))
    m_sc[...] = m_new


def _subln(att, g, beta):
    return _rms(att, g) * (1.0 - LAM_INIT) * beta


def _sample_page_copies(pt_ref, pool_k, pool_v, kbuf, vbuf, sem, chunk, slot, chunks_per_seq):
    seq = chunk // chunks_per_seq
    first_page = (chunk % chunks_per_seq) * PAGES_PER_STEP
    copies = []
    for i in range(PAGES_PER_STEP):
        page = pt_ref[seq, first_page + i]
        copies.append(pltpu.make_async_copy(pool_k.at[page], kbuf.at[slot, i], sem.at[0, slot]))
        copies.append(pltpu.make_async_copy(pool_v.at[page], vbuf.at[slot, i], sem.at[1, slot]))
    return copies


def _attn_kernel(pt_ref, qt_ref, k_ref, vt_ref, bias_ref, lam_ref, g_ref, beta_ref,
                 sq_ref, pool_k, pool_v, kn_ref, vn_ref, sbias_ref, o_ref, so_ref,
                 m_sc, acc_sc, kbuf, vbuf, sem, sm_sc, sl_sc, sacc_sc, cnt_ref):
    n_heads, e, blk = qt_ref.shape
    qi = pl.program_id(2)
    n_seq_s, chunks_per_seq = so_ref.shape[0], pt_ref.shape[1] // PAGES_PER_STEP
    n_chunks = n_seq_s * chunks_per_seq
    first_grid_step = jnp.logical_and(pl.program_id(0) == 0, qi == 0)
    last_grid_step = jnp.logical_and(pl.program_id(0) == pl.num_programs(0) - 1, qi == pl.num_programs(2) - 1)
    ring = (pt_ref, pool_k, pool_v, kbuf, vbuf, sem)

    n_slots = kbuf.shape[0]
    lookahead = n_slots - 1
    last_chunk = n_chunks - 1

    @pl.when(first_grid_step)
    def _():
        cnt_ref[0] = 0
        for d in range(lookahead):
            for cp in _sample_page_copies(*ring, min(d, last_chunk), d, chunks_per_seq):
                cp.start()

    s_lane = lax.broadcasted_iota(jnp.int32, sq_ref.shape[1:], 1)
    s_row = lax.broadcasted_iota(jnp.int32, sq_ref.shape[1:], 0)

    def sample_fetch():
        n = cnt_ref[0]
        slot = lax.rem(n, n_slots)
        valid = n < n_chunks
        chunk = jnp.minimum(n, last_chunk)
        seq = chunk // chunks_per_seq
        c = chunk % chunks_per_seq
        for cp in _sample_page_copies(*ring, chunk, slot, chunks_per_seq):
            cp.wait()
        ahead_slot = lax.rem(n + lookahead, n_slots)
        for cp in _sample_page_copies(*ring, jnp.minimum(n + lookahead, last_chunk), ahead_slot, chunks_per_seq):
            cp.start()
        cnt_ref[0] = n + 1
        return slot, seq, c, valid

    def sample_logits(slot, seq, c, valid):
        q = sq_ref[seq]
        qm = jnp.where((s_lane >> DIFF_HEAD_SHIFT) == (s_row & 1), q, jnp.zeros_like(q))
        gate = jnp.where(valid, 0.0, MASKED)
        far_bias = sbias_ref[0] + gate
        last_bias = jnp.where(c == chunks_per_seq - 1, sbias_ref[1] + gate, far_bias)
        s = jnp.concatenate(
            [_nt_dot(qm, kbuf[slot, i].astype(BF16)) + (last_bias if i == PAGES_PER_STEP - 1 else far_bias)
             for i in range(PAGES_PER_STEP)], axis=1)
        return qm, s

    def sample_update(slot, seq, c, valid, qm, s):
        fresh = c == 0
        m_prev = jnp.where(fresh, MASKED, sm_sc[...])
        m_new = jnp.maximum(m_prev, jnp.max(s, axis=1, keepdims=True))
        alpha = jnp.exp2(m_prev - m_new)
        p = jnp.exp2(s - jnp.concatenate([m_new] * (s.shape[1] // LANES), axis=1))
        sl_sc[...] = alpha * jnp.where(fresh, 0.0, sl_sc[...]) + jnp.sum(p, axis=1, keepdims=True)
        pb = p.astype(BF16)
        cols = kbuf.shape[2]
        pv = None
        for i in range(PAGES_PER_STEP):
            part = jnp.dot(pb[:, i * cols:(i + 1) * cols], vbuf[slot, i].astype(BF16),
                           preferred_element_type=F32)
            pv = part if pv is None else pv + part
        sacc_sc[...] = alpha * jnp.where(fresh, 0.0, sacc_sc[...]) + pv
        sm_sc[...] = m_new
        return seq, qm, jnp.logical_and(valid, c == chunks_per_seq - 1)

    def sample_finish(seq, qm, seq_done):
        @pl.when(seq_done)
        def _():
            n_new = kn_ref.shape[1]
            s_new = _nt_dot(qm, kn_ref[seq].astype(BF16)) + sbias_ref[2][:, :n_new]
            _softmax_update(s_new, lambda pn: jnp.dot(pn, vn_ref[seq].astype(BF16), preferred_element_type=F32),
                            sm_sc, sl_sc, sacc_sc)
            sign = jnp.where((s_row & 1) == 0, 1.0, -lam_ref[0:1, 0:1])
            z = sacc_sc[...] / sl_sc[...] * sign
            out_row = lax.broadcasted_iota(jnp.int32, (so_ref.shape[1], ATT_V_DIM), 0)
            for h in range(N_ATT_HEADS):
                att = jnp.zeros(out_row.shape, F32)
                for t in range(MAX_NEW_TOKENS):
                    r0 = t * ROWS_PER_TOKEN + h * 2
                    att = jnp.where(out_row == t, jnp.broadcast_to(z[r0:r0 + 1] + z[r0 + 1:r0 + 2], att.shape),
                                    att)
                sl = slice(h * ATT_V_DIM, (h + 1) * ATT_V_DIM)
                so_ref[seq, :, sl] = _subln(att, g_ref[...], beta_ref[:, sl])

    ones_rows = jnp.ones((acc_sc.shape[2] - e, blk), BF16)
    sub = lax.broadcasted_iota(jnp.int32, qt_ref.shape[1:], 0)
    qst = []
    for h in range(n_heads):
        qt = qt_ref[h]
        zero = jnp.zeros_like(qt)
        qst.append(jnp.concatenate([jnp.where(sub < DIFF_HEAD_DIM, qt, zero),
                                    jnp.where(sub >= DIFF_HEAD_DIM, qt, zero)], axis=1))
    m_sc[...] = jnp.full(m_sc.shape, MASKED, F32)
    acc_sc[0] = jnp.zeros(acc_sc.shape[1:], F32)

    def step(j, n_blk, biased, stale_max, src, with_sample):
        rows = pl.ds(pl.multiple_of(j * blk, blk), n_blk * blk)
        dst = 1 - src
        if with_sample:
            chunk_state = sample_fetch()

        def logits(h):
            st = jnp.dot(k_ref[h, rows, :], qst[h], preferred_element_type=F32)
            if biased:
                bias = bias_ref[h, (2 - n_blk) * blk:, :]
                st = st + jnp.concatenate([bias, bias], axis=1)
            return st

        def weighted_values(h, pb):
            pv = None
            for i in range(n_blk):
                vt_aug = jnp.concatenate([vt_ref[h, j + i], ones_rows], axis=0)
                part = jnp.dot(vt_aug, pb[i * blk:(i + 1) * blk], preferred_element_type=F32)
                pv = part if pv is None else pv + part
            return pv

        def exact_update(h, st):
            m_prev = m_sc[h]
            m_new = jnp.maximum(m_prev, jnp.max(st, axis=0, keepdims=True))
            alpha = jnp.exp2(m_prev - m_new)
            pv = weighted_values(h, jnp.exp2(st - m_new).astype(BF16))
            acc_sc[dst, h] = alpha * acc_sc[src, h] + pv
            m_sc[h] = m_new

        def stale_update(h, st):
            m_ref = m_sc[h]
            pv = weighted_values(h, jnp.exp2(st - m_ref).astype(BF16))
            acc_sc[dst, h] = acc_sc[src, h] + pv
            return jnp.max(st, axis=0, keepdims=True) - m_ref

        ahead = 2
        if with_sample:
            chunk_logits = sample_logits(*chunk_state)
        pending = [logits(h) for h in range(min(ahead, n_heads))]
        excess = None
        for h in range(n_heads):
            if h + ahead < n_heads:
                pending.append(logits(h + ahead))
            if stale_max:
                over = stale_update(h, pending[h])
                excess = over if excess is None else jnp.maximum(excess, over)
            else:
                exact_update(h, pending[h])
        if with_sample:
            sample_finish(*sample_update(*chunk_state, *chunk_logits))
        if stale_max:
            @pl.when(jnp.max(excess) > STALE_MAX_MARGIN)
            def _():
                for h in range(n_heads):
                    exact_update(h, logits(h))

    odd = (qi + 1) & 1
    n_pairs = jnp.maximum(((qi + 1) >> 1) - 1, 0)
    single = jnp.logical_and(qi > 0, odd == 1)

    def key_blocks(with_sample):
        @pl.when(qi == 0)
        def _():
            step(0, 1, True, False, 0, with_sample)

        @pl.when(qi > 0)
        def _():
            step(qi - 1, 2, True, False, 0, with_sample)

        def far_pair(i, carry):
            step(odd + 2 * i, 2, False, True, (i + 1) & 1, with_sample)
            return carry

        lax.fori_loop(0, n_pairs, far_pair, 0)

        @pl.when(single)
        def _():
            step(0, 1, False, True, (n_pairs + 1) & 1, with_sample)

    chunks_left = cnt_ref[0] < n_chunks

    @pl.when(chunks_left)
    def _():
        key_blocks(True)

    @pl.when(jnp.logical_not(chunks_left))
    def _():
        key_blocks(False)

    res = (n_pairs + 1 + single.astype(jnp.int32)) & 1
    for h in range(n_heads):
        ot = acc_sc[res, h, 0:e] / acc_sc[res, h, e:e + 1]
        att = (ot[:, :blk] - lam_ref[0:1, 0:1] * ot[:, blk:]).T
        sl = slice(h * ATT_V_DIM, (h + 1) * ATT_V_DIM)
        o_ref[:, sl] = _subln(att, g_ref[...], beta_ref[:, sl]).astype(o_ref.dtype)

    @pl.when(last_grid_step)
    def _():
        def drain(i, carry):
            chunk_state = sample_fetch()
            sample_finish(*sample_update(*chunk_state, *sample_logits(*chunk_state)))
            return carry

        lax.fori_loop(0, jnp.maximum(n_chunks - cnt_ref[0], 0), drain, 0)
        n = cnt_ref[0]
        for d in range(lookahead):
            for cp in _sample_page_copies(*ring, jnp.minimum(n + d, last_chunk), lax.rem(n + d, n_slots),
                                          chunks_per_seq):
                cp.wait()


def _attention(qt, k, vt, bias, lam, subln_g, beta_att, page_table, q_rows, pool_k, pool_v, k_new, v_new,
               s_bias):
    b, h, t, e = k.shape
    blk = bias.shape[-1]
    n_blk = t // blk
    hs = ATT_HEADS_PER_STEP
    assert hs == h, "the sample epilogue reads every head's beta from the prompt block"
    n_seq, n_pages = page_table.shape
    rows = q_rows.shape[1]
    page_rows = pool_k.shape[1]
    npg = PAGES_PER_STEP
    assert n_pages % npg == 0
    once = pl.Buffered(1)
    whole = lambda a: pl.BlockSpec(a.shape, lambda i, j, n, pt: (0,) * a.ndim, pipeline_mode=once)
    sample_out = jax.ShapeDtypeStruct((n_seq, SAMPLE_ROWS, h * e), F32)
    grid_spec = pltpu.PrefetchScalarGridSpec(
        num_scalar_prefetch=1,
        grid=(b, h // hs, n_blk),
        in_specs=[pl.BlockSpec((None, hs, None, e, blk), lambda i, j, n, pt: (i, j, n, 0, 0)),
                  pl.BlockSpec((None, hs, t, e), lambda i, j, n, pt: (i, j, 0, 0), pipeline_mode=once),
                  pl.BlockSpec((None, hs, n_blk, e, blk), lambda i, j, n, pt: (i, j, 0, 0, 0), pipeline_mode=once),
                  pl.BlockSpec((hs, 2 * blk, blk), lambda i, j, n, pt: (j, 0, 0), pipeline_mode=once),
                  whole(lam), whole(subln_g),
                  pl.BlockSpec((1, hs * e), lambda i, j, n, pt: (0, j)),
                  whole(q_rows),
                  pl.BlockSpec(memory_space=pl.ANY), pl.BlockSpec(memory_space=pl.ANY),
                  whole(k_new), whole(v_new), whole(s_bias)],
        out_specs=(pl.BlockSpec((None, blk, hs * e), lambda i, j, n, pt: (i, n, j)),
                   pl.BlockSpec(sample_out.shape, lambda i, j, n, pt: (0, 0, 0))),
        scratch_shapes=[pltpu.VMEM((hs, 1, 2 * blk), F32),
                        pltpu.VMEM((2, hs, e + BF16_ROWS, 2 * blk), F32),
                        pltpu.VMEM((SAMPLE_RING_SLOTS, npg, page_rows, e), F32),
                        pltpu.VMEM((SAMPLE_RING_SLOTS, npg, page_rows, e), F32),
                        pltpu.SemaphoreType.DMA((2, SAMPLE_RING_SLOTS)),
                        pltpu.VMEM((rows, LANES), F32), pltpu.VMEM((rows, LANES), F32),
                        pltpu.VMEM((rows, e), F32),
                        pltpu.SMEM((1,), jnp.int32)],
    )
    est = (hs * (2 * t * e * 2) + hs * 2 * blk * blk * 4 + 4 * hs * 2 * blk * 2 * blk * 4
           + SAMPLE_RING_SLOTS * 2 * npg * page_rows * e * 4 + 2 * k_new.size * 4 + 8 * rows * npg * page_rows * 4)
    return pl.pallas_call(
        _attn_kernel,
        grid_spec=grid_spec,
        out_shape=(jax.ShapeDtypeStruct((b, t, h * e), BF16), sample_out),
        compiler_params=pltpu.CompilerParams(dimension_semantics=("arbitrary", "arbitrary", "arbitrary"),
                                             vmem_limit_bytes=_vmem_limit(est)),
        name="attention",
    )(page_table, qt, k, vt, bias, lam, subln_g, beta_att, q_rows, pool_k, pool_v, k_new, v_new, s_bias)


def _mem_kv_kernel(mem_ref, wk_ref, wv_ref, k_ref, v_ref):
    m = mem_ref[...].astype(BF16)
    k_ref[...] = jnp.dot(m, wk_ref[...], preferred_element_type=F32)
    v_ref[...] = jnp.dot(m, wv_ref[...], preferred_element_type=F32)


def _mem_kv(mem, w_xk, w_xv):
    b, n, d = mem.shape
    blk = pl.BlockSpec((None, n, d), lambda i: (i, 0, 0))
    w = pl.BlockSpec((d, d), lambda i: (0, 0))
    out = jax.ShapeDtypeStruct((b, n, d), F32)
    return pl.pallas_call(
        _mem_kv_kernel,
        grid=(b,),
        in_specs=[blk, w, w],
        out_specs=(blk, blk),
        out_shape=(out, out),
        compiler_params=pltpu.CompilerParams(dimension_semantics=("parallel",),
                                             vmem_limit_bytes=_vmem_limit(4 * d * d * 2 + 6 * n * d * 4)),
        name="mem_kv",
    )(mem, w_xk, w_xv)


def _mix_out_kernel(att_ref, conv_ref, h_ref, wo_ref, g_post_ref, g_x_ref, wq_ref, h1_ref, qx_ref):
    half = att_ref.shape[1]
    mo = (jnp.dot(att_ref[...].astype(BF16), wo_ref[0:half, :], preferred_element_type=F32)
          + jnp.dot(conv_ref[...].astype(BF16), wo_ref[half:, :], preferred_element_type=F32))
    h1 = h_ref[...] + _rms(mo, g_post_ref[...])
    h1_ref[...] = h1
    xn = _rms(h1, g_x_ref[...]).astype(BF16)
    x_scale = (wq_ref.shape[1] // N_X_HEADS) ** -0.5
    qx_ref[...] = (jnp.dot(xn, wq_ref[...], preferred_element_type=F32) * x_scale).astype(BF16)


def _mix_out(att, conv, h, w_out, g_post, g_x, w_xq):
    m, d = h.shape
    tm = min(MIX_ROW_TILE, m)
    row = lambda w: pl.BlockSpec((tm, w), lambda i: (i, 0))
    const = lambda a: pl.BlockSpec(a.shape, lambda i: (0, 0))
    est = 2 * 2 * d * d * 2 + 2 * tm * (2 * d * 4 + d * 2 + att.shape[1] * 6) + 4 * tm * d * 4
    return pl.pallas_call(
        _mix_out_kernel,
        grid=(m // tm,),
        in_specs=[row(att.shape[1]), row(conv.shape[1]), row(d), const(w_out), const(g_post), const(g_x),
                  const(w_xq)],
        out_specs=(row(d), row(d)),
        out_shape=(jax.ShapeDtypeStruct((m, d), F32), jax.ShapeDtypeStruct((m, d), BF16)),
        compiler_params=pltpu.CompilerParams(dimension_semantics=("parallel",),
                                             vmem_limit_bytes=_vmem_limit(est)),
        name="mix_out",
    )(att, conv, h, w_out, g_post, g_x, w_xq)


def _xattn_kernel(q_ref, mk_ref, mv_ref, o_ref):
    d = q_ref.shape[1]
    hd = d // N_X_HEADS
    by_piece = mk_ref.shape[1] != d
    width = LANES if by_piece else hd
    n_piece = hd // width
    stride = n_piece * N_X_HEADS
    n_tok = mk_ref.shape[0] // stride if by_piece else mk_ref.shape[0]

    def piece(ref, h, i):
        if by_piece:
            return ref[pl.ds(i * N_X_HEADS + h, n_tok, stride=stride), :].astype(BF16)
        return ref[:, h * hd:(h + 1) * hd].astype(BF16)

    def logits(h):
        s = None
        for i in range(n_piece):
            cols = slice(h * hd + i * width, h * hd + (i + 1) * width)
            part = _nt_dot(q_ref[:, cols], piece(mk_ref, h, i))
            s = part if s is None else s + part
        return s

    s_all = [logits(h) for h in range(N_X_HEADS)]
    for h in range(N_X_HEADS):
        s = s_all[h]
        p = jnp.exp(s - jnp.max(s, axis=1, keepdims=True))
        l = jnp.sum(p, axis=1, keepdims=True)
        pb = p.astype(BF16)
        for i in range(n_piece):
            cols = slice(h * hd + i * width, h * hd + (i + 1) * width)
            o = jnp.dot(pb, piece(mv_ref, h, i), preferred_element_type=F32)
            o_ref[:, cols] = (o / l).astype(o_ref.dtype)


def _xattn(qx, mem_k, mem_v):
    b, t, d = qx.shape
    n = mem_k.shape[1]
    tm = min(XATTN_ROW_TILE, t)
    row = pl.BlockSpec((None, tm, d), lambda i, j: (i, j, 0))
    mem = pl.BlockSpec((None,) + mem_k.shape[1:], lambda i, j: (i, 0, 0))
    return pl.pallas_call(
        _xattn_kernel,
        grid=(b, t // tm),
        in_specs=[row, mem, mem],
        out_specs=row,
        out_shape=jax.ShapeDtypeStruct((b, t, d), BF16),
        compiler_params=pltpu.CompilerParams(dimension_semantics=("parallel", "parallel"),
                                             vmem_limit_bytes=_vmem_limit(4 * n * d * 4 + 8 * tm * d * 4)),
        name="xattn",
    )(qx, mem_k, mem_v)


def _ffn_kernel(o_ref, h1_ref, wxo_ref, g_xpost_ref, g_pre_ref, wg_ref, wu_ref, wd_ref, g_post_ref, y_ref):
    h2 = h1_ref[...] + _rms(jnp.dot(o_ref[...], wxo_ref[...], preferred_element_type=F32), g_xpost_ref[...])
    xf = _rms(h2, g_pre_ref[...]).astype(BF16)
    d_ff = wg_ref.shape[1]
    f = jnp.zeros(h2.shape, F32)
    for c0 in range(0, d_ff, FFN_CHUNK):
        sl = slice(c0, c0 + FFN_CHUNK)
        g = jnp.dot(xf, wg_ref[:, sl], preferred_element_type=F32)
        u = jnp.dot(xf, wu_ref[:, sl], preferred_element_type=F32)
        a = (g * _sigmoid(g) * u).astype(BF16)
        f = f + jnp.dot(a, wd_ref[sl, :], preferred_element_type=F32)
    y_ref[...] = h2 + _rms(f, g_post_ref[...])


def _ffn(o, h1, w_xo, g_xpost, g_pre, w_gate, w_up, w_down, g_post):
    m, d = h1.shape
    d_ff = w_gate.shape[1]
    assert d_ff % FFN_CHUNK == 0
    tm = min(ROW_TILE, m)
    row = pl.BlockSpec((tm, d), lambda i: (i, 0))
    const = lambda a: pl.BlockSpec(a.shape, lambda i: (0, 0), pipeline_mode=pl.Buffered(1))
    est = (d * d + 3 * d * d_ff) * 2 + 2 * tm * d * (2 + 4 + 4) + 6 * tm * d * 4
    return pl.pallas_call(
        _ffn_kernel,
        grid=(m // tm,),
        in_specs=[row, row, const(w_xo), const(g_xpost), const(g_pre), const(w_gate), const(w_up),
                  const(w_down), const(g_post)],
        out_specs=row,
        out_shape=jax.ShapeDtypeStruct((m, d), F32),
        compiler_params=pltpu.CompilerParams(dimension_semantics=("parallel",),
                                             vmem_limit_bytes=_vmem_limit(est)),
        name="ffn",
    )(o, h1, w_xo, g_xpost, g_pre, w_gate, w_up, w_down, g_post)


def kernel(x_prompt, x_sample, mem_prompt, cache_k, cache_v, state_conv, cache_mem_k, cache_mem_v, page_table, rel_bias_table, norm_mix_pre, norm_mix_post, w_in, lambda_q1, lambda_k1, lambda_q2, lambda_k2, subln_g, dw_w, dw_b, conv_ln_g, conv_ln_b, beta_att, beta_conv, w_out, norm_x_pre, norm_x_post, w_xq, w_xk, w_xv, w_xo, norm_ffn_pre, norm_ffn_post, w_gate, w_up, w_down):
    assert w_in.shape[0] == 1, "single-layer trunk"
    bp, tp, d = x_prompt.shape
    bs, ts, _ = x_sample.shape
    assert ts <= MAX_NEW_TOKENS and tp >= CONV_K - 1
    n_mem = mem_prompt.shape[1]
    vec = lambda a: a[0].reshape(1, -1)
    wb = lambda a: a[0].astype(BF16)
    g_mix_pre, g_mix_post = vec(norm_mix_pre), vec(norm_mix_post)
    g_x_pre, g_x_post = vec(norm_x_pre), vec(norm_x_post)
    g_ffn_pre, g_ffn_post = vec(norm_ffn_pre), vec(norm_ffn_post)
    sub_g, b_att, b_conv = vec(subln_g), vec(beta_att), vec(beta_conv)
    c_b, c_g, c_bt = vec(dw_b), vec(conv_ln_g), vec(conv_ln_b)
    w_in_b, w_out_b, w_xq_b, w_xk_b, w_xv_b, w_xo_b = (wb(w) for w in (w_in, w_out, w_xq, w_xk, w_xv, w_xo))
    w_gate_b, w_up_b, w_down_b = wb(w_gate), wb(w_up), wb(w_down)
    dw = dw_w[0]

    blk = min(ATT_BLOCK, tp)
    lam_vecs = jnp.stack([lambda_q1[0], lambda_k1[0], lambda_q2[0], lambda_k2[0]])
    p_bias, s_bias, lam = _bias_tiles(rel_bias_table, lam_vecs, blk, ts)

    def tail(att, conv, h, mem_k, mem_v):
        b, t, _ = h.shape
        flat = lambda a: a.reshape(b * t, a.shape[-1])
        h1, qx = _mix_out(flat(att), flat(conv), flat(h), w_out_b, g_mix_post, g_x_pre, w_xq_b)
        o = _xattn(qx.reshape(b, t, d), mem_k, mem_v)
        y = _ffn(flat(o), h1, w_xo_b, g_x_post, g_ffn_pre, w_gate_b, w_up_b, w_down_b, g_ffn_post)
        return y.reshape(b, t, d)

    conv0 = jnp.zeros((bp, CONV_HALO, CONV_WIDTH), F32)
    qt_p, kb_p, vt_p, kf_p, vf_p, glu_p, conv_p = _in_proj(x_prompt, g_mix_pre, w_in_b, blk,
                                                          conv=(conv0, dw, c_b, c_g, c_bt, b_conv))
    x_s = jnp.pad(x_sample, ((0, 0), (0, SAMPLE_ROWS - ts), (0, 0)))
    rows_s = bs * SAMPLE_ROWS
    qt_s, _, _, kf_s, vf_s, glu_s = _in_proj(x_s.reshape(1, rows_s, d), g_mix_pre, w_in_b, min(blk, rows_s))
    new_rows = ts * N_ATT_HEADS
    kf_s = kf_s.reshape(bs, SAMPLE_ROWS * N_ATT_HEADS, ATT_V_DIM)[:, :new_rows]
    vf_s = vf_s.reshape(bs, SAMPLE_ROWS * N_ATT_HEADS, ATT_V_DIM)[:, :new_rows]
    glu_s = glu_s.reshape(bs, SAMPLE_ROWS, CONV_WIDTH)[:, :ts]
    q_s = jnp.transpose(qt_s[0], (0, 1, 3, 2))
    q_th = jnp.transpose(q_s.reshape(N_ATT_HEADS, bs, SAMPLE_ROWS, ATT_V_DIM), (1, 2, 0, 3))
    q_th = jnp.pad(q_th[:, :ts], ((0, 0), (0, MAX_NEW_TOKENS - ts), (0, 0), (0, 0)))
    q_rows = jnp.repeat(q_th.reshape(bs, MAX_NEW_TOKENS * N_ATT_HEADS, ATT_V_DIM), 2, axis=1)
    pad_page = lambda a: jnp.pad(a, ((0, 0), (0, PAGE_SIZE - new_rows), (0, 0)))
    n_phys = cache_k.shape[1]
    pool = lambda c: c.reshape(n_phys, PAGE_SIZE * N_ATT_HEADS, ATT_V_DIM)

    att_p, att_s = _attention(qt_p, kb_p, vt_p, p_bias, lam, sub_g, b_att, page_table, q_rows,
                              pool(cache_k), pool(cache_v), pad_page(kf_s), pad_page(vf_s), s_bias)

    mk_p, mv_p = _mem_kv(mem_prompt, w_xk_b, w_xv_b)
    y_p = tail(att_p, conv_p, x_prompt, mk_p, mv_p)

    xp_s = jnp.concatenate([state_conv[0], glu_s], axis=1)
    conv_s = _conv_sample(jnp.transpose(xp_s, (1, 0, 2)), ts, dw, c_b, c_g, c_bt, b_conv)
    conv_s = jnp.pad(jnp.transpose(conv_s, (1, 0, 2)), ((0, 0), (0, SAMPLE_ROWS - ts), (0, 0)))
    hd_x = d // N_X_HEADS
    mem_s = lambda c: jnp.transpose(c[0].reshape(bs, n_mem, N_X_HEADS, hd_x // LANES, LANES),
                                    (0, 1, 3, 2, 4)).reshape(bs, n_mem * d // LANES, LANES)
    y_s = tail(att_s, conv_s, x_s, mem_s(cache_mem_k), mem_s(cache_mem_v))

    heads = lambda a: a.reshape(1, a.shape[0], a.shape[1] // N_ATT_HEADS, N_ATT_HEADS, ATT_V_DIM)
    mem_heads = lambda a: a.reshape(1, bp, n_mem, N_X_HEADS, d // N_X_HEADS)
    return (y_p, y_s[:, :ts],
            heads(kf_p), heads(vf_p), glu_p[None, :, tp - (CONV_K - 1):],
            mem_heads(mk_p), mem_heads(mv_p),
            heads(kf_s), heads(vf_s), xp_s[None, :, ts:])
```

```python
import functools
import math

import numpy as np
import jax
import jax.numpy as jnp
from jax import lax
from jax.experimental import pallas as pl
from jax.experimental.pallas import tpu as pltpu

F32 = jnp.float32
BF16 = jnp.bfloat16

DIFF_HEAD_DIM = 64
DIFF_HEAD_SHIFT = DIFF_HEAD_DIM.bit_length() - 1
ATT_V_DIM = 2 * DIFF_HEAD_DIM
N_ATT_HEADS = 4
ATT_WIDTH = N_ATT_HEADS * ATT_V_DIM
CONV_WIDTH = 512
CONV_K = 31
N_BUCKETS = 32
MAX_DISTANCE = 128
N_X_HEADS = 4
PAGE_SIZE = 128
RMS_EPS = 1e-6
LN_EPS = 1e-5
ATT_SCALE = DIFF_HEAD_DIM ** -0.5
LOG2_E = math.log2(math.e)
LAM_INIT = 0.8 - 0.6 * math.exp(-0.3 * 0)
MASKED = -1e30
STALE_MAX_MARGIN = 30.0

V7X_VMEM_BYTES = 64 * 1024 * 1024
LANES = 128
SUBLANES = 8
BF16_ROWS = 16

ROW_TILE = 512
IN_ROW_TILE = 1024
MIX_ROW_TILE = 1024
XATTN_ROW_TILE = 1024
ATT_BLOCK = 256
ATT_HEADS_PER_STEP = 4
CONV_CHUNK = 64
CONV_HALO = 32
PAGES_PER_STEP = 8
SAMPLE_RING_SLOTS = 3
FFN_CHUNK = 256
SAMPLE_ROWS = 8
MAX_NEW_TOKENS = 4
ROWS_PER_TOKEN = 2 * N_ATT_HEADS
SAMPLE_Q_ROWS = MAX_NEW_TOKENS * ROWS_PER_TOKEN


VMEM_TEMPORARIES_FACTOR = 2
VMEM_SMALL_CALL_BYTES = 16 * 1024 * 1024
VMEM_LEFT_FREE_BYTES = 8 * 1024 * 1024


def _vmem_limit(block_bytes):
    wanted = max(VMEM_TEMPORARIES_FACTOR * block_bytes, VMEM_SMALL_CALL_BYTES)
    return int(min(wanted, V7X_VMEM_BYTES - VMEM_LEFT_FREE_BYTES))


def _rms(x, g):
    return x * lax.rsqrt(jnp.mean(x * x, axis=-1, keepdims=True) + RMS_EPS) * g


def _sigmoid(x):
    return 1.0 / (1.0 + jnp.exp(-x))


def _nt_dot(a, b):
    return lax.dot_general(a, b, (((1,), (1,)), ((), ())), preferred_element_type=F32)


def _bucket_np(n):
    n = np.maximum(n, 0)
    max_exact = N_BUCKETS // 2
    nf = np.maximum(n, 1).astype(np.float32)
    large = max_exact + (np.log(nf / max_exact) / math.log(MAX_DISTANCE / max_exact)
                         * (N_BUCKETS - max_exact)).astype(np.int32)
    large = np.minimum(large, N_BUCKETS - 1)
    return np.where(n < max_exact, n, large).astype(np.int32)


def _prompt_codes(blk):
    i = np.arange(blk)[None, :]
    j = np.arange(blk)[:, None]
    prev = _bucket_np(i - j + blk)
    diag = np.where(j > i, -1, _bucket_np(i - j))
    return np.concatenate([prev, diag]).astype(np.int32)


def _sample_codes(n_new):
    r = np.arange(SAMPLE_Q_ROWS)[:, None]
    c = np.arange(PAGE_SIZE * N_ATT_HEADS)[None, :]
    t, head = r // ROWS_PER_TOKEN, (r // 2) % N_ATT_HEADS
    tok, key_head = c // N_ATT_HEADS, c % N_ATT_HEADS
    own = key_head == head
    far = np.where(own, N_BUCKETS - 1, -1)
    last = np.where(own, _bucket_np(t + PAGE_SIZE - tok), -1)
    new = np.where(own & (tok <= t) & (tok < n_new) & (c < PAGE_SIZE), _bucket_np(t - tok), -1)
    return np.stack([far, last, new]).astype(np.int32)


def _bias_kernel(tab_ref, lam_in_ref, pcode_ref, scode_ref, pbias_ref, sbias_ref, lam_ref):
    far = N_BUCKETS - 1

    def lookup(code, h):
        out = jnp.zeros(code.shape, F32)
        for b in range(far):
            out = jnp.where(code == b, (tab_ref[b, h] - tab_ref[far, h]) * LOG2_E, out)
        return jnp.where(code < 0, MASKED, out)

    for h in range(N_ATT_HEADS):
        pbias_ref[h] = lookup(pcode_ref[...], h)
    row_head = (lax.broadcasted_iota(jnp.int32, scode_ref.shape[1:], 0) >> 1) & (N_ATT_HEADS - 1)
    for i in range(scode_ref.shape[0]):
        code = scode_ref[i]
        out = jnp.zeros(code.shape, F32)
        for h in range(N_ATT_HEADS):
            out = jnp.where(row_head == h, lookup(code, h), out)
        sbias_ref[i] = out
    lv = lam_in_ref[...]
    d1 = jnp.sum(lv[0:1] * lv[1:2], axis=1, keepdims=True)
    d2 = jnp.sum(lv[2:3] * lv[3:4], axis=1, keepdims=True)
    lam = jnp.exp(d1) - jnp.exp(d2) + LAM_INIT
    lam_ref[...] = jnp.broadcast_to(lam, lam_ref.shape)


def _bias_tiles(table, lam_vecs, blk, n_new):
    pcode = jnp.asarray(_prompt_codes(blk))
    scode = jnp.asarray(_sample_codes(n_new))
    vm = pl.BlockSpec(memory_space=pltpu.VMEM)
    return pl.pallas_call(
        _bias_kernel,
        out_shape=(jax.ShapeDtypeStruct((N_ATT_HEADS, 2 * blk, blk), F32),
                   jax.ShapeDtypeStruct(scode.shape, F32),
                   jax.ShapeDtypeStruct((SUBLANES, LANES), F32)),
        in_specs=[pl.BlockSpec(memory_space=pltpu.SMEM), vm, vm, vm],
        out_specs=(vm, vm, vm),
        name="bias_tiles",
    )(table, lam_vecs, pcode, scode)


def _in_proj_kernel(x_ref, g_ref, w_ref, *refs, with_conv):
    if with_conv:
        conv_in, refs = refs[:6], refs[6:]
        qt_ref, kb_ref, vt_ref, kf_ref, vf_ref, glu_ref, conv_ref, buf, ybuf = refs
    else:
        qt_ref, kb_ref, vt_ref, kf_ref, vf_ref, glu_ref = refs
    xn = _rms(x_ref[...], g_ref[...]).astype(BF16)

    def cols(c):
        return jnp.dot(xn, w_ref[:, c * ATT_WIDTH:(c + 1) * ATT_WIDTH], preferred_element_type=F32)

    tm = xn.shape[0]
    blk = qt_ref.shape[-1]
    glu = cols(3) * _sigmoid(cols(4))
    glu_ref[...] = glu
    if with_conv:
        _conv_tile(glu, *conv_in, conv_ref, buf, ybuf, pl.program_id(1) == 0)
    q = cols(0) * (ATT_SCALE * LOG2_E)
    k = cols(1)
    v = cols(2)
    kb = k.astype(BF16)
    for h in range(N_ATT_HEADS):
        sl = slice(h * ATT_V_DIM, (h + 1) * ATT_V_DIM)
        kb_ref[h] = kb[:, sl]
        for c in range(tm // blk):
            rows = slice(c * blk, (c + 1) * blk)
            qt_ref[h, c] = q[rows, sl].T.astype(BF16)
            vt_ref[h, c] = v[rows, sl].T.astype(BF16)
        kf_ref[pl.ds(h, tm, stride=N_ATT_HEADS), :] = k[:, sl]
        vf_ref[pl.ds(h, tm, stride=N_ATT_HEADS), :] = v[:, sl]


def _in_proj(x, gain, w_in, blk, conv=None):
    b, t, d = x.shape
    tm = min(IN_ROW_TILE, t)
    n_cols = w_in.shape[1]
    assert tm % blk == 0
    c = CONV_WIDTH
    conv = () if conv is None else tuple(conv)
    vec = pl.BlockSpec((1, c), lambda i, j: (0, 0))
    conv_specs = [pl.BlockSpec((None, CONV_HALO, c), lambda i, j: (i, 0, 0)),
                  pl.BlockSpec((CONV_K, c), lambda i, j: (0, 0)), vec, vec, vec, vec] if conv else []
    conv_out = [pl.BlockSpec((None, tm, c), lambda i, j: (i, j, 0))] if conv else []
    conv_shape = [jax.ShapeDtypeStruct((b, t, c), BF16)] if conv else []
    conv_scratch = [pltpu.VMEM((c // LANES, tm + CONV_HALO, LANES), F32),
                    pltpu.VMEM((c // LANES, tm, LANES), F32)] if conv else []
    row = lambda w: pl.BlockSpec((None, tm, w), lambda i, j: (i, j, 0))
    heads = pl.BlockSpec((None, N_ATT_HEADS, tm, ATT_V_DIM), lambda i, j: (i, 0, j, 0))
    hm = jax.ShapeDtypeStruct((b, N_ATT_HEADS, t, ATT_V_DIM), BF16)
    heads_t = pl.BlockSpec((None, N_ATT_HEADS, tm // blk, ATT_V_DIM, blk), lambda i, j: (i, 0, j, 0, 0))
    hm_t = jax.ShapeDtypeStruct((b, N_ATT_HEADS, t // blk, ATT_V_DIM, blk), BF16)
    flat = jax.ShapeDtypeStruct((b, t * N_ATT_HEADS, ATT_V_DIM), F32)
    tok_head = pl.BlockSpec((None, tm * N_ATT_HEADS, ATT_V_DIM), lambda i, j: (i, j, 0))
    est = 2 * d * n_cols * 2 + 2 * tm * (d * 4 + 3 * ATT_WIDTH * 2 + 3 * ATT_WIDTH * 4) + 6 * tm * ATT_WIDTH * 4
    return pl.pallas_call(
        functools.partial(_in_proj_kernel, with_conv=bool(conv)),
        grid=(b, t // tm),
        in_specs=[row(d),
                  pl.BlockSpec((1, d), lambda i, j: (0, 0)),
                  pl.BlockSpec((d, n_cols), lambda i, j: (0, 0))] + conv_specs,
        out_specs=tuple([heads_t, heads, heads_t, tok_head, tok_head, row(CONV_WIDTH)] + conv_out),
        out_shape=tuple([hm_t, hm, hm_t, flat, flat, jax.ShapeDtypeStruct((b, t, CONV_WIDTH), F32)]
                        + conv_shape),
        scratch_shapes=conv_scratch,
        compiler_params=pltpu.CompilerParams(dimension_semantics=("parallel", "arbitrary"),
                                             vmem_limit_bytes=_vmem_limit(est)),
        name="in_proj",
    )(x, gain, w_in, *conv)


def _conv_post(y, b_ref, g_ref, bt_ref, beta_ref):
    y = y + b_ref[...]
    mu = jnp.mean(y, axis=-1, keepdims=True)
    yc = y - mu
    yn = yc * lax.rsqrt(jnp.mean(yc * yc, axis=-1, keepdims=True) + LN_EPS) * g_ref[...] + bt_ref[...]
    return yn * _sigmoid(yn) * beta_ref[...]


def _conv_tile(glu, prev_ref, w_ref, b_ref, g_ref, bt_ref, beta_ref, o_ref, buf, ybuf, first):
    tt = glu.shape[0]
    n_slab = buf.shape[0]
    slabs = [slice(s * LANES, (s + 1) * LANES) for s in range(n_slab)]

    @pl.when(first)
    def _():
        for s in range(n_slab):
            buf[s, 0:CONV_HALO] = prev_ref[:, slabs[s]]

    @pl.when(jnp.logical_not(first))
    def _():
        for s in range(n_slab):
            buf[s, 0:CONV_HALO] = buf[s, tt:tt + CONV_HALO]

    for s in range(n_slab):
        buf[s, CONV_HALO:CONV_HALO + tt] = glu[:, slabs[s]]

    shift = CONV_HALO - (CONV_K - 1)
    rc = min(CONV_CHUNK, tt // 2)
    for c0 in range(0, tt, 2 * rc):
        for parity in range(2):
            start = c0 + parity
            accs = []
            for s in range(n_slab):
                acc = jnp.zeros((rc, LANES), F32)
                for j in range(CONV_K):
                    x = buf[s, pl.ds(start + j + shift, rc, stride=2), :]
                    acc = acc + w_ref[j:j + 1, slabs[s]] * x
                accs.append(acc)
            y = _conv_post(jnp.concatenate(accs, axis=1), b_ref, g_ref, bt_ref, beta_ref)
            for s in range(n_slab):
                ybuf[s, pl.ds(start, rc, stride=2), :] = y[:, slabs[s]]
    for s in range(n_slab):
        o_ref[:, slabs[s]] = ybuf[s].astype(o_ref.dtype)


def _conv_sample_kernel(xp_ref, w_ref, b_ref, g_ref, bt_ref, beta_ref, o_ref):
    n_t = o_ref.shape[0]
    for t in range(n_t):
        acc = jnp.zeros(xp_ref.shape[1:], F32)
        for j in range(CONV_K):
            acc = acc + w_ref[j:j + 1, :] * xp_ref[t + j]
        o_ref[t] = _conv_post(acc, b_ref, g_ref, bt_ref, beta_ref)


def _conv_sample(xp_t, n_t, dw_w, dw_b, ln_g, ln_b, beta):
    vm = pl.BlockSpec(memory_space=pltpu.VMEM)
    return pl.pallas_call(
        _conv_sample_kernel,
        out_shape=jax.ShapeDtypeStruct((n_t,) + xp_t.shape[1:], F32),
        in_specs=[vm] * 6,
        out_specs=vm,
        name="conv_sample",
    )(xp_t, dw_w, dw_b, ln_g, ln_b, beta)


def _softmax_update(s, v_dot, m_sc, l_sc, acc_sc):
    width = s.shape[1]
    m_prev = m_sc[...]
    m_new = jnp.maximum(m_prev, jnp.max(s, axis=1, keepdims=True))
    alpha = jnp.exp2(m_prev - m_new)
    p = jnp.exp2(s - jnp.concatenate([m_new] * (width // LANES), axis=1))
    l_sc[...] = alpha * l_sc[...] + jnp.sum(p, axis=1, keepdims=True)
    reps = acc_sc.shape[1] // LANES
    acc_sc[...] = jnp.concatenate([alpha] * reps, axis=1) * acc_sc[...] + v_dot(p.astype(BF16))
    m_sc[...] = m_new


def _subln(att, g, beta):
    return _rms(att, g) * (1.0 - LAM_INIT) * beta


def _sample_page_copies(pt_ref, pool_k, pool_v, kbuf, vbuf, sem, chunk, slot, chunks_per_seq):
    seq = chunk // chunks_per_seq
    first_page = (chunk % chunks_per_seq) * PAGES_PER_STEP
    copies = []
    for i in range(PAGES_PER_STEP):
        page = pt_ref[seq, first_page + i]
        copies.append(pltpu.make_async_copy(pool_k.at[page], kbuf.at[slot, i], sem.at[0, slot]))
        copies.append(pltpu.make_async_copy(pool_v.at[page], vbuf.at[slot, i], sem.at[1, slot]))
    return copies


def _attn_kernel(pt_ref, qt_ref, k_ref, vt_ref, bias_ref, lam_ref, g_ref, beta_ref,
                 sq_ref, pool_k, pool_v, kn_ref, vn_ref, sbias_ref, o_ref, so_ref,
                 m_sc, acc_sc, kbuf, vbuf, sem, sm_sc, sl_sc, sacc_sc, cnt_ref):
    n_heads, e, blk = qt_ref.shape
    qi = pl.program_id(2)
    n_seq_s, chunks_per_seq = so_ref.shape[0], pt_ref.shape[1] // PAGES_PER_STEP
    n_chunks = n_seq_s * chunks_per_seq
    first_grid_step = jnp.logical_and(pl.program_id(0) == 0, qi == 0)
    last_grid_step = jnp.logical_and(pl.program_id(0) == pl.num_programs(0) - 1, qi == pl.num_programs(2) - 1)
    ring = (pt_ref, pool_k, pool_v, kbuf, vbuf, sem)

    n_slots = kbuf.shape[0]
    lookahead = n_slots - 1
    last_chunk = n_chunks - 1

    @pl.when(first_grid_step)
    def _():
        cnt_ref[0] = 0
        for d in range(lookahead):
            for cp in _sample_page_copies(*ring, min(d, last_chunk), d, chunks_per_seq):
                cp.start()

    s_lane = lax.broadcasted_iota(jnp.int32, sq_ref.shape[1:], 1)
    s_row = lax.broadcasted_iota(jnp.int32, sq_ref.shape[1:], 0)

    def sample_fetch():
        n = cnt_ref[0]
        slot = lax.rem(n, n_slots)
        valid = n < n_chunks
        chunk = jnp.minimum(n, last_chunk)
        seq = chunk // chunks_per_seq
        c = chunk % chunks_per_seq
        for cp in _sample_page_copies(*ring, chunk, slot, chunks_per_seq):
            cp.wait()
        ahead_slot = lax.rem(n + lookahead, n_slots)
        for cp in _sample_page_copies(*ring, jnp.minimum(n + lookahead, last_chunk), ahead_slot, chunks_per_seq):
            cp.start()
        cnt_ref[0] = n + 1
        return slot, seq, c, valid

    def sample_logits(slot, seq, c, valid):
        q = sq_ref[seq]
        qm = jnp.where((s_lane >> DIFF_HEAD_SHIFT) == (s_row & 1), q, jnp.zeros_like(q))
        gate = jnp.where(valid, 0.0, MASKED)
        far_bias = sbias_ref[0] + gate
        last_bias = jnp.where(c == chunks_per_seq - 1, sbias_ref[1] + gate, far_bias)
        s = jnp.concatenate(
            [_nt_dot(qm, kbuf[slot, i].astype(BF16)) + (last_bias if i == PAGES_PER_STEP - 1 else far_bias)
             for i in range(PAGES_PER_STEP)], axis=1)
        return qm, s

    def sample_update(slot, seq, c, valid, qm, s):
        fresh = c == 0
        m_prev = jnp.where(fresh, MASKED, sm_sc[...])
        m_new = jnp.maximum(m_prev, jnp.max(s, axis=1, keepdims=True))
        alpha = jnp.exp2(m_prev - m_new)
        p = jnp.exp2(s - jnp.concatenate([m_new] * (s.shape[1] // LANES), axis=1))
        sl_sc[...] = alpha * jnp.where(fresh, 0.0, sl_sc[...]) + jnp.sum(p, axis=1, keepdims=True)
        pb = p.astype(BF16)
        cols = kbuf.shape[2]
        pv = None
        for i in range(PAGES_PER_STEP):
            part = jnp.dot(pb[:, i * cols:(i + 1) * cols], vbuf[slot, i].astype(BF16),
                           preferred_element_type=F32)
            pv = part if pv is None else pv + part
        sacc_sc[...] = alpha * jnp.where(fresh, 0.0, sacc_sc[...]) + pv
        sm_sc[...] = m_new
        return seq, qm, jnp.logical_and(valid, c == chunks_per_seq - 1)

    def sample_finish(seq, qm, seq_done):
        @pl.when(seq_done)
        def _():
            n_new = kn_ref.shape[1]
            s_new = _nt_dot(qm, kn_ref[seq].astype(BF16)) + sbias_ref[2][:, :n_new]
            _softmax_update(s_new, lambda pn: jnp.dot(pn, vn_ref[seq].astype(BF16), preferred_element_type=F32),
                            sm_sc, sl_sc, sacc_sc)
            sign = jnp.where((s_row & 1) == 0, 1.0, -lam_ref[0:1, 0:1])
            z = sacc_sc[...] / sl_sc[...] * sign
            out_row = lax.broadcasted_iota(jnp.int32, (so_ref.shape[1], ATT_V_DIM), 0)
            for h in range(N_ATT_HEADS):
                att = jnp.zeros(out_row.shape, F32)
                for t in range(MAX_NEW_TOKENS):
                    r0 = t * ROWS_PER_TOKEN + h * 2
                    att = jnp.where(out_row == t, jnp.broadcast_to(z[r0:r0 + 1] + z[r0 + 1:r0 + 2], att.shape),
                                    att)
                sl = slice(h * ATT_V_DIM, (h + 1) * ATT_V_DIM)
                so_ref[seq, :, sl] = _subln(att, g_ref[...], beta_ref[:, sl])

    ones_rows = jnp.ones((acc_sc.shape[2] - e, blk), BF16)
    sub = lax.broadcasted_iota(jnp.int32, qt_ref.shape[1:], 0)
    qst = []
    for h in range(n_heads):
        qt = qt_ref[h]
        zero = jnp.zeros_like(qt)
        qst.append(jnp.concatenate([jnp.where(sub < DIFF_HEAD_DIM, qt, zero),
                                    jnp.where(sub >= DIFF_HEAD_DIM, qt, zero)], axis=1))
    m_sc[...] = jnp.full(m_sc.shape, MASKED, F32)
    acc_sc[0] = jnp.zeros(acc_sc.shape[1:], F32)

    def step(j, n_blk, biased, stale_max, src, with_sample):
        rows = pl.ds(pl.multiple_of(j * blk, blk), n_blk * blk)
        dst = 1 - src
        if with_sample:
            chunk_state = sample_fetch()

        def logits(h):
            st = jnp.dot(k_ref[h, rows, :], qst[h], preferred_element_type=F32)
            if biased:
                bias = bias_ref[h, (2 - n_blk) * blk:, :]
                st = st + jnp.concatenate([bias, bias], axis=1)
            return st

        def weighted_values(h, pb):
            pv = None
            for i in range(n_blk):
                vt_aug = jnp.concatenate([vt_ref[h, j + i], ones_rows], axis=0)
                part = jnp.dot(vt_aug, pb[i * blk:(i + 1) * blk], preferred_element_type=F32)
                pv = part if pv is None else pv + part
            return pv

        def exact_update(h, st):
            m_prev = m_sc[h]
            m_new = jnp.maximum(m_prev, jnp.max(st, axis=0, keepdims=True))
            alpha = jnp.exp2(m_prev - m_new)
            pv = weighted_values(h, jnp.exp2(st - m_new).astype(BF16))
            acc_sc[dst, h] = alpha * acc_sc[src, h] + pv
            m_sc[h] = m_new

        def stale_update(h, st):
            m_ref = m_sc[h]
            pv = weighted_values(h, jnp.exp2(st - m_ref).astype(BF16))
            acc_sc[dst, h] = acc_sc[src, h] + pv
            return jnp.max(st, axis=0, keepdims=True) - m_ref

        ahead = 2
        if with_sample:
            chunk_logits = sample_logits(*chunk_state)
        pending = [logits(h) for h in range(min(ahead, n_heads))]
        excess = None
        for h in range(n_heads):
            if h + ahead < n_heads:
                pending.append(logits(h + ahead))
            if stale_max:
                over = stale_update(h, pending[h])
                excess = over if excess is None else jnp.maximum(excess, over)
            else:
                exact_update(h, pending[h])
        if with_sample:
            sample_finish(*sample_update(*chunk_state, *chunk_logits))
        if stale_max:
            @pl.when(jnp.max(excess) > STALE_MAX_MARGIN)
            def _():
                for h in range(n_heads):
                    exact_update(h, logits(h))

    odd = (qi + 1) & 1
    n_pairs = jnp.maximum(((qi + 1) >> 1) - 1, 0)
    single = jnp.logical_and(qi > 0, odd == 1)

    def key_blocks(with_sample):
        @pl.when(qi == 0)
        def _():
            step(0, 1, True, False, 0, with_sample)

        @pl.when(qi > 0)
        def _():
            step(qi - 1, 2, True, False, 0, with_sample)

        def far_pair(i, carry):
            step(odd + 2 * i, 2, False, True, (i + 1) & 1, with_sample)
            return carry

        lax.fori_loop(0, n_pairs, far_pair, 0)

        @pl.when(single)
        def _():
            step(0, 1, False, True, (n_pairs + 1) & 1, with_sample)

    chunks_left = cnt_ref[0] < n_chunks

    @pl.when(chunks_left)
    def _():
        key_blocks(True)

    @pl.when(jnp.logical_not(chunks_left))
    def _():
        key_blocks(False)

    res = (n_pairs + 1 + single.astype(jnp.int32)) & 1
    for h in range(n_heads):
        ot = acc_sc[res, h, 0:e] / acc_sc[res, h, e:e + 1]
        att = (ot[:, :blk] - lam_ref[0:1, 0:1] * ot[:, blk:]).T
        sl = slice(h * ATT_V_DIM, (h + 1) * ATT_V_DIM)
        o_ref[:, sl] = _subln(att, g_ref[...], beta_ref[:, sl]).astype(o_ref.dtype)

    @pl.when(last_grid_step)
    def _():
        def drain(i, carry):
            chunk_state = sample_fetch()
            sample_finish(*sample_update(*chunk_state, *sample_logits(*chunk_state)))
            return carry

        lax.fori_loop(0, jnp.maximum(n_chunks - cnt_ref[0], 0), drain, 0)
        n = cnt_ref[0]
        for d in range(lookahead):
            for cp in _sample_page_copies(*ring, jnp.minimum(n + d, last_chunk), lax.rem(n + d, n_slots),
                                          chunks_per_seq):
                cp.wait()


def _attention(qt, k, vt, bias, lam, subln_g, beta_att, page_table, q_rows, pool_k, pool_v, k_new, v_new,
               s_bias):
    b, h, t, e = k.shape
    blk = bias.shape[-1]
    n_blk = t // blk
    hs = ATT_HEADS_PER_STEP
    assert hs == h, "the sample epilogue reads every head's beta from the prompt block"
    n_seq, n_pages = page_table.shape
    rows = q_rows.shape[1]
    page_rows = pool_k.shape[1]
    npg = PAGES_PER_STEP
    assert n_pages % npg == 0
    once = pl.Buffered(1)
    whole = lambda a: pl.BlockSpec(a.shape, lambda i, j, n, pt: (0,) * a.ndim, pipeline_mode=once)
    sample_out = jax.ShapeDtypeStruct((n_seq, SAMPLE_ROWS, h * e), F32)
    grid_spec = pltpu.PrefetchScalarGridSpec(
        num_scalar_prefetch=1,
        grid=(b, h // hs, n_blk),
        in_specs=[pl.BlockSpec((None, hs, None, e, blk), lambda i, j, n, pt: (i, j, n, 0, 0)),
                  pl.BlockSpec((None, hs, t, e), lambda i, j, n, pt: (i, j, 0, 0), pipeline_mode=once),
                  pl.BlockSpec((None, hs, n_blk, e, blk), lambda i, j, n, pt: (i, j, 0, 0, 0), pipeline_mode=once),
                  pl.BlockSpec((hs, 2 * blk, blk), lambda i, j, n, pt: (j, 0, 0), pipeline_mode=once),
                  whole(lam), whole(subln_g),
                  pl.BlockSpec((1, hs * e), lambda i, j, n, pt: (0, j)),
                  whole(q_rows),
                  pl.BlockSpec(memory_space=pl.ANY), pl.BlockSpec(memory_space=pl.ANY),
                  whole(k_new), whole(v_new), whole(s_bias)],
        out_specs=(pl.BlockSpec((None, blk, hs * e), lambda i, j, n, pt: (i, n, j)),
                   pl.BlockSpec(sample_out.shape, lambda i, j, n, pt: (0, 0, 0))),
        scratch_shapes=[pltpu.VMEM((hs, 1, 2 * blk), F32),
                        pltpu.VMEM((2, hs, e + BF16_ROWS, 2 * blk), F32),
                        pltpu.VMEM((SAMPLE_RING_SLOTS, npg, page_rows, e), F32),
                        pltpu.VMEM((SAMPLE_RING_SLOTS, npg, page_rows, e), F32),
                        pltpu.SemaphoreType.DMA((2, SAMPLE_RING_SLOTS)),
                        pltpu.VMEM((rows, LANES), F32), pltpu.VMEM((rows, LANES), F32),
                        pltpu.VMEM((rows, e), F32),
                        pltpu.SMEM((1,), jnp.int32)],
    )
    est = (hs * (2 * t * e * 2) + hs * 2 * blk * blk * 4 + 4 * hs * 2 * blk * 2 * blk * 4
           + SAMPLE_RING_SLOTS * 2 * npg * page_rows * e * 4 + 2 * k_new.size * 4 + 8 * rows * npg * page_rows * 4)
    return pl.pallas_call(
        _attn_kernel,
        grid_spec=grid_spec,
        out_shape=(jax.ShapeDtypeStruct((b, t, h * e), BF16), sample_out),
        compiler_params=pltpu.CompilerParams(dimension_semantics=("arbitrary", "arbitrary", "arbitrary"),
                                             vmem_limit_bytes=_vmem_limit(est)),
        name="attention",
    )(page_table, qt, k, vt, bias, lam, subln_g, beta_att, q_rows, pool_k, pool_v, k_new, v_new, s_bias)


def _mem_kv_kernel(mem_ref, wk_ref, wv_ref, k_ref, v_ref):
    m = mem_ref[...].astype(BF16)
    k_ref[...] = jnp.dot(m, wk_ref[...], preferred_element_type=F32)
    v_ref[...] = jnp.dot(m, wv_ref[...], preferred_element_type=F32)


def _mem_kv(mem, w_xk, w_xv):
    b, n, d = mem.shape
    blk = pl.BlockSpec((None, n, d), lambda i: (i, 0, 0))
    w = pl.BlockSpec((d, d), lambda i: (0, 0))
    out = jax.ShapeDtypeStruct((b, n, d), F32)
    return pl.pallas_call(
        _mem_kv_kernel,
        grid=(b,),
        in_specs=[blk, w, w],
        out_specs=(blk, blk),
        out_shape=(out, out),
        compiler_params=pltpu.CompilerParams(dimension_semantics=("parallel",),
                                             vmem_limit_bytes=_vmem_limit(4 * d * d * 2 + 6 * n * d * 4)),
        name="mem_kv",
    )(mem, w_xk, w_xv)


def _mix_out_kernel(att_ref, conv_ref, h_ref, wo_ref, g_post_ref, g_x_ref, wq_ref, *refs):
    mem_refs, (h1_ref, qx_ref) = refs[:-2], refs[-2:]
    half = att_ref.shape[1]
    mo = (jnp.dot(att_ref[...].astype(BF16), wo_ref[0:half, :], preferred_element_type=F32)
          + jnp.dot(conv_ref[...].astype(BF16), wo_ref[half:, :], preferred_element_type=F32))
    h1 = h_ref[...] + _rms(mo, g_post_ref[...])
    h1_ref[...] = h1
    xn = _rms(h1, g_x_ref[...]).astype(BF16)
    x_scale = (wq_ref.shape[1] // N_X_HEADS) ** -0.5
    qx = (jnp.dot(xn, wq_ref[...], preferred_element_type=F32) * x_scale).astype(BF16)
    if mem_refs:
        _xattn_kernel(qx, *mem_refs, qx_ref)
    else:
        qx_ref[...] = qx


def _mix_out(att, conv, h, w_out, g_post, g_x, w_xq, mem=None):
    m, d = h.shape
    tm = min(MIX_ROW_TILE, m)
    row = lambda w: pl.BlockSpec((tm, w), lambda i: (i, 0))
    const = lambda a: pl.BlockSpec(a.shape, lambda i: (0, 0))
    mem = () if mem is None else tuple(mem)
    mem_specs = []
    if mem:
        tiles_per_seq = m // mem[0].shape[0] // tm
        assert tiles_per_seq * tm * mem[0].shape[0] == m
        mem_specs = [pl.BlockSpec((None,) + a.shape[1:], lambda i: (i // tiles_per_seq, 0, 0)) for a in mem]
    est = (2 * 2 * d * d * 2 + 2 * tm * (2 * d * 4 + d * 2 + att.shape[1] * 6) + 4 * tm * d * 4
           + sum(2 * a[0].size * 4 for a in mem))
    return pl.pallas_call(
        _mix_out_kernel,
        grid=(m // tm,),
        in_specs=[row(att.shape[1]), row(conv.shape[1]), row(d), const(w_out), const(g_post), const(g_x),
                  const(w_xq)] + mem_specs,
        out_specs=(row(d), row(d)),
        out_shape=(jax.ShapeDtypeStruct((m, d), F32), jax.ShapeDtypeStruct((m, d), BF16)),
        compiler_params=pltpu.CompilerParams(dimension_semantics=("parallel",),
                                             vmem_limit_bytes=_vmem_limit(est)),
        name="mix_out",
    )(att, conv, h, w_out, g_post, g_x, w_xq, *mem)


def _xattn_kernel(q_ref, mk_ref, mv_ref, o_ref):
    d = q_ref.shape[1]
    hd = d // N_X_HEADS
    by_piece = mk_ref.shape[1] != d
    width = LANES if by_piece else hd
    n_piece = hd // width
    stride = n_piece * N_X_HEADS
    n_tok = mk_ref.shape[0] // stride if by_piece else mk_ref.shape[0]

    def piece(ref, h, i):
        if by_piece:
            return ref[pl.ds(i * N_X_HEADS + h, n_tok, stride=stride), :].astype(BF16)
        return ref[:, h * hd:(h + 1) * hd].astype(BF16)

    def logits(h):
        s = None
        for i in range(n_piece):
            cols = slice(h * hd + i * width, h * hd + (i + 1) * width)
            part = _nt_dot(q_ref[:, cols], piece(mk_ref, h, i))
            s = part if s is None else s + part
        return s

    s_all = [logits(h) for h in range(N_X_HEADS)]
    for h in range(N_X_HEADS):
        s = s_all[h]
        p = jnp.exp(s - jnp.max(s, axis=1, keepdims=True))
        l = jnp.sum(p, axis=1, keepdims=True)
        pb = p.astype(BF16)
        for i in range(n_piece):
            cols = slice(h * hd + i * width, h * hd + (i + 1) * width)
            o = jnp.dot(pb, piece(mv_ref, h, i), preferred_element_type=F32)
            o_ref[:, cols] = (o / l).astype(o_ref.dtype)


def _xattn(qx, mem_k, mem_v):
    b, t, d = qx.shape
    n = mem_k.shape[1]
    tm = min(XATTN_ROW_TILE, t)
    row = pl.BlockSpec((None, tm, d), lambda i, j: (i, j, 0))
    mem = pl.BlockSpec((None,) + mem_k.shape[1:], lambda i, j: (i, 0, 0))
    return pl.pallas_call(
        _xattn_kernel,
        grid=(b, t // tm),
        in_specs=[row, mem, mem],
        out_specs=row,
        out_shape=jax.ShapeDtypeStruct((b, t, d), BF16),
        compiler_params=pltpu.CompilerParams(dimension_semantics=("parallel", "parallel"),
                                             vmem_limit_bytes=_vmem_limit(4 * n * d * 4 + 8 * tm * d * 4)),
        name="xattn",
    )(qx, mem_k, mem_v)


def _ffn_kernel(o_ref, h1_ref, wxo_ref, g_xpost_ref, g_pre_ref, wg_ref, wu_ref, wd_ref, g_post_ref, y_ref):
    h2 = h1_ref[...] + _rms(jnp.dot(o_ref[...], wxo_ref[...], preferred_element_type=F32), g_xpost_ref[...])
    xf = _rms(h2, g_pre_ref[...]).astype(BF16)
    d_ff = wg_ref.shape[1]
    f = jnp.zeros(h2.shape, F32)
    for c0 in range(0, d_ff, FFN_CHUNK):
        sl = slice(c0, c0 + FFN_CHUNK)
        g = jnp.dot(xf, wg_ref[:, sl], preferred_element_type=F32)
        u = jnp.dot(xf, wu_ref[:, sl], preferred_element_type=F32)
        a = (g * _sigmoid(g) * u).astype(BF16)
        f = f + jnp.dot(a, wd_ref[sl, :], preferred_element_type=F32)
    y_ref[...] = h2 + _rms(f, g_post_ref[...])


def _ffn(o, h1, w_xo, g_xpost, g_pre, w_gate, w_up, w_down, g_post):
    m, d = h1.shape
    d_ff = w_gate.shape[1]
    assert d_ff % FFN_CHUNK == 0
    tm = min(ROW_TILE, m)
    row = pl.BlockSpec((tm, d), lambda i: (i, 0))
    const = lambda a: pl.BlockSpec(a.shape, lambda i: (0, 0), pipeline_mode=pl.Buffered(1))
    est = (d * d + 3 * d * d_ff) * 2 + 2 * tm * d * (2 + 4 + 4) + 6 * tm * d * 4
    return pl.pallas_call(
        _ffn_kernel,
        grid=(m // tm,),
        in_specs=[row, row, const(w_xo), const(g_xpost), const(g_pre), const(w_gate), const(w_up),
                  const(w_down), const(g_post)],
        out_specs=row,
        out_shape=jax.ShapeDtypeStruct((m, d), F32),
        compiler_params=pltpu.CompilerParams(dimension_semantics=("parallel",),
                                             vmem_limit_bytes=_vmem_limit(est)),
        name="ffn",
    )(o, h1, w_xo, g_xpost, g_pre, w_gate, w_up, w_down, g_post)


def kernel(x_prompt, x_sample, mem_prompt, cache_k, cache_v, state_conv, cache_mem_k, cache_mem_v, page_table, rel_bias_table, norm_mix_pre, norm_mix_post, w_in, lambda_q1, lambda_k1, lambda_q2, lambda_k2, subln_g, dw_w, dw_b, conv_ln_g, conv_ln_b, beta_att, beta_conv, w_out, norm_x_pre, norm_x_post, w_xq, w_xk, w_xv, w_xo, norm_ffn_pre, norm_ffn_post, w_gate, w_up, w_down):
    assert w_in.shape[0] == 1, "single-layer trunk"
    bp, tp, d = x_prompt.shape
    bs, ts, _ = x_sample.shape
    assert ts <= MAX_NEW_TOKENS and tp >= CONV_K - 1
    n_mem = mem_prompt.shape[1]
    vec = lambda a: a[0].reshape(1, -1)
    wb = lambda a: a[0].astype(BF16)
    g_mix_pre, g_mix_post = vec(norm_mix_pre), vec(norm_mix_post)
    g_x_pre, g_x_post = vec(norm_x_pre), vec(norm_x_post)
    g_ffn_pre, g_ffn_post = vec(norm_ffn_pre), vec(norm_ffn_post)
    sub_g, b_att, b_conv = vec(subln_g), vec(beta_att), vec(beta_conv)
    c_b, c_g, c_bt = vec(dw_b), vec(conv_ln_g), vec(conv_ln_b)
    w_in_b, w_out_b, w_xq_b, w_xk_b, w_xv_b, w_xo_b = (wb(w) for w in (w_in, w_out, w_xq, w_xk, w_xv, w_xo))
    w_gate_b, w_up_b, w_down_b = wb(w_gate), wb(w_up), wb(w_down)
    dw = dw_w[0]

    blk = min(ATT_BLOCK, tp)
    lam_vecs = jnp.stack([lambda_q1[0], lambda_k1[0], lambda_q2[0], lambda_k2[0]])
    p_bias, s_bias, lam = _bias_tiles(rel_bias_table, lam_vecs, blk, ts)

    def tail(att, conv, h, mem_k, mem_v):
        b, t, _ = h.shape
        flat = lambda a: a.reshape(b * t, a.shape[-1])
        mix = (flat(att), flat(conv), flat(h), w_out_b, g_mix_post, g_x_pre, w_xq_b)
        if t % min(MIX_ROW_TILE, b * t) == 0:
            h1, o = _mix_out(*mix, mem=(mem_k, mem_v))
        else:
            h1, qx = _mix_out(*mix)
            o = _xattn(qx.reshape(b, t, d), mem_k, mem_v)
        y = _ffn(flat(o), h1, w_xo_b, g_x_post, g_ffn_pre, w_gate_b, w_up_b, w_down_b, g_ffn_post)
        return y.reshape(b, t, d)

    conv0 = jnp.zeros((bp, CONV_HALO, CONV_WIDTH), F32)
    qt_p, kb_p, vt_p, kf_p, vf_p, glu_p, conv_p = _in_proj(x_prompt, g_mix_pre, w_in_b, blk,
                                                          conv=(conv0, dw, c_b, c_g, c_bt, b_conv))
    x_s = jnp.pad(x_sample, ((0, 0), (0, SAMPLE_ROWS - ts), (0, 0)))
    rows_s = bs * SAMPLE_ROWS
    qt_s, _, _, kf_s, vf_s, glu_s = _in_proj(x_s.reshape(1, rows_s, d), g_mix_pre, w_in_b, min(blk, rows_s))
    new_rows = ts * N_ATT_HEADS
    kf_s = kf_s.reshape(bs, SAMPLE_ROWS * N_ATT_HEADS, ATT_V_DIM)[:, :new_rows]
    vf_s = vf_s.reshape(bs, SAMPLE_ROWS * N_ATT_HEADS, ATT_V_DIM)[:, :new_rows]
    glu_s = glu_s.reshape(bs, SAMPLE_ROWS, CONV_WIDTH)[:, :ts]
    q_s = jnp.transpose(qt_s[0], (0, 1, 3, 2))
    q_th = jnp.transpose(q_s.reshape(N_ATT_HEADS, bs, SAMPLE_ROWS, ATT_V_DIM), (1, 2, 0, 3))
    q_th = jnp.pad(q_th[:, :ts], ((0, 0), (0, MAX_NEW_TOKENS - ts), (0, 0), (0, 0)))
    q_rows = jnp.repeat(q_th.reshape(bs, MAX_NEW_TOKENS * N_ATT_HEADS, ATT_V_DIM), 2, axis=1)
    pad_page = lambda a: jnp.pad(a, ((0, 0), (0, PAGE_SIZE - new_rows), (0, 0)))
    n_phys = cache_k.shape[1]
    pool = lambda c: c.reshape(n_phys, PAGE_SIZE * N_ATT_HEADS, ATT_V_DIM)

    att_p, att_s = _attention(qt_p, kb_p, vt_p, p_bias, lam, sub_g, b_att, page_table, q_rows,
                              pool(cache_k), pool(cache_v), pad_page(kf_s), pad_page(vf_s), s_bias)

    mk_p, mv_p = _mem_kv(mem_prompt, w_xk_b, w_xv_b)
    y_p = tail(att_p, conv_p, x_prompt, mk_p, mv_p)

    xp_s = jnp.concatenate([state_conv[0], glu_s], axis=1)
    conv_s = _conv_sample(jnp.transpose(xp_s, (1, 0, 2)), ts, dw, c_b, c_g, c_bt, b_conv)
    conv_s = jnp.pad(jnp.transpose(conv_s, (1, 0, 2)), ((0, 0), (0, SAMPLE_ROWS - ts), (0, 0)))
    hd_x = d // N_X_HEADS
    mem_s = lambda c: jnp.transpose(c[0].reshape(bs, n_mem, N_X_HEADS, hd_x // LANES, LANES),
                                    (0, 1, 3, 2, 4)).reshape(bs, n_mem * d // LANES, LANES)
    y_s = tail(att_s, conv_s, x_s, mem_s(cache_mem_k), mem_s(cache_mem_v))

    heads = lambda a: a.reshape(1, a.shape[0], a.shape[1] // N_ATT_HEADS, N_ATT_HEADS, ATT_V_DIM)
    mem_heads = lambda a: a.reshape(1, bp, n_mem, N_X_HEADS, d // N_X_HEADS)
    return (y_p, y_s[:, :ts],
            heads(kf_p), heads(vf_p), glu_p[None, :, tp - (CONV_K - 1):],
            mem_heads(mk_p), mem_heads(mv_p),
            heads(kf_s), heads(vf_s), xp_s[None, :, ts:])
```
